```python
import jax
import jax.numpy as jnp
from jax import lax
import numpy as np

D_MODEL = 1024
BATCH = 8
SEQ = 2048
DEPTH = 4

N_MIXERS = 3
ROPE_THETA = 10000.0
Q_BLOCK = 128
LN_EPS = 1e-5
RMS_EPS = 1e-6
NEG_INF = -1e30
DEEPNORM_ALPHA = (2 * DEPTH) ** 0.25
DEEPNORM_BETA = (8 * DEPTH) ** -0.25
N_MLA_LAYERS = (DEPTH + 2) // 3
N_NSA_LAYERS = (DEPTH + 1) // 3
N_SB_LAYERS = DEPTH // 3

MLA_HEADS = 8
MLA_D_NOPE = 128
MLA_D_ROPE = 64
MLA_D_V = 128
MLA_Q_RANK = 256
MLA_KV_RANK = 256

NSA_HEADS = 8
NSA_KV_HEADS = 2
NSA_HEAD_DIM = 128
NSA_CMP_BLOCK = 32
NSA_CMP_STRIDE = 16
NSA_CMP_HIDDEN = 128
NSA_SLC_BLOCK = 64
NSA_SLC_TOPK = 8
NSA_SLC_Q_CHUNK = 64
NSA_WINDOW = 512
NSA_FORCE_BONUS = 1e3

SB_HEADS = 8
SB_HEAD_DIM = 128

MOE_GROUPS = 4
MOE_EXPERTS_PER_GROUP = 8
MOE_EXPERTS = MOE_GROUPS * MOE_EXPERTS_PER_GROUP
MOE_TOPK = 2
MOE_D_FF = 512
MOE_ROW_BLOCK = 128

kernel_name = 'hybrid_mla_nsa_stickbreak_hmoe_deepnorm'


def layer_norm(x, g, b):
    xf = x.astype(jnp.float32)
    mu = jnp.mean(xf, -1, keepdims=True)
    var = jnp.mean(jnp.square(xf - mu), -1, keepdims=True)
    return ((xf - mu) * lax.rsqrt(var + LN_EPS)).astype(x.dtype) * g + b


def rms_norm(x, g):
    xf = x.astype(jnp.float32)
    return (xf * lax.rsqrt(jnp.mean(xf * xf, -1, keepdims=True) + RMS_EPS)).astype(x.dtype) * g


def rope_tables(positions, dim):
    inv_freq = 1.0 / (ROPE_THETA ** (jnp.arange(0, dim, 2, dtype=jnp.float32) / dim))
    ang = positions.astype(jnp.float32)[..., None] * inv_freq
    return jnp.cos(ang), jnp.sin(ang)


def apply_rope(x, cos, sin):
    bshape = cos.shape[:2] + (1,) * (x.ndim - 3) + cos.shape[-1:]
    cos, sin = cos.reshape(bshape), sin.reshape(bshape)
    x1, x2 = jnp.split(x.astype(jnp.float32), 2, axis=-1)
    return jnp.concatenate([x1 * cos - x2 * sin, x2 * cos + x1 * sin], -1).astype(x.dtype)


def masked_softmax(s, mask):
    p = jax.nn.softmax(jnp.where(mask, s, NEG_INF), axis=-1)
    return jnp.where(mask, p, 0.0)


def adaln(c, w, b):
    m = (jax.nn.silu(c) @ w + b)[:, None, :]
    return jnp.split(m, 3, axis=-1)


def mla_mixer(h, cos, sin, w_in, g_q, w_uq, g_kv, w_ukv, w_o):
    B, S, _ = h.shape
    H, dn, dr, dv = MLA_HEADS, MLA_D_NOPE, MLA_D_ROPE, MLA_D_V
    c_q, c_kv, k_rope = jnp.split(h @ w_in, [MLA_Q_RANK, MLA_Q_RANK + MLA_KV_RANK], axis=-1)
    q = (rms_norm(c_q, g_q) @ w_uq).reshape(B, S, H, dn + dr)
    q_nope, q_rope = q[..., :dn], apply_rope(q[..., dn:], cos, sin)
    kv = (rms_norm(c_kv, g_kv) @ w_ukv).reshape(B, S, H, dn + dv)
    k_nope, v = kv[..., :dn], kv[..., dn:]
    k_rope = apply_rope(k_rope, cos, sin)
    scale = (dn + dr) ** -0.5
    k_pos = jnp.arange(S)

    def q_block(i):
        start = i * Q_BLOCK
        qn = lax.dynamic_slice_in_dim(q_nope, start, Q_BLOCK, axis=1)
        qr = lax.dynamic_slice_in_dim(q_rope, start, Q_BLOCK, axis=1)
        s = (jnp.einsum('bqhd,bkhd->bhqk', qn, k_nope)
             + jnp.einsum('bqhd,bkd->bhqk', qr, k_rope)).astype(jnp.float32) * scale
        causal = k_pos[None, :] <= (start + jnp.arange(Q_BLOCK))[:, None]
        p = masked_softmax(s, causal).astype(v.dtype)
        return jnp.einsum('bhqk,bkhd->bqhd', p, v)

    o = lax.map(q_block, jnp.arange(S // Q_BLOCK))
    o = jnp.moveaxis(o, 0, 1).reshape(B, S, H * dv)
    return o @ w_o


def nsa_mixer(h, cos, sin, w_in, pe_k, pe_v, w_ck1, w_ck2, w_cv1, w_cv2, w_o):
    B, S, _ = h.shape
    H, KH, Dh = NSA_HEADS, NSA_KV_HEADS, NSA_HEAD_DIM
    G = H // KH
    kvw = KH * Dh
    cuts = [int(v) for v in np.cumsum([H * Dh] + [kvw] * 6)]
    q, kc, vc, ks, vs, kw, vw, gl = jnp.split(h @ w_in, cuts, axis=-1)
    q = apply_rope(q.reshape(B, S, KH, G, Dh), cos, sin)
    kv_shape = (B, S, KH, Dh)
    kc, ks, kw = [apply_rope(t.reshape(kv_shape), cos, sin) for t in (kc, ks, kw)]
    vc, vs, vw = [t.reshape(kv_shape) for t in (vc, vs, vw)]
    gates = jax.nn.sigmoid(gl.astype(jnp.float32)).astype(h.dtype).reshape(B, S, KH, G, 3)
    scale = Dh ** -0.5
    pos = jnp.arange(S)

    n_cmp = (S - NSA_CMP_BLOCK) // NSA_CMP_STRIDE + 1
    cmp_start = np.arange(n_cmp, dtype=np.int32) * NSA_CMP_STRIDE
    cidx = cmp_start[:, None] + np.arange(NSA_CMP_BLOCK, dtype=np.int32)[None, :]

    def compress(t, pe, w1, w2):
        blk = t[:, cidx] + pe[:, None, :]
        return jax.nn.silu(jnp.einsum('bnlkd,lde->bnke', blk, w1)) @ w2

    k_cmp = compress(kc, pe_k, w_ck1, w_ck2)
    v_cmp = compress(vc, pe_v, w_cv1, w_cv2)
    s_cmp = jnp.einsum('bskgd,bnkd->bkgsn', q, k_cmp).astype(jnp.float32) * scale
    cmp_mask = (cmp_start + NSA_CMP_BLOCK - 1)[None, :] <= pos[:, None]
    p_cmp = masked_softmax(s_cmp, cmp_mask)
    o_cmp = jnp.einsum('bkgsn,bnkd->bskgd', p_cmp.astype(v_cmp.dtype), v_cmp)

    n_slc = S // NSA_SLC_BLOCK
    top_n = min(NSA_SLC_TOPK, n_slc)
    slc_start = np.arange(n_slc, dtype=np.int32) * NSA_SLC_BLOCK
    overlap = np.clip(np.minimum(cmp_start[:, None] + NSA_CMP_BLOCK, slc_start[None, :] + NSA_SLC_BLOCK)
                      - np.maximum(cmp_start[:, None], slc_start[None, :]), 0, None).astype(np.float32) / NSA_CMP_BLOCK
    imp = jnp.einsum('bkgsn,nj->bksj', p_cmp, jnp.asarray(overlap))
    blk_id = jnp.arange(n_slc)[None, :]
    cur = (pos // NSA_SLC_BLOCK)[:, None]
    forced = (blk_id == 0) | (blk_id == cur) | (blk_id == cur - 1)
    valid = jnp.asarray(slc_start)[None, :] <= pos[:, None]
    imp = jnp.where(valid, imp + jnp.where(forced, NSA_FORCE_BONUS, 0.0), -jnp.inf)
    _, sel = lax.top_k(imp, top_n)
    ks_blk = jnp.moveaxis(ks.reshape(B, n_slc, NSA_SLC_BLOCK, KH, Dh), 3, 1)
    vs_blk = jnp.moveaxis(vs.reshape(B, n_slc, NSA_SLC_BLOCK, KH, Dh), 3, 1)
    b_ix = jnp.arange(B)[:, None, None, None]
    k_ix = jnp.arange(KH)[None, :, None, None]
    C = NSA_SLC_Q_CHUNK
    n_tok = top_n * NSA_SLC_BLOCK

    def slc_chunk(i):
        start = i * C
        qc = lax.dynamic_slice_in_dim(q, start, C, axis=1)
        selc = lax.dynamic_slice_in_dim(sel, start, C, axis=2)
        kg = ks_blk[b_ix, k_ix, selc].reshape(B, KH, C, n_tok, Dh)
        vg = vs_blk[b_ix, k_ix, selc].reshape(B, KH, C, n_tok, Dh)
        tok_pos = (selc[..., None] * NSA_SLC_BLOCK + jnp.arange(NSA_SLC_BLOCK)).reshape(B, KH, C, n_tok)
        m = tok_pos <= (start + jnp.arange(C))[None, None, :, None]
        s = jnp.einsum('bckgd,bkcnd->bkgcn', qc, kg).astype(jnp.float32) * scale
        p = masked_softmax(s, m[:, :, None])
        return jnp.einsum('bkgcn,bkcnd->bckgd', p.astype(vg.dtype), vg)

    o_slc = jnp.moveaxis(lax.map(slc_chunk, jnp.arange(S // C)), 0, 1).reshape(B, S, KH, G, Dh)

    QB = Q_BLOCK
    nb = S // QB
    span = NSA_WINDOW + QB
    widx = np.arange(nb, dtype=np.int32)[:, None] * QB + np.arange(span, dtype=np.int32)[None, :]
    pad = ((0, 0), (NSA_WINDOW, 0), (0, 0), (0, 0))
    k_win = jnp.pad(kw, pad)[:, widx]
    v_win = jnp.pad(vw, pad)[:, widx]
    k_pos = widx - NSA_WINDOW
    q_pos = np.arange(S, dtype=np.int32).reshape(nb, QB)
    diff = q_pos[:, :, None] - k_pos[:, None, :]
    band = (diff >= 0) & (diff < NSA_WINDOW) & (k_pos[:, None, :] >= 0)
    s_win = jnp.einsum('bnqkgd,bnskd->bnkgqs', q.reshape(B, nb, QB, KH, G, Dh), k_win).astype(jnp.float32) * scale
    p_win = masked_softmax(s_win, band[None, :, None, None])
    o_win = jnp.einsum('bnkgqs,bnskd->bnqkgd', p_win.astype(v_win.dtype), v_win).reshape(B, S, KH, G, Dh)

    o = gates[..., 0:1] * o_cmp + gates[..., 1:2] * o_slc + gates[..., 2:3] * o_win
    return o.reshape(B, S, H * Dh) @ w_o


def sb_mixer(h, w_qkv, w_o):
    B, S, _ = h.shape
    H, d = SB_HEADS, SB_HEAD_DIM
    qkv = (h @ w_qkv).reshape(B, S, 3, H, d)
    q, k, v = qkv[:, :, 0], qkv[:, :, 1], qkv[:, :, 2]
    scale = d ** -0.5
    k_pos = jnp.arange(S)

    def q_block(i):
        start = i * Q_BLOCK
        qb = lax.dynamic_slice_in_dim(q, start, Q_BLOCK, axis=1)
        z = jnp.einsum('bqhd,bkhd->bhqk', qb, k).astype(jnp.float32) * scale
        strict = k_pos[None, :] < (start + jnp.arange(Q_BLOCK))[:, None]
        log_keep = jnp.where(strict, jax.nn.log_sigmoid(-z), 0.0)
        between = lax.cumsum(log_keep, axis=3, reverse=True) - log_keep
        a = jnp.where(strict, jnp.exp(jax.nn.log_sigmoid(z) + between), 0.0)
        return jnp.einsum('bhqk,bkhd->bqhd', a.astype(v.dtype), v)

    o = lax.map(q_block, jnp.arange(S // Q_BLOCK))
    o = jnp.moveaxis(o, 0, 1).reshape(B, S, H * d)
    return o @ w_o


def grouped_swiglu(xt, experts, weights, w_gate, w_up, w_down):
    T, D = xt.shape
    E, R = w_gate.shape[0], MOE_ROW_BLOCK
    K = experts.shape[1]
    N = T * K
    e_flat = experts.reshape(N)
    order = jnp.argsort(e_flat)
    e_sorted = e_flat[order]
    tok_sorted = (order // K).astype(jnp.int32)
    w_sorted = weights.reshape(N)[order].astype(xt.dtype)
    counts = jnp.bincount(e_flat, length=E)
    padded = (counts + R - 1) // R * R
    ends = jnp.cumsum(padded)
    dest = (ends - padded)[e_sorted] + jnp.arange(N) - (jnp.cumsum(counts) - counts)[e_sorted]
    P = ((N + R - 1) // R + E) * R
    row_tok = jnp.full((P,), T, jnp.int32).at[dest].set(tok_sorted)
    row_w = jnp.zeros((P,), xt.dtype).at[dest].set(w_sorted)
    n_blk = P // R
    blk_exp = jnp.minimum(jnp.sum(jnp.arange(n_blk)[:, None] * R >= ends[None, :], axis=1), E - 1)
    x_rows = jnp.concatenate([xt, jnp.zeros((1, D), xt.dtype)], axis=0)[row_tok].reshape(n_blk, R, D)

    def expert_block(args):
        xb, e = args
        return (jax.nn.silu(xb @ w_gate[e]) * (xb @ w_up[e])) @ w_down[e]

    y = lax.map(expert_block, (x_rows, blk_exp)).reshape(P, D) * row_w[:, None]
    return jnp.zeros((T + 1, D), y.dtype).at[row_tok].add(y)[:T]


def hier_moe(h, w_grp, b_grp, w_rt, b_rt, w_gate, w_up, w_down):
    B, S, D = h.shape
    xt = h.reshape(B * S, D)
    T = xt.shape[0]
    tok = jnp.arange(T)
    g_logit = (xt @ w_grp + b_grp).astype(jnp.float32)
    g_idx = jnp.argmax(g_logit, axis=-1)
    g_w = jax.nn.softmax(g_logit, axis=-1)[tok, g_idx]
    e_logit = (xt @ w_rt + b_rt).astype(jnp.float32).reshape(T, MOE_GROUPS, MOE_EXPERTS_PER_GROUP)[tok, g_idx]
    e_p, e_local = lax.top_k(jax.nn.softmax(e_logit, axis=-1), MOE_TOPK)
    weights = g_w[:, None] * e_p / jnp.sum(e_p, -1, keepdims=True)
    experts = g_idx[:, None] * MOE_EXPERTS_PER_GROUP + e_local
    return grouped_swiglu(xt, experts, weights, w_gate, w_up, w_down).reshape(B, S, D)


def setup_inputs(seed: int = 0) -> dict:
    key = jax.random.key(seed)
    keys = iter(jax.random.split(key, 40))
    D = D_MODEL

    def normal(shape, std):
        return jax.random.normal(next(keys), shape, jnp.float32) * std

    nA, nB, nC = N_MLA_LAYERS, N_NSA_LAYERS, N_SB_LAYERS
    mla_in = MLA_Q_RANK + MLA_KV_RANK + MLA_D_ROPE
    nsa_in = NSA_HEADS * NSA_HEAD_DIM + 6 * NSA_KV_HEADS * NSA_HEAD_DIM + 3 * NSA_HEADS
    E, F = MOE_EXPERTS, MOE_D_FF
    return {
        'x': normal((BATCH, SEQ, D), 1.0),
        'c': normal((BATCH, D), 1.0),
        'positions': jnp.arange(SEQ, dtype=jnp.int32)[None, :]
                     + jax.random.randint(next(keys), (BATCH, 1), 0, 4096, dtype=jnp.int32),
        'ada_w': normal((DEPTH, 2, D, 3 * D), 0.5 * D ** -0.5),
        'ada_b': normal((DEPTH, 2, 3 * D), 0.02),
        'ln_g': 1.0 + normal((DEPTH, 2, D), 0.05),
        'ln_b': normal((DEPTH, 2, D), 0.02),
        'mla_w_in': normal((nA, D, mla_in), D ** -0.5),
        'mla_g_q': 1.0 + normal((nA, MLA_Q_RANK), 0.05),
        'mla_w_uq': normal((nA, MLA_Q_RANK, MLA_HEADS * (MLA_D_NOPE + MLA_D_ROPE)), MLA_Q_RANK ** -0.5),
        'mla_g_kv': 1.0 + normal((nA, MLA_KV_RANK), 0.05),
        'mla_w_ukv': normal((nA, MLA_KV_RANK, MLA_HEADS * (MLA_D_NOPE + MLA_D_V)), MLA_KV_RANK ** -0.5),
        'mla_w_o': normal((nA, MLA_HEADS * MLA_D_V, D), DEEPNORM_BETA * (MLA_HEADS * MLA_D_V) ** -0.5),
        'nsa_w_in': normal((nB, D, nsa_in), D ** -0.5),
        'nsa_pe_k': normal((nB, NSA_CMP_BLOCK, NSA_HEAD_DIM), 0.1),
        'nsa_pe_v': normal((nB, NSA_CMP_BLOCK, NSA_HEAD_DIM), 0.1),
        'nsa_w_ck1': normal((nB, NSA_CMP_BLOCK, NSA_HEAD_DIM, NSA_CMP_HIDDEN), (NSA_CMP_BLOCK * NSA_HEAD_DIM) ** -0.5),
        'nsa_w_ck2': normal((nB, NSA_CMP_HIDDEN, NSA_HEAD_DIM), NSA_CMP_HIDDEN ** -0.5),
        'nsa_w_cv1': normal((nB, NSA_CMP_BLOCK, NSA_HEAD_DIM, NSA_CMP_HIDDEN), (NSA_CMP_BLOCK * NSA_HEAD_DIM) ** -0.5),
        'nsa_w_cv2': normal((nB, NSA_CMP_HIDDEN, NSA_HEAD_DIM), NSA_CMP_HIDDEN ** -0.5),
        'nsa_w_o': normal((nB, NSA_HEADS * NSA_HEAD_DIM, D), DEEPNORM_BETA * (NSA_HEADS * NSA_HEAD_DIM) ** -0.5),
        'sb_w_qkv': normal((nC, D, 3 * SB_HEADS * SB_HEAD_DIM), D ** -0.5),
        'sb_w_o': normal((nC, SB_HEADS * SB_HEAD_DIM, D), DEEPNORM_BETA * (SB_HEADS * SB_HEAD_DIM) ** -0.5),
        'moe_w_grp': normal((DEPTH, D, MOE_GROUPS), D ** -0.5),
        'moe_b_grp': normal((DEPTH, MOE_GROUPS), 0.01),
        'moe_w_rt': normal((DEPTH, D, E), D ** -0.5),
        'moe_b_rt': normal((DEPTH, E), 0.01),
        'moe_w_gate': normal((DEPTH, E, D, F), D ** -0.5),
        'moe_w_up': normal((DEPTH, E, D, F), D ** -0.5),
        'moe_w_down': normal((DEPTH, E, F, D), DEEPNORM_BETA * F ** -0.5),
    }


def reference(x, c, positions, ada_w, ada_b, ln_g, ln_b,
              mla_w_in, mla_g_q, mla_w_uq, mla_g_kv, mla_w_ukv, mla_w_o,
              nsa_w_in, nsa_pe_k, nsa_pe_v, nsa_w_ck1, nsa_w_ck2, nsa_w_cv1, nsa_w_cv2, nsa_w_o,
              sb_w_qkv, sb_w_o,
              moe_w_grp, moe_b_grp, moe_w_rt, moe_b_rt, moe_w_gate, moe_w_up, moe_w_down):
    cos_mla, sin_mla = rope_tables(positions, MLA_D_ROPE)
    cos_nsa, sin_nsa = rope_tables(positions, NSA_HEAD_DIM)
    for i in range(DEPTH):
        kind, j = i % N_MIXERS, i // N_MIXERS
        shift, scale, gate = adaln(c, ada_w[i, 0], ada_b[i, 0])
        h = x * (1.0 + scale) + shift
        if kind == 0:
            y = mla_mixer(h, cos_mla, sin_mla, mla_w_in[j], mla_g_q[j], mla_w_uq[j],
                          mla_g_kv[j], mla_w_ukv[j], mla_w_o[j])
        elif kind == 1:
            y = nsa_mixer(h, cos_nsa, sin_nsa, nsa_w_in[j], nsa_pe_k[j], nsa_pe_v[j],
                          nsa_w_ck1[j], nsa_w_ck2[j], nsa_w_cv1[j], nsa_w_cv2[j], nsa_w_o[j])
        else:
            y = sb_mixer(h, sb_w_qkv[j], sb_w_o[j])
        x = layer_norm(DEEPNORM_ALPHA * x + gate * y, ln_g[i, 0], ln_b[i, 0])
        shift, scale, gate = adaln(c, ada_w[i, 1], ada_b[i, 1])
        h = x * (1.0 + scale) + shift
        y = hier_moe(h, moe_w_grp[i], moe_b_grp[i], moe_w_rt[i], moe_b_rt[i],
                     moe_w_gate[i], moe_w_up[i], moe_w_down[i])
        x = layer_norm(DEEPNORM_ALPHA * x + gate * y, ln_g[i, 1], ln_b[i, 1])
    return x
```

```python
import functools

import numpy as np
import jax
import jax.numpy as jnp
from jax import lax
from jax.experimental import pallas as pl
from jax.experimental.pallas import tpu as pltpu

BF16 = jnp.bfloat16
F32 = jnp.float32

DEPTH = 4
N_MIXERS = 3
ROPE_THETA = 10000.0
LN_EPS = 1e-5
RMS_EPS = 1e-6
NEG = -1e30
ALPHA = (2 * DEPTH) ** 0.25

MLA_HEADS, MLA_DN, MLA_DR, MLA_DV = 8, 128, 64, 128
MLA_QR, MLA_KVR = 256, 256

NSA_HEADS, NSA_KVH, NSA_DH = 8, 2, 128
NSA_G = NSA_HEADS // NSA_KVH
NSA_CMP_BLOCK, NSA_CMP_STRIDE = 32, 16
NSA_SLC_BLOCK, NSA_TOPK = 64, 8
SLC_SHIFT = 6
NSA_WINDOW = 512
NSA_BONUS = 1e3

SB_HEADS, SB_DH = 8, 128

MOE_GROUPS, MOE_EPG = 4, 8
EPG_SHIFT = 3
MOE_E = MOE_GROUPS * MOE_EPG
MOE_ROWS = 256

LANES = 128
VMEM_BYTES_V7X = 64 * 1024 * 1024


def _cp(sem, vmem_mb):
    assert vmem_mb * 2**20 < VMEM_BYTES_V7X
    return pltpu.CompilerParams(dimension_semantics=sem, vmem_limit_bytes=vmem_mb * 2**20)


def _nt_dot(a, b):
    return lax.dot_general(a, b, (((1,), (1,)), ((), ())), preferred_element_type=F32)


def _dot(a, b):
    return jnp.dot(a, b, preferred_element_type=F32)


def _split_bf16(a):
    hi = a.astype(BF16)
    lo = (a - hi.astype(F32)).astype(BF16)
    return hi, lo


def _layer_norm(z, g, b):
    mu = jnp.mean(z, axis=-1, keepdims=True)
    d = z - mu
    var = jnp.mean(d * d, axis=-1, keepdims=True)
    return d * lax.rsqrt(var + LN_EPS) * g + b


def _ada_kernel(c_ref, w_ref, b_ref, o_ref):
    c = c_ref[...]
    s_hi, s_lo = _split_bf16(c * jax.nn.sigmoid(c))
    w_hi, w_lo = _split_bf16(w_ref[0])
    o_ref[0] = _dot(s_hi, w_hi) + _dot(s_lo, w_hi) + _dot(s_hi, w_lo) + b_ref[0]


def _ada_all(c, ada_w, ada_b):
    L, two, D, D3 = ada_w.shape
    B = c.shape[0]
    n = L * two
    tn = 1024
    return pl.pallas_call(
        _ada_kernel,
        grid=(n, D3 // tn),
        in_specs=[pl.BlockSpec((B, D), lambda i, j: (0, 0)),
                  pl.BlockSpec((1, D, tn), lambda i, j: (i, 0, j)),
                  pl.BlockSpec((1, 1, tn), lambda i, j: (i, 0, j))],
        out_specs=pl.BlockSpec((1, B, tn), lambda i, j: (i, 0, j)),
        out_shape=jax.ShapeDtypeStruct((n, B, D3), F32),
        compiler_params=_cp(("arbitrary", "arbitrary"), 32),
        name="ada",
    )(c, ada_w.reshape(n, D, D3), ada_b.reshape(n, 1, D3))


def _res_ln_kernel(x_ref, y_ref, gate_ref, g_ref, b_ref, o_ref):
    z = ALPHA * x_ref[0] + gate_ref[0] * y_ref[0]
    o_ref[0] = _layer_norm(z, g_ref[...], b_ref[...])


def _res_ln(x, y, gate, g, b, ts=512):
    B, S, D = x.shape
    ts = min(ts, S)
    tok = pl.BlockSpec((1, ts, D), lambda bi, i: (bi, i, 0))
    return pl.pallas_call(
        _res_ln_kernel,
        grid=(B, S // ts),
        in_specs=[tok, tok,
                  pl.BlockSpec((1, 1, D), lambda bi, i: (bi, 0, 0)),
                  pl.BlockSpec((1, D), lambda bi, i: (0, 0)),
                  pl.BlockSpec((1, D), lambda bi, i: (0, 0))],
        out_specs=tok,
        out_shape=jax.ShapeDtypeStruct((B, S, D), F32),
        compiler_params=_cp(("arbitrary", "arbitrary"), 32),
        name="res_ln",
    )(x, y, gate, g.reshape(1, D), b.reshape(1, D))


def _out_ln_kernel(o_ref, w_ref, x_ref, gate_ref, g_ref, b_ref, out_ref):
    y = _dot(o_ref[0], w_ref[...])
    z = ALPHA * x_ref[0] + gate_ref[0] * y
    out_ref[0] = _layer_norm(z, g_ref[...], b_ref[...])


def _out_ln(o, w_o, x, gate, g, b, ts=512):
    B, S, D = x.shape
    K = o.shape[-1]
    ts = min(ts, S)
    return pl.pallas_call(
        _out_ln_kernel,
        grid=(B, S // ts),
        in_specs=[pl.BlockSpec((1, ts, K), lambda bi, i: (bi, i, 0)),
                  pl.BlockSpec((K, D), lambda bi, i: (0, 0)),
                  pl.BlockSpec((1, ts, D), lambda bi, i: (bi, i, 0)),
                  pl.BlockSpec((1, 1, D), lambda bi, i: (bi, 0, 0)),
                  pl.BlockSpec((1, D), lambda bi, i: (0, 0)),
                  pl.BlockSpec((1, D), lambda bi, i: (0, 0))],
        out_specs=pl.BlockSpec((1, ts, D), lambda bi, i: (bi, i, 0)),
        out_shape=jax.ShapeDtypeStruct((B, S, D), F32),
        compiler_params=_cp(("arbitrary", "arbitrary"), 40),
        name="out_ln",
    )(o, w_o.astype(BF16), x, gate, g.reshape(1, D), b.reshape(1, D))


def _rope_tables(positions, dim):
    inv_freq = 1.0 / (ROPE_THETA ** (jnp.arange(0, dim, 2, dtype=F32) / dim))
    ang = positions.astype(F32)[..., None] * inv_freq
    return jnp.cos(ang), jnp.sin(ang)


def _rot_half_cols(w, half):
    return jnp.concatenate([-w[..., half:], w[..., :half]], axis=-1)


def _mla_front_kernel(x_ref, sc_ref, sh_ref, win_ref, gq_ref, gkv_ref, wq_ref, wkv_ref, t_ref,
                      q_ref, k_ref, v_ref, *, scale):
    hb = (x_ref[0] * (1.0 + sc_ref[0]) + sh_ref[0]).astype(BF16)
    r = _dot(hb, win_ref[...])
    tab = t_ref[0]
    lane = lax.broadcasted_iota(jnp.int32, tab.shape, 1)
    low = lane < MLA_DR

    def rope_hi(u):
        t = u * tab
        return jnp.where(low, t + pltpu.roll(t, MLA_DR, axis=1), 0.0)

    def rms(cx, g):
        return (cx * lax.rsqrt(jnp.mean(cx * cx, axis=-1, keepdims=True) + RMS_EPS) * g).astype(BF16)

    cq = rms(r[:, :MLA_QR], gq_ref[...])
    ckv = rms(r[:, MLA_QR:MLA_QR + MLA_KVR], gkv_ref[...])
    k_rope = rope_hi(r[:, MLA_QR + MLA_KVR:]).astype(BF16)
    for h in range(MLA_HEADS):
        qh = _dot(cq, wq_ref[h])
        q_ref[0, h, :, :MLA_DN] = (qh[:, :MLA_DN] * scale).astype(BF16)
        q_ref[0, h, :, MLA_DN:] = (rope_hi(qh[:, MLA_DN:]) * scale).astype(BF16)
        kvh = _dot(ckv, wkv_ref[h])
        k_ref[0, h, :, :MLA_DN] = kvh[:, :MLA_DN].astype(BF16)
        k_ref[0, h, :, MLA_DN:] = k_rope
        v_ref[0, h] = kvh[:, MLA_DN:].astype(BF16)


def _flash_causal_kernel(q_ref, k_ref, v_ref, o_ref, m_sc, l_sc, acc_sc, *, tq, tk):
    qi = pl.program_id(2)
    q = q_ref[0, 0]
    m_sc[...] = jnp.full(m_sc.shape, NEG, F32)
    l_sc[...] = jnp.zeros(l_sc.shape, F32)
    acc_sc[...] = jnp.zeros(acc_sc.shape, F32)
    row = qi * tq + lax.broadcasted_iota(jnp.int32, (tq, 1), 0)
    n_kt = ((qi + 1) * tq + tk - 1) // tk

    def body(kt, carry):
        ks = pl.multiple_of(kt * tk, tk)
        s = _nt_dot(q, k_ref[0, 0, pl.ds(ks, tk), :])
        col = ks + lax.broadcasted_iota(jnp.int32, (1, tk), 1)
        s = jnp.where(col <= row, s, NEG)
        m_prev = m_sc[...]
        m_new = jnp.maximum(m_prev, jnp.max(s, axis=1, keepdims=True))
        p = jnp.exp(s - m_new)
        a = jnp.exp(m_prev - m_new)
        l_sc[...] = a * l_sc[...] + jnp.sum(p, axis=1, keepdims=True)
        acc_sc[...] = a * acc_sc[...] + _dot(p.astype(BF16), v_ref[0, 0, pl.ds(ks, tk), :])
        m_sc[...] = m_new
        return carry

    lax.fori_loop(0, n_kt, body, 0)
    o_ref[0] = (acc_sc[...] / l_sc[...]).astype(o_ref.dtype)


def _mla_layer(x, shift, scale_v, gate, cos, sin, w_in, g_q, w_uq, g_kv, w_ukv, w_o, ln_g, ln_b):
    B, S, D = x.shape
    H, dn, dr, dv = MLA_HEADS, MLA_DN, MLA_DR, MLA_DV
    half = dr // 2
    kr = w_in[:, MLA_QR + MLA_KVR:]
    win_p = jnp.concatenate([w_in, _rot_half_cols(kr, half)], axis=1).astype(BF16)
    wq = w_uq.reshape(MLA_QR, H, dn + dr)
    wq = jnp.concatenate([wq, _rot_half_cols(wq[..., dn:], half)], axis=-1)
    wq = jnp.transpose(wq, (1, 0, 2)).astype(BF16)
    wkv = jnp.transpose(w_ukv.reshape(MLA_KVR, H, dn + dv), (1, 0, 2)).astype(BF16)
    tab = jnp.concatenate([cos, cos, sin, sin], axis=-1)
    ts = min(512, S)
    nw = win_p.shape[1]
    hd = dn + 2 * dr
    q, k, v = pl.pallas_call(
        functools.partial(_mla_front_kernel, scale=(dn + dr) ** -0.5),
        grid=(B, S // ts),
        in_specs=[pl.BlockSpec((1, ts, D), lambda b, i: (b, i, 0)),
                  pl.BlockSpec((1, 1, D), lambda b, i: (b, 0, 0)),
                  pl.BlockSpec((1, 1, D), lambda b, i: (b, 0, 0)),
                  pl.BlockSpec((D, nw), lambda b, i: (0, 0)),
                  pl.BlockSpec((1, MLA_QR), lambda b, i: (0, 0)),
                  pl.BlockSpec((1, MLA_KVR), lambda b, i: (0, 0)),
                  pl.BlockSpec((H, MLA_QR, hd), lambda b, i: (0, 0, 0)),
                  pl.BlockSpec((H, MLA_KVR, dn + dv), lambda b, i: (0, 0, 0)),
                  pl.BlockSpec((1, ts, LANES), lambda b, i: (b, i, 0))],
        out_specs=[pl.BlockSpec((1, H, ts, hd), lambda b, i: (b, 0, i, 0)),
                   pl.BlockSpec((1, H, ts, hd), lambda b, i: (b, 0, i, 0)),
                   pl.BlockSpec((1, H, ts, dv), lambda b, i: (b, 0, i, 0))],
        out_shape=[jax.ShapeDtypeStruct((B, H, S, hd), BF16),
                   jax.ShapeDtypeStruct((B, H, S, hd), BF16),
                   jax.ShapeDtypeStruct((B, H, S, dv), BF16)],
        compiler_params=_cp(("arbitrary", "arbitrary"), 48),
        name="mla_front",
    )(x, scale_v, shift, win_p, g_q.reshape(1, -1), g_kv.reshape(1, -1), wq, wkv, tab)

    tq = min(512, S)
    tk = min(512, S)
    o = pl.pallas_call(
        functools.partial(_flash_causal_kernel, tq=tq, tk=tk),
        grid=(B, H, S // tq),
        in_specs=[pl.BlockSpec((1, 1, tq, hd), lambda b, h, i: (b, h, i, 0)),
                  pl.BlockSpec((1, 1, S, hd), lambda b, h, i: (b, h, 0, 0)),
                  pl.BlockSpec((1, 1, S, dv), lambda b, h, i: (b, h, 0, 0))],
        out_specs=pl.BlockSpec((1, tq, dv), lambda b, h, i: (b, i, h)),
        out_shape=jax.ShapeDtypeStruct((B, S, H * dv), BF16),
        scratch_shapes=[pltpu.VMEM((tq, 1), F32), pltpu.VMEM((tq, 1), F32), pltpu.VMEM((tq, dv), F32)],
        compiler_params=_cp(("arbitrary", "arbitrary", "arbitrary"), 32),
        name="mla_attn",
    )(q, k, v)
    return _out_ln(o, w_o, x, gate, ln_g, ln_b)


def _proj_kernel(*refs, segs, chunk, has_rope):
    x_ref, sc_ref, sh_ref, w_ref = refs[:4]
    n_in = 6 if has_rope else 4
    outs = refs[n_in:]
    hb = (x_ref[0] * (1.0 + sc_ref[0]) + sh_ref[0]).astype(BF16)
    if has_rope:
        cos = refs[4][0]
        sin = refs[5][0]
    for (c0, width, kind, scale), o_ref in zip(segs, outs):
        for j in range(0, width, chunk):
            r = _dot(hb, w_ref[:, c0 + j:c0 + j + chunk])
            if kind == "rope":
                parts = []
                for t in range(0, chunk, LANES):
                    u = r[:, t:t + LANES]
                    parts.append(u * cos + pltpu.roll(u, LANES // 2, axis=1) * sin)
                r = jnp.concatenate(parts, axis=1) if len(parts) > 1 else parts[0]
            elif kind == "sigmoid":
                r = jax.nn.sigmoid(r)
            if scale != 1.0:
                r = r * scale
            o_ref[0, :, j:j + chunk] = r.astype(o_ref.dtype)


def _mod_proj(x, shift, scale_v, w, segs, out_dtypes, rope=None, ts=512, chunk=256):
    B, S, D = x.shape
    ts = min(ts, S)
    nw = w.shape[1]
    in_specs = [pl.BlockSpec((1, ts, D), lambda b, i: (b, i, 0)),
                pl.BlockSpec((1, 1, D), lambda b, i: (b, 0, 0)),
                pl.BlockSpec((1, 1, D), lambda b, i: (b, 0, 0)),
                pl.BlockSpec((D, nw), lambda b, i: (0, 0))]
    args = [x, scale_v, shift, w]
    if rope is not None:
        in_specs += [pl.BlockSpec((1, ts, LANES), lambda b, i: (b, i, 0))] * 2
        args += list(rope)
    return pl.pallas_call(
        functools.partial(_proj_kernel, segs=tuple(segs), chunk=chunk, has_rope=rope is not None),
        grid=(B, S // ts),
        in_specs=in_specs,
        out_specs=[pl.BlockSpec((1, ts, sg[1]), lambda b, i: (b, i, 0)) for sg in segs],
        out_shape=[jax.ShapeDtypeStruct((B, S, sg[1]), dt) for sg, dt in zip(segs, out_dtypes)],
        compiler_params=_cp(("arbitrary", "arbitrary"), 48),
        name="mod_proj",
    )(*args)


def _nsa_cmp_kernel(kc_ref, vc_ref, pek_ref, pev_ref, wk1_ref, wk2_ref, wv1_ref, wv2_ref, ko_ref, vo_ref):
    half = NSA_CMP_BLOCK // 2
    nchunk = kc_ref.shape[1]
    rowi = lax.broadcasted_iota(jnp.int32, (nchunk, 1), 0)
    for src, pe_ref, w1_ref, w2_ref, dst in ((kc_ref, pek_ref, wk1_ref, wk2_ref, ko_ref),
                                             (vc_ref, pev_ref, wv1_ref, wv2_ref, vo_ref)):
        for kh in range(NSA_KVH):
            a = jnp.zeros((nchunk, w1_ref.shape[2]), F32)
            bm = jnp.zeros((nchunk, w1_ref.shape[2]), F32)
            for l in range(half):
                c0 = (l * NSA_KVH + kh) * NSA_DH
                t = src[0, :, c0:c0 + NSA_DH].astype(F32)
                a = a + _dot((t + pe_ref[l:l + 1, :]).astype(BF16), w1_ref[l])
                bm = bm + _dot((t + pe_ref[half + l:half + l + 1, :]).astype(BF16), w1_ref[half + l])
            hid = a + pltpu.roll(bm, nchunk - 1, axis=0)
            hid = hid * jax.nn.sigmoid(hid)
            out = _dot(hid.astype(BF16), w2_ref[...])
            dst[0, kh] = jnp.where(rowi < nchunk - 1, out, 0.0).astype(dst.dtype)


def _nsa_attn_kernel(q_ref, kc_ref, vc_ref, ks_ref, vs_ref, kw_ref, vw_ref, gt_ref, ov_ref, o_ref,
                     q4_sc, m_sc, l_sc, acc_sc, *, tq, tk, span, n_slc):
    qi = pl.program_id(2)
    G = NSA_G
    for g in range(G):
        q4_sc[g * tq:(g + 1) * tq, :] = q_ref[0, :, g * NSA_DH:(g + 1) * NSA_DH]
    q4 = q4_sc[...]
    pos = qi * tq + lax.broadcasted_iota(jnp.int32, (tq, 1), 0)
    pos4 = jnp.concatenate([pos] * G, axis=0)

    def softmax_rows(s, mask):
        s = jnp.where(mask, s, NEG)
        m = jnp.max(s, axis=1, keepdims=True)
        p = jnp.where(mask, jnp.exp(s - m), 0.0)
        l = jnp.sum(p, axis=1, keepdims=True)
        return p / jnp.where(l > 0.0, l, 1.0)

    n_cmp = kc_ref.shape[2]
    nid = lax.broadcasted_iota(jnp.int32, (1, n_cmp), 1)
    cmask = (nid * NSA_CMP_STRIDE + (NSA_CMP_BLOCK - 1) <= pos4) & (nid < n_cmp - 1)
    p_cmp = softmax_rows(_nt_dot(q4, kc_ref[0, 0]), cmask)
    o_cmp = _dot(p_cmp.astype(BF16), vc_ref[0, 0])

    psum = p_cmp[0:tq]
    for g in range(1, G):
        psum = psum + p_cmp[g * tq:(g + 1) * tq]
    p_hi, p_lo = _split_bf16(psum)
    imp = _dot(p_hi, ov_ref[...]) + _dot(p_lo, ov_ref[...])
    blk = lax.broadcasted_iota(jnp.int32, (1, LANES), 1)
    blk_f = blk.astype(F32)
    cur = jnp.right_shift(pos, SLC_SHIFT)
    forced = (blk == 0) | (blk == cur) | (blk == cur - 1)
    valid = (blk * NSA_SLC_BLOCK <= pos) & (blk < n_slc)
    work = jnp.where(valid, imp + jnp.where(forced, NSA_BONUS, 0.0), -jnp.inf)
    selbits = jnp.zeros((tq, 1), jnp.int32)
    for _ in range(min(NSA_TOPK, n_slc)):
        mx = jnp.max(work, axis=1, keepdims=True)
        idx = jnp.min(jnp.where(work == mx, blk_f, float(LANES)), axis=1, keepdims=True)
        work = jnp.where(blk_f == idx, -jnp.inf, work)
        selbits = selbits | jnp.left_shift(jnp.ones_like(selbits), idx.astype(jnp.int32))
    selbits4 = jnp.concatenate([selbits] * G, axis=0)

    m_sc[...] = jnp.full(m_sc.shape, NEG, F32)
    l_sc[...] = jnp.zeros(l_sc.shape, F32)
    acc_sc[...] = jnp.zeros(acc_sc.shape, F32)
    n_kt = ((qi + 1) * tq + tk - 1) // tk

    def body(kt, carry):
        ks = pl.multiple_of(kt * tk, tk)
        s = _nt_dot(q4, ks_ref[0, pl.ds(ks, tk), :])
        col = ks + lax.broadcasted_iota(jnp.int32, (1, tk), 1)
        bit = jnp.right_shift(selbits4, jnp.right_shift(col, SLC_SHIFT)) & 1
        mask = (bit == 1) & (col <= pos4)
        s = jnp.where(mask, s, NEG)
        m_prev = m_sc[...]
        m_new = jnp.maximum(m_prev, jnp.max(s, axis=1, keepdims=True))
        p = jnp.where(mask, jnp.exp(s - m_new), 0.0)
        a = jnp.exp(m_prev - m_new)
        l_sc[...] = a * l_sc[...] + jnp.sum(p, axis=1, keepdims=True)
        acc_sc[...] = a * acc_sc[...] + _dot(p.astype(BF16), vs_ref[0, pl.ds(ks, tk), :])
        m_sc[...] = m_new
        return carry

    lax.fori_loop(0, n_kt, body, 0)
    l = l_sc[...]
    o_slc = acc_sc[...] / jnp.where(l > 0.0, l, 1.0)

    kstart = pl.multiple_of(jnp.maximum(qi * tq + tq - span, 0), tq)
    colw = kstart + lax.broadcasted_iota(jnp.int32, (1, span), 1)
    diff = pos4 - colw
    band = (diff >= 0) & (diff < NSA_WINDOW)
    p_win = softmax_rows(_nt_dot(q4, kw_ref[0, pl.ds(kstart, span), :]), band)
    o_win = _dot(p_win.astype(BF16), vw_ref[0, pl.ds(kstart, span), :])

    gt = gt_ref[0]
    for g in range(G):
        rows = slice(g * tq, (g + 1) * tq)
        o = (gt[:, 3 * g:3 * g + 1] * o_cmp[rows] + gt[:, 3 * g + 1:3 * g + 2] * o_slc[rows]
             + gt[:, 3 * g + 2:3 * g + 3] * o_win[rows])
        o_ref[0, :, g * NSA_DH:(g + 1) * NSA_DH] = o.astype(o_ref.dtype)


def _nsa_layer(x, shift, scale_v, gate, cos, sin, w_in, pe_k, pe_v, w_ck1, w_ck2, w_cv1, w_cv2, w_o,
               ln_g, ln_b):
    B, S, D = x.shape
    H, KH, Dh, G = NSA_HEADS, NSA_KVH, NSA_DH, NSA_G
    kvw = KH * Dh
    cuts = [int(v) for v in np.cumsum([H * Dh] + [kvw] * 6)]
    wq, wkc, wvc, wks, wvs, wkw, wvw, wgl = jnp.split(w_in, cuts, axis=1)
    wgl = jnp.pad(wgl.reshape(D, KH, G * 3), ((0, 0), (0, 0), (0, LANES - G * 3))).reshape(D, KH * LANES)
    w = jnp.concatenate([wq, wkc, wks, wkw, wvc, wvs, wvw, wgl], axis=1).astype(BF16)
    segs, c0 = [], 0
    for width, kind, sc in ((H * Dh, "rope", Dh ** -0.5), (kvw, "rope", 1.0), (kvw, "rope", 1.0),
                            (kvw, "rope", 1.0), (kvw, "plain", 1.0), (kvw, "plain", 1.0),
                            (kvw, "plain", 1.0), (KH * LANES, "sigmoid", 1.0)):
        segs.append((c0, width, kind, sc))
        c0 += width
    rope = (jnp.concatenate([cos, cos], axis=-1), jnp.concatenate([-sin, sin], axis=-1))
    q, kc, ks, kw, vc, vs, vw, gates = _mod_proj(
        x, shift, scale_v, w, segs, [BF16] * 7 + [F32], rope=rope)

    st = NSA_CMP_STRIDE
    nchunk = S // st
    kc2 = kc.reshape(B, nchunk, st * kvw)
    vc2 = vc.reshape(B, nchunk, st * kvw)
    hid = w_ck1.shape[-1]
    full = lambda shape: pl.BlockSpec(shape, lambda b: (0,) * len(shape))
    k_cmp, v_cmp = pl.pallas_call(
        _nsa_cmp_kernel,
        grid=(B,),
        in_specs=[pl.BlockSpec((1, nchunk, st * kvw), lambda b: (b, 0, 0)),
                  pl.BlockSpec((1, nchunk, st * kvw), lambda b: (b, 0, 0)),
                  full((NSA_CMP_BLOCK, Dh)), full((NSA_CMP_BLOCK, Dh)),
                  full((NSA_CMP_BLOCK, Dh, hid)), full((hid, Dh)),
                  full((NSA_CMP_BLOCK, Dh, hid)), full((hid, Dh))],
        out_specs=[pl.BlockSpec((1, KH, nchunk, Dh), lambda b: (b, 0, 0, 0))] * 2,
        out_shape=[jax.ShapeDtypeStruct((B, KH, nchunk, Dh), BF16)] * 2,
        compiler_params=_cp(("arbitrary",), 32),
        name="nsa_cmp",
    )(kc2, vc2, pe_k, pe_v, w_ck1.astype(BF16), w_ck2.astype(BF16), w_cv1.astype(BF16), w_cv2.astype(BF16))

    n_slc = S // NSA_SLC_BLOCK
    cmp_start = np.arange(nchunk) * st
    slc_start = np.arange(n_slc) * NSA_SLC_BLOCK
    ov = np.clip(np.minimum(cmp_start[:, None] + NSA_CMP_BLOCK, slc_start[None, :] + NSA_SLC_BLOCK)
                 - np.maximum(cmp_start[:, None], slc_start[None, :]), 0, None).astype(np.float32) / NSA_CMP_BLOCK
    ov[nchunk - 1:] = 0.0
    ov_p = np.zeros((nchunk, LANES), np.float32)
    ov_p[:, :n_slc] = ov
    assert n_slc <= 32 and nchunk % 8 == 0

    tq = min(128, S)
    tk = min(256, S)
    span = min(NSA_WINDOW + tq, S)
    kv_spec = pl.BlockSpec((1, S, Dh), lambda b, kh, i: (b, 0, kh))
    cmp_spec = pl.BlockSpec((1, 1, nchunk, Dh), lambda b, kh, i: (b, kh, 0, 0))
    o = pl.pallas_call(
        functools.partial(_nsa_attn_kernel, tq=tq, tk=tk, span=span, n_slc=n_slc),
        grid=(B, KH, S // tq),
        in_specs=[pl.BlockSpec((1, tq, G * Dh), lambda b, kh, i: (b, i, kh)),
                  cmp_spec, cmp_spec, kv_spec, kv_spec, kv_spec, kv_spec,
                  pl.BlockSpec((1, tq, LANES), lambda b, kh, i: (b, i, kh)),
                  pl.BlockSpec((nchunk, LANES), lambda b, kh, i: (0, 0))],
        out_specs=pl.BlockSpec((1, tq, G * Dh), lambda b, kh, i: (b, i, kh)),
        out_shape=jax.ShapeDtypeStruct((B, S, H * Dh), BF16),
        scratch_shapes=[pltpu.VMEM((G * tq, Dh), BF16), pltpu.VMEM((G * tq, 1), F32),
                        pltpu.VMEM((G * tq, 1), F32), pltpu.VMEM((G * tq, Dh), F32)],
        compiler_params=_cp(("arbitrary", "arbitrary", "arbitrary"), 32),
        name="nsa_attn",
    )(q, k_cmp, v_cmp, ks, vs, kw, vw, gates, jnp.asarray(ov_p, BF16))
    return _out_ln(o, w_o, x, gate, ln_g, ln_b)


def _sb_attn_kernel(q_ref, k_ref, v_ref, u_ref, o_ref, r_sc, acc_sc, *, tq, tk):
    qi = pl.program_id(2)
    q = q_ref[0]
    r_sc[...] = jnp.zeros(r_sc.shape, F32)
    acc_sc[...] = jnp.zeros(acc_sc.shape, F32)
    row = qi * tq + lax.broadcasted_iota(jnp.int32, (tq, 1), 0)
    n_kt = ((qi + 1) * tq + tk - 1) // tk
    u = u_ref[...]

    def body(i, carry):
        ks = pl.multiple_of((n_kt - 1 - i) * tk, tk)
        z = _nt_dot(q, k_ref[0, pl.ds(ks, tk), :])
        col = ks + lax.broadcasted_iota(jnp.int32, (1, tk), 1)
        strict = col < row
        lk = jnp.where(strict, -(jnp.maximum(z, 0.0) + jnp.log(1.0 + jnp.exp(-jnp.abs(z)))), 0.0)
        hi, lo = _split_bf16(lk)
        between = r_sc[...] + _dot(hi, u) + _dot(lo, u)
        a = jnp.where(strict, jnp.exp(lk + z + between), 0.0)
        acc_sc[...] += _dot(a.astype(BF16), v_ref[0, pl.ds(ks, tk), :])
        r_sc[...] += jnp.sum(lk, axis=1, keepdims=True)
        return carry

    lax.fori_loop(0, n_kt, body, 0)
    o_ref[0] = acc_sc[...].astype(o_ref.dtype)


def _sb_layer(x, shift, scale_v, gate, w_qkv, w_o, ln_g, ln_b):
    B, S, D = x.shape
    H, d = SB_HEADS, SB_DH
    hw = H * d
    segs = [(0, hw, "plain", d ** -0.5), (hw, hw, "plain", 1.0), (2 * hw, hw, "plain", 1.0)]
    q, k, v = _mod_proj(x, shift, scale_v, w_qkv.astype(BF16), segs, [BF16] * 3)
    tq = min(256, S)
    tk = min(256, S)
    tri = np.tril(np.ones((tk, tk), np.float32), -1)
    o = pl.pallas_call(
        functools.partial(_sb_attn_kernel, tq=tq, tk=tk),
        grid=(B, H, S // tq),
        in_specs=[pl.BlockSpec((1, tq, d), lambda b, h, i: (b, i, h)),
                  pl.BlockSpec((1, S, d), lambda b, h, i: (b, 0, h)),
                  pl.BlockSpec((1, S, d), lambda b, h, i: (b, 0, h)),
                  pl.BlockSpec((tk, tk), lambda b, h, i: (0, 0))],
        out_specs=pl.BlockSpec((1, tq, d), lambda b, h, i: (b, i, h)),
        out_shape=jax.ShapeDtypeStruct((B, S, hw), BF16),
        scratch_shapes=[pltpu.VMEM((tq, 1), F32), pltpu.VMEM((tq, d), F32)],
        compiler_params=_cp(("arbitrary", "arbitrary", "arbitrary"), 32),
        name="sb_attn",
    )(q, k, v, jnp.asarray(tri, BF16))
    return _out_ln(o, w_o, x, gate, ln_g, ln_b)


def _router_kernel(x_ref, sc_ref, sh_ref, whi_ref, wlo_ref, b_ref, tri_ref, hb_ref, info_ref, cnt_ref,
                   base_sc):
    first = (pl.program_id(0) == 0) & (pl.program_id(1) == 0)

    @pl.when(first)
    def _():
        base_sc[...] = jnp.zeros(base_sc.shape, F32)

    h = x_ref[0] * (1.0 + sc_ref[0]) + sh_ref[0]
    h_hi, h_lo = _split_bf16(h)
    hb_ref[0] = h_hi
    logit = (_dot(h_hi, whi_ref[...]) + _dot(h_lo, whi_ref[...]) + _dot(h_hi, wlo_ref[...])
             + b_ref[...])
    tm = logit.shape[0]
    lane = lax.broadcasted_iota(jnp.int32, (1, LANES), 1)
    lane_f = lane.astype(F32)
    big = float(LANES)

    def first_max(vals):
        mx = jnp.max(vals, axis=1, keepdims=True)
        idx = jnp.min(jnp.where(vals == mx, lane_f, big), axis=1, keepdims=True)
        return mx, idx

    gl = jnp.where(lane < MOE_GROUPS, logit, -jnp.inf)
    gmax, g_idx = first_max(gl)
    g_w = 1.0 / jnp.sum(jnp.exp(gl - gmax), axis=1, keepdims=True)
    e_lane = lane - MOE_GROUPS
    in_grp = (e_lane >= 0) & (e_lane < MOE_E) & (jnp.right_shift(e_lane, EPG_SHIFT).astype(F32) == g_idx)
    el = jnp.where(in_grp, logit, -jnp.inf)
    emax = jnp.max(el, axis=1, keepdims=True)
    ep = jnp.exp(el - emax)
    prob = jnp.where(in_grp, ep / jnp.sum(ep, axis=1, keepdims=True), -1.0)
    p1, i1 = first_max(prob)
    p2, i2 = first_max(jnp.where(lane_f == i1, -1.0, prob))
    e1 = i1 - float(MOE_GROUPS)
    e2 = i2 - float(MOE_GROUPS)
    w1 = g_w * p1 / (p1 + p2)
    w2 = g_w * p2 / (p1 + p2)

    hot1 = lane_f == e1
    hot2 = lane_f == e2
    cnt = jnp.where(hot1 | hot2, 1.0, 0.0)
    before = base_sc[...] + _dot(tri_ref[...], cnt.astype(BF16))
    r1 = jnp.sum(jnp.where(hot1, before, 0.0), axis=1, keepdims=True)
    r2 = jnp.sum(jnp.where(hot2, before, 0.0), axis=1, keepdims=True)
    base_sc[...] += jnp.sum(cnt, axis=0, keepdims=True)
    cnt_ref[...] = base_sc[...]

    info = jnp.zeros((tm, LANES), F32)
    for k, val in enumerate((e1, e2, r1, r2, w1, w2)):
        info = jnp.where(lane == k, val, info)
    info_ref[0] = info


def _ffn_kernel(be_ref, nu_ref, x_ref, wg_ref, wu_ref, wd_ref, y_ref, wg_sc, wu_sc, wd_sc):
    i = pl.program_id(0)
    new_expert = (i == 0) | (be_ref[i] != be_ref[jnp.maximum(i - 1, 0)])

    @pl.when(new_expert & (i < nu_ref[0]))
    def _():
        wg_sc[...] = wg_ref[0].astype(BF16)
        wu_sc[...] = wu_ref[0].astype(BF16)
        wd_sc[...] = wd_ref[0].astype(BF16)

    @pl.when(i < nu_ref[0])
    def _():
        xb = x_ref[...]
        g = _dot(xb, wg_sc[...])
        u = _dot(xb, wu_sc[...])
        a = (g * jax.nn.sigmoid(g) * u).astype(BF16)
        y_ref[...] = _dot(a, wd_sc[...])

    @pl.when(i >= nu_ref[0])
    def _():
        y_ref[...] = jnp.zeros(y_ref.shape, y_ref.dtype)


def _moe_layer(x, shift, scale_v, gate, w_grp, b_grp, w_rt, b_rt, w_gate, w_up, w_down, ln_g, ln_b):
    B, S, D = x.shape
    T = B * S
    E, R, F = MOE_E, MOE_ROWS, w_gate.shape[-1]
    w_r = jnp.pad(jnp.concatenate([w_grp, w_rt], axis=1), ((0, 0), (0, LANES - MOE_GROUPS - E)))
    w_hi = w_r.astype(BF16)
    w_lo = (w_r - w_hi.astype(F32)).astype(BF16)
    b_r = jnp.pad(jnp.concatenate([b_grp, b_rt]), (0, LANES - MOE_GROUPS - E)).reshape(1, LANES)
    tm = min(256, S)
    tri = jnp.asarray(np.tril(np.ones((tm, tm), np.float32), -1), BF16)
    hb, info, cnt = pl.pallas_call(
        _router_kernel,
        grid=(B, S // tm),
        in_specs=[pl.BlockSpec((1, tm, D), lambda b, i: (b, i, 0)),
                  pl.BlockSpec((1, 1, D), lambda b, i: (b, 0, 0)),
                  pl.BlockSpec((1, 1, D), lambda b, i: (b, 0, 0)),
                  pl.BlockSpec((D, LANES), lambda b, i: (0, 0)),
                  pl.BlockSpec((D, LANES), lambda b, i: (0, 0)),
                  pl.BlockSpec((1, LANES), lambda b, i: (0, 0)),
                  pl.BlockSpec((tm, tm), lambda b, i: (0, 0))],
        out_specs=[pl.BlockSpec((1, tm, D), lambda b, i: (b, i, 0)),
                   pl.BlockSpec((1, tm, LANES), lambda b, i: (b, i, 0)),
                   pl.BlockSpec((1, LANES), lambda b, i: (0, 0))],
        out_shape=[jax.ShapeDtypeStruct((B, S, D), BF16),
                   jax.ShapeDtypeStruct((B, S, LANES), F32),
                   jax.ShapeDtypeStruct((1, LANES), F32)],
        scratch_shapes=[pltpu.VMEM((1, LANES), F32)],
        compiler_params=_cp(("arbitrary", "arbitrary"), 32),
        name="moe_router",
    )(x, scale_v, shift, w_hi, w_lo, b_r, tri)

    info = info.reshape(T, LANES)
    experts = info[:, 0:2].astype(jnp.int32)
    ranks = info[:, 2:4].astype(jnp.int32)
    weights = info[:, 4:6]
    counts = cnt[0, :E].astype(jnp.int32)
    padded = (counts + R - 1) // R * R
    ends = jnp.cumsum(padded)
    starts = ends - padded
    dest = starts[experts] + ranks
    n_blk = (T * 2) // R + E
    P = n_blk * R
    row_tok = jnp.zeros((P,), jnp.int32).at[dest.reshape(-1)].set(
        jnp.repeat(jnp.arange(T, dtype=jnp.int32), 2))
    blk_exp = jnp.minimum(jnp.sum(jnp.arange(n_blk)[:, None] * R >= ends[None, :], axis=1), E - 1).astype(jnp.int32)
    n_used = (ends[-1] // R).astype(jnp.int32).reshape(1)
    x_rows = hb.reshape(T, D)[row_tok]

    y_rows = pl.pallas_call(
        _ffn_kernel,
        grid_spec=pltpu.PrefetchScalarGridSpec(
            num_scalar_prefetch=2,
            grid=(n_blk,),
            in_specs=[pl.BlockSpec((R, D), lambda i, be, nu: (i, 0)),
                      pl.BlockSpec((1, D, F), lambda i, be, nu: (be[i], 0, 0)),
                      pl.BlockSpec((1, D, F), lambda i, be, nu: (be[i], 0, 0)),
                      pl.BlockSpec((1, F, D), lambda i, be, nu: (be[i], 0, 0))],
            out_specs=pl.BlockSpec((R, D), lambda i, be, nu: (i, 0)),
            scratch_shapes=[pltpu.VMEM((D, F), BF16), pltpu.VMEM((D, F), BF16), pltpu.VMEM((F, D), BF16)]),
        out_shape=jax.ShapeDtypeStruct((P, D), F32),
        compiler_params=_cp(("arbitrary",), 48),
        name="moe_ffn",
    )(blk_exp, n_used, x_rows, w_gate, w_up, w_down)

    y = (weights[:, 0:1] * y_rows[dest[:, 0]] + weights[:, 1:2] * y_rows[dest[:, 1]]).reshape(B, S, D)
    return _res_ln(x, y, gate, ln_g, ln_b)


def kernel(x, c, positions, ada_w, ada_b, ln_g, ln_b, mla_w_in, mla_g_q, mla_w_uq, mla_g_kv, mla_w_ukv, mla_w_o, nsa_w_in, nsa_pe_k, nsa_pe_v, nsa_w_ck1, nsa_w_ck2, nsa_w_cv1, nsa_w_cv2, nsa_w_o, sb_w_qkv, sb_w_o, moe_w_grp, moe_b_grp, moe_w_rt, moe_b_rt, moe_w_gate, moe_w_up, moe_w_down):
    D = x.shape[-1]
    depth = ada_w.shape[0]
    cos_mla, sin_mla = _rope_tables(positions, MLA_DR)
    cos_nsa, sin_nsa = _rope_tables(positions, NSA_DH)
    mods = _ada_all(c, ada_w, ada_b)

    def mod(i, sub):
        m = mods[2 * i + sub][:, None, :]
        return m[..., :D], m[..., D:2 * D], m[..., 2 * D:]

    for i in range(depth):
        kind, j = i % N_MIXERS, i // N_MIXERS
        shift, scale_v, gate = mod(i, 0)
        if kind == 0:
            x = _mla_layer(x, shift, scale_v, gate, cos_mla, sin_mla, mla_w_in[j], mla_g_q[j], mla_w_uq[j],
                           mla_g_kv[j], mla_w_ukv[j], mla_w_o[j], ln_g[i, 0], ln_b[i, 0])
        elif kind == 1:
            x = _nsa_layer(x, shift, scale_v, gate, cos_nsa, sin_nsa, nsa_w_in[j], nsa_pe_k[j], nsa_pe_v[j],
                           nsa_w_ck1[j], nsa_w_ck2[j], nsa_w_cv1[j], nsa_w_cv2[j], nsa_w_o[j],
                           ln_g[i, 0], ln_b[i, 0])
        else:
            x = _sb_layer(x, shift, scale_v, gate, sb_w_qkv[j], sb_w_o[j], ln_g[i, 0], ln_b[i, 0])
        shift, scale_v, gate = mod(i, 1)
        x = _moe_layer(x, shift, scale_v, gate, moe_w_grp[i], moe_b_grp[i], moe_w_rt[i], moe_b_rt[i],
                       moe_w_gate[i], moe_w_up[i], moe_w_down[i], ln_g[i, 1], ln_b[i, 1])
    return x
```

```python
import functools

import numpy as np
import jax
import jax.numpy as jnp
from jax import lax
from jax.experimental import pallas as pl
from jax.experimental.pallas import tpu as pltpu

BF16 = jnp.bfloat16
F32 = jnp.float32

DEPTH = 4
N_MIXERS = 3
ROPE_THETA = 10000.0
LN_EPS = 1e-5
RMS_EPS = 1e-6
NEG = -1e30
LOG2E = 1.4426950408889634
ALPHA = (2 * DEPTH) ** 0.25

MLA_HEADS, MLA_DN, MLA_DR, MLA_DV = 8, 128, 64, 128
MLA_QR, MLA_KVR = 256, 256

NSA_HEADS, NSA_KVH, NSA_DH = 8, 2, 128
NSA_G = NSA_HEADS // NSA_KVH
NSA_CMP_BLOCK, NSA_CMP_STRIDE = 32, 16
NSA_SLC_BLOCK, NSA_TOPK = 64, 8
SLC_SHIFT = 6
NSA_WINDOW = 512
NSA_BONUS = 1e3
GATE_ROWS = 16

SB_HEADS, SB_DH = 8, 128

MOE_GROUPS, MOE_EPG = 4, 8
EPG_SHIFT = 3
MOE_E = MOE_GROUPS * MOE_EPG
MOE_ROWS = 256

LANES = 128
VMEM_BYTES_V7X = 64 * 1024 * 1024


def _cp(sem, vmem_mb):
    assert vmem_mb * 2**20 < VMEM_BYTES_V7X
    return pltpu.CompilerParams(dimension_semantics=sem, vmem_limit_bytes=vmem_mb * 2**20)


def _nt_dot(a, b):
    return lax.dot_general(a, b, (((1,), (1,)), ((), ())), preferred_element_type=F32)


def _dot(a, b):
    return jnp.dot(a, b, preferred_element_type=F32)


def _split_bf16(a):
    hi = a.astype(BF16)
    lo = (a - hi.astype(F32)).astype(BF16)
    return hi, lo


def _layer_norm(z, g, b):
    mu = jnp.mean(z, axis=-1, keepdims=True)
    d = z - mu
    var = jnp.mean(d * d, axis=-1, keepdims=True)
    return d * lax.rsqrt(var + LN_EPS) * g + b


def _ada_kernel(c_ref, w_ref, b_ref, o_ref):
    c = c_ref[...]
    s_hi, s_lo = _split_bf16(c * jax.nn.sigmoid(c))
    w_hi, w_lo = _split_bf16(w_ref[0])
    o_ref[0] = _dot(s_hi, w_hi) + _dot(s_lo, w_hi) + _dot(s_hi, w_lo) + b_ref[0]


def _ada_all(c, ada_w, ada_b):
    L, two, D, D3 = ada_w.shape
    B = c.shape[0]
    n = L * two
    tn = 1024
    return pl.pallas_call(
        _ada_kernel,
        grid=(n, D3 // tn),
        in_specs=[pl.BlockSpec((B, D), lambda i, j: (0, 0)),
                  pl.BlockSpec((1, D, tn), lambda i, j: (i, 0, j)),
                  pl.BlockSpec((1, 1, tn), lambda i, j: (i, 0, j))],
        out_specs=pl.BlockSpec((1, B, tn), lambda i, j: (i, 0, j)),
        out_shape=jax.ShapeDtypeStruct((n, B, D3), F32),
        compiler_params=_cp(("arbitrary", "arbitrary"), 32),
        name="ada",
    )(c, ada_w.reshape(n, D, D3), ada_b.reshape(n, 1, D3))


def _res_ln_kernel(x_ref, y_ref, gate_ref, g_ref, b_ref, o_ref):
    z = ALPHA * x_ref[0] + gate_ref[0] * y_ref[0]
    o_ref[0] = _layer_norm(z, g_ref[...], b_ref[...])


def _res_ln(x, y, gate, g, b, ts=512):
    B, S, D = x.shape
    ts = min(ts, S)
    tok = pl.BlockSpec((1, ts, D), lambda bi, i: (bi, i, 0))
    return pl.pallas_call(
        _res_ln_kernel,
        grid=(B, S // ts),
        in_specs=[tok, tok,
                  pl.BlockSpec((1, 1, D), lambda bi, i: (bi, 0, 0)),
                  pl.BlockSpec((1, D), lambda bi, i: (0, 0)),
                  pl.BlockSpec((1, D), lambda bi, i: (0, 0))],
        out_specs=tok,
        out_shape=jax.ShapeDtypeStruct((B, S, D), F32),
        compiler_params=_cp(("arbitrary", "arbitrary"), 32),
        name="res_ln",
    )(x, y, gate, g.reshape(1, D), b.reshape(1, D))


def _out_ln_kernel(o_ref, w_ref, x_ref, gate_ref, g_ref, b_ref, out_ref):
    y = _dot(o_ref[0], w_ref[...])
    z = ALPHA * x_ref[0] + gate_ref[0] * y
    out_ref[0] = _layer_norm(z, g_ref[...], b_ref[...])


def _out_ln(o, w_o, x, gate, g, b, ts=512):
    B, S, D = x.shape
    K = o.shape[-1]
    ts = min(ts, S)
    return pl.pallas_call(
        _out_ln_kernel,
        grid=(B, S // ts),
        in_specs=[pl.BlockSpec((1, ts, K), lambda bi, i: (bi, i, 0)),
                  pl.BlockSpec((K, D), lambda bi, i: (0, 0)),
                  pl.BlockSpec((1, ts, D), lambda bi, i: (bi, i, 0)),
                  pl.BlockSpec((1, 1, D), lambda bi, i: (bi, 0, 0)),
                  pl.BlockSpec((1, D), lambda bi, i: (0, 0)),
                  pl.BlockSpec((1, D), lambda bi, i: (0, 0))],
        out_specs=pl.BlockSpec((1, ts, D), lambda bi, i: (bi, i, 0)),
        out_shape=jax.ShapeDtypeStruct((B, S, D), F32),
        compiler_params=_cp(("arbitrary", "arbitrary"), 40),
        name="out_ln",
    )(o, w_o.astype(BF16), x, gate, g.reshape(1, D), b.reshape(1, D))


def _rope_tables(positions, dim):
    inv_freq = 1.0 / (ROPE_THETA ** (jnp.arange(0, dim, 2, dtype=F32) / dim))
    ang = positions.astype(F32)[..., None] * inv_freq
    return jnp.cos(ang), jnp.sin(ang)


def _rot_half_cols(w, half):
    return jnp.concatenate([-w[..., half:], w[..., :half]], axis=-1)


def _mla_front_kernel(x_ref, sc_ref, sh_ref, win_ref, gq_ref, gkv_ref, wqt_ref, wk_ref, wvt_ref, t_ref, tt_ref,
                      qt_ref, k_ref, vt_ref, *, scale):
    hb = (x_ref[0] * (1.0 + sc_ref[0]) + sh_ref[0]).astype(BF16)
    r = _dot(hb, win_ref[...])
    tab = t_ref[0]
    tab_t = tt_ref[0]
    low = lax.broadcasted_iota(jnp.int32, tab.shape, 1) < MLA_DR

    def rms(cx, g):
        return (cx * lax.rsqrt(jnp.mean(cx * cx, axis=-1, keepdims=True) + RMS_EPS) * g).astype(BF16)

    cq = rms(r[:, :MLA_QR], gq_ref[...])
    ckv = rms(r[:, MLA_QR:MLA_QR + MLA_KVR], gkv_ref[...])
    t = r[:, MLA_QR + MLA_KVR:] * tab
    k_rope = jnp.where(low, t + pltpu.roll(t, MLA_DR, axis=1), 0.0).astype(BF16)
    zeros = jnp.zeros((MLA_DR, hb.shape[0]), BF16)
    for h in range(MLA_HEADS):
        qh = _nt_dot(wqt_ref[h], cq)
        tq = qh[MLA_DN:] * tab_t
        qt_ref[0, h, :MLA_DN, :] = (qh[:MLA_DN] * scale).astype(BF16)
        qt_ref[0, h, MLA_DN:MLA_DN + MLA_DR, :] = ((tq[:MLA_DR] + tq[MLA_DR:]) * scale).astype(BF16)
        qt_ref[0, h, MLA_DN + MLA_DR:, :] = zeros
        k_ref[0, h, :, :MLA_DN] = _dot(ckv, wk_ref[h]).astype(BF16)
        k_ref[0, h, :, MLA_DN:] = k_rope
        vt_ref[0, h] = _nt_dot(wvt_ref[h], ckv).astype(BF16)


def _flash_update(s, m_sc, l_sc, acc_sc, vt):
    m_prev = m_sc[...]
    m_new = jnp.maximum(m_prev, jnp.max(s, axis=0, keepdims=True))
    p = jnp.exp2(s - m_new)
    a = jnp.exp2(m_prev - m_new)
    l_sc[...] = a * l_sc[...] + jnp.sum(p, axis=0, keepdims=True)
    acc_sc[...] = a * acc_sc[...] + _dot(vt, p.astype(BF16))
    m_sc[...] = m_new


def _flash_causal_kernel(qt_ref, k_ref, vt_ref, o_ref, m_sc, l_sc, acc_sc, *, t):
    qi = pl.program_id(2)
    qt = qt_ref[0, 0]
    m_sc[...] = jnp.full(m_sc.shape, NEG, F32)
    l_sc[...] = jnp.zeros(l_sc.shape, F32)
    acc_sc[...] = jnp.zeros(acc_sc.shape, F32)

    def scores(kt):
        return _dot(k_ref[0, 0, pl.ds(pl.multiple_of(kt * t, t), t), :], qt)

    def body(kt, s):
        s_next = scores(kt + 1)
        _flash_update(s, m_sc, l_sc, acc_sc, vt_ref[0, 0, :, pl.ds(pl.multiple_of(kt * t, t), t)])
        return s_next

    s = lax.fori_loop(0, qi, body, scores(0))
    kpos = lax.broadcasted_iota(jnp.int32, (t, 1), 0)
    qpos = lax.broadcasted_iota(jnp.int32, (1, t), 1)
    s = jnp.where(kpos <= qpos, s, NEG)
    _flash_update(s, m_sc, l_sc, acc_sc, vt_ref[0, 0, :, pl.ds(pl.multiple_of(qi * t, t), t)])
    o_ref[0] = (acc_sc[...] / l_sc[...]).T.astype(o_ref.dtype)


def _mla_layer(x, shift, scale_v, gate, cos, sin, w_in, g_q, w_uq, g_kv, w_ukv, w_o, ln_g, ln_b):
    B, S, D = x.shape
    H, dn, dr, dv = MLA_HEADS, MLA_DN, MLA_DR, MLA_DV
    half = dr // 2
    kr = w_in[:, MLA_QR + MLA_KVR:]
    win_p = jnp.concatenate([w_in, _rot_half_cols(kr, half)], axis=1).astype(BF16)
    wq = w_uq.reshape(MLA_QR, H, dn + dr)
    wq = jnp.concatenate([wq, _rot_half_cols(wq[..., dn:], half)], axis=-1)
    wqt = jnp.transpose(wq, (1, 2, 0)).astype(BF16)
    wkv = w_ukv.reshape(MLA_KVR, H, dn + dv)
    wk = jnp.transpose(wkv[..., :dn], (1, 0, 2)).astype(BF16)
    wvt = jnp.transpose(wkv[..., dn:], (1, 2, 0)).astype(BF16)
    tab = jnp.concatenate([cos, cos, sin, sin], axis=-1)
    tab_t = jnp.swapaxes(tab, 1, 2)
    ts = min(512, S)
    nw = win_p.shape[1]
    hd = dn + 2 * dr
    qt, k, vt = pl.pallas_call(
        functools.partial(_mla_front_kernel, scale=LOG2E * (dn + dr) ** -0.5),
        grid=(B, S // ts),
        in_specs=[pl.BlockSpec((1, ts, D), lambda b, i: (b, i, 0)),
                  pl.BlockSpec((1, 1, D), lambda b, i: (b, 0, 0)),
                  pl.BlockSpec((1, 1, D), lambda b, i: (b, 0, 0)),
                  pl.BlockSpec((D, nw), lambda b, i: (0, 0)),
                  pl.BlockSpec((1, MLA_QR), lambda b, i: (0, 0)),
                  pl.BlockSpec((1, MLA_KVR), lambda b, i: (0, 0)),
                  pl.BlockSpec((H, hd, MLA_QR), lambda b, i: (0, 0, 0)),
                  pl.BlockSpec((H, MLA_KVR, dn), lambda b, i: (0, 0, 0)),
                  pl.BlockSpec((H, dv, MLA_KVR), lambda b, i: (0, 0, 0)),
                  pl.BlockSpec((1, ts, LANES), lambda b, i: (b, i, 0)),
                  pl.BlockSpec((1, LANES, ts), lambda b, i: (b, 0, i))],
        out_specs=[pl.BlockSpec((1, H, hd, ts), lambda b, i: (b, 0, 0, i)),
                   pl.BlockSpec((1, H, ts, hd), lambda b, i: (b, 0, i, 0)),
                   pl.BlockSpec((1, H, dv, ts), lambda b, i: (b, 0, 0, i))],
        out_shape=[jax.ShapeDtypeStruct((B, H, hd, S), BF16),
                   jax.ShapeDtypeStruct((B, H, S, hd), BF16),
                   jax.ShapeDtypeStruct((B, H, dv, S), BF16)],
        compiler_params=_cp(("arbitrary", "arbitrary"), 48),
        name="mla_front",
    )(x, scale_v, shift, win_p, g_q.reshape(1, -1), g_kv.reshape(1, -1), wqt, wk, wvt, tab, tab_t)

    t = min(512, S)
    o = pl.pallas_call(
        functools.partial(_flash_causal_kernel, t=t),
        grid=(B, H, S // t),
        in_specs=[pl.BlockSpec((1, 1, hd, t), lambda b, h, i: (b, h, 0, i)),
                  pl.BlockSpec((1, 1, S, hd), lambda b, h, i: (b, h, 0, 0)),
                  pl.BlockSpec((1, 1, dv, S), lambda b, h, i: (b, h, 0, 0))],
        out_specs=pl.BlockSpec((1, t, dv), lambda b, h, i: (b, i, h)),
        out_shape=jax.ShapeDtypeStruct((B, S, H * dv), BF16),
        scratch_shapes=[pltpu.VMEM((1, t), F32), pltpu.VMEM((1, t), F32), pltpu.VMEM((dv, t), F32)],
        compiler_params=_cp(("arbitrary", "arbitrary", "arbitrary"), 32),
        name="mla_attn",
    )(qt, k, vt)
    return _out_ln(o, w_o, x, gate, ln_g, ln_b)


def _proj_kernel(*refs, segs, chunk, has_rope):
    x_ref, sc_ref, sh_ref, w_ref, wt_ref = refs[:5]
    n_in = 9 if has_rope else 5
    outs = refs[n_in:]
    hb = (x_ref[0] * (1.0 + sc_ref[0]) + sh_ref[0]).astype(BF16)
    if has_rope:
        cos, sin = refs[5][0], refs[6][0]
        cos_t, sin_t = refs[7][0], refs[8][0]
    half = LANES // 2
    for (c0, width, kind, scale, transposed), o_ref in zip(segs, outs):
        for j in range(0, width, chunk):
            cw = min(chunk, width - j)
            if transposed:
                r = _nt_dot(wt_ref[c0 + j:c0 + j + cw, :], hb)
            else:
                r = _dot(hb, w_ref[:, c0 + j:c0 + j + cw])
            if kind == "rope":
                parts = []
                for t in range(0, cw, LANES):
                    if transposed:
                        u = r[t:t + LANES]
                        parts.append(u * cos_t + pltpu.roll(u, half, axis=0) * sin_t)
                    else:
                        u = r[:, t:t + LANES]
                        parts.append(u * cos + pltpu.roll(u, half, axis=1) * sin)
                r = jnp.concatenate(parts, axis=0 if transposed else 1) if len(parts) > 1 else parts[0]
            elif kind == "sigmoid":
                r = jax.nn.sigmoid(r)
            if scale != 1.0:
                r = r * scale
            if transposed:
                o_ref[0, j:j + cw, :] = r.astype(o_ref.dtype)
            else:
                o_ref[0, :, j:j + cw] = r.astype(o_ref.dtype)


def _mod_proj(x, shift, scale_v, w, wt, segs, out_dtypes, rope=None, ts=512, chunk=256):
    B, S, D = x.shape
    ts = min(ts, S)
    in_specs = [pl.BlockSpec((1, ts, D), lambda b, i: (b, i, 0)),
                pl.BlockSpec((1, 1, D), lambda b, i: (b, 0, 0)),
                pl.BlockSpec((1, 1, D), lambda b, i: (b, 0, 0)),
                pl.BlockSpec(w.shape, lambda b, i: (0, 0)),
                pl.BlockSpec(wt.shape, lambda b, i: (0, 0))]
    args = [x, scale_v, shift, w, wt]
    if rope is not None:
        cos, sin = rope
        in_specs += [pl.BlockSpec((1, ts, LANES), lambda b, i: (b, i, 0))] * 2
        in_specs += [pl.BlockSpec((1, LANES, ts), lambda b, i: (b, 0, i))] * 2
        args += [cos, sin, jnp.swapaxes(cos, 1, 2), jnp.swapaxes(sin, 1, 2)]
    out_specs, out_shape = [], []
    for sg, dt in zip(segs, out_dtypes):
        if sg[4]:
            out_specs.append(pl.BlockSpec((1, sg[1], ts), lambda b, i: (b, 0, i)))
            out_shape.append(jax.ShapeDtypeStruct((B, sg[1], S), dt))
        else:
            out_specs.append(pl.BlockSpec((1, ts, sg[1]), lambda b, i: (b, i, 0)))
            out_shape.append(jax.ShapeDtypeStruct((B, S, sg[1]), dt))
    return pl.pallas_call(
        functools.partial(_proj_kernel, segs=tuple(segs), chunk=chunk, has_rope=rope is not None),
        grid=(B, S // ts),
        in_specs=in_specs,
        out_specs=out_specs,
        out_shape=out_shape,
        compiler_params=_cp(("arbitrary", "arbitrary"), 48),
        name="mod_proj",
    )(*args)


def _nsa_cmp_kernel(kc_ref, vc_ref, pek_ref, pev_ref, wk1_ref, wk2_ref, wv1_ref, wv2t_ref, ko_ref, vto_ref):
    half = NSA_CMP_BLOCK // 2
    nchunk = kc_ref.shape[1]

    def hidden(src, pe_ref, w1_ref, kh):
        a = jnp.zeros((nchunk, w1_ref.shape[2]), F32)
        bm = jnp.zeros((nchunk, w1_ref.shape[2]), F32)
        for l in range(half):
            c0 = (l * NSA_KVH + kh) * NSA_DH
            t = src[0, :, c0:c0 + NSA_DH].astype(F32)
            a = a + _dot((t + pe_ref[l:l + 1, :]).astype(BF16), w1_ref[l])
            bm = bm + _dot((t + pe_ref[half + l:half + l + 1, :]).astype(BF16), w1_ref[half + l])
        hid = a + pltpu.roll(bm, nchunk - 1, axis=0)
        return (hid * jax.nn.sigmoid(hid)).astype(BF16)

    rowi = lax.broadcasted_iota(jnp.int32, (nchunk, 1), 0)
    coli = lax.broadcasted_iota(jnp.int32, (1, nchunk), 1)
    for kh in range(NSA_KVH):
        out = _dot(hidden(kc_ref, pek_ref, wk1_ref, kh), wk2_ref[...])
        ko_ref[0, kh] = jnp.where(rowi < nchunk - 1, out, 0.0).astype(ko_ref.dtype)
        out_t = _nt_dot(wv2t_ref[...], hidden(vc_ref, pev_ref, wv1_ref, kh))
        vto_ref[0, kh] = jnp.where(coli < nchunk - 1, out_t, 0.0).astype(vto_ref.dtype)


def _nsa_attn_kernel(qt_ref, kc_ref, vct_ref, ks_ref, vst_ref, kw_ref, vwt_ref, gt_ref, ovt_ref, e_ref, o_ref,
                     q4_sc, m_sc, l_sc, acc_sc, *, tq, tk, span, n_slc):
    qi = pl.program_id(2)
    G = NSA_G
    for g in range(G):
        q4_sc[:, g * tq:(g + 1) * tq] = qt_ref[0, g * NSA_DH:(g + 1) * NSA_DH, :]
    q4 = q4_sc[...]
    qpos = qi * tq + lax.broadcasted_iota(jnp.int32, (1, tq), 1)

    def tile_g(a):
        return jnp.concatenate([a] * G, axis=1)

    qpos4 = tile_g(qpos)

    n_cmp = kc_ref.shape[2]
    nid = lax.broadcasted_iota(jnp.int32, (n_cmp, 1), 0)
    cmask = (nid * NSA_CMP_STRIDE + (NSA_CMP_BLOCK - 1) <= qpos4) & (nid < n_cmp - 1)
    s = jnp.where(cmask, _dot(kc_ref[0, 0], q4), NEG)
    p = jnp.where(cmask, jnp.exp2(s - jnp.max(s, axis=0, keepdims=True)), 0.0)
    l = jnp.sum(p, axis=0, keepdims=True)
    p_cmp = p / jnp.where(l > 0.0, l, 1.0)
    o_cmp = _dot(vct_ref[0, 0], p_cmp.astype(BF16))

    psum = p_cmp[:, 0:tq]
    for g in range(1, G):
        psum = psum + p_cmp[:, g * tq:(g + 1) * tq]
    p_hi, p_lo = _split_bf16(psum)
    imp = _dot(ovt_ref[...], p_hi) + _dot(ovt_ref[...], p_lo)
    nb = ovt_ref.shape[0]
    blk = lax.broadcasted_iota(jnp.int32, (nb, 1), 0)
    cur = jnp.right_shift(qpos, SLC_SHIFT)
    forced = (blk == 0) | (blk == cur) | (blk == cur - 1)
    valid = (blk * NSA_SLC_BLOCK <= qpos) & (blk < n_slc)
    work = jnp.where(valid, imp + jnp.where(forced, NSA_BONUS, 0.0), -jnp.inf)
    beaten = jnp.zeros((nb, tq), F32)
    for j in range(n_slc):
        wj = work[j:j + 1, :]
        beaten = beaten + jnp.where((wj > work) | ((wj == work) & (blk > j)), 1.0, 0.0)
    sel = jnp.where(beaten < float(min(NSA_TOPK, n_slc)), 1.0, 0.0).astype(BF16)

    m_sc[...] = jnp.full(m_sc.shape, NEG, F32)
    l_sc[...] = jnp.zeros(l_sc.shape, F32)
    acc_sc[...] = jnp.zeros(acc_sc.shape, F32)
    n_kt = ((qi + 1) * tq + tk - 1) // tk

    def scores(kt):
        return _dot(ks_ref[0, pl.ds(pl.multiple_of(kt * tk, tk), tk), :], q4)

    def body(kt, s):
        s_next = scores(jnp.minimum(kt + 1, n_kt - 1))
        ks = pl.multiple_of(kt * tk, tk)
        chosen = _dot(e_ref[pl.ds(ks, tk), :], sel)
        kpos = ks + lax.broadcasted_iota(jnp.int32, (tk, 1), 0)
        bias = jnp.where((chosen > 0.5) & (kpos <= qpos), 0.0, NEG)
        _flash_update(s + tile_g(bias), m_sc, l_sc, acc_sc, vst_ref[0, :, pl.ds(ks, tk)])
        return s_next

    lax.fori_loop(0, n_kt, body, scores(0))
    o_slc = acc_sc[...] / l_sc[...]

    kstart = pl.multiple_of(jnp.maximum(qi * tq + tq - span, 0), tq)
    diff = qpos - (kstart + lax.broadcasted_iota(jnp.int32, (span, 1), 0))
    bias = jnp.where((diff >= 0) & (diff < NSA_WINDOW), 0.0, NEG)
    s = _dot(kw_ref[0, pl.ds(kstart, span), :], q4) + tile_g(bias)
    p = jnp.exp2(s - jnp.max(s, axis=0, keepdims=True))
    o_win = _dot(vwt_ref[0, :, pl.ds(kstart, span)], p.astype(BF16)) / jnp.sum(p, axis=0, keepdims=True)

    gt = gt_ref[0]
    for g in range(G):
        cols = slice(g * tq, (g + 1) * tq)
        o = (gt[3 * g:3 * g + 1] * o_cmp[:, cols] + gt[3 * g + 1:3 * g + 2] * o_slc[:, cols]
             + gt[3 * g + 2:3 * g + 3] * o_win[:, cols])
        o_ref[0, :, g * NSA_DH:(g + 1) * NSA_DH] = o.T.astype(o_ref.dtype)


def _nsa_layer(x, shift, scale_v, gate, cos, sin, w_in, pe_k, pe_v, w_ck1, w_ck2, w_cv1, w_cv2, w_o,
               ln_g, ln_b):
    B, S, D = x.shape
    H, KH, Dh, G = NSA_HEADS, NSA_KVH, NSA_DH, NSA_G
    kvw = KH * Dh
    cuts = [int(v) for v in np.cumsum([H * Dh] + [kvw] * 6)]
    wq, wkc, wvc, wks, wvs, wkw, wvw, wgl = jnp.split(w_in, cuts, axis=1)
    wgl = jnp.pad(wgl.reshape(D, KH, G * 3), ((0, 0), (0, 0), (0, GATE_ROWS - G * 3))).reshape(D, KH * GATE_ROWS)
    w = jnp.concatenate([wkc, wks, wkw, wvc], axis=1).astype(BF16)
    wt = jnp.concatenate([wq, wvs, wvw, wgl], axis=1).T.astype(BF16)
    segs = [(0, H * Dh, "rope", LOG2E * Dh ** -0.5, True),
            (0, kvw, "rope", 1.0, False), (kvw, kvw, "rope", 1.0, False), (2 * kvw, kvw, "rope", 1.0, False),
            (3 * kvw, kvw, "plain", 1.0, False),
            (H * Dh, kvw, "plain", 1.0, True), (H * Dh + kvw, kvw, "plain", 1.0, True),
            (H * Dh + 2 * kvw, KH * GATE_ROWS, "sigmoid", 1.0, True)]
    rope = (jnp.concatenate([cos, cos], axis=-1), jnp.concatenate([-sin, sin], axis=-1))
    qt, kc, ks, kw, vc, vst, vwt, gates_t = _mod_proj(
        x, shift, scale_v, w, wt, segs, [BF16] * 7 + [F32], rope=rope)

    st = NSA_CMP_STRIDE
    nchunk = S // st
    kc2 = kc.reshape(B, nchunk, st * kvw)
    vc2 = vc.reshape(B, nchunk, st * kvw)
    hid = w_ck1.shape[-1]
    full = lambda shape: pl.BlockSpec(shape, lambda b: (0,) * len(shape))
    k_cmp, v_cmp_t = pl.pallas_call(
        _nsa_cmp_kernel,
        grid=(B,),
        in_specs=[pl.BlockSpec((1, nchunk, st * kvw), lambda b: (b, 0, 0)),
                  pl.BlockSpec((1, nchunk, st * kvw), lambda b: (b, 0, 0)),
                  full((NSA_CMP_BLOCK, Dh)), full((NSA_CMP_BLOCK, Dh)),
                  full((NSA_CMP_BLOCK, Dh, hid)), full((hid, Dh)),
                  full((NSA_CMP_BLOCK, Dh, hid)), full((Dh, hid))],
        out_specs=[pl.BlockSpec((1, KH, nchunk, Dh), lambda b: (b, 0, 0, 0)),
                   pl.BlockSpec((1, KH, Dh, nchunk), lambda b: (b, 0, 0, 0))],
        out_shape=[jax.ShapeDtypeStruct((B, KH, nchunk, Dh), BF16),
                   jax.ShapeDtypeStruct((B, KH, Dh, nchunk), BF16)],
        compiler_params=_cp(("arbitrary",), 32),
        name="nsa_cmp",
    )(kc2, vc2, pe_k, pe_v, w_ck1.astype(BF16), w_ck2.astype(BF16), w_cv1.astype(BF16), w_cv2.T.astype(BF16))

    n_slc = S // NSA_SLC_BLOCK
    nb = 32
    assert n_slc <= nb and nchunk % 8 == 0
    cmp_start = np.arange(nchunk) * st
    slc_start = np.arange(n_slc) * NSA_SLC_BLOCK
    ov = np.clip(np.minimum(cmp_start[:, None] + NSA_CMP_BLOCK, slc_start[None, :] + NSA_SLC_BLOCK)
                 - np.maximum(cmp_start[:, None], slc_start[None, :]), 0, None).astype(np.float32) / NSA_CMP_BLOCK
    ov[nchunk - 1:] = 0.0
    ov_t = np.zeros((nb, nchunk), np.float32)
    ov_t[:n_slc] = ov.T
    key_blk = np.zeros((S, nb), np.float32)
    key_blk[np.arange(S), np.arange(S) // NSA_SLC_BLOCK] = 1.0

    tq = min(128, S)
    tk = min(256, S)
    span = min(NSA_WINDOW + tq, S)
    k_spec = pl.BlockSpec((1, S, Dh), lambda b, kh, i: (b, 0, kh))
    vt_spec = pl.BlockSpec((1, Dh, S), lambda b, kh, i: (b, kh, 0))
    o = pl.pallas_call(
        functools.partial(_nsa_attn_kernel, tq=tq, tk=tk, span=span, n_slc=n_slc),
        grid=(B, KH, S // tq),
        in_specs=[pl.BlockSpec((1, G * Dh, tq), lambda b, kh, i: (b, kh, i)),
                  pl.BlockSpec((1, 1, nchunk, Dh), lambda b, kh, i: (b, kh, 0, 0)),
                  pl.BlockSpec((1, 1, Dh, nchunk), lambda b, kh, i: (b, kh, 0, 0)),
                  k_spec, vt_spec, k_spec, vt_spec,
                  pl.BlockSpec((1, GATE_ROWS, tq), lambda b, kh, i: (b, kh, i)),
                  pl.BlockSpec((nb, nchunk), lambda b, kh, i: (0, 0)),
                  pl.BlockSpec((S, nb), lambda b, kh, i: (0, 0))],
        out_specs=pl.BlockSpec((1, tq, G * Dh), lambda b, kh, i: (b, i, kh)),
        out_shape=jax.ShapeDtypeStruct((B, S, H * Dh), BF16),
        scratch_shapes=[pltpu.VMEM((Dh, G * tq), BF16), pltpu.VMEM((1, G * tq), F32),
                        pltpu.VMEM((1, G * tq), F32), pltpu.VMEM((Dh, G * tq), F32)],
        compiler_params=_cp(("arbitrary", "arbitrary", "arbitrary"), 32),
        name="nsa_attn",
    )(qt, k_cmp, v_cmp_t, ks, vst, kw, vwt, gates_t, jnp.asarray(ov_t, BF16), jnp.asarray(key_blk, BF16))
    return _out_ln(o, w_o, x, gate, ln_g, ln_b)


def _sb_attn_kernel(qt_ref, k_ref, vt_ref, u_ref, o_ref, r_sc, acc_sc, *, tq, tk):
    qi = pl.program_id(2)
    qt = qt_ref[0]
    r_sc[...] = jnp.zeros(r_sc.shape, F32)
    acc_sc[...] = jnp.zeros(acc_sc.shape, F32)
    u = u_ref[...]
    nd = tq // tk
    last = (qi + 1) * nd - 1

    def scores(kt):
        return _dot(k_ref[0, pl.ds(pl.multiple_of(kt * tk, tk), tk), :], qt)

    def step(kt, z, masked):
        ks = pl.multiple_of(kt * tk, tk)
        lk = -(jnp.maximum(z, 0.0) + jnp.log2(1.0 + jnp.exp2(-jnp.abs(z))))
        if masked:
            kpos = ks + lax.broadcasted_iota(jnp.int32, (tk, 1), 0)
            qpos = qi * tq + lax.broadcasted_iota(jnp.int32, (1, tq), 1)
            strict = kpos < qpos
            lk = jnp.where(strict, lk, 0.0)
        hi, lo = _split_bf16(lk)
        between = r_sc[...] + _dot(u, hi) + _dot(u, lo)
        a = jnp.exp2(lk + z + between)
        if masked:
            a = jnp.where(strict, a, 0.0)
        acc_sc[...] += _dot(vt_ref[0, :, pl.ds(ks, tk)], a.astype(BF16))
        r_sc[...] += jnp.sum(lk, axis=0, keepdims=True)

    z = scores(last)
    for i in range(nd):
        z_next = scores(jnp.maximum(last - i - 1, 0))
        step(last - i, z, True)
        z = z_next
    n_full = qi * nd

    def body(i, z):
        kt = n_full - 1 - i
        z_next = scores(jnp.maximum(kt - 1, 0))
        step(kt, z, False)
        return z_next

    lax.fori_loop(0, n_full, body, z)
    o_ref[0] = acc_sc[...].T.astype(o_ref.dtype)


def _sb_layer(x, shift, scale_v, gate, w_qkv, w_o, ln_g, ln_b):
    B, S, D = x.shape
    H, d = SB_HEADS, SB_DH
    hw = H * d
    w = w_qkv[:, hw:2 * hw].astype(BF16)
    wt = jnp.concatenate([w_qkv[:, :hw], w_qkv[:, 2 * hw:]], axis=1).T.astype(BF16)
    segs = [(0, hw, "plain", LOG2E * d ** -0.5, True), (0, hw, "plain", 1.0, False), (hw, hw, "plain", 1.0, True)]
    qt, k, vt = _mod_proj(x, shift, scale_v, w, wt, segs, [BF16] * 3)
    tq = min(512, S)
    tk = min(256, S)
    tri = np.triu(np.ones((tk, tk), np.float32), 1)
    o = pl.pallas_call(
        functools.partial(_sb_attn_kernel, tq=tq, tk=tk),
        grid=(B, H, S // tq),
        in_specs=[pl.BlockSpec((1, d, tq), lambda b, h, i: (b, h, i)),
                  pl.BlockSpec((1, S, d), lambda b, h, i: (b, 0, h)),
                  pl.BlockSpec((1, d, S), lambda b, h, i: (b, h, 0)),
                  pl.BlockSpec((tk, tk), lambda b, h, i: (0, 0))],
        out_specs=pl.BlockSpec((1, tq, d), lambda b, h, i: (b, i, h)),
        out_shape=jax.ShapeDtypeStruct((B, S, hw), BF16),
        scratch_shapes=[pltpu.VMEM((1, tq), F32), pltpu.VMEM((d, tq), F32)],
        compiler_params=_cp(("arbitrary", "arbitrary", "arbitrary"), 32),
        name="sb_attn",
    )(qt, k, vt, jnp.asarray(tri, BF16))
    return _out_ln(o, w_o, x, gate, ln_g, ln_b)


def _router_kernel(x_ref, sc_ref, sh_ref, whi_ref, wlo_ref, b_ref, tri_ref, hb_ref, info_ref, cnt_ref,
                   base_sc):
    first = (pl.program_id(0) == 0) & (pl.program_id(1) == 0)

    @pl.when(first)
    def _():
        base_sc[...] = jnp.zeros(base_sc.shape, F32)

    h = x_ref[0] * (1.0 + sc_ref[0]) + sh_ref[0]
    h_hi, h_lo = _split_bf16(h)
    hb_ref[0] = h
    logit = (_dot(h_hi, whi_ref[...]) + _dot(h_lo, whi_ref[...]) + _dot(h_hi, wlo_ref[...])
             + b_ref[...])
    tm = logit.shape[0]
    lane = lax.broadcasted_iota(jnp.int32, (1, LANES), 1)
    lane_f = lane.astype(F32)
    big = float(LANES)

    def first_max(vals):
        mx = jnp.max(vals, axis=1, keepdims=True)
        idx = jnp.min(jnp.where(vals == mx, lane_f, big), axis=1, keepdims=True)
        return mx, idx

    gl = jnp.where(lane < MOE_GROUPS, logit, -jnp.inf)
    gmax, g_idx = first_max(gl)
    g_w = 1.0 / jnp.sum(jnp.exp(gl - gmax), axis=1, keepdims=True)
    e_lane = lane - MOE_GROUPS
    in_grp = (e_lane >= 0) & (e_lane < MOE_E) & (jnp.right_shift(e_lane, EPG_SHIFT).astype(F32) == g_idx)
    el = jnp.where(in_grp, logit, -jnp.inf)
    emax = jnp.max(el, axis=1, keepdims=True)
    ep = jnp.exp(el - emax)
    prob = jnp.where(in_grp, ep / jnp.sum(ep, axis=1, keepdims=True), -1.0)
    p1, i1 = first_max(prob)
    p2, i2 = first_max(jnp.where(lane_f == i1, -1.0, prob))
    e1 = i1 - float(MOE_GROUPS)
    e2 = i2 - float(MOE_GROUPS)
    w1 = g_w * p1 / (p1 + p2)
    w2 = g_w * p2 / (p1 + p2)

    hot1 = lane_f == e1
    hot2 = lane_f == e2
    cnt = jnp.where(hot1 | hot2, 1.0, 0.0)
    before = base_sc[...] + _dot(tri_ref[...], cnt.astype(BF16))
    r1 = jnp.sum(jnp.where(hot1, before, 0.0), axis=1, keepdims=True)
    r2 = jnp.sum(jnp.where(hot2, before, 0.0), axis=1, keepdims=True)
    base_sc[...] += jnp.sum(cnt, axis=0, keepdims=True)
    cnt_ref[...] = base_sc[...]

    info = jnp.zeros((tm, LANES), F32)
    for k, val in enumerate((e1, e2, r1, r2, w1, w2)):
        info = jnp.where(lane == k, val, info)
    info_ref[0] = info


def _ffn_kernel(be_ref, nu_ref, x_ref, wg_ref, wu_ref, wd_ref, y_ref, wg_sc, wu_sc, wd_sc):
    i = pl.program_id(0)
    new_expert = (i == 0) | (be_ref[i] != be_ref[jnp.maximum(i - 1, 0)])

    @pl.when(new_expert & (i < nu_ref[0]))
    def _():
        wg_sc[...] = wg_ref[0, 0].astype(BF16)
        wu_sc[...] = wu_ref[0, 0].astype(BF16)
        wd_sc[...] = wd_ref[0, 0].astype(BF16)

    @pl.when(i < nu_ref[0])
    def _():
        xb = x_ref[...].astype(BF16)
        g = _dot(xb, wg_sc[...])
        u = _dot(xb, wu_sc[...])
        a = (g * jax.nn.sigmoid(g) * u).astype(BF16)
        y_ref[...] = _dot(a, wd_sc[...])

    @pl.when(i >= nu_ref[0])
    def _():
        y_ref[...] = jnp.zeros(y_ref.shape, y_ref.dtype)


def _moe_out_kernel(x_ref, y1_ref, y2_ref, info_ref, gate_ref, g_ref, b_ref, o_ref):
    info = info_ref[0]
    y = info[:, 4:5] * y1_ref[0] + info[:, 5:6] * y2_ref[0]
    z = ALPHA * x_ref[0] + gate_ref[0] * y
    o_ref[0] = _layer_norm(z, g_ref[...], b_ref[...])


def _moe_out(x, y1, y2, info, gate, g, b, ts=512):
    B, S, D = x.shape
    ts = min(ts, S)
    tok = pl.BlockSpec((1, ts, D), lambda bi, i: (bi, i, 0))
    return pl.pallas_call(
        _moe_out_kernel,
        grid=(B, S // ts),
        in_specs=[tok, tok, tok,
                  pl.BlockSpec((1, ts, LANES), lambda bi, i: (bi, i, 0)),
                  pl.BlockSpec((1, 1, D), lambda bi, i: (bi, 0, 0)),
                  pl.BlockSpec((1, D), lambda bi, i: (0, 0)),
                  pl.BlockSpec((1, D), lambda bi, i: (0, 0))],
        out_specs=tok,
        out_shape=jax.ShapeDtypeStruct((B, S, D), F32),
        compiler_params=_cp(("arbitrary", "arbitrary"), 40),
        name="moe_out",
    )(x, y1, y2, info, gate, g.reshape(1, D), b.reshape(1, D))


def _moe_layer(x, shift, scale_v, gate, w_grp, b_grp, w_rt, b_rt, layer, w_gate, w_up, w_down, ln_g, ln_b):
    B, S, D = x.shape
    T = B * S
    E, R, F = MOE_E, MOE_ROWS, w_gate.shape[-1]
    w_r = jnp.pad(jnp.concatenate([w_grp, w_rt], axis=1), ((0, 0), (0, LANES - MOE_GROUPS - E)))
    w_hi = w_r.astype(BF16)
    w_lo = (w_r - w_hi.astype(F32)).astype(BF16)
    b_r = jnp.pad(jnp.concatenate([b_grp, b_rt]), (0, LANES - MOE_GROUPS - E)).reshape(1, LANES)
    tm = min(256, S)
    tri = jnp.asarray(np.tril(np.ones((tm, tm), np.float32), -1), BF16)
    hb, info, cnt = pl.pallas_call(
        _router_kernel,
        grid=(B, S // tm),
        in_specs=[pl.BlockSpec((1, tm, D), lambda b, i: (b, i, 0)),
                  pl.BlockSpec((1, 1, D), lambda b, i: (b, 0, 0)),
                  pl.BlockSpec((1, 1, D), lambda b, i: (b, 0, 0)),
                  pl.BlockSpec((D, LANES), lambda b, i: (0, 0)),
                  pl.BlockSpec((D, LANES), lambda b, i: (0, 0)),
                  pl.BlockSpec((1, LANES), lambda b, i: (0, 0)),
                  pl.BlockSpec((tm, tm), lambda b, i: (0, 0))],
        out_specs=[pl.BlockSpec((1, tm, D), lambda b, i: (b, i, 0)),
                   pl.BlockSpec((1, tm, LANES), lambda b, i: (b, i, 0)),
                   pl.BlockSpec((1, LANES), lambda b, i: (0, 0))],
        out_shape=[jax.ShapeDtypeStruct((B, S, D), F32),
                   jax.ShapeDtypeStruct((B, S, LANES), F32),
                   jax.ShapeDtypeStruct((1, LANES), F32)],
        scratch_shapes=[pltpu.VMEM((1, LANES), F32)],
        compiler_params=_cp(("arbitrary", "arbitrary"), 32),
        name="moe_router",
    )(x, scale_v, shift, w_hi, w_lo, b_r, tri)

    info2 = info.reshape(T, LANES)
    experts = info2[:, 0:2].astype(jnp.int32)
    ranks = info2[:, 2:4].astype(jnp.int32)
    counts = cnt[0, :E].astype(jnp.int32)
    padded = (counts + R - 1) // R * R
    ends = jnp.cumsum(padded)
    starts = ends - padded
    dest = starts[experts] + ranks
    n_blk = (T * 2) // R + E
    P = n_blk * R
    row_tok = jnp.zeros((P,), jnp.int32).at[dest.reshape(-1)].set(
        jnp.repeat(jnp.arange(T, dtype=jnp.int32), 2))
    blk_exp = jnp.minimum(jnp.sum(jnp.arange(n_blk)[:, None] * R >= ends[None, :], axis=1), E - 1).astype(jnp.int32)
    n_used = (ends[-1] // R).astype(jnp.int32).reshape(1)
    x_rows = hb.reshape(T, D)[row_tok]

    y_rows = pl.pallas_call(
        _ffn_kernel,
        grid_spec=pltpu.PrefetchScalarGridSpec(
            num_scalar_prefetch=2,
            grid=(n_blk,),
            in_specs=[pl.BlockSpec((R, D), lambda i, be, nu: (i, 0)),
                      pl.BlockSpec((1, 1, D, F), lambda i, be, nu: (layer, be[i], 0, 0)),
                      pl.BlockSpec((1, 1, D, F), lambda i, be, nu: (layer, be[i], 0, 0)),
                      pl.BlockSpec((1, 1, F, D), lambda i, be, nu: (layer, be[i], 0, 0))],
            out_specs=pl.BlockSpec((R, D), lambda i, be, nu: (i, 0)),
            scratch_shapes=[pltpu.VMEM((D, F), BF16), pltpu.VMEM((D, F), BF16), pltpu.VMEM((F, D), BF16)]),
        out_shape=jax.ShapeDtypeStruct((P, D), F32),
        compiler_params=_cp(("arbitrary",), 48),
        name="moe_ffn",
    )(blk_exp, n_used, x_rows, w_gate, w_up, w_down)

    y1 = y_rows[dest[:, 0]].reshape(B, S, D)
    y2 = y_rows[dest[:, 1]].reshape(B, S, D)
    return _moe_out(x, y1, y2, info, gate, ln_g, ln_b)


def kernel(x, c, positions, ada_w, ada_b, ln_g, ln_b, mla_w_in, mla_g_q, mla_w_uq, mla_g_kv, mla_w_ukv, mla_w_o, nsa_w_in, nsa_pe_k, nsa_pe_v, nsa_w_ck1, nsa_w_ck2, nsa_w_cv1, nsa_w_cv2, nsa_w_o, sb_w_qkv, sb_w_o, moe_w_grp, moe_b_grp, moe_w_rt, moe_b_rt, moe_w_gate, moe_w_up, moe_w_down):
    D = x.shape[-1]
    depth = ada_w.shape[0]
    cos_mla, sin_mla = _rope_tables(positions, MLA_DR)
    cos_nsa, sin_nsa = _rope_tables(positions, NSA_DH)
    mods = _ada_all(c, ada_w, ada_b)

    def mod(i, sub):
        m = mods[2 * i + sub][:, None, :]
        return m[..., :D], m[..., D:2 * D], m[..., 2 * D:]

    for i in range(depth):
        kind, j = i % N_MIXERS, i // N_MIXERS
        shift, scale_v, gate = mod(i, 0)
        if kind == 0:
            x = _mla_layer(x, shift, scale_v, gate, cos_mla, sin_mla, mla_w_in[j], mla_g_q[j], mla_w_uq[j],
                           mla_g_kv[j], mla_w_ukv[j], mla_w_o[j], ln_g[i, 0], ln_b[i, 0])
        elif kind == 1:
            x = _nsa_layer(x, shift, scale_v, gate, cos_nsa, sin_nsa, nsa_w_in[j], nsa_pe_k[j], nsa_pe_v[j],
                           nsa_w_ck1[j], nsa_w_ck2[j], nsa_w_cv1[j], nsa_w_cv2[j], nsa_w_o[j],
                           ln_g[i, 0], ln_b[i, 0])
        else:
            x = _sb_layer(x, shift, scale_v, gate, sb_w_qkv[j], sb_w_o[j], ln_g[i, 0], ln_b[i, 0])
        shift, scale_v, gate = mod(i, 1)
        x = _moe_layer(x, shift, scale_v, gate, moe_w_grp[i], moe_b_grp[i], moe_w_rt[i], moe_b_rt[i],
                       i, moe_w_gate, moe_w_up, moe_w_down, ln_g[i, 1], ln_b[i, 1])
    return x
```

```python
import functools

import numpy as np
import jax
import jax.numpy as jnp
from jax import lax
from jax.experimental import pallas as pl
from jax.experimental.pallas import tpu as pltpu
from jax.experimental.pallas import tpu_sc as plsc

BF16 = jnp.bfloat16
F32 = jnp.float32

DEPTH = 4
N_MIXERS = 3
ROPE_THETA = 10000.0
LN_EPS = 1e-5
RMS_EPS = 1e-6
NEG = -1e30
LOG2E = 1.4426950408889634
ALPHA = (2 * DEPTH) ** 0.25

MLA_HEADS, MLA_DN, MLA_DR, MLA_DV = 8, 128, 64, 128
MLA_QR, MLA_KVR = 256, 256

NSA_HEADS, NSA_KVH, NSA_DH = 8, 2, 128
NSA_G = NSA_HEADS // NSA_KVH
NSA_CMP_BLOCK, NSA_CMP_STRIDE = 32, 16
NSA_SLC_BLOCK, NSA_TOPK = 64, 8
SLC_SHIFT = 6
NSA_WINDOW = 512
NSA_BONUS = 1e3
GATE_ROWS = 16

SB_HEADS, SB_DH = 8, 128

MOE_GROUPS, MOE_EPG = 4, 8
EPG_SHIFT = 3
MOE_E = MOE_GROUPS * MOE_EPG
MOE_ROWS = 256

SC_CORES, SC_SUBCORES = 2, 16
SC_CHUNK = 64

LANES = 128
VMEM_BYTES_V7X = 64 * 1024 * 1024


def _cp(sem, vmem_mb):
    assert vmem_mb * 2**20 < VMEM_BYTES_V7X
    return pltpu.CompilerParams(dimension_semantics=sem, vmem_limit_bytes=vmem_mb * 2**20)


def _nt_dot(a, b):
    return lax.dot_general(a, b, (((1,), (1,)), ((), ())), preferred_element_type=F32)


def _dot(a, b):
    return jnp.dot(a, b, preferred_element_type=F32)


def _split_bf16(a):
    hi = a.astype(BF16)
    lo = (a - hi.astype(F32)).astype(BF16)
    return hi, lo


def _pack_pairs(a):
    n = a.shape[1] // 2
    hi = pltpu.bitcast(a[:, :n].astype(BF16).astype(F32), jnp.uint32)
    lo = pltpu.bitcast(a[:, n:].astype(BF16).astype(F32), jnp.uint32)
    return hi | (lo >> 16)


def _unpack_pairs(u):
    hi = pltpu.bitcast(u & jnp.uint32(0xFFFF0000), F32)
    lo = pltpu.bitcast(u << 16, F32)
    return jnp.concatenate([hi, lo], axis=1)


def _layer_norm(z, g, b):
    mu = jnp.mean(z, axis=-1, keepdims=True)
    d = z - mu
    var = jnp.mean(d * d, axis=-1, keepdims=True)
    return d * lax.rsqrt(var + LN_EPS) * g + b


def _ada_kernel(c_ref, w_ref, b_ref, o_ref):
    c = c_ref[...]
    s_hi, s_lo = _split_bf16(c * jax.nn.sigmoid(c))
    w_hi, w_lo = _split_bf16(w_ref[0])
    o_ref[0] = _dot(s_hi, w_hi) + _dot(s_lo, w_hi) + _dot(s_hi, w_lo) + b_ref[0]


def _ada_all(c, ada_w, ada_b):
    L, two, D, D3 = ada_w.shape
    B = c.shape[0]
    n = L * two
    tn = 1024
    return pl.pallas_call(
        _ada_kernel,
        grid=(n, D3 // tn),
        in_specs=[pl.BlockSpec((B, D), lambda i, j: (0, 0)),
                  pl.BlockSpec((1, D, tn), lambda i, j: (i, 0, j)),
                  pl.BlockSpec((1, 1, tn), lambda i, j: (i, 0, j))],
        out_specs=pl.BlockSpec((1, B, tn), lambda i, j: (i, 0, j)),
        out_shape=jax.ShapeDtypeStruct((n, B, D3), F32),
        compiler_params=_cp(("arbitrary", "arbitrary"), 32),
        name="ada",
    )(c, ada_w.reshape(n, D, D3), ada_b.reshape(n, 1, D3))


def _res_ln_kernel(x_ref, y_ref, gate_ref, g_ref, b_ref, o_ref):
    z = ALPHA * x_ref[0] + gate_ref[0] * y_ref[0]
    o_ref[0] = _layer_norm(z, g_ref[...], b_ref[...])


def _res_ln(x, y, gate, g, b, ts=512):
    B, S, D = x.shape
    ts = min(ts, S)
    tok = pl.BlockSpec((1, ts, D), lambda bi, i: (bi, i, 0))
    return pl.pallas_call(
        _res_ln_kernel,
        grid=(B, S // ts),
        in_specs=[tok, tok,
                  pl.BlockSpec((1, 1, D), lambda bi, i: (bi, 0, 0)),
                  pl.BlockSpec((1, D), lambda bi, i: (0, 0)),
                  pl.BlockSpec((1, D), lambda bi, i: (0, 0))],
        out_specs=tok,
        out_shape=jax.ShapeDtypeStruct((B, S, D), F32),
        compiler_params=_cp(("arbitrary", "arbitrary"), 32),
        name="res_ln",
    )(x, y, gate, g.reshape(1, D), b.reshape(1, D))


def _out_ln_kernel(o_ref, w_ref, x_ref, gate_ref, g_ref, b_ref, out_ref):
    y = _dot(o_ref[0], w_ref[...])
    z = ALPHA * x_ref[0] + gate_ref[0] * y
    out_ref[0] = _layer_norm(z, g_ref[...], b_ref[...])


def _out_ln(o, w_o, x, gate, g, b, ts=512):
    B, S, D = x.shape
    K = o.shape[-1]
    ts = min(ts, S)
    return pl.pallas_call(
        _out_ln_kernel,
        grid=(B, S // ts),
        in_specs=[pl.BlockSpec((1, ts, K), lambda bi, i: (bi, i, 0)),
                  pl.BlockSpec((K, D), lambda bi, i: (0, 0)),
                  pl.BlockSpec((1, ts, D), lambda bi, i: (bi, i, 0)),
                  pl.BlockSpec((1, 1, D), lambda bi, i: (bi, 0, 0)),
                  pl.BlockSpec((1, D), lambda bi, i: (0, 0)),
                  pl.BlockSpec((1, D), lambda bi, i: (0, 0))],
        out_specs=pl.BlockSpec((1, ts, D), lambda bi, i: (bi, i, 0)),
        out_shape=jax.ShapeDtypeStruct((B, S, D), F32),
        compiler_params=_cp(("arbitrary", "arbitrary"), 40),
        name="out_ln",
    )(o, w_o.astype(BF16), x, gate, g.reshape(1, D), b.reshape(1, D))


def _rope_tables(positions, dim):
    inv_freq = 1.0 / (ROPE_THETA ** (jnp.arange(0, dim, 2, dtype=F32) / dim))
    ang = positions.astype(F32)[..., None] * inv_freq
    return jnp.cos(ang), jnp.sin(ang)


def _rot_half_cols(w, half):
    return jnp.concatenate([-w[..., half:], w[..., :half]], axis=-1)


def _mla_front_kernel(x_ref, sc_ref, sh_ref, win_ref, gq_ref, gkv_ref, wqt_ref, wk_ref, wvt_ref, t_ref, tt_ref,
                      qt_ref, k_ref, vt_ref, *, scale):
    hb = (x_ref[0] * (1.0 + sc_ref[0]) + sh_ref[0]).astype(BF16)
    r = _dot(hb, win_ref[...])
    tab = t_ref[0]
    tab_t = tt_ref[0]
    low = lax.broadcasted_iota(jnp.int32, tab.shape, 1) < MLA_DR

    def rms(cx, g):
        return (cx * lax.rsqrt(jnp.mean(cx * cx, axis=-1, keepdims=True) + RMS_EPS) * g).astype(BF16)

    cq = rms(r[:, :MLA_QR], gq_ref[...])
    ckv = rms(r[:, MLA_QR:MLA_QR + MLA_KVR], gkv_ref[...])
    t = r[:, MLA_QR + MLA_KVR:] * tab
    k_rope = jnp.where(low, t + pltpu.roll(t, MLA_DR, axis=1), 0.0).astype(BF16)
    zeros = jnp.zeros((MLA_DR, hb.shape[0]), BF16)
    for h in range(MLA_HEADS):
        qh = _nt_dot(wqt_ref[h], cq)
        tq = qh[MLA_DN:] * tab_t
        qt_ref[0, h, :MLA_DN, :] = (qh[:MLA_DN] * scale).astype(BF16)
        qt_ref[0, h, MLA_DN:MLA_DN + MLA_DR, :] = ((tq[:MLA_DR] + tq[MLA_DR:]) * scale).astype(BF16)
        qt_ref[0, h, MLA_DN + MLA_DR:, :] = zeros
        k_ref[0, h, :, :MLA_DN] = _dot(ckv, wk_ref[h]).astype(BF16)
        k_ref[0, h, :, MLA_DN:] = k_rope
        vt_ref[0, h] = _nt_dot(wvt_ref[h], ckv).astype(BF16)


def _flash_update(s, m_sc, l_sc, acc_sc, vt):
    m_prev = m_sc[...]
    m_new = jnp.maximum(m_prev, jnp.max(s, axis=0, keepdims=True))
    p = jnp.exp2(s - m_new)
    a = jnp.exp2(m_prev - m_new)
    l_sc[...] = a * l_sc[...] + jnp.sum(p, axis=0, keepdims=True)
    acc_sc[...] = a * acc_sc[...] + _dot(vt, p.astype(BF16))
    m_sc[...] = m_new


def _flash_causal_kernel(qt_ref, k_ref, vt_ref, o_ref, m_sc, l_sc, acc_sc, *, t):
    qi = pl.program_id(2)
    qt = qt_ref[0, 0]
    m_sc[...] = jnp.full(m_sc.shape, NEG, F32)
    l_sc[...] = jnp.zeros(l_sc.shape, F32)
    acc_sc[...] = jnp.zeros(acc_sc.shape, F32)

    def scores(kt):
        return _dot(k_ref[0, 0, pl.ds(pl.multiple_of(kt * t, t), t), :], qt)

    def body(kt, s):
        s_next = scores(kt + 1)
        _flash_update(s, m_sc, l_sc, acc_sc, vt_ref[0, 0, :, pl.ds(pl.multiple_of(kt * t, t), t)])
        return s_next

    s = lax.fori_loop(0, qi, body, scores(0))
    kpos = lax.broadcasted_iota(jnp.int32, (t, 1), 0)
    qpos = lax.broadcasted_iota(jnp.int32, (1, t), 1)
    s = jnp.where(kpos <= qpos, s, NEG)
    _flash_update(s, m_sc, l_sc, acc_sc, vt_ref[0, 0, :, pl.ds(pl.multiple_of(qi * t, t), t)])
    o_ref[0] = (acc_sc[...] / l_sc[...]).T.astype(o_ref.dtype)


def _mla_layer(x, shift, scale_v, gate, cos, sin, w_in, g_q, w_uq, g_kv, w_ukv, w_o, ln_g, ln_b):
    B, S, D = x.shape
    H, dn, dr, dv = MLA_HEADS, MLA_DN, MLA_DR, MLA_DV
    half = dr // 2
    kr = w_in[:, MLA_QR + MLA_KVR:]
    win_p = jnp.concatenate([w_in, _rot_half_cols(kr, half)], axis=1).astype(BF16)
    wq = w_uq.reshape(MLA_QR, H, dn + dr)
    wq = jnp.concatenate([wq, _rot_half_cols(wq[..., dn:], half)], axis=-1)
    wqt = jnp.transpose(wq, (1, 2, 0)).astype(BF16)
    wkv = w_ukv.reshape(MLA_KVR, H, dn + dv)
    wk = jnp.transpose(wkv[..., :dn], (1, 0, 2)).astype(BF16)
    wvt = jnp.transpose(wkv[..., dn:], (1, 2, 0)).astype(BF16)
    tab = jnp.concatenate([cos, cos, sin, sin], axis=-1)
    tab_t = jnp.swapaxes(tab, 1, 2)
    ts = min(512, S)
    nw = win_p.shape[1]
    hd = dn + 2 * dr
    qt, k, vt = pl.pallas_call(
        functools.partial(_mla_front_kernel, scale=LOG2E * (dn + dr) ** -0.5),
        grid=(B, S // ts),
        in_specs=[pl.BlockSpec((1, ts, D), lambda b, i: (b, i, 0)),
                  pl.BlockSpec((1, 1, D), lambda b, i: (b, 0, 0)),
                  pl.BlockSpec((1, 1, D), lambda b, i: (b, 0, 0)),
                  pl.BlockSpec((D, nw), lambda b, i: (0, 0)),
                  pl.BlockSpec((1, MLA_QR), lambda b, i: (0, 0)),
                  pl.BlockSpec((1, MLA_KVR), lambda b, i: (0, 0)),
                  pl.BlockSpec((H, hd, MLA_QR), lambda b, i: (0, 0, 0)),
                  pl.BlockSpec((H, MLA_KVR, dn), lambda b, i: (0, 0, 0)),
                  pl.BlockSpec((H, dv, MLA_KVR), lambda b, i: (0, 0, 0)),
                  pl.BlockSpec((1, ts, LANES), lambda b, i: (b, i, 0)),
                  pl.BlockSpec((1, LANES, ts), lambda b, i: (b, 0, i))],
        out_specs=[pl.BlockSpec((1, H, hd, ts), lambda b, i: (b, 0, 0, i)),
                   pl.BlockSpec((1, H, ts, hd), lambda b, i: (b, 0, i, 0)),
                   pl.BlockSpec((1, H, dv, ts), lambda b, i: (b, 0, 0, i))],
        out_shape=[jax.ShapeDtypeStruct((B, H, hd, S), BF16),
                   jax.ShapeDtypeStruct((B, H, S, hd), BF16),
                   jax.ShapeDtypeStruct((B, H, dv, S), BF16)],
        compiler_params=_cp(("arbitrary", "arbitrary"), 48),
        name="mla_front",
    )(x, scale_v, shift, win_p, g_q.reshape(1, -1), g_kv.reshape(1, -1), wqt, wk, wvt, tab, tab_t)

    t = min(512, S)
    o = pl.pallas_call(
        functools.partial(_flash_causal_kernel, t=t),
        grid=(B, H, S // t),
        in_specs=[pl.BlockSpec((1, 1, hd, t), lambda b, h, i: (b, h, 0, i)),
                  pl.BlockSpec((1, 1, S, hd), lambda b, h, i: (b, h, 0, 0)),
                  pl.BlockSpec((1, 1, dv, S), lambda b, h, i: (b, h, 0, 0))],
        out_specs=pl.BlockSpec((1, t, dv), lambda b, h, i: (b, i, h)),
        out_shape=jax.ShapeDtypeStruct((B, S, H * dv), BF16),
        scratch_shapes=[pltpu.VMEM((1, t), F32), pltpu.VMEM((1, t), F32), pltpu.VMEM((dv, t), F32)],
        compiler_params=_cp(("arbitrary", "arbitrary", "arbitrary"), 32),
        name="mla_attn",
    )(qt, k, vt)
    return _out_ln(o, w_o, x, gate, ln_g, ln_b)


def _proj_kernel(*refs, segs, chunk, has_rope):
    x_ref, sc_ref, sh_ref, w_ref, wt_ref = refs[:5]
    n_in = 9 if has_rope else 5
    outs = refs[n_in:]
    hb = (x_ref[0] * (1.0 + sc_ref[0]) + sh_ref[0]).astype(BF16)
    if has_rope:
        cos, sin = refs[5][0], refs[6][0]
        cos_t, sin_t = refs[7][0], refs[8][0]
    half = LANES // 2
    for (c0, width, kind, scale, transposed), o_ref in zip(segs, outs):
        for j in range(0, width, chunk):
            cw = min(chunk, width - j)
            if transposed:
                r = _nt_dot(wt_ref[c0 + j:c0 + j + cw, :], hb)
            else:
                r = _dot(hb, w_ref[:, c0 + j:c0 + j + cw])
            if kind == "rope":
                parts = []
                for t in range(0, cw, LANES):
                    if transposed:
                        u = r[t:t + LANES]
                        parts.append(u * cos_t + pltpu.roll(u, half, axis=0) * sin_t)
                    else:
                        u = r[:, t:t + LANES]
                        parts.append(u * cos + pltpu.roll(u, half, axis=1) * sin)
                r = jnp.concatenate(parts, axis=0 if transposed else 1) if len(parts) > 1 else parts[0]
            elif kind == "sigmoid":
                r = jax.nn.sigmoid(r)
            if scale != 1.0:
                r = r * scale
            if transposed:
                o_ref[0, j:j + cw, :] = r.astype(o_ref.dtype)
            else:
                o_ref[0, :, j:j + cw] = r.astype(o_ref.dtype)


def _mod_proj(x, shift, scale_v, w, wt, segs, out_dtypes, rope=None, ts=512, chunk=256):
    B, S, D = x.shape
    ts = min(ts, S)
    in_specs = [pl.BlockSpec((1, ts, D), lambda b, i: (b, i, 0)),
                pl.BlockSpec((1, 1, D), lambda b, i: (b, 0, 0)),
                pl.BlockSpec((1, 1, D), lambda b, i: (b, 0, 0)),
                pl.BlockSpec(w.shape, lambda b, i: (0, 0)),
                pl.BlockSpec(wt.shape, lambda b, i: (0, 0))]
    args = [x, scale_v, shift, w, wt]
    if rope is not None:
        cos, sin = rope
        in_specs += [pl.BlockSpec((1, ts, LANES), lambda b, i: (b, i, 0))] * 2
        in_specs += [pl.BlockSpec((1, LANES, ts), lambda b, i: (b, 0, i))] * 2
        args += [cos, sin, jnp.swapaxes(cos, 1, 2), jnp.swapaxes(sin, 1, 2)]
    out_specs, out_shape = [], []
    for sg, dt in zip(segs, out_dtypes):
        if sg[4]:
            out_specs.append(pl.BlockSpec((1, sg[1], ts), lambda b, i: (b, 0, i)))
            out_shape.append(jax.ShapeDtypeStruct((B, sg[1], S), dt))
        else:
            out_specs.append(pl.BlockSpec((1, ts, sg[1]), lambda b, i: (b, i, 0)))
            out_shape.append(jax.ShapeDtypeStruct((B, S, sg[1]), dt))
    return pl.pallas_call(
        functools.partial(_proj_kernel, segs=tuple(segs), chunk=chunk, has_rope=rope is not None),
        grid=(B, S // ts),
        in_specs=in_specs,
        out_specs=out_specs,
        out_shape=out_shape,
        compiler_params=_cp(("arbitrary", "arbitrary"), 48),
        name="mod_proj",
    )(*args)


def _nsa_cmp_kernel(kc_ref, vc_ref, pek_ref, pev_ref, wk1_ref, wk2_ref, wv1_ref, wv2t_ref, ko_ref, vto_ref):
    half = NSA_CMP_BLOCK // 2
    nchunk = kc_ref.shape[1]

    def hidden(src, pe_ref, w1_ref, kh):
        a = jnp.zeros((nchunk, w1_ref.shape[2]), F32)
        bm = jnp.zeros((nchunk, w1_ref.shape[2]), F32)
        for l in range(half):
            c0 = (l * NSA_KVH + kh) * NSA_DH
            t = src[0, :, c0:c0 + NSA_DH].astype(F32)
            a = a + _dot((t + pe_ref[l:l + 1, :]).astype(BF16), w1_ref[l])
            bm = bm + _dot((t + pe_ref[half + l:half + l + 1, :]).astype(BF16), w1_ref[half + l])
        hid = a + pltpu.roll(bm, nchunk - 1, axis=0)
        return (hid * jax.nn.sigmoid(hid)).astype(BF16)

    rowi = lax.broadcasted_iota(jnp.int32, (nchunk, 1), 0)
    coli = lax.broadcasted_iota(jnp.int32, (1, nchunk), 1)
    for kh in range(NSA_KVH):
        out = _dot(hidden(kc_ref, pek_ref, wk1_ref, kh), wk2_ref[...])
        ko_ref[0, kh] = jnp.where(rowi < nchunk - 1, out, 0.0).astype(ko_ref.dtype)
        out_t = _nt_dot(wv2t_ref[...], hidden(vc_ref, pev_ref, wv1_ref, kh))
        vto_ref[0, kh] = jnp.where(coli < nchunk - 1, out_t, 0.0).astype(vto_ref.dtype)


def _nsa_attn_kernel(qt_ref, kc_ref, vct_ref, ks_ref, vst_ref, kw_ref, vwt_ref, gt_ref, ovt_ref, e_ref, o_ref,
                     q4_sc, m_sc, l_sc, acc_sc, *, tq, tk, span, n_slc):
    qi = pl.program_id(2)
    G = NSA_G
    for g in range(G):
        q4_sc[:, g * tq:(g + 1) * tq] = qt_ref[0, g * NSA_DH:(g + 1) * NSA_DH, :]
    q4 = q4_sc[...]
    qpos = qi * tq + lax.broadcasted_iota(jnp.int32, (1, tq), 1)

    def tile_g(a):
        return jnp.concatenate([a] * G, axis=1)

    qpos4 = tile_g(qpos)

    n_cmp = kc_ref.shape[2]
    nid = lax.broadcasted_iota(jnp.int32, (n_cmp, 1), 0)
    cmask = (nid * NSA_CMP_STRIDE + (NSA_CMP_BLOCK - 1) <= qpos4) & (nid < n_cmp - 1)
    s = jnp.where(cmask, _dot(kc_ref[0, 0], q4), NEG)
    p = jnp.where(cmask, jnp.exp2(s - jnp.max(s, axis=0, keepdims=True)), 0.0)
    l = jnp.sum(p, axis=0, keepdims=True)
    p_cmp = p / jnp.where(l > 0.0, l, 1.0)
    o_cmp = _dot(vct_ref[0, 0], p_cmp.astype(BF16))

    psum = p_cmp[:, 0:tq]
    for g in range(1, G):
        psum = psum + p_cmp[:, g * tq:(g + 1) * tq]
    p_hi, p_lo = _split_bf16(psum)
    imp = _dot(ovt_ref[...], p_hi) + _dot(ovt_ref[...], p_lo)
    nb = ovt_ref.shape[0]
    blk = lax.broadcasted_iota(jnp.int32, (nb, 1), 0)
    cur = jnp.right_shift(qpos, SLC_SHIFT)
    forced = (blk == 0) | (blk == cur) | (blk == cur - 1)
    valid = (blk * NSA_SLC_BLOCK <= qpos) & (blk < n_slc)
    work = jnp.where(valid, imp + jnp.where(forced, NSA_BONUS, 0.0), -jnp.inf)
    beaten = jnp.zeros((nb, tq), F32)
    for j in range(n_slc):
        wj = work[j:j + 1, :]
        beaten = beaten + jnp.where((wj > work) | ((wj == work) & (blk > j)), 1.0, 0.0)
    sel = jnp.where(beaten < float(min(NSA_TOPK, n_slc)), 1.0, 0.0).astype(BF16)

    m_sc[...] = jnp.full(m_sc.shape, NEG, F32)
    l_sc[...] = jnp.zeros(l_sc.shape, F32)
    acc_sc[...] = jnp.zeros(acc_sc.shape, F32)
    n_kt = ((qi + 1) * tq + tk - 1) // tk

    def scores(kt):
        return _dot(ks_ref[0, pl.ds(pl.multiple_of(kt * tk, tk), tk), :], q4)

    def body(kt, s):
        s_next = scores(jnp.minimum(kt + 1, n_kt - 1))
        ks = pl.multiple_of(kt * tk, tk)
        chosen = _dot(e_ref[pl.ds(ks, tk), :], sel)
        kpos = ks + lax.broadcasted_iota(jnp.int32, (tk, 1), 0)
        bias = jnp.where((chosen > 0.5) & (kpos <= qpos), 0.0, NEG)
        _flash_update(s + tile_g(bias), m_sc, l_sc, acc_sc, vst_ref[0, :, pl.ds(ks, tk)])
        return s_next

    lax.fori_loop(0, n_kt, body, scores(0))
    o_slc = acc_sc[...] / l_sc[...]

    kstart = pl.multiple_of(jnp.maximum(qi * tq + tq - span, 0), tq)
    diff = qpos - (kstart + lax.broadcasted_iota(jnp.int32, (span, 1), 0))
    bias = jnp.where((diff >= 0) & (diff < NSA_WINDOW), 0.0, NEG)
    s = _dot(kw_ref[0, pl.ds(kstart, span), :], q4) + tile_g(bias)
    p = jnp.exp2(s - jnp.max(s, axis=0, keepdims=True))
    o_win = _dot(vwt_ref[0, :, pl.ds(kstart, span)], p.astype(BF16)) / jnp.sum(p, axis=0, keepdims=True)

    gt = gt_ref[0]
    for g in range(G):
        cols = slice(g * tq, (g + 1) * tq)
        o = (gt[3 * g:3 * g + 1] * o_cmp[:, cols] + gt[3 * g + 1:3 * g + 2] * o_slc[:, cols]
             + gt[3 * g + 2:3 * g + 3] * o_win[:, cols])
        o_ref[0, :, g * NSA_DH:(g + 1) * NSA_DH] = o.T.astype(o_ref.dtype)


def _nsa_layer(x, shift, scale_v, gate, cos, sin, w_in, pe_k, pe_v, w_ck1, w_ck2, w_cv1, w_cv2, w_o,
               ln_g, ln_b):
    B, S, D = x.shape
    H, KH, Dh, G = NSA_HEADS, NSA_KVH, NSA_DH, NSA_G
    kvw = KH * Dh
    cuts = [int(v) for v in np.cumsum([H * Dh] + [kvw] * 6)]
    wq, wkc, wvc, wks, wvs, wkw, wvw, wgl = jnp.split(w_in, cuts, axis=1)
    wgl = jnp.pad(wgl.reshape(D, KH, G * 3), ((0, 0), (0, 0), (0, GATE_ROWS - G * 3))).reshape(D, KH * GATE_ROWS)
    w = jnp.concatenate([wkc, wks, wkw, wvc], axis=1).astype(BF16)
    wt = jnp.concatenate([wq, wvs, wvw, wgl], axis=1).T.astype(BF16)
    segs = [(0, H * Dh, "rope", LOG2E * Dh ** -0.5, True),
            (0, kvw, "rope", 1.0, False), (kvw, kvw, "rope", 1.0, False), (2 * kvw, kvw, "rope", 1.0, False),
            (3 * kvw, kvw, "plain", 1.0, False),
            (H * Dh, kvw, "plain", 1.0, True), (H * Dh + kvw, kvw, "plain", 1.0, True),
            (H * Dh + 2 * kvw, KH * GATE_ROWS, "sigmoid", 1.0, True)]
    rope = (jnp.concatenate([cos, cos], axis=-1), jnp.concatenate([-sin, sin], axis=-1))
    qt, kc, ks, kw, vc, vst, vwt, gates_t = _mod_proj(
        x, shift, scale_v, w, wt, segs, [BF16] * 7 + [F32], rope=rope)

    st = NSA_CMP_STRIDE
    nchunk = S // st
    kc2 = kc.reshape(B, nchunk, st * kvw)
    vc2 = vc.reshape(B, nchunk, st * kvw)
    hid = w_ck1.shape[-1]
    full = lambda shape: pl.BlockSpec(shape, lambda b: (0,) * len(shape))
    k_cmp, v_cmp_t = pl.pallas_call(
        _nsa_cmp_kernel,
        grid=(B,),
        in_specs=[pl.BlockSpec((1, nchunk, st * kvw), lambda b: (b, 0, 0)),
                  pl.BlockSpec((1, nchunk, st * kvw), lambda b: (b, 0, 0)),
                  full((NSA_CMP_BLOCK, Dh)), full((NSA_CMP_BLOCK, Dh)),
                  full((NSA_CMP_BLOCK, Dh, hid)), full((hid, Dh)),
                  full((NSA_CMP_BLOCK, Dh, hid)), full((Dh, hid))],
        out_specs=[pl.BlockSpec((1, KH, nchunk, Dh), lambda b: (b, 0, 0, 0)),
                   pl.BlockSpec((1, KH, Dh, nchunk), lambda b: (b, 0, 0, 0))],
        out_shape=[jax.ShapeDtypeStruct((B, KH, nchunk, Dh), BF16),
                   jax.ShapeDtypeStruct((B, KH, Dh, nchunk), BF16)],
        compiler_params=_cp(("arbitrary",), 32),
        name="nsa_cmp",
    )(kc2, vc2, pe_k, pe_v, w_ck1.astype(BF16), w_ck2.astype(BF16), w_cv1.astype(BF16), w_cv2.T.astype(BF16))

    n_slc = S // NSA_SLC_BLOCK
    nb = 32
    assert n_slc <= nb and nchunk % 8 == 0
    cmp_start = np.arange(nchunk) * st
    slc_start = np.arange(n_slc) * NSA_SLC_BLOCK
    ov = np.clip(np.minimum(cmp_start[:, None] + NSA_CMP_BLOCK, slc_start[None, :] + NSA_SLC_BLOCK)
                 - np.maximum(cmp_start[:, None], slc_start[None, :]), 0, None).astype(np.float32) / NSA_CMP_BLOCK
    ov[nchunk - 1:] = 0.0
    ov_t = np.zeros((nb, nchunk), np.float32)
    ov_t[:n_slc] = ov.T
    key_blk = np.zeros((S, nb), np.float32)
    key_blk[np.arange(S), np.arange(S) // NSA_SLC_BLOCK] = 1.0

    tq = min(128, S)
    tk = min(256, S)
    span = min(NSA_WINDOW + tq, S)
    k_spec = pl.BlockSpec((1, S, Dh), lambda b, kh, i: (b, 0, kh))
    vt_spec = pl.BlockSpec((1, Dh, S), lambda b, kh, i: (b, kh, 0))
    o = pl.pallas_call(
        functools.partial(_nsa_attn_kernel, tq=tq, tk=tk, span=span, n_slc=n_slc),
        grid=(B, KH, S // tq),
        in_specs=[pl.BlockSpec((1, G * Dh, tq), lambda b, kh, i: (b, kh, i)),
                  pl.BlockSpec((1, 1, nchunk, Dh), lambda b, kh, i: (b, kh, 0, 0)),
                  pl.BlockSpec((1, 1, Dh, nchunk), lambda b, kh, i: (b, kh, 0, 0)),
                  k_spec, vt_spec, k_spec, vt_spec,
                  pl.BlockSpec((1, GATE_ROWS, tq), lambda b, kh, i: (b, kh, i)),
                  pl.BlockSpec((nb, nchunk), lambda b, kh, i: (0, 0)),
                  pl.BlockSpec((S, nb), lambda b, kh, i: (0, 0))],
        out_specs=pl.BlockSpec((1, tq, G * Dh), lambda b, kh, i: (b, i, kh)),
        out_shape=jax.ShapeDtypeStruct((B, S, H * Dh), BF16),
        scratch_shapes=[pltpu.VMEM((Dh, G * tq), BF16), pltpu.VMEM((1, G * tq), F32),
                        pltpu.VMEM((1, G * tq), F32), pltpu.VMEM((Dh, G * tq), F32)],
        compiler_params=_cp(("arbitrary", "arbitrary", "arbitrary"), 32),
        name="nsa_attn",
    )(qt, k_cmp, v_cmp_t, ks, vst, kw, vwt, gates_t, jnp.asarray(ov_t, BF16), jnp.asarray(key_blk, BF16))
    return _out_ln(o, w_o, x, gate, ln_g, ln_b)


def _sb_attn_kernel(qt_ref, k_ref, vt_ref, u_ref, o_ref, r_sc, acc_sc, *, tq, tk):
    qi = pl.program_id(2)
    qt = qt_ref[0]
    r_sc[...] = jnp.zeros(r_sc.shape, F32)
    acc_sc[...] = jnp.zeros(acc_sc.shape, F32)
    u = u_ref[...]
    nd = tq // tk
    last = (qi + 1) * nd - 1

    def scores(kt):
        return _dot(k_ref[0, pl.ds(pl.multiple_of(kt * tk, tk), tk), :], qt)

    def step(kt, z, masked):
        ks = pl.multiple_of(kt * tk, tk)
        lk = -(jnp.maximum(z, 0.0) + jnp.log2(1.0 + jnp.exp2(-jnp.abs(z))))
        if masked:
            kpos = ks + lax.broadcasted_iota(jnp.int32, (tk, 1), 0)
            qpos = qi * tq + lax.broadcasted_iota(jnp.int32, (1, tq), 1)
            strict = kpos < qpos
            lk = jnp.where(strict, lk, 0.0)
        hi, lo = _split_bf16(lk)
        between = r_sc[...] + _dot(u, hi) + _dot(u, lo)
        a = jnp.exp2(lk + z + between)
        if masked:
            a = jnp.where(strict, a, 0.0)
        acc_sc[...] += _dot(vt_ref[0, :, pl.ds(ks, tk)], a.astype(BF16))
        r_sc[...] += jnp.sum(lk, axis=0, keepdims=True)

    z = scores(last)
    for i in range(nd):
        z_next = scores(jnp.maximum(last - i - 1, 0))
        step(last - i, z, True)
        z = z_next
    n_full = qi * nd

    def body(i, z):
        kt = n_full - 1 - i
        z_next = scores(jnp.maximum(kt - 1, 0))
        step(kt, z, False)
        return z_next

    lax.fori_loop(0, n_full, body, z)
    o_ref[0] = acc_sc[...].T.astype(o_ref.dtype)


def _sb_layer(x, shift, scale_v, gate, w_qkv, w_o, ln_g, ln_b):
    B, S, D = x.shape
    H, d = SB_HEADS, SB_DH
    hw = H * d
    w = w_qkv[:, hw:2 * hw].astype(BF16)
    wt = jnp.concatenate([w_qkv[:, :hw], w_qkv[:, 2 * hw:]], axis=1).T.astype(BF16)
    segs = [(0, hw, "plain", LOG2E * d ** -0.5, True), (0, hw, "plain", 1.0, False), (hw, hw, "plain", 1.0, True)]
    qt, k, vt = _mod_proj(x, shift, scale_v, w, wt, segs, [BF16] * 3)
    tq = min(512, S)
    tk = min(256, S)
    tri = np.triu(np.ones((tk, tk), np.float32), 1)
    o = pl.pallas_call(
        functools.partial(_sb_attn_kernel, tq=tq, tk=tk),
        grid=(B, H, S // tq),
        in_specs=[pl.BlockSpec((1, d, tq), lambda b, h, i: (b, h, i)),
                  pl.BlockSpec((1, S, d), lambda b, h, i: (b, 0, h)),
                  pl.BlockSpec((1, d, S), lambda b, h, i: (b, h, 0)),
                  pl.BlockSpec((tk, tk), lambda b, h, i: (0, 0))],
        out_specs=pl.BlockSpec((1, tq, d), lambda b, h, i: (b, i, h)),
        out_shape=jax.ShapeDtypeStruct((B, S, hw), BF16),
        scratch_shapes=[pltpu.VMEM((1, tq), F32), pltpu.VMEM((d, tq), F32)],
        compiler_params=_cp(("arbitrary", "arbitrary", "arbitrary"), 32),
        name="sb_attn",
    )(qt, k, vt, jnp.asarray(tri, BF16))
    return _out_ln(o, w_o, x, gate, ln_g, ln_b)


def _router_kernel(x_ref, sc_ref, sh_ref, whi_ref, wlo_ref, b_ref, tri_ref, hb_ref, info_ref, cnt_ref,
                   base_sc):
    first = (pl.program_id(0) == 0) & (pl.program_id(1) == 0)

    @pl.when(first)
    def _():
        base_sc[...] = jnp.zeros(base_sc.shape, F32)

    h = x_ref[0] * (1.0 + sc_ref[0]) + sh_ref[0]
    h_hi, h_lo = _split_bf16(h)
    hb_ref[0] = _pack_pairs(h)
    logit = (_dot(h_hi, whi_ref[...]) + _dot(h_lo, whi_ref[...]) + _dot(h_hi, wlo_ref[...])
             + b_ref[...])
    tm = logit.shape[0]
    lane = lax.broadcasted_iota(jnp.int32, (1, LANES), 1)
    lane_f = lane.astype(F32)
    big = float(LANES)

    def first_max(vals):
        mx = jnp.max(vals, axis=1, keepdims=True)
        idx = jnp.min(jnp.where(vals == mx, lane_f, big), axis=1, keepdims=True)
        return mx, idx

    gl = jnp.where(lane < MOE_GROUPS, logit, -jnp.inf)
    gmax, g_idx = first_max(gl)
    g_w = 1.0 / jnp.sum(jnp.exp(gl - gmax), axis=1, keepdims=True)
    e_lane = lane - MOE_GROUPS
    in_grp = (e_lane >= 0) & (e_lane < MOE_E) & (jnp.right_shift(e_lane, EPG_SHIFT).astype(F32) == g_idx)
    el = jnp.where(in_grp, logit, -jnp.inf)
    emax = jnp.max(el, axis=1, keepdims=True)
    ep = jnp.exp(el - emax)
    prob = jnp.where(in_grp, ep / jnp.sum(ep, axis=1, keepdims=True), -1.0)
    p1, i1 = first_max(prob)
    p2, i2 = first_max(jnp.where(lane_f == i1, -1.0, prob))
    e1 = i1 - float(MOE_GROUPS)
    e2 = i2 - float(MOE_GROUPS)
    w1 = g_w * p1 / (p1 + p2)
    w2 = g_w * p2 / (p1 + p2)

    hot1 = lane_f == e1
    hot2 = lane_f == e2
    cnt = jnp.where(hot1 | hot2, 1.0, 0.0)
    before = base_sc[...] + _dot(tri_ref[...], cnt.astype(BF16))
    r1 = jnp.sum(jnp.where(hot1, before, 0.0), axis=1, keepdims=True)
    r2 = jnp.sum(jnp.where(hot2, before, 0.0), axis=1, keepdims=True)
    base_sc[...] += jnp.sum(cnt, axis=0, keepdims=True)
    cnt_ref[...] = base_sc[...]

    info = jnp.zeros((tm, LANES), F32)
    for k, val in enumerate((e1, e2, r1, r2, w1, w2)):
        info = jnp.where(lane == k, val, info)
    info_ref[0] = info


def _ffn_kernel(be_ref, nu_ref, x_ref, wg_ref, wu_ref, wd_ref, y_ref, wg_sc, wu_sc, wd_sc):
    i = pl.program_id(0)
    new_expert = (i == 0) | (be_ref[i] != be_ref[jnp.maximum(i - 1, 0)])

    @pl.when(new_expert & (i < nu_ref[0]))
    def _():
        wg_sc[...] = wg_ref[0, 0].astype(BF16)
        wu_sc[...] = wu_ref[0, 0].astype(BF16)
        wd_sc[...] = wd_ref[0, 0].astype(BF16)

    @pl.when(i < nu_ref[0])
    def _():
        xb = _unpack_pairs(x_ref[...]).astype(BF16)
        g = _dot(xb, wg_sc[...])
        u = _dot(xb, wu_sc[...])
        a = (g * jax.nn.sigmoid(g) * u).astype(BF16)
        y_ref[...] = _pack_pairs(_dot(a, wd_sc[...]))

    @pl.when(i >= nu_ref[0])
    def _():
        y_ref[...] = jnp.zeros(y_ref.shape, y_ref.dtype)


def _sc_gather(table, idx):
    n, d = idx.shape[0], table.shape[1]
    workers = SC_CORES * SC_SUBCORES
    per_w = n // workers
    assert per_w * workers == n and per_w % SC_CHUNK == 0
    mesh = plsc.VectorSubcoreMesh(core_axis_name="c", subcore_axis_name="s")

    @functools.partial(
        pl.kernel, mesh=mesh,
        out_type=jax.ShapeDtypeStruct((n, d), table.dtype),
        scratch_types=[pltpu.VMEM((SC_CHUNK,), jnp.int32),
                       pltpu.VMEM((SC_CHUNK, d), table.dtype),
                       pltpu.SemaphoreType.DMA],
        name="sc_gather")
    def gather(table_hbm, idx_hbm, out_hbm, idx_v, rows_v, sem):
        wid = lax.axis_index("s") * SC_CORES + lax.axis_index("c")

        @pl.loop(0, per_w // SC_CHUNK)
        def _(j):
            base = pl.multiple_of(wid * per_w + j * SC_CHUNK, SC_CHUNK)
            pltpu.sync_copy(idx_hbm.at[pl.ds(base, SC_CHUNK)], idx_v)
            pltpu.async_copy(table_hbm.at[idx_v], rows_v, sem).wait()
            pltpu.sync_copy(rows_v, out_hbm.at[pl.ds(base, SC_CHUNK)])

    return gather(table, idx)


def _moe_out_kernel(x_ref, y1_ref, y2_ref, info_ref, gate_ref, g_ref, b_ref, o_ref):
    info = info_ref[0]
    y = info[:, 4:5] * _unpack_pairs(y1_ref[0, 0]) + info[:, 5:6] * _unpack_pairs(y2_ref[0, 0])
    z = ALPHA * x_ref[0] + gate_ref[0] * y
    o_ref[0] = _layer_norm(z, g_ref[...], b_ref[...])


def _moe_out(x, y12, info, gate, g, b, ts=512):
    B, S, D = x.shape
    ts = min(ts, S)
    tok = pl.BlockSpec((1, ts, D), lambda bi, i: (bi, i, 0))
    return pl.pallas_call(
        _moe_out_kernel,
        grid=(B, S // ts),
        in_specs=[tok,
                  pl.BlockSpec((1, 1, ts, D // 2), lambda bi, i: (0, bi, i, 0)),
                  pl.BlockSpec((1, 1, ts, D // 2), lambda bi, i: (1, bi, i, 0)),
                  pl.BlockSpec((1, ts, LANES), lambda bi, i: (bi, i, 0)),
                  pl.BlockSpec((1, 1, D), lambda bi, i: (bi, 0, 0)),
                  pl.BlockSpec((1, D), lambda bi, i: (0, 0)),
                  pl.BlockSpec((1, D), lambda bi, i: (0, 0))],
        out_specs=tok,
        out_shape=jax.ShapeDtypeStruct((B, S, D), F32),
        compiler_params=_cp(("arbitrary", "arbitrary"), 40),
        name="moe_out",
    )(x, y12, y12, info, gate, g.reshape(1, D), b.reshape(1, D))


def _moe_layer(x, shift, scale_v, gate, w_grp, b_grp, w_rt, b_rt, layer, w_gate, w_up, w_down, ln_g, ln_b):
    B, S, D = x.shape
    T = B * S
    E, R, F = MOE_E, MOE_ROWS, w_gate.shape[-1]
    w_r = jnp.pad(jnp.concatenate([w_grp, w_rt], axis=1), ((0, 0), (0, LANES - MOE_GROUPS - E)))
    w_hi = w_r.astype(BF16)
    w_lo = (w_r - w_hi.astype(F32)).astype(BF16)
    b_r = jnp.pad(jnp.concatenate([b_grp, b_rt]), (0, LANES - MOE_GROUPS - E)).reshape(1, LANES)
    tm = min(256, S)
    tri = jnp.asarray(np.tril(np.ones((tm, tm), np.float32), -1), BF16)
    hb, info, cnt = pl.pallas_call(
        _router_kernel,
        grid=(B, S // tm),
        in_specs=[pl.BlockSpec((1, tm, D), lambda b, i: (b, i, 0)),
                  pl.BlockSpec((1, 1, D), lambda b, i: (b, 0, 0)),
                  pl.BlockSpec((1, 1, D), lambda b, i: (b, 0, 0)),
                  pl.BlockSpec((D, LANES), lambda b, i: (0, 0)),
                  pl.BlockSpec((D, LANES), lambda b, i: (0, 0)),
                  pl.BlockSpec((1, LANES), lambda b, i: (0, 0)),
                  pl.BlockSpec((tm, tm), lambda b, i: (0, 0))],
        out_specs=[pl.BlockSpec((1, tm, D // 2), lambda b, i: (b, i, 0)),
                   pl.BlockSpec((1, tm, LANES), lambda b, i: (b, i, 0)),
                   pl.BlockSpec((1, LANES), lambda b, i: (0, 0))],
        out_shape=[jax.ShapeDtypeStruct((B, S, D // 2), jnp.uint32),
                   jax.ShapeDtypeStruct((B, S, LANES), F32),
                   jax.ShapeDtypeStruct((1, LANES), F32)],
        scratch_shapes=[pltpu.VMEM((1, LANES), F32)],
        compiler_params=_cp(("arbitrary", "arbitrary"), 32),
        name="moe_router",
    )(x, scale_v, shift, w_hi, w_lo, b_r, tri)

    info2 = info.reshape(T, LANES)
    experts = info2[:, 0:2].astype(jnp.int32)
    ranks = info2[:, 2:4].astype(jnp.int32)
    counts = cnt[0, :E].astype(jnp.int32)
    padded = (counts + R - 1) // R * R
    ends = jnp.cumsum(padded)
    starts = ends - padded
    dest = starts[experts] + ranks
    n_blk = (T * 2) // R + E
    P = n_blk * R
    row_tok = (jnp.arange(P, dtype=jnp.int32) % T).at[dest.reshape(-1)].set(
        jnp.repeat(jnp.arange(T, dtype=jnp.int32), 2))
    blk_exp = jnp.minimum(jnp.sum(jnp.arange(n_blk)[:, None] * R >= ends[None, :], axis=1), E - 1).astype(jnp.int32)
    n_used = (ends[-1] // R).astype(jnp.int32).reshape(1)
    x_rows = _sc_gather(hb.reshape(T, D // 2), row_tok)

    y_rows = pl.pallas_call(
        _ffn_kernel,
        grid_spec=pltpu.PrefetchScalarGridSpec(
            num_scalar_prefetch=2,
            grid=(n_blk,),
            in_specs=[pl.BlockSpec((R, D // 2), lambda i, be, nu: (i, 0)),
                      pl.BlockSpec((1, 1, D, F), lambda i, be, nu: (layer, be[i], 0, 0)),
                      pl.BlockSpec((1, 1, D, F), lambda i, be, nu: (layer, be[i], 0, 0)),
                      pl.BlockSpec((1, 1, F, D), lambda i, be, nu: (layer, be[i], 0, 0))],
            out_specs=pl.BlockSpec((R, D // 2), lambda i, be, nu: (i, 0)),
            scratch_shapes=[pltpu.VMEM((D, F), BF16), pltpu.VMEM((D, F), BF16), pltpu.VMEM((F, D), BF16)]),
        out_shape=jax.ShapeDtypeStruct((P, D // 2), jnp.uint32),
        compiler_params=_cp(("arbitrary",), 48),
        name="moe_ffn",
    )(blk_exp, n_used, x_rows, w_gate, w_up, w_down)

    y12 = _sc_gather(y_rows, dest.T.reshape(-1)).reshape(2, B, S, D // 2)
    return _moe_out(x, y12, info, gate, ln_g, ln_b)


def kernel(x, c, positions, ada_w, ada_b, ln_g, ln_b, mla_w_in, mla_g_q, mla_w_uq, mla_g_kv, mla_w_ukv, mla_w_o, nsa_w_in, nsa_pe_k, nsa_pe_v, nsa_w_ck1, nsa_w_ck2, nsa_w_cv1, nsa_w_cv2, nsa_w_o, sb_w_qkv, sb_w_o, moe_w_grp, moe_b_grp, moe_w_rt, moe_b_rt, moe_w_gate, moe_w_up, moe_w_down):
    D = x.shape[-1]
    depth = ada_w.shape[0]
    cos_mla, sin_mla = _rope_tables(positions, MLA_DR)
    cos_nsa, sin_nsa = _rope_tables(positions, NSA_DH)
    mods = _ada_all(c, ada_w, ada_b)

    def mod(i, sub):
        m = mods[2 * i + sub][:, None, :]
        return m[..., :D], m[..., D:2 * D], m[..., 2 * D:]

    for i in range(depth):
        kind, j = i % N_MIXERS, i // N_MIXERS
        shift, scale_v, gate = mod(i, 0)
        if kind == 0:
            x = _mla_layer(x, shift, scale_v, gate, cos_mla, sin_mla, mla_w_in[j], mla_g_q[j], mla_w_uq[j],
                           mla_g_kv[j], mla_w_ukv[j], mla_w_o[j], ln_g[i, 0], ln_b[i, 0])
        elif kind == 1:
            x = _nsa_layer(x, shift, scale_v, gate, cos_nsa, sin_nsa, nsa_w_in[j], nsa_pe_k[j], nsa_pe_v[j],
                           nsa_w_ck1[j], nsa_w_ck2[j], nsa_w_cv1[j], nsa_w_cv2[j], nsa_w_o[j],
                           ln_g[i, 0], ln_b[i, 0])
        else:
            x = _sb_layer(x, shift, scale_v, gate, sb_w_qkv[j], sb_w_o[j], ln_g[i, 0], ln_b[i, 0])
        shift, scale_v, gate = mod(i, 1)
        x = _moe_layer(x, shift, scale_v, gate, moe_w_grp[i], moe_b_grp[i], moe_w_rt[i], moe_b_rt[i],
                       i, moe_w_gate, moe_w_up, moe_w_down, ln_g[i, 1], ln_b[i, 1])
    return x
```

```python
import functools

import numpy as np
import jax
import jax.numpy as jnp
from jax import lax
from jax.experimental import pallas as pl
from jax.experimental.pallas import tpu as pltpu
from jax.experimental.pallas import tpu_sc as plsc

BF16 = jnp.bfloat16
F32 = jnp.float32

DEPTH = 4
N_MIXERS = 3
ROPE_THETA = 10000.0
LN_EPS = 1e-5
RMS_EPS = 1e-6
NEG = -1e30
LOG2E = 1.4426950408889634
ALPHA = (2 * DEPTH) ** 0.25

MLA_HEADS, MLA_DN, MLA_DR, MLA_DV = 8, 128, 64, 128
MLA_QR, MLA_KVR = 256, 256

NSA_HEADS, NSA_KVH, NSA_DH = 8, 2, 128
NSA_G = NSA_HEADS // NSA_KVH
NSA_CMP_BLOCK, NSA_CMP_STRIDE = 32, 16
NSA_SLC_BLOCK, NSA_TOPK = 64, 8
SLC_SHIFT = 6
NSA_WINDOW = 512
NSA_BONUS = 1e3
GATE_ROWS = 16

SB_HEADS, SB_DH = 8, 128
SB_EXIT_LOG2 = -160.0

MOE_GROUPS, MOE_EPG = 4, 8
EPG_SHIFT = 3
MOE_E = MOE_GROUPS * MOE_EPG
MOE_ROWS = 256
ROUTER_ROWS = 64
INFO_ROWS = 8

SC_CORES, SC_SUBCORES = 2, 16
SC_CHUNK = 64

LANES = 128
VMEM_BYTES_V7X = 64 * 1024 * 1024


def _cp(sem, vmem_mb):
    assert vmem_mb * 2**20 < VMEM_BYTES_V7X
    return pltpu.CompilerParams(dimension_semantics=sem, vmem_limit_bytes=vmem_mb * 2**20)


def _nt_dot(a, b):
    return lax.dot_general(a, b, (((1,), (1,)), ((), ())), preferred_element_type=F32)


def _dot(a, b):
    return jnp.dot(a, b, preferred_element_type=F32)


def _split_bf16(a):
    hi = a.astype(BF16)
    lo = (a - hi.astype(F32)).astype(BF16)
    return hi, lo


def _pack_pairs(a):
    n = a.shape[1] // 2
    hi = pltpu.bitcast(a[:, :n].astype(BF16).astype(F32), jnp.uint32)
    lo = pltpu.bitcast(a[:, n:].astype(BF16).astype(F32), jnp.uint32)
    return hi | (lo >> 16)


def _unpack_pairs(u):
    hi = pltpu.bitcast(u & jnp.uint32(0xFFFF0000), F32)
    lo = pltpu.bitcast(u << 16, F32)
    return jnp.concatenate([hi, lo], axis=1)


def _layer_norm(z, g, b):
    mu = jnp.mean(z, axis=-1, keepdims=True)
    d = z - mu
    var = jnp.mean(d * d, axis=-1, keepdims=True)
    return d * lax.rsqrt(var + LN_EPS) * g + b


def _ada_kernel(c_ref, w_ref, b_ref, o_ref):
    c = c_ref[...]
    s_hi, s_lo = _split_bf16(c * jax.nn.sigmoid(c))
    w_hi, w_lo = _split_bf16(w_ref[0])
    o_ref[0] = _dot(s_hi, w_hi) + _dot(s_lo, w_hi) + _dot(s_hi, w_lo) + b_ref[0]


def _ada_all(c, ada_w, ada_b):
    L, two, D, D3 = ada_w.shape
    B = c.shape[0]
    n = L * two
    tn = 1024
    return pl.pallas_call(
        _ada_kernel,
        grid=(n, D3 // tn),
        in_specs=[pl.BlockSpec((B, D), lambda i, j: (0, 0)),
                  pl.BlockSpec((1, D, tn), lambda i, j: (i, 0, j)),
                  pl.BlockSpec((1, 1, tn), lambda i, j: (i, 0, j))],
        out_specs=pl.BlockSpec((1, B, tn), lambda i, j: (i, 0, j)),
        out_shape=jax.ShapeDtypeStruct((n, B, D3), F32),
        compiler_params=_cp(("arbitrary", "arbitrary"), 32),
        name="ada",
    )(c, ada_w.reshape(n, D, D3), ada_b.reshape(n, 1, D3))


def _res_ln_kernel(x_ref, y_ref, gate_ref, g_ref, b_ref, o_ref):
    z = ALPHA * x_ref[0] + gate_ref[0] * y_ref[0]
    o_ref[0] = _layer_norm(z, g_ref[...], b_ref[...])


def _res_ln(x, y, gate, g, b, ts=512):
    B, S, D = x.shape
    ts = min(ts, S)
    tok = pl.BlockSpec((1, ts, D), lambda bi, i: (bi, i, 0))
    return pl.pallas_call(
        _res_ln_kernel,
        grid=(B, S // ts),
        in_specs=[tok, tok,
                  pl.BlockSpec((1, 1, D), lambda bi, i: (bi, 0, 0)),
                  pl.BlockSpec((1, D), lambda bi, i: (0, 0)),
                  pl.BlockSpec((1, D), lambda bi, i: (0, 0))],
        out_specs=tok,
        out_shape=jax.ShapeDtypeStruct((B, S, D), F32),
        compiler_params=_cp(("arbitrary", "arbitrary"), 32),
        name="res_ln",
    )(x, y, gate, g.reshape(1, D), b.reshape(1, D))


def _out_ln_kernel(o_ref, w_ref, x_ref, gate_ref, g_ref, b_ref, out_ref):
    y = _dot(o_ref[0], w_ref[...])
    z = ALPHA * x_ref[0] + gate_ref[0] * y
    out_ref[0] = _layer_norm(z, g_ref[...], b_ref[...])


def _out_ln(o, w_o, x, gate, g, b, ts=512):
    B, S, D = x.shape
    K = o.shape[-1]
    ts = min(ts, S)
    return pl.pallas_call(
        _out_ln_kernel,
        grid=(B, S // ts),
        in_specs=[pl.BlockSpec((1, ts, K), lambda bi, i: (bi, i, 0)),
                  pl.BlockSpec((K, D), lambda bi, i: (0, 0)),
                  pl.BlockSpec((1, ts, D), lambda bi, i: (bi, i, 0)),
                  pl.BlockSpec((1, 1, D), lambda bi, i: (bi, 0, 0)),
                  pl.BlockSpec((1, D), lambda bi, i: (0, 0)),
                  pl.BlockSpec((1, D), lambda bi, i: (0, 0))],
        out_specs=pl.BlockSpec((1, ts, D), lambda bi, i: (bi, i, 0)),
        out_shape=jax.ShapeDtypeStruct((B, S, D), F32),
        compiler_params=_cp(("arbitrary", "arbitrary"), 40),
        name="out_ln",
    )(o, w_o.astype(BF16), x, gate, g.reshape(1, D), b.reshape(1, D))


def _rope_tables(positions, dim):
    inv_freq = 1.0 / (ROPE_THETA ** (jnp.arange(0, dim, 2, dtype=F32) / dim))
    ang = positions.astype(F32)[..., None] * inv_freq
    return jnp.cos(ang), jnp.sin(ang)


def _rot_half_cols(w, half):
    return jnp.concatenate([-w[..., half:], w[..., :half]], axis=-1)


def _mla_front_kernel(x_ref, sc_ref, sh_ref, win_ref, gq_ref, gkv_ref, wqt_ref, wk_ref, wvt_ref, t_ref, tt_ref,
                      qt_ref, k_ref, vt_ref, *, scale):
    hb = (x_ref[0] * (1.0 + sc_ref[0]) + sh_ref[0]).astype(BF16)
    r = _dot(hb, win_ref[...])
    tab = t_ref[0]
    tab_t = tt_ref[0]
    low = lax.broadcasted_iota(jnp.int32, tab.shape, 1) < MLA_DR

    def rms(cx, g):
        return (cx * lax.rsqrt(jnp.mean(cx * cx, axis=-1, keepdims=True) + RMS_EPS) * g).astype(BF16)

    cq = rms(r[:, :MLA_QR], gq_ref[...])
    ckv = rms(r[:, MLA_QR:MLA_QR + MLA_KVR], gkv_ref[...])
    t = r[:, MLA_QR + MLA_KVR:] * tab
    k_rope = jnp.where(low, t + pltpu.roll(t, MLA_DR, axis=1), 0.0).astype(BF16)
    zeros = jnp.zeros((MLA_DR, hb.shape[0]), BF16)
    for h in range(MLA_HEADS):
        qh = _nt_dot(wqt_ref[h], cq)
        tq = qh[MLA_DN:] * tab_t
        qt_ref[0, h, :MLA_DN, :] = (qh[:MLA_DN] * scale).astype(BF16)
        qt_ref[0, h, MLA_DN:MLA_DN + MLA_DR, :] = ((tq[:MLA_DR] + tq[MLA_DR:]) * scale).astype(BF16)
        qt_ref[0, h, MLA_DN + MLA_DR:, :] = zeros
        k_ref[0, h, :, :MLA_DN] = _dot(ckv, wk_ref[h]).astype(BF16)
        k_ref[0, h, :, MLA_DN:] = k_rope
        vt_ref[0, h] = _nt_dot(wvt_ref[h], ckv).astype(BF16)


def _flash_update(s, m_sc, l_sc, acc_sc, vt):
    m_prev = m_sc[...]
    m_new = jnp.maximum(m_prev, jnp.max(s, axis=0, keepdims=True))
    p = jnp.exp2(s - m_new)
    a = jnp.exp2(m_prev - m_new)
    l_sc[...] = a * l_sc[...] + jnp.sum(p, axis=0, keepdims=True)
    acc_sc[...] = a * acc_sc[...] + _dot(vt, p.astype(BF16))
    m_sc[...] = m_new


def _flash_causal_kernel(qt_ref, k_ref, vt_ref, o_ref, m_sc, l_sc, acc_sc, *, t):
    qi = pl.program_id(2)
    qt = qt_ref[0, 0]
    m_sc[...] = jnp.full(m_sc.shape, NEG, F32)
    l_sc[...] = jnp.zeros(l_sc.shape, F32)
    acc_sc[...] = jnp.zeros(acc_sc.shape, F32)

    def scores(kt):
        return _dot(k_ref[0, 0, pl.ds(pl.multiple_of(kt * t, t), t), :], qt)

    def body(kt, s):
        s_next = scores(kt + 1)
        _flash_update(s, m_sc, l_sc, acc_sc, vt_ref[0, 0, :, pl.ds(pl.multiple_of(kt * t, t), t)])
        return s_next

    s = lax.fori_loop(0, qi, body, scores(0))
    kpos = lax.broadcasted_iota(jnp.int32, (t, 1), 0)
    qpos = lax.broadcasted_iota(jnp.int32, (1, t), 1)
    s = jnp.where(kpos <= qpos, s, NEG)
    _flash_update(s, m_sc, l_sc, acc_sc, vt_ref[0, 0, :, pl.ds(pl.multiple_of(qi * t, t), t)])
    o_ref[0] = (acc_sc[...] / l_sc[...]).T.astype(o_ref.dtype)


def _mla_layer(x, shift, scale_v, gate, cos, sin, w_in, g_q, w_uq, g_kv, w_ukv, w_o, ln_g, ln_b):
    B, S, D = x.shape
    H, dn, dr, dv = MLA_HEADS, MLA_DN, MLA_DR, MLA_DV
    half = dr // 2
    kr = w_in[:, MLA_QR + MLA_KVR:]
    win_p = jnp.concatenate([w_in, _rot_half_cols(kr, half)], axis=1).astype(BF16)
    wq = w_uq.reshape(MLA_QR, H, dn + dr)
    wq = jnp.concatenate([wq, _rot_half_cols(wq[..., dn:], half)], axis=-1)
    wqt = jnp.transpose(wq, (1, 2, 0)).astype(BF16)
    wkv = w_ukv.reshape(MLA_KVR, H, dn + dv)
    wk = jnp.transpose(wkv[..., :dn], (1, 0, 2)).astype(BF16)
    wvt = jnp.transpose(wkv[..., dn:], (1, 2, 0)).astype(BF16)
    tab = jnp.concatenate([cos, cos, sin, sin], axis=-1)
    tab_t = jnp.swapaxes(tab, 1, 2)
    ts = min(512, S)
    nw = win_p.shape[1]
    hd = dn + 2 * dr
    qt, k, vt = pl.pallas_call(
        functools.partial(_mla_front_kernel, scale=LOG2E * (dn + dr) ** -0.5),
        grid=(B, S // ts),
        in_specs=[pl.BlockSpec((1, ts, D), lambda b, i: (b, i, 0)),
                  pl.BlockSpec((1, 1, D), lambda b, i: (b, 0, 0)),
                  pl.BlockSpec((1, 1, D), lambda b, i: (b, 0, 0)),
                  pl.BlockSpec((D, nw), lambda b, i: (0, 0)),
                  pl.BlockSpec((1, MLA_QR), lambda b, i: (0, 0)),
                  pl.BlockSpec((1, MLA_KVR), lambda b, i: (0, 0)),
                  pl.BlockSpec((H, hd, MLA_QR), lambda b, i: (0, 0, 0)),
                  pl.BlockSpec((H, MLA_KVR, dn), lambda b, i: (0, 0, 0)),
                  pl.BlockSpec((H, dv, MLA_KVR), lambda b, i: (0, 0, 0)),
                  pl.BlockSpec((1, ts, LANES), lambda b, i: (b, i, 0)),
                  pl.BlockSpec((1, LANES, ts), lambda b, i: (b, 0, i))],
        out_specs=[pl.BlockSpec((1, H, hd, ts), lambda b, i: (b, 0, 0, i)),
                   pl.BlockSpec((1, H, ts, hd), lambda b, i: (b, 0, i, 0)),
                   pl.BlockSpec((1, H, dv, ts), lambda b, i: (b, 0, 0, i))],
        out_shape=[jax.ShapeDtypeStruct((B, H, hd, S), BF16),
                   jax.ShapeDtypeStruct((B, H, S, hd), BF16),
                   jax.ShapeDtypeStruct((B, H, dv, S), BF16)],
        compiler_params=_cp(("arbitrary", "arbitrary"), 48),
        name="mla_front",
    )(x, scale_v, shift, win_p, g_q.reshape(1, -1), g_kv.reshape(1, -1), wqt, wk, wvt, tab, tab_t)

    t = min(512, S)
    o = pl.pallas_call(
        functools.partial(_flash_causal_kernel, t=t),
        grid=(B, H, S // t),
        in_specs=[pl.BlockSpec((1, 1, hd, t), lambda b, h, i: (b, h, 0, i)),
                  pl.BlockSpec((1, 1, S, hd), lambda b, h, i: (b, h, 0, 0)),
                  pl.BlockSpec((1, 1, dv, S), lambda b, h, i: (b, h, 0, 0))],
        out_specs=pl.BlockSpec((1, t, dv), lambda b, h, i: (b, i, h)),
        out_shape=jax.ShapeDtypeStruct((B, S, H * dv), BF16),
        scratch_shapes=[pltpu.VMEM((1, t), F32), pltpu.VMEM((1, t), F32), pltpu.VMEM((dv, t), F32)],
        compiler_params=_cp(("arbitrary", "arbitrary", "arbitrary"), 32),
        name="mla_attn",
    )(qt, k, vt)
    return _out_ln(o, w_o, x, gate, ln_g, ln_b)


def _proj_kernel(*refs, segs, chunk, has_rope):
    x_ref, sc_ref, sh_ref, w_ref, wt_ref = refs[:5]
    n_in = 9 if has_rope else 5
    outs = refs[n_in:]
    hb = (x_ref[0] * (1.0 + sc_ref[0]) + sh_ref[0]).astype(BF16)
    if has_rope:
        cos, sin = refs[5][0], refs[6][0]
        cos_t, sin_t = refs[7][0], refs[8][0]
    half = LANES // 2
    for (c0, width, kind, scale, transposed), o_ref in zip(segs, outs):
        for j in range(0, width, chunk):
            cw = min(chunk, width - j)
            if transposed:
                r = _nt_dot(wt_ref[c0 + j:c0 + j + cw, :], hb)
            else:
                r = _dot(hb, w_ref[:, c0 + j:c0 + j + cw])
            if kind == "rope":
                parts = []
                for t in range(0, cw, LANES):
                    if transposed:
                        u = r[t:t + LANES]
                        parts.append(u * cos_t + pltpu.roll(u, half, axis=0) * sin_t)
                    else:
                        u = r[:, t:t + LANES]
                        parts.append(u * cos + pltpu.roll(u, half, axis=1) * sin)
                r = jnp.concatenate(parts, axis=0 if transposed else 1) if len(parts) > 1 else parts[0]
            elif kind == "sigmoid":
                r = jax.nn.sigmoid(r)
            if scale != 1.0:
                r = r * scale
            if transposed:
                o_ref[0, j:j + cw, :] = r.astype(o_ref.dtype)
            else:
                o_ref[0, :, j:j + cw] = r.astype(o_ref.dtype)


def _mod_proj(x, shift, scale_v, w, wt, segs, out_dtypes, rope=None, ts=512, chunk=256):
    B, S, D = x.shape
    ts = min(ts, S)
    in_specs = [pl.BlockSpec((1, ts, D), lambda b, i: (b, i, 0)),
                pl.BlockSpec((1, 1, D), lambda b, i: (b, 0, 0)),
                pl.BlockSpec((1, 1, D), lambda b, i: (b, 0, 0)),
                pl.BlockSpec(w.shape, lambda b, i: (0, 0)),
                pl.BlockSpec(wt.shape, lambda b, i: (0, 0))]
    args = [x, scale_v, shift, w, wt]
    if rope is not None:
        cos, sin = rope
        in_specs += [pl.BlockSpec((1, ts, LANES), lambda b, i: (b, i, 0))] * 2
        in_specs += [pl.BlockSpec((1, LANES, ts), lambda b, i: (b, 0, i))] * 2
        args += [cos, sin, jnp.swapaxes(cos, 1, 2), jnp.swapaxes(sin, 1, 2)]
    out_specs, out_shape = [], []
    for sg, dt in zip(segs, out_dtypes):
        if sg[4]:
            out_specs.append(pl.BlockSpec((1, sg[1], ts), lambda b, i: (b, 0, i)))
            out_shape.append(jax.ShapeDtypeStruct((B, sg[1], S), dt))
        else:
            out_specs.append(pl.BlockSpec((1, ts, sg[1]), lambda b, i: (b, i, 0)))
            out_shape.append(jax.ShapeDtypeStruct((B, S, sg[1]), dt))
    return pl.pallas_call(
        functools.partial(_proj_kernel, segs=tuple(segs), chunk=chunk, has_rope=rope is not None),
        grid=(B, S // ts),
        in_specs=in_specs,
        out_specs=out_specs,
        out_shape=out_shape,
        compiler_params=_cp(("arbitrary", "arbitrary"), 48),
        name="mod_proj",
    )(*args)


def _nsa_cmp_kernel(kc_ref, vc_ref, pek_ref, pev_ref, wk1_ref, wk2_ref, wv1_ref, wv2t_ref, ko_ref, vto_ref):
    half = NSA_CMP_BLOCK // 2
    nchunk = kc_ref.shape[1]

    def hidden(src, pe_ref, w1_ref, kh):
        a = jnp.zeros((nchunk, w1_ref.shape[2]), F32)
        bm = jnp.zeros((nchunk, w1_ref.shape[2]), F32)
        for l in range(half):
            c0 = (l * NSA_KVH + kh) * NSA_DH
            t = src[0, :, c0:c0 + NSA_DH].astype(F32)
            a = a + _dot((t + pe_ref[l:l + 1, :]).astype(BF16), w1_ref[l])
            bm = bm + _dot((t + pe_ref[half + l:half + l + 1, :]).astype(BF16), w1_ref[half + l])
        hid = a + pltpu.roll(bm, nchunk - 1, axis=0)
        return (hid * jax.nn.sigmoid(hid)).astype(BF16)

    rowi = lax.broadcasted_iota(jnp.int32, (nchunk, 1), 0)
    coli = lax.broadcasted_iota(jnp.int32, (1, nchunk), 1)
    for kh in range(NSA_KVH):
        out = _dot(hidden(kc_ref, pek_ref, wk1_ref, kh), wk2_ref[...])
        ko_ref[0, kh] = jnp.where(rowi < nchunk - 1, out, 0.0).astype(ko_ref.dtype)
        out_t = _nt_dot(wv2t_ref[...], hidden(vc_ref, pev_ref, wv1_ref, kh))
        vto_ref[0, kh] = jnp.where(coli < nchunk - 1, out_t, 0.0).astype(vto_ref.dtype)


def _nsa_attn_kernel(qt_ref, kc_ref, vct_ref, ks_ref, vst_ref, kw_ref, vwt_ref, gt_ref, ovt_ref, e_ref, o_ref,
                     q4_sc, m_sc, l_sc, acc_sc, *, tq, tk, span, n_slc):
    qi = pl.program_id(2)
    G = NSA_G
    for g in range(G):
        q4_sc[:, g * tq:(g + 1) * tq] = qt_ref[0, g * NSA_DH:(g + 1) * NSA_DH, :]
    q4 = q4_sc[...]
    qpos = qi * tq + lax.broadcasted_iota(jnp.int32, (1, tq), 1)

    def tile_g(a):
        return jnp.concatenate([a] * G, axis=1)

    qpos4 = tile_g(qpos)

    n_cmp = kc_ref.shape[2]
    nid = lax.broadcasted_iota(jnp.int32, (n_cmp, 1), 0)
    cmask = (nid * NSA_CMP_STRIDE + (NSA_CMP_BLOCK - 1) <= qpos4) & (nid < n_cmp - 1)
    s = jnp.where(cmask, _dot(kc_ref[0, 0], q4), NEG)
    p = jnp.where(cmask, jnp.exp2(s - jnp.max(s, axis=0, keepdims=True)), 0.0)
    l = jnp.sum(p, axis=0, keepdims=True)
    p_cmp = p / jnp.where(l > 0.0, l, 1.0)
    o_cmp = _dot(vct_ref[0, 0], p_cmp.astype(BF16))

    psum = p_cmp[:, 0:tq]
    for g in range(1, G):
        psum = psum + p_cmp[:, g * tq:(g + 1) * tq]
    p_hi, p_lo = _split_bf16(psum)
    imp = _dot(ovt_ref[...], p_hi) + _dot(ovt_ref[...], p_lo)
    nb = ovt_ref.shape[0]
    blk = lax.broadcasted_iota(jnp.int32, (nb, 1), 0)
    cur = jnp.right_shift(qpos, SLC_SHIFT)
    forced = (blk == 0) | (blk == cur) | (blk == cur - 1)
    valid = (blk * NSA_SLC_BLOCK <= qpos) & (blk < n_slc)
    work = jnp.where(valid, imp + jnp.where(forced, NSA_BONUS, 0.0), -jnp.inf)
    beaten = jnp.zeros((nb, tq), F32)
    for j in range(n_slc):
        wj = work[j:j + 1, :]
        beaten = beaten + jnp.where((wj > work) | ((wj == work) & (blk > j)), 1.0, 0.0)
    sel = jnp.where(beaten < float(min(NSA_TOPK, n_slc)), 1.0, 0.0).astype(BF16)

    m_sc[...] = jnp.full(m_sc.shape, NEG, F32)
    l_sc[...] = jnp.zeros(l_sc.shape, F32)
    acc_sc[...] = jnp.zeros(acc_sc.shape, F32)
    n_kt = ((qi + 1) * tq + tk - 1) // tk

    def scores(kt):
        return _dot(ks_ref[0, pl.ds(pl.multiple_of(kt * tk, tk), tk), :], q4)

    def body(kt, s):
        s_next = scores(jnp.minimum(kt + 1, n_kt - 1))
        ks = pl.multiple_of(kt * tk, tk)
        chosen = _dot(e_ref[pl.ds(ks, tk), :], sel)
        kpos = ks + lax.broadcasted_iota(jnp.int32, (tk, 1), 0)
        bias = jnp.where((chosen > 0.5) & (kpos <= qpos), 0.0, NEG)
        _flash_update(s + tile_g(bias), m_sc, l_sc, acc_sc, vst_ref[0, :, pl.ds(ks, tk)])
        return s_next

    lax.fori_loop(0, n_kt, body, scores(0))
    o_slc = acc_sc[...] / l_sc[...]

    kstart = pl.multiple_of(jnp.maximum(qi * tq + tq - span, 0), tq)
    diff = qpos - (kstart + lax.broadcasted_iota(jnp.int32, (span, 1), 0))
    bias = jnp.where((diff >= 0) & (diff < NSA_WINDOW), 0.0, NEG)
    s = _dot(kw_ref[0, pl.ds(kstart, span), :], q4) + tile_g(bias)
    p = jnp.exp2(s - jnp.max(s, axis=0, keepdims=True))
    o_win = _dot(vwt_ref[0, :, pl.ds(kstart, span)], p.astype(BF16)) / jnp.sum(p, axis=0, keepdims=True)

    gt = gt_ref[0]
    for g in range(G):
        cols = slice(g * tq, (g + 1) * tq)
        o = (gt[3 * g:3 * g + 1] * o_cmp[:, cols] + gt[3 * g + 1:3 * g + 2] * o_slc[:, cols]
             + gt[3 * g + 2:3 * g + 3] * o_win[:, cols])
        o_ref[0, :, g * NSA_DH:(g + 1) * NSA_DH] = o.T.astype(o_ref.dtype)


def _nsa_layer(x, shift, scale_v, gate, cos, sin, w_in, pe_k, pe_v, w_ck1, w_ck2, w_cv1, w_cv2, w_o,
               ln_g, ln_b):
    B, S, D = x.shape
    H, KH, Dh, G = NSA_HEADS, NSA_KVH, NSA_DH, NSA_G
    kvw = KH * Dh
    cuts = [int(v) for v in np.cumsum([H * Dh] + [kvw] * 6)]
    wq, wkc, wvc, wks, wvs, wkw, wvw, wgl = jnp.split(w_in, cuts, axis=1)
    wgl = jnp.pad(wgl.reshape(D, KH, G * 3), ((0, 0), (0, 0), (0, GATE_ROWS - G * 3))).reshape(D, KH * GATE_ROWS)
    w = jnp.concatenate([wkc, wks, wkw, wvc], axis=1).astype(BF16)
    wt = jnp.concatenate([wq, wvs, wvw, wgl], axis=1).T.astype(BF16)
    segs = [(0, H * Dh, "rope", LOG2E * Dh ** -0.5, True),
            (0, kvw, "rope", 1.0, False), (kvw, kvw, "rope", 1.0, False), (2 * kvw, kvw, "rope", 1.0, False),
            (3 * kvw, kvw, "plain", 1.0, False),
            (H * Dh, kvw, "plain", 1.0, True), (H * Dh + kvw, kvw, "plain", 1.0, True),
            (H * Dh + 2 * kvw, KH * GATE_ROWS, "sigmoid", 1.0, True)]
    rope = (jnp.concatenate([cos, cos], axis=-1), jnp.concatenate([-sin, sin], axis=-1))
    qt, kc, ks, kw, vc, vst, vwt, gates_t = _mod_proj(
        x, shift, scale_v, w, wt, segs, [BF16] * 7 + [F32], rope=rope)

    st = NSA_CMP_STRIDE
    nchunk = S // st
    kc2 = kc.reshape(B, nchunk, st * kvw)
    vc2 = vc.reshape(B, nchunk, st * kvw)
    hid = w_ck1.shape[-1]
    full = lambda shape: pl.BlockSpec(shape, lambda b: (0,) * len(shape))
    k_cmp, v_cmp_t = pl.pallas_call(
        _nsa_cmp_kernel,
        grid=(B,),
        in_specs=[pl.BlockSpec((1, nchunk, st * kvw), lambda b: (b, 0, 0)),
                  pl.BlockSpec((1, nchunk, st * kvw), lambda b: (b, 0, 0)),
                  full((NSA_CMP_BLOCK, Dh)), full((NSA_CMP_BLOCK, Dh)),
                  full((NSA_CMP_BLOCK, Dh, hid)), full((hid, Dh)),
                  full((NSA_CMP_BLOCK, Dh, hid)), full((Dh, hid))],
        out_specs=[pl.BlockSpec((1, KH, nchunk, Dh), lambda b: (b, 0, 0, 0)),
                   pl.BlockSpec((1, KH, Dh, nchunk), lambda b: (b, 0, 0, 0))],
        out_shape=[jax.ShapeDtypeStruct((B, KH, nchunk, Dh), BF16),
                   jax.ShapeDtypeStruct((B, KH, Dh, nchunk), BF16)],
        compiler_params=_cp(("arbitrary",), 32),
        name="nsa_cmp",
    )(kc2, vc2, pe_k, pe_v, w_ck1.astype(BF16), w_ck2.astype(BF16), w_cv1.astype(BF16), w_cv2.T.astype(BF16))

    n_slc = S // NSA_SLC_BLOCK
    nb = 32
    assert n_slc <= nb and nchunk % 8 == 0
    cmp_start = np.arange(nchunk) * st
    slc_start = np.arange(n_slc) * NSA_SLC_BLOCK
    ov = np.clip(np.minimum(cmp_start[:, None] + NSA_CMP_BLOCK, slc_start[None, :] + NSA_SLC_BLOCK)
                 - np.maximum(cmp_start[:, None], slc_start[None, :]), 0, None).astype(np.float32) / NSA_CMP_BLOCK
    ov[nchunk - 1:] = 0.0
    ov_t = np.zeros((nb, nchunk), np.float32)
    ov_t[:n_slc] = ov.T
    key_blk = np.zeros((S, nb), np.float32)
    key_blk[np.arange(S), np.arange(S) // NSA_SLC_BLOCK] = 1.0

    tq = min(128, S)
    tk = min(256, S)
    span = min(NSA_WINDOW + tq, S)
    k_spec = pl.BlockSpec((1, S, Dh), lambda b, kh, i: (b, 0, kh))
    vt_spec = pl.BlockSpec((1, Dh, S), lambda b, kh, i: (b, kh, 0))
    o = pl.pallas_call(
        functools.partial(_nsa_attn_kernel, tq=tq, tk=tk, span=span, n_slc=n_slc),
        grid=(B, KH, S // tq),
        in_specs=[pl.BlockSpec((1, G * Dh, tq), lambda b, kh, i: (b, kh, i)),
                  pl.BlockSpec((1, 1, nchunk, Dh), lambda b, kh, i: (b, kh, 0, 0)),
                  pl.BlockSpec((1, 1, Dh, nchunk), lambda b, kh, i: (b, kh, 0, 0)),
                  k_spec, vt_spec, k_spec, vt_spec,
                  pl.BlockSpec((1, GATE_ROWS, tq), lambda b, kh, i: (b, kh, i)),
                  pl.BlockSpec((nb, nchunk), lambda b, kh, i: (0, 0)),
                  pl.BlockSpec((S, nb), lambda b, kh, i: (0, 0))],
        out_specs=pl.BlockSpec((1, tq, G * Dh), lambda b, kh, i: (b, i, kh)),
        out_shape=jax.ShapeDtypeStruct((B, S, H * Dh), BF16),
        scratch_shapes=[pltpu.VMEM((Dh, G * tq), BF16), pltpu.VMEM((1, G * tq), F32),
                        pltpu.VMEM((1, G * tq), F32), pltpu.VMEM((Dh, G * tq), F32)],
        compiler_params=_cp(("arbitrary", "arbitrary", "arbitrary"), 32),
        name="nsa_attn",
    )(qt, k_cmp, v_cmp_t, ks, vst, kw, vwt, gates_t, jnp.asarray(ov_t, BF16), jnp.asarray(key_blk, BF16))
    return _out_ln(o, w_o, x, gate, ln_g, ln_b)


def _sb_attn_kernel(qt_ref, k_ref, vt_ref, u_ref, o_ref, r_sc, acc_sc, *, tq, tk):
    qi = pl.program_id(2)
    qt = qt_ref[0]
    r_sc[...] = jnp.zeros(r_sc.shape, F32)
    acc_sc[...] = jnp.zeros(acc_sc.shape, F32)
    u = u_ref[...]
    nd = tq // tk
    last = (qi + 1) * nd - 1

    def scores(kt):
        return _dot(k_ref[0, pl.ds(pl.multiple_of(kt * tk, tk), tk), :], qt)

    def step(kt, z, masked):
        ks = pl.multiple_of(kt * tk, tk)
        lk = -(jnp.maximum(z, 0.0) + jnp.log2(1.0 + jnp.exp2(-jnp.abs(z))))
        if masked:
            kpos = ks + lax.broadcasted_iota(jnp.int32, (tk, 1), 0)
            qpos = qi * tq + lax.broadcasted_iota(jnp.int32, (1, tq), 1)
            strict = kpos < qpos
            lk = jnp.where(strict, lk, 0.0)
        hi, lo = _split_bf16(lk)
        between = r_sc[...] + _dot(u, hi) + _dot(u, lo)
        a = jnp.exp2(lk + z + between)
        if masked:
            a = jnp.where(strict, a, 0.0)
        acc_sc[...] += _dot(vt_ref[0, :, pl.ds(ks, tk)], a.astype(BF16))
        r_sc[...] += jnp.sum(lk, axis=0, keepdims=True)

    z = scores(last)
    for i in range(nd):
        z_next = scores(jnp.maximum(last - i - 1, 0))
        step(last - i, z, True)
        z = z_next

    def cond(carry):
        kt, _ = carry
        return (kt >= 0) & (jnp.max(r_sc[...]) > SB_EXIT_LOG2)

    def body(carry):
        kt, z = carry
        z_next = scores(jnp.maximum(kt - 1, 0))
        step(kt, z, False)
        return kt - 1, z_next

    lax.while_loop(cond, body, (qi * nd - 1, z))
    o_ref[0] = acc_sc[...].T.astype(o_ref.dtype)


def _sb_layer(x, shift, scale_v, gate, w_qkv, w_o, ln_g, ln_b):
    B, S, D = x.shape
    H, d = SB_HEADS, SB_DH
    hw = H * d
    w = w_qkv[:, hw:2 * hw].astype(BF16)
    wt = jnp.concatenate([w_qkv[:, :hw], w_qkv[:, 2 * hw:]], axis=1).T.astype(BF16)
    segs = [(0, hw, "plain", LOG2E * d ** -0.5, True), (0, hw, "plain", 1.0, False), (hw, hw, "plain", 1.0, True)]
    qt, k, vt = _mod_proj(x, shift, scale_v, w, wt, segs, [BF16] * 3)
    tq = min(512, S)
    tk = min(256, S)
    tri = np.triu(np.ones((tk, tk), np.float32), 1)
    o = pl.pallas_call(
        functools.partial(_sb_attn_kernel, tq=tq, tk=tk),
        grid=(B, H, S // tq),
        in_specs=[pl.BlockSpec((1, d, tq), lambda b, h, i: (b, h, i)),
                  pl.BlockSpec((1, S, d), lambda b, h, i: (b, 0, h)),
                  pl.BlockSpec((1, d, S), lambda b, h, i: (b, h, 0)),
                  pl.BlockSpec((tk, tk), lambda b, h, i: (0, 0))],
        out_specs=pl.BlockSpec((1, tq, d), lambda b, h, i: (b, i, h)),
        out_shape=jax.ShapeDtypeStruct((B, S, hw), BF16),
        scratch_shapes=[pltpu.VMEM((1, tq), F32), pltpu.VMEM((d, tq), F32)],
        compiler_params=_cp(("arbitrary", "arbitrary", "arbitrary"), 32),
        name="sb_attn",
    )(qt, k, vt, jnp.asarray(tri, BF16))
    return _out_ln(o, w_o, x, gate, ln_g, ln_b)


def _router_kernel(x_ref, sc_ref, sh_ref, whi_ref, wlo_ref, b_ref, tri_ref, hb_ref, info_ref, cnt_ref,
                   base_sc):
    first = (pl.program_id(0) == 0) & (pl.program_id(1) == 0)

    @pl.when(first)
    def _():
        base_sc[...] = jnp.zeros(base_sc.shape, F32)

    h = x_ref[0] * (1.0 + sc_ref[0]) + sh_ref[0]
    h_hi, h_lo = _split_bf16(h)
    hb_ref[0] = _pack_pairs(h)
    logit = (_nt_dot(whi_ref[...], h_hi) + _nt_dot(whi_ref[...], h_lo) + _nt_dot(wlo_ref[...], h_hi)
             + b_ref[...])
    tm = logit.shape[1]
    epg = MOE_EPG
    row = lax.broadcasted_iota(jnp.int32, (epg, 1), 0)
    row_f = row.astype(F32)

    def first_max(vals):
        mx = jnp.max(vals, axis=0, keepdims=True)
        idx = jnp.min(jnp.where(vals == mx, row_f, float(epg)), axis=0, keepdims=True)
        return mx, idx

    gl = jnp.where(row < MOE_GROUPS, logit[0:epg], -jnp.inf)
    gmax, g_idx = first_max(gl)
    g_w = 1.0 / jnp.sum(jnp.exp(gl - gmax), axis=0, keepdims=True)
    el = logit[epg:2 * epg]
    for g in range(1, MOE_GROUPS):
        el = jnp.where(g_idx == float(g), logit[epg * (g + 1):epg * (g + 2)], el)
    ep = jnp.exp(el - jnp.max(el, axis=0, keepdims=True))
    prob = ep / jnp.sum(ep, axis=0, keepdims=True)
    p1, i1 = first_max(prob)
    p2, i2 = first_max(jnp.where(row_f == i1, -1.0, prob))
    e1 = g_idx * float(epg) + i1
    e2 = g_idx * float(epg) + i2
    w1 = g_w * p1 / (p1 + p2)
    w2 = g_w * p2 / (p1 + p2)

    erow = lax.broadcasted_iota(jnp.int32, (MOE_E, 1), 0).astype(F32)
    hot1 = erow == e1
    hot2 = erow == e2
    cnt = jnp.where(hot1 | hot2, 1.0, 0.0)
    before = base_sc[:, 0:1] + _dot(cnt.astype(BF16), tri_ref[...])
    r1 = jnp.sum(jnp.where(hot1, before, 0.0), axis=0, keepdims=True)
    r2 = jnp.sum(jnp.where(hot2, before, 0.0), axis=0, keepdims=True)
    base_sc[...] += jnp.sum(cnt, axis=1, keepdims=True)
    cnt_ref[...] = base_sc[...]

    irow = lax.broadcasted_iota(jnp.int32, (INFO_ROWS, 1), 0)
    info = jnp.zeros((INFO_ROWS, tm), F32)
    for k, val in enumerate((e1, e2, r1, r2, w1, w2)):
        info = jnp.where(irow == k, val, info)
    info_ref[0] = info


def _ffn_kernel(be_ref, nu_ref, x_ref, wg_ref, wu_ref, wd_ref, y_ref, wg_sc, wu_sc, wd_sc):
    i = pl.program_id(0)
    new_expert = (i == 0) | (be_ref[i] != be_ref[jnp.maximum(i - 1, 0)])

    @pl.when(new_expert & (i < nu_ref[0]))
    def _():
        wg_sc[...] = wg_ref[0, 0].astype(BF16)
        wu_sc[...] = wu_ref[0, 0].astype(BF16)
        wd_sc[...] = wd_ref[0, 0].astype(BF16)

    @pl.when(i < nu_ref[0])
    def _():
        xb = _unpack_pairs(x_ref[...]).astype(BF16)
        g = _dot(xb, wg_sc[...])
        u = _dot(xb, wu_sc[...])
        a = (g * jax.nn.sigmoid(g) * u).astype(BF16)
        y_ref[...] = _pack_pairs(_dot(a, wd_sc[...]))

    @pl.when(i >= nu_ref[0])
    def _():
        y_ref[...] = jnp.zeros(y_ref.shape, y_ref.dtype)


def _sc_gather(table, idx):
    n, d = idx.shape[0], table.shape[1]
    workers = SC_CORES * SC_SUBCORES
    per_w = n // workers
    assert per_w * workers == n and per_w % SC_CHUNK == 0
    mesh = plsc.VectorSubcoreMesh(core_axis_name="c", subcore_axis_name="s")

    @functools.partial(
        pl.kernel, mesh=mesh,
        out_type=jax.ShapeDtypeStruct((n, d), table.dtype),
        scratch_types=[pltpu.VMEM((SC_CHUNK,), jnp.int32),
                       pltpu.VMEM((SC_CHUNK, d), table.dtype),
                       pltpu.SemaphoreType.DMA],
        name="sc_gather")
    def gather(table_hbm, idx_hbm, out_hbm, idx_v, rows_v, sem):
        wid = lax.axis_index("s") * SC_CORES + lax.axis_index("c")

        @pl.loop(0, per_w // SC_CHUNK)
        def _(j):
            base = pl.multiple_of(wid * per_w + j * SC_CHUNK, SC_CHUNK)
            pltpu.sync_copy(idx_hbm.at[pl.ds(base, SC_CHUNK)], idx_v)
            pltpu.async_copy(table_hbm.at[idx_v], rows_v, sem).wait()
            pltpu.sync_copy(rows_v, out_hbm.at[pl.ds(base, SC_CHUNK)])

    return gather(table, idx)


def _sc_dispatch(src, dest0, dest1, n_out):
    n, d = src.shape
    workers = SC_CORES * SC_SUBCORES
    per_w = n // workers
    assert per_w * workers == n and per_w % SC_CHUNK == 0
    mesh = plsc.VectorSubcoreMesh(core_axis_name="c", subcore_axis_name="s")

    @functools.partial(
        pl.kernel, mesh=mesh,
        out_type=jax.ShapeDtypeStruct((n_out, d), src.dtype),
        scratch_types=[pltpu.VMEM((SC_CHUNK,), jnp.int32),
                       pltpu.VMEM((SC_CHUNK,), jnp.int32),
                       pltpu.VMEM((SC_CHUNK, d), src.dtype),
                       pltpu.SemaphoreType.DMA],
        name="sc_dispatch")
    def dispatch(src_hbm, d0_hbm, d1_hbm, out_hbm, i0_v, i1_v, rows_v, sem):
        wid = lax.axis_index("s") * SC_CORES + lax.axis_index("c")

        @pl.loop(0, per_w // SC_CHUNK)
        def _(j):
            base = pl.multiple_of(wid * per_w + j * SC_CHUNK, SC_CHUNK)
            pltpu.sync_copy(d0_hbm.at[pl.ds(base, SC_CHUNK)], i0_v)
            pltpu.sync_copy(d1_hbm.at[pl.ds(base, SC_CHUNK)], i1_v)
            pltpu.sync_copy(src_hbm.at[pl.ds(base, SC_CHUNK)], rows_v)
            pltpu.async_copy(rows_v, out_hbm.at[i0_v], sem).wait()
            pltpu.async_copy(rows_v, out_hbm.at[i1_v], sem).wait()

    return dispatch(src, dest0, dest1)


def _moe_out_kernel(x_ref, y1_ref, y2_ref, info_ref, gate_ref, g_ref, b_ref, o_ref):
    info = info_ref[0]
    ts = info.shape[1]
    info_tok = jnp.concatenate([info, jnp.zeros((LANES - INFO_ROWS, ts), F32)], axis=0).T
    y = info_tok[:, 4:5] * _unpack_pairs(y1_ref[0, 0]) + info_tok[:, 5:6] * _unpack_pairs(y2_ref[0, 0])
    z = ALPHA * x_ref[0] + gate_ref[0] * y
    o_ref[0] = _layer_norm(z, g_ref[...], b_ref[...])


def _moe_out(x, y12, info, gate, g, b, ts=512):
    B, S, D = x.shape
    ts = min(ts, S)
    tok = pl.BlockSpec((1, ts, D), lambda bi, i: (bi, i, 0))
    return pl.pallas_call(
        _moe_out_kernel,
        grid=(B, S // ts),
        in_specs=[tok,
                  pl.BlockSpec((1, 1, ts, D // 2), lambda bi, i: (0, bi, i, 0)),
                  pl.BlockSpec((1, 1, ts, D // 2), lambda bi, i: (1, bi, i, 0)),
                  pl.BlockSpec((1, INFO_ROWS, ts), lambda bi, i: (bi, 0, i)),
                  pl.BlockSpec((1, 1, D), lambda bi, i: (bi, 0, 0)),
                  pl.BlockSpec((1, D), lambda bi, i: (0, 0)),
                  pl.BlockSpec((1, D), lambda bi, i: (0, 0))],
        out_specs=tok,
        out_shape=jax.ShapeDtypeStruct((B, S, D), F32),
        compiler_params=_cp(("arbitrary", "arbitrary"), 40),
        name="moe_out",
    )(x, y12, y12, info, gate, g.reshape(1, D), b.reshape(1, D))


def _moe_layer(x, shift, scale_v, gate, w_grp, b_grp, w_rt, b_rt, layer, w_gate, w_up, w_down, ln_g, ln_b):
    B, S, D = x.shape
    T = B * S
    E, R, F = MOE_E, MOE_ROWS, w_gate.shape[-1]
    g0 = MOE_EPG
    w_r = jnp.zeros((ROUTER_ROWS, D), F32).at[:MOE_GROUPS].set(w_grp.T).at[g0:g0 + E].set(w_rt.T)
    w_hi = w_r.astype(BF16)
    w_lo = (w_r - w_hi.astype(F32)).astype(BF16)
    b_r = jnp.zeros((ROUTER_ROWS, 1), F32).at[:MOE_GROUPS, 0].set(b_grp).at[g0:g0 + E, 0].set(b_rt)
    tm = min(256, S)
    tri = jnp.asarray(np.triu(np.ones((tm, tm), np.float32), 1), BF16)
    hb, info, cnt = pl.pallas_call(
        _router_kernel,
        grid=(B, S // tm),
        in_specs=[pl.BlockSpec((1, tm, D), lambda b, i: (b, i, 0)),
                  pl.BlockSpec((1, 1, D), lambda b, i: (b, 0, 0)),
                  pl.BlockSpec((1, 1, D), lambda b, i: (b, 0, 0)),
                  pl.BlockSpec((ROUTER_ROWS, D), lambda b, i: (0, 0)),
                  pl.BlockSpec((ROUTER_ROWS, D), lambda b, i: (0, 0)),
                  pl.BlockSpec((ROUTER_ROWS, 1), lambda b, i: (0, 0)),
                  pl.BlockSpec((tm, tm), lambda b, i: (0, 0))],
        out_specs=[pl.BlockSpec((1, tm, D // 2), lambda b, i: (b, i, 0)),
                   pl.BlockSpec((1, INFO_ROWS, tm), lambda b, i: (b, 0, i)),
                   pl.BlockSpec((E, LANES), lambda b, i: (0, 0))],
        out_shape=[jax.ShapeDtypeStruct((B, S, D // 2), jnp.uint32),
                   jax.ShapeDtypeStruct((B, INFO_ROWS, S), F32),
                   jax.ShapeDtypeStruct((E, LANES), F32)],
        scratch_shapes=[pltpu.VMEM((E, LANES), F32)],
        compiler_params=_cp(("arbitrary", "arbitrary"), 32),
        name="moe_router",
    )(x, scale_v, shift, w_hi, w_lo, b_r, tri)

    counts = cnt[:, 0].astype(jnp.int32)
    padded = (counts + R - 1) // R * R
    ends = jnp.cumsum(padded)
    starts = ends - padded
    dest0 = (starts[info[:, 0, :].astype(jnp.int32)] + info[:, 2, :].astype(jnp.int32)).reshape(T)
    dest1 = (starts[info[:, 1, :].astype(jnp.int32)] + info[:, 3, :].astype(jnp.int32)).reshape(T)
    n_blk = (T * 2) // R + E
    P = n_blk * R
    blk_exp = jnp.minimum(jnp.sum(jnp.arange(n_blk)[:, None] * R >= ends[None, :], axis=1), E - 1).astype(jnp.int32)
    n_used = (ends[-1] // R).astype(jnp.int32).reshape(1)
    x_rows = _sc_dispatch(hb.reshape(T, D // 2), dest0, dest1, P)

    y_rows = pl.pallas_call(
        _ffn_kernel,
        grid_spec=pltpu.PrefetchScalarGridSpec(
            num_scalar_prefetch=2,
            grid=(n_blk,),
            in_specs=[pl.BlockSpec((R, D // 2), lambda i, be, nu: (i, 0)),
                      pl.BlockSpec((1, 1, D, F), lambda i, be, nu: (layer, be[i], 0, 0)),
                      pl.BlockSpec((1, 1, D, F), lambda i, be, nu: (layer, be[i], 0, 0)),
                      pl.BlockSpec((1, 1, F, D), lambda i, be, nu: (layer, be[i], 0, 0))],
            out_specs=pl.BlockSpec((R, D // 2), lambda i, be, nu: (i, 0)),
            scratch_shapes=[pltpu.VMEM((D, F), BF16), pltpu.VMEM((D, F), BF16), pltpu.VMEM((F, D), BF16)]),
        out_shape=jax.ShapeDtypeStruct((P, D // 2), jnp.uint32),
        compiler_params=_cp(("arbitrary",), 48),
        name="moe_ffn",
    )(blk_exp, n_used, x_rows, w_gate, w_up, w_down)

    y12 = _sc_gather(y_rows, jnp.concatenate([dest0, dest1])).reshape(2, B, S, D // 2)
    return _moe_out(x, y12, info, gate, ln_g, ln_b)


def kernel(x, c, positions, ada_w, ada_b, ln_g, ln_b, mla_w_in, mla_g_q, mla_w_uq, mla_g_kv, mla_w_ukv, mla_w_o, nsa_w_in, nsa_pe_k, nsa_pe_v, nsa_w_ck1, nsa_w_ck2, nsa_w_cv1, nsa_w_cv2, nsa_w_o, sb_w_qkv, sb_w_o, moe_w_grp, moe_b_grp, moe_w_rt, moe_b_rt, moe_w_gate, moe_w_up, moe_w_down):
    D = x.shape[-1]
    depth = ada_w.shape[0]
    cos_mla, sin_mla = _rope_tables(positions, MLA_DR)
    cos_nsa, sin_nsa = _rope_tables(positions, NSA_DH)
    mods = _ada_all(c, ada_w, ada_b)

    def mod(i, sub):
        m = mods[2 * i + sub][:, None, :]
        return m[..., :D], m[..., D:2 * D], m[..., 2 * D:]

    for i in range(depth):
        kind, j = i % N_MIXERS, i // N_MIXERS
        shift, scale_v, gate = mod(i, 0)
        if kind == 0:
            x = _mla_layer(x, shift, scale_v, gate, cos_mla, sin_mla, mla_w_in[j], mla_g_q[j], mla_w_uq[j],
                           mla_g_kv[j], mla_w_ukv[j], mla_w_o[j], ln_g[i, 0], ln_b[i, 0])
        elif kind == 1:
            x = _nsa_layer(x, shift, scale_v, gate, cos_nsa, sin_nsa, nsa_w_in[j], nsa_pe_k[j], nsa_pe_v[j],
                           nsa_w_ck1[j], nsa_w_ck2[j], nsa_w_cv1[j], nsa_w_cv2[j], nsa_w_o[j],
                           ln_g[i, 0], ln_b[i, 0])
        else:
            x = _sb_layer(x, shift, scale_v, gate, sb_w_qkv[j], sb_w_o[j], ln_g[i, 0], ln_b[i, 0])
        shift, scale_v, gate = mod(i, 1)
        x = _moe_layer(x, shift, scale_v, gate, moe_w_grp[i], moe_b_grp[i], moe_w_rt[i], moe_b_rt[i],
                       i, moe_w_gate, moe_w_up, moe_w_down, ln_g[i, 1], ln_b[i, 1])
    return x
```

```python
import functools

import numpy as np
import jax
import jax.numpy as jnp
from jax import lax
from jax.experimental import pallas as pl
from jax.experimental.pallas import tpu as pltpu
from jax.experimental.pallas import tpu_sc as plsc

BF16 = jnp.bfloat16
F32 = jnp.float32

DEPTH = 4
N_MIXERS = 3
ROPE_THETA = 10000.0
LN_EPS = 1e-5
RMS_EPS = 1e-6
NEG = -1e30
LOG2E = 1.4426950408889634
ALPHA = (2 * DEPTH) ** 0.25

MLA_HEADS, MLA_DN, MLA_DR, MLA_DV = 8, 128, 64, 128
MLA_QR, MLA_KVR = 256, 256

NSA_HEADS, NSA_KVH, NSA_DH = 8, 2, 128
NSA_G = NSA_HEADS // NSA_KVH
NSA_CMP_BLOCK, NSA_CMP_STRIDE = 32, 16
NSA_SLC_BLOCK, NSA_TOPK = 64, 8
SLC_SHIFT = 6
NSA_WINDOW = 512
NSA_BONUS = 1e3
GATE_ROWS = 16

SB_HEADS, SB_DH = 8, 128
SB_EXIT_LOG2 = -160.0

MOE_GROUPS, MOE_EPG = 4, 8
EPG_SHIFT = 3
MOE_E = MOE_GROUPS * MOE_EPG
MOE_ROWS = 256
ROUTER_ROWS = 64
INFO_ROWS = 8

SC_CORES, SC_SUBCORES = 2, 16
SC_CHUNK = 64

LANES = 128
VMEM_BYTES_V7X = 64 * 1024 * 1024


def _cp(sem, vmem_mb):
    assert vmem_mb * 2**20 < VMEM_BYTES_V7X
    return pltpu.CompilerParams(dimension_semantics=sem, vmem_limit_bytes=vmem_mb * 2**20)


def _nt_dot(a, b):
    return lax.dot_general(a, b, (((1,), (1,)), ((), ())), preferred_element_type=F32)


def _dot(a, b):
    return jnp.dot(a, b, preferred_element_type=F32)


def _split_bf16(a):
    hi = a.astype(BF16)
    lo = (a - hi.astype(F32)).astype(BF16)
    return hi, lo


def _pack_pairs(a):
    n = a.shape[1] // 2
    hi = pltpu.bitcast(a[:, :n].astype(BF16).astype(F32), jnp.uint32)
    lo = pltpu.bitcast(a[:, n:].astype(BF16).astype(F32), jnp.uint32)
    return hi | (lo >> 16)


def _unpack_pairs(u):
    hi = pltpu.bitcast(u & jnp.uint32(0xFFFF0000), F32)
    lo = pltpu.bitcast(u << 16, F32)
    return jnp.concatenate([hi, lo], axis=1)


def _layer_norm(z, g, b):
    mu = jnp.mean(z, axis=-1, keepdims=True)
    d = z - mu
    var = jnp.mean(d * d, axis=-1, keepdims=True)
    return d * lax.rsqrt(var + LN_EPS) * g + b


def _ada_kernel(c_ref, w_ref, b_ref, o_ref):
    c = c_ref[...]
    s_hi, s_lo = _split_bf16(c * jax.nn.sigmoid(c))
    w_hi, w_lo = _split_bf16(w_ref[0])
    o_ref[0] = _dot(s_hi, w_hi) + _dot(s_lo, w_hi) + _dot(s_hi, w_lo) + b_ref[0]


def _ada_all(c, ada_w, ada_b):
    L, two, D, D3 = ada_w.shape
    B = c.shape[0]
    n = L * two
    tn = 1024
    return pl.pallas_call(
        _ada_kernel,
        grid=(n, D3 // tn),
        in_specs=[pl.BlockSpec((B, D), lambda i, j: (0, 0)),
                  pl.BlockSpec((1, D, tn), lambda i, j: (i, 0, j)),
                  pl.BlockSpec((1, 1, tn), lambda i, j: (i, 0, j))],
        out_specs=pl.BlockSpec((1, B, tn), lambda i, j: (i, 0, j)),
        out_shape=jax.ShapeDtypeStruct((n, B, D3), F32),
        compiler_params=_cp(("arbitrary", "arbitrary"), 32),
        name="ada",
    )(c, ada_w.reshape(n, D, D3), ada_b.reshape(n, 1, D3))


def _res_ln_kernel(x_ref, y_ref, gate_ref, g_ref, b_ref, o_ref):
    z = ALPHA * x_ref[0] + gate_ref[0] * y_ref[0]
    o_ref[0] = _layer_norm(z, g_ref[...], b_ref[...])


def _res_ln(x, y, gate, g, b, ts=512):
    B, S, D = x.shape
    ts = min(ts, S)
    tok = pl.BlockSpec((1, ts, D), lambda bi, i: (bi, i, 0))
    return pl.pallas_call(
        _res_ln_kernel,
        grid=(B, S // ts),
        in_specs=[tok, tok,
                  pl.BlockSpec((1, 1, D), lambda bi, i: (bi, 0, 0)),
                  pl.BlockSpec((1, D), lambda bi, i: (0, 0)),
                  pl.BlockSpec((1, D), lambda bi, i: (0, 0))],
        out_specs=tok,
        out_shape=jax.ShapeDtypeStruct((B, S, D), F32),
        compiler_params=_cp(("arbitrary", "arbitrary"), 32),
        name="res_ln",
    )(x, y, gate, g.reshape(1, D), b.reshape(1, D))


def _out_ln_kernel(o_ref, w_ref, x_ref, gate_ref, g_ref, b_ref, out_ref):
    y = _dot(o_ref[0], w_ref[...])
    z = ALPHA * x_ref[0] + gate_ref[0] * y
    out_ref[0] = _layer_norm(z, g_ref[...], b_ref[...])


def _out_ln(o, w_o, x, gate, g, b, ts=512):
    B, S, D = x.shape
    K = o.shape[-1]
    ts = min(ts, S)
    return pl.pallas_call(
        _out_ln_kernel,
        grid=(B, S // ts),
        in_specs=[pl.BlockSpec((1, ts, K), lambda bi, i: (bi, i, 0)),
                  pl.BlockSpec((K, D), lambda bi, i: (0, 0)),
                  pl.BlockSpec((1, ts, D), lambda bi, i: (bi, i, 0)),
                  pl.BlockSpec((1, 1, D), lambda bi, i: (bi, 0, 0)),
                  pl.BlockSpec((1, D), lambda bi, i: (0, 0)),
                  pl.BlockSpec((1, D), lambda bi, i: (0, 0))],
        out_specs=pl.BlockSpec((1, ts, D), lambda bi, i: (bi, i, 0)),
        out_shape=jax.ShapeDtypeStruct((B, S, D), F32),
        compiler_params=_cp(("arbitrary", "arbitrary"), 40),
        name="out_ln",
    )(o, w_o.astype(BF16), x, gate, g.reshape(1, D), b.reshape(1, D))


def _rope_tables(positions, dim):
    inv_freq = 1.0 / (ROPE_THETA ** (jnp.arange(0, dim, 2, dtype=F32) / dim))
    ang = positions.astype(F32)[..., None] * inv_freq
    return jnp.cos(ang), jnp.sin(ang)


def _rot_half_cols(w, half):
    return jnp.concatenate([-w[..., half:], w[..., :half]], axis=-1)


def _mla_front_kernel(x_ref, sc_ref, sh_ref, win_ref, gq_ref, gkv_ref, wqt_ref, wk_ref, wvt_ref, t_ref, tt_ref,
                      qt_ref, k_ref, vt_ref, *, scale):
    hb = (x_ref[0] * (1.0 + sc_ref[0]) + sh_ref[0]).astype(BF16)
    r = _dot(hb, win_ref[...])
    tab = t_ref[0]
    tab_t = tt_ref[0]
    low = lax.broadcasted_iota(jnp.int32, tab.shape, 1) < MLA_DR

    def rms(cx, g):
        return (cx * lax.rsqrt(jnp.mean(cx * cx, axis=-1, keepdims=True) + RMS_EPS) * g).astype(BF16)

    cq = rms(r[:, :MLA_QR], gq_ref[...])
    ckv = rms(r[:, MLA_QR:MLA_QR + MLA_KVR], gkv_ref[...])
    t = r[:, MLA_QR + MLA_KVR:] * tab
    k_rope = jnp.where(low, t + pltpu.roll(t, MLA_DR, axis=1), 0.0).astype(BF16)
    zeros = jnp.zeros((MLA_DR, hb.shape[0]), BF16)
    for h in range(MLA_HEADS):
        qh = _nt_dot(wqt_ref[h], cq)
        tq = qh[MLA_DN:] * tab_t
        qt_ref[0, h, :MLA_DN, :] = (qh[:MLA_DN] * scale).astype(BF16)
        qt_ref[0, h, MLA_DN:MLA_DN + MLA_DR, :] = ((tq[:MLA_DR] + tq[MLA_DR:]) * scale).astype(BF16)
        qt_ref[0, h, MLA_DN + MLA_DR:, :] = zeros
        k_ref[0, h, :, :MLA_DN] = _dot(ckv, wk_ref[h]).astype(BF16)
        k_ref[0, h, :, MLA_DN:] = k_rope
        vt_ref[0, h] = _nt_dot(wvt_ref[h], ckv).astype(BF16)


def _flash_update(s, m_sc, l_sc, acc_sc, vt):
    m_prev = m_sc[...]
    m_new = jnp.maximum(m_prev, jnp.max(s, axis=0, keepdims=True))
    p = jnp.exp2(s - m_new)
    a = jnp.exp2(m_prev - m_new)
    l_sc[...] = a * l_sc[...] + jnp.sum(p, axis=0, keepdims=True)
    acc_sc[...] = a * acc_sc[...] + _dot(vt, p.astype(BF16))
    m_sc[...] = m_new


def _flash_causal_kernel(qt_ref, k_ref, vt_ref, o_ref, *, t):
    dv, s_len = vt_ref.shape[2], vt_ref.shape[3]
    pairs = [(qi, kt) for qi in range(s_len // t) for kt in range(qi + 1)]
    kpos = lax.broadcasted_iota(jnp.int32, (t, 1), 0)
    qpos = lax.broadcasted_iota(jnp.int32, (1, t), 1)
    diag = kpos <= qpos

    def scores(j):
        qi, kt = pairs[j]
        return _dot(k_ref[0, 0, kt * t:(kt + 1) * t, :], qt_ref[0, 0, :, qi * t:(qi + 1) * t])

    def value_matmul(pending, l, acc):
        qi, kt, p, a = pending
        acc = a * acc + _dot(vt_ref[0, 0, :, kt * t:(kt + 1) * t], p)
        if kt == qi:
            o_ref[0, qi * t:(qi + 1) * t, :] = (acc / l).T.astype(o_ref.dtype)
        return acc

    s_cur, pending = scores(0), None
    m = l = acc = None
    for j, (qi, kt) in enumerate(pairs):
        s_next = scores(j + 1) if j + 1 < len(pairs) else None
        if pending is not None:
            acc = value_matmul(pending, l, acc)
        if kt == 0:
            m, l, acc = jnp.full((1, t), NEG, F32), jnp.zeros((1, t), F32), jnp.zeros((dv, t), F32)
        s = jnp.where(diag, s_cur, NEG) if kt == qi else s_cur
        m_new = jnp.maximum(m, jnp.max(s, axis=0, keepdims=True))
        p = jnp.exp2(s - m_new)
        a = jnp.exp2(m - m_new)
        l = a * l + jnp.sum(p, axis=0, keepdims=True)
        m = m_new
        pending = (qi, kt, p.astype(BF16), a)
        s_cur = s_next
    value_matmul(pending, l, acc)


def _mla_layer(x, shift, scale_v, gate, cos, sin, w_in, g_q, w_uq, g_kv, w_ukv, w_o, ln_g, ln_b):
    B, S, D = x.shape
    H, dn, dr, dv = MLA_HEADS, MLA_DN, MLA_DR, MLA_DV
    half = dr // 2
    kr = w_in[:, MLA_QR + MLA_KVR:]
    win_p = jnp.concatenate([w_in, _rot_half_cols(kr, half)], axis=1).astype(BF16)
    wq = w_uq.reshape(MLA_QR, H, dn + dr)
    wq = jnp.concatenate([wq, _rot_half_cols(wq[..., dn:], half)], axis=-1)
    wqt = jnp.transpose(wq, (1, 2, 0)).astype(BF16)
    wkv = w_ukv.reshape(MLA_KVR, H, dn + dv)
    wk = jnp.transpose(wkv[..., :dn], (1, 0, 2)).astype(BF16)
    wvt = jnp.transpose(wkv[..., dn:], (1, 2, 0)).astype(BF16)
    tab = jnp.concatenate([cos, cos, sin, sin], axis=-1)
    tab_t = jnp.swapaxes(tab, 1, 2)
    ts = min(512, S)
    nw = win_p.shape[1]
    hd = dn + 2 * dr
    qt, k, vt = pl.pallas_call(
        functools.partial(_mla_front_kernel, scale=LOG2E * (dn + dr) ** -0.5),
        grid=(B, S // ts),
        in_specs=[pl.BlockSpec((1, ts, D), lambda b, i: (b, i, 0)),
                  pl.BlockSpec((1, 1, D), lambda b, i: (b, 0, 0)),
                  pl.BlockSpec((1, 1, D), lambda b, i: (b, 0, 0)),
                  pl.BlockSpec((D, nw), lambda b, i: (0, 0)),
                  pl.BlockSpec((1, MLA_QR), lambda b, i: (0, 0)),
                  pl.BlockSpec((1, MLA_KVR), lambda b, i: (0, 0)),
                  pl.BlockSpec((H, hd, MLA_QR), lambda b, i: (0, 0, 0)),
                  pl.BlockSpec((H, MLA_KVR, dn), lambda b, i: (0, 0, 0)),
                  pl.BlockSpec((H, dv, MLA_KVR), lambda b, i: (0, 0, 0)),
                  pl.BlockSpec((1, ts, LANES), lambda b, i: (b, i, 0)),
                  pl.BlockSpec((1, LANES, ts), lambda b, i: (b, 0, i))],
        out_specs=[pl.BlockSpec((1, H, hd, ts), lambda b, i: (b, 0, 0, i)),
                   pl.BlockSpec((1, H, ts, hd), lambda b, i: (b, 0, i, 0)),
                   pl.BlockSpec((1, H, dv, ts), lambda b, i: (b, 0, 0, i))],
        out_shape=[jax.ShapeDtypeStruct((B, H, hd, S), BF16),
                   jax.ShapeDtypeStruct((B, H, S, hd), BF16),
                   jax.ShapeDtypeStruct((B, H, dv, S), BF16)],
        compiler_params=_cp(("arbitrary", "arbitrary"), 48),
        name="mla_front",
    )(x, scale_v, shift, win_p, g_q.reshape(1, -1), g_kv.reshape(1, -1), wqt, wk, wvt, tab, tab_t)

    t = min(512, S)
    o = pl.pallas_call(
        functools.partial(_flash_causal_kernel, t=t),
        grid=(B, H),
        in_specs=[pl.BlockSpec((1, 1, hd, S), lambda b, h: (b, h, 0, 0)),
                  pl.BlockSpec((1, 1, S, hd), lambda b, h: (b, h, 0, 0)),
                  pl.BlockSpec((1, 1, dv, S), lambda b, h: (b, h, 0, 0))],
        out_specs=pl.BlockSpec((1, S, dv), lambda b, h: (b, 0, h)),
        out_shape=jax.ShapeDtypeStruct((B, S, H * dv), BF16),
        compiler_params=_cp(("arbitrary", "arbitrary"), 48),
        name="mla_attn",
    )(qt, k, vt)
    return _out_ln(o, w_o, x, gate, ln_g, ln_b)


def _proj_kernel(*refs, segs, chunk, has_rope):
    x_ref, sc_ref, sh_ref, w_ref, wt_ref = refs[:5]
    n_in = 9 if has_rope else 5
    outs = refs[n_in:]
    hb = (x_ref[0] * (1.0 + sc_ref[0]) + sh_ref[0]).astype(BF16)
    if has_rope:
        cos, sin = refs[5][0], refs[6][0]
        cos_t, sin_t = refs[7][0], refs[8][0]
    half = LANES // 2
    for (c0, width, kind, scale, transposed), o_ref in zip(segs, outs):
        for j in range(0, width, chunk):
            cw = min(chunk, width - j)
            if transposed:
                r = _nt_dot(wt_ref[c0 + j:c0 + j + cw, :], hb)
            else:
                r = _dot(hb, w_ref[:, c0 + j:c0 + j + cw])
            if kind == "rope":
                parts = []
                for t in range(0, cw, LANES):
                    if transposed:
                        u = r[t:t + LANES]
                        parts.append(u * cos_t + pltpu.roll(u, half, axis=0) * sin_t)
                    else:
                        u = r[:, t:t + LANES]
                        parts.append(u * cos + pltpu.roll(u, half, axis=1) * sin)
                r = jnp.concatenate(parts, axis=0 if transposed else 1) if len(parts) > 1 else parts[0]
            elif kind == "sigmoid":
                r = jax.nn.sigmoid(r)
            if scale != 1.0:
                r = r * scale
            if transposed:
                o_ref[0, j:j + cw, :] = r.astype(o_ref.dtype)
            else:
                o_ref[0, :, j:j + cw] = r.astype(o_ref.dtype)


def _mod_proj(x, shift, scale_v, w, wt, segs, out_dtypes, rope=None, ts=512, chunk=256):
    B, S, D = x.shape
    ts = min(ts, S)
    in_specs = [pl.BlockSpec((1, ts, D), lambda b, i: (b, i, 0)),
                pl.BlockSpec((1, 1, D), lambda b, i: (b, 0, 0)),
                pl.BlockSpec((1, 1, D), lambda b, i: (b, 0, 0)),
                pl.BlockSpec(w.shape, lambda b, i: (0, 0)),
                pl.BlockSpec(wt.shape, lambda b, i: (0, 0))]
    args = [x, scale_v, shift, w, wt]
    if rope is not None:
        cos, sin = rope
        in_specs += [pl.BlockSpec((1, ts, LANES), lambda b, i: (b, i, 0))] * 2
        in_specs += [pl.BlockSpec((1, LANES, ts), lambda b, i: (b, 0, i))] * 2
        args += [cos, sin, jnp.swapaxes(cos, 1, 2), jnp.swapaxes(sin, 1, 2)]
    out_specs, out_shape = [], []
    for sg, dt in zip(segs, out_dtypes):
        if sg[4]:
            out_specs.append(pl.BlockSpec((1, sg[1], ts), lambda b, i: (b, 0, i)))
            out_shape.append(jax.ShapeDtypeStruct((B, sg[1], S), dt))
        else:
            out_specs.append(pl.BlockSpec((1, ts, sg[1]), lambda b, i: (b, i, 0)))
            out_shape.append(jax.ShapeDtypeStruct((B, S, sg[1]), dt))
    return pl.pallas_call(
        functools.partial(_proj_kernel, segs=tuple(segs), chunk=chunk, has_rope=rope is not None),
        grid=(B, S // ts),
        in_specs=in_specs,
        out_specs=out_specs,
        out_shape=out_shape,
        compiler_params=_cp(("arbitrary", "arbitrary"), 48),
        name="mod_proj",
    )(*args)


def _nsa_cmp_kernel(kc_ref, vc_ref, pek_ref, pev_ref, wk1_ref, wk2_ref, wv1_ref, wv2t_ref, ko_ref, vto_ref):
    half = NSA_CMP_BLOCK // 2
    nchunk = kc_ref.shape[1]

    def hidden(src, pe_ref, w1_ref, kh):
        a = jnp.zeros((nchunk, w1_ref.shape[2]), F32)
        bm = jnp.zeros((nchunk, w1_ref.shape[2]), F32)
        for l in range(half):
            c0 = (l * NSA_KVH + kh) * NSA_DH
            t = src[0, :, c0:c0 + NSA_DH].astype(F32)
            a = a + _dot((t + pe_ref[l:l + 1, :]).astype(BF16), w1_ref[l])
            bm = bm + _dot((t + pe_ref[half + l:half + l + 1, :]).astype(BF16), w1_ref[half + l])
        hid = a + pltpu.roll(bm, nchunk - 1, axis=0)
        return (hid * jax.nn.sigmoid(hid)).astype(BF16)

    rowi = lax.broadcasted_iota(jnp.int32, (nchunk, 1), 0)
    coli = lax.broadcasted_iota(jnp.int32, (1, nchunk), 1)
    for kh in range(NSA_KVH):
        out = _dot(hidden(kc_ref, pek_ref, wk1_ref, kh), wk2_ref[...])
        ko_ref[0, kh] = jnp.where(rowi < nchunk - 1, out, 0.0).astype(ko_ref.dtype)
        out_t = _nt_dot(wv2t_ref[...], hidden(vc_ref, pev_ref, wv1_ref, kh))
        vto_ref[0, kh] = jnp.where(coli < nchunk - 1, out_t, 0.0).astype(vto_ref.dtype)


def _nsa_attn_kernel(qt_ref, kc_ref, vct_ref, ks_ref, vst_ref, kw_ref, vwt_ref, gt_ref, ovt_ref, e_ref, o_ref,
                     q4_sc, m_sc, l_sc, acc_sc, *, tq, tk, span, n_slc):
    qi = pl.program_id(2)
    G = NSA_G
    for g in range(G):
        q4_sc[:, g * tq:(g + 1) * tq] = qt_ref[0, g * NSA_DH:(g + 1) * NSA_DH, :]
    q4 = q4_sc[...]
    qpos = qi * tq + lax.broadcasted_iota(jnp.int32, (1, tq), 1)

    def tile_g(a):
        return jnp.concatenate([a] * G, axis=1)

    qpos4 = tile_g(qpos)

    kstart = pl.multiple_of(jnp.maximum(qi * tq + tq - span, 0), tq)
    s_win = _dot(kw_ref[0, pl.ds(kstart, span), :], q4)

    n_cmp = kc_ref.shape[2]
    nid = lax.broadcasted_iota(jnp.int32, (n_cmp, 1), 0)
    cmask = (nid * NSA_CMP_STRIDE + (NSA_CMP_BLOCK - 1) <= qpos4) & (nid < n_cmp - 1)
    s = jnp.where(cmask, _dot(kc_ref[0, 0], q4), NEG)
    p = jnp.where(cmask, jnp.exp2(s - jnp.max(s, axis=0, keepdims=True)), 0.0)
    l = jnp.sum(p, axis=0, keepdims=True)
    p_cmp = p / jnp.where(l > 0.0, l, 1.0)
    o_cmp = _dot(vct_ref[0, 0], p_cmp.astype(BF16))

    diff = qpos - (kstart + lax.broadcasted_iota(jnp.int32, (span, 1), 0))
    s_win = s_win + tile_g(jnp.where((diff >= 0) & (diff < NSA_WINDOW), 0.0, NEG))
    p = jnp.exp2(s_win - jnp.max(s_win, axis=0, keepdims=True))
    o_win = _dot(vwt_ref[0, :, pl.ds(kstart, span)], p.astype(BF16)) / jnp.sum(p, axis=0, keepdims=True)

    psum = p_cmp[:, 0:tq]
    for g in range(1, G):
        psum = psum + p_cmp[:, g * tq:(g + 1) * tq]
    p_hi, p_lo = _split_bf16(psum)
    imp = _dot(ovt_ref[...], p_hi) + _dot(ovt_ref[...], p_lo)
    nb = ovt_ref.shape[0]
    blk = lax.broadcasted_iota(jnp.int32, (nb, 1), 0)
    cur = jnp.right_shift(qpos, SLC_SHIFT)
    forced = (blk == 0) | (blk == cur) | (blk == cur - 1)
    valid = (blk * NSA_SLC_BLOCK <= qpos) & (blk < n_slc)
    work = jnp.where(valid, imp + jnp.where(forced, NSA_BONUS, 0.0), -jnp.inf)
    beaten = jnp.zeros((nb, tq), F32)
    for j in range(n_slc):
        wj = work[j:j + 1, :]
        beaten = beaten + jnp.where((wj > work) | ((wj == work) & (blk > j)), 1.0, 0.0)
    sel = jnp.where(beaten < float(min(NSA_TOPK, n_slc)), 1.0, 0.0).astype(BF16)

    m_sc[...] = jnp.full(m_sc.shape, NEG, F32)
    l_sc[...] = jnp.zeros(l_sc.shape, F32)
    acc_sc[...] = jnp.zeros(acc_sc.shape, F32)
    n_kt = ((qi + 1) * tq + tk - 1) // tk

    def slc_tiles(n):
        def scores(kt):
            return _dot(ks_ref[0, kt * tk:(kt + 1) * tk, :], q4)

        def value_matmul(kt, p, a):
            acc_sc[...] = a * acc_sc[...] + _dot(vst_ref[0, :, kt * tk:(kt + 1) * tk], p)

        s, pending = scores(0), None
        for kt in range(n):
            s_next = scores(kt + 1) if kt + 1 < n else None
            if pending is not None:
                value_matmul(*pending)
            chosen = _dot(e_ref[kt * tk:(kt + 1) * tk, :], sel)
            kpos = kt * tk + lax.broadcasted_iota(jnp.int32, (tk, 1), 0)
            bias = jnp.where((chosen > 0.5) & (kpos <= qpos), 0.0, NEG)
            s = s + tile_g(bias)
            m_prev = m_sc[...]
            m_new = jnp.maximum(m_prev, jnp.max(s, axis=0, keepdims=True))
            p = jnp.exp2(s - m_new)
            a = jnp.exp2(m_prev - m_new)
            l_sc[...] = a * l_sc[...] + jnp.sum(p, axis=0, keepdims=True)
            m_sc[...] = m_new
            pending = (kt, p.astype(BF16), a)
            s = s_next
        value_matmul(*pending)

    for n in range(1, (ks_ref.shape[1] + tk - 1) // tk + 1):
        pl.when(n_kt == n)(functools.partial(slc_tiles, n))
    o_slc = acc_sc[...] / l_sc[...]

    gt = gt_ref[0]
    for g in range(G):
        cols = slice(g * tq, (g + 1) * tq)
        o = (gt[3 * g:3 * g + 1] * o_cmp[:, cols] + gt[3 * g + 1:3 * g + 2] * o_slc[:, cols]
             + gt[3 * g + 2:3 * g + 3] * o_win[:, cols])
        o_ref[0, :, g * NSA_DH:(g + 1) * NSA_DH] = o.T.astype(o_ref.dtype)


def _nsa_layer(x, shift, scale_v, gate, cos, sin, w_in, pe_k, pe_v, w_ck1, w_ck2, w_cv1, w_cv2, w_o,
               ln_g, ln_b):
    B, S, D = x.shape
    H, KH, Dh, G = NSA_HEADS, NSA_KVH, NSA_DH, NSA_G
    kvw = KH * Dh
    cuts = [int(v) for v in np.cumsum([H * Dh] + [kvw] * 6)]
    wq, wkc, wvc, wks, wvs, wkw, wvw, wgl = jnp.split(w_in, cuts, axis=1)
    wgl = jnp.pad(wgl.reshape(D, KH, G * 3), ((0, 0), (0, 0), (0, GATE_ROWS - G * 3))).reshape(D, KH * GATE_ROWS)
    w = jnp.concatenate([wkc, wks, wkw, wvc], axis=1).astype(BF16)
    wt = jnp.concatenate([wq, wvs, wvw, wgl], axis=1).T.astype(BF16)
    segs = [(0, H * Dh, "rope", LOG2E * Dh ** -0.5, True),
            (0, kvw, "rope", 1.0, False), (kvw, kvw, "rope", 1.0, False), (2 * kvw, kvw, "rope", 1.0, False),
            (3 * kvw, kvw, "plain", 1.0, False),
            (H * Dh, kvw, "plain", 1.0, True), (H * Dh + kvw, kvw, "plain", 1.0, True),
            (H * Dh + 2 * kvw, KH * GATE_ROWS, "sigmoid", 1.0, True)]
    rope = (jnp.concatenate([cos, cos], axis=-1), jnp.concatenate([-sin, sin], axis=-1))
    qt, kc, ks, kw, vc, vst, vwt, gates_t = _mod_proj(
        x, shift, scale_v, w, wt, segs, [BF16] * 7 + [F32], rope=rope)

    st = NSA_CMP_STRIDE
    nchunk = S // st
    kc2 = kc.reshape(B, nchunk, st * kvw)
    vc2 = vc.reshape(B, nchunk, st * kvw)
    hid = w_ck1.shape[-1]
    full = lambda shape: pl.BlockSpec(shape, lambda b: (0,) * len(shape))
    k_cmp, v_cmp_t = pl.pallas_call(
        _nsa_cmp_kernel,
        grid=(B,),
        in_specs=[pl.BlockSpec((1, nchunk, st * kvw), lambda b: (b, 0, 0)),
                  pl.BlockSpec((1, nchunk, st * kvw), lambda b: (b, 0, 0)),
                  full((NSA_CMP_BLOCK, Dh)), full((NSA_CMP_BLOCK, Dh)),
                  full((NSA_CMP_BLOCK, Dh, hid)), full((hid, Dh)),
                  full((NSA_CMP_BLOCK, Dh, hid)), full((Dh, hid))],
        out_specs=[pl.BlockSpec((1, KH, nchunk, Dh), lambda b: (b, 0, 0, 0)),
                   pl.BlockSpec((1, KH, Dh, nchunk), lambda b: (b, 0, 0, 0))],
        out_shape=[jax.ShapeDtypeStruct((B, KH, nchunk, Dh), BF16),
                   jax.ShapeDtypeStruct((B, KH, Dh, nchunk), BF16)],
        compiler_params=_cp(("arbitrary",), 32),
        name="nsa_cmp",
    )(kc2, vc2, pe_k, pe_v, w_ck1.astype(BF16), w_ck2.astype(BF16), w_cv1.astype(BF16), w_cv2.T.astype(BF16))

    n_slc = S // NSA_SLC_BLOCK
    nb = 32
    assert n_slc <= nb and nchunk % 8 == 0
    cmp_start = np.arange(nchunk) * st
    slc_start = np.arange(n_slc) * NSA_SLC_BLOCK
    ov = np.clip(np.minimum(cmp_start[:, None] + NSA_CMP_BLOCK, slc_start[None, :] + NSA_SLC_BLOCK)
                 - np.maximum(cmp_start[:, None], slc_start[None, :]), 0, None).astype(np.float32) / NSA_CMP_BLOCK
    ov[nchunk - 1:] = 0.0
    ov_t = np.zeros((nb, nchunk), np.float32)
    ov_t[:n_slc] = ov.T
    key_blk = np.zeros((S, nb), np.float32)
    key_blk[np.arange(S), np.arange(S) // NSA_SLC_BLOCK] = 1.0

    tq = min(128, S)
    tk = min(256, S)
    span = min(NSA_WINDOW + tq, S)
    k_spec = pl.BlockSpec((1, S, Dh), lambda b, kh, i: (b, 0, kh))
    vt_spec = pl.BlockSpec((1, Dh, S), lambda b, kh, i: (b, kh, 0))
    o = pl.pallas_call(
        functools.partial(_nsa_attn_kernel, tq=tq, tk=tk, span=span, n_slc=n_slc),
        grid=(B, KH, S // tq),
        in_specs=[pl.BlockSpec((1, G * Dh, tq), lambda b, kh, i: (b, kh, i)),
                  pl.BlockSpec((1, 1, nchunk, Dh), lambda b, kh, i: (b, kh, 0, 0)),
                  pl.BlockSpec((1, 1, Dh, nchunk), lambda b, kh, i: (b, kh, 0, 0)),
                  k_spec, vt_spec, k_spec, vt_spec,
                  pl.BlockSpec((1, GATE_ROWS, tq), lambda b, kh, i: (b, kh, i)),
                  pl.BlockSpec((nb, nchunk), lambda b, kh, i: (0, 0)),
                  pl.BlockSpec((S, nb), lambda b, kh, i: (0, 0))],
        out_specs=pl.BlockSpec((1, tq, G * Dh), lambda b, kh, i: (b, i, kh)),
        out_shape=jax.ShapeDtypeStruct((B, S, H * Dh), BF16),
        scratch_shapes=[pltpu.VMEM((Dh, G * tq), BF16), pltpu.VMEM((1, G * tq), F32),
                        pltpu.VMEM((1, G * tq), F32), pltpu.VMEM((Dh, G * tq), F32)],
        compiler_params=_cp(("arbitrary", "arbitrary", "arbitrary"), 32),
        name="nsa_attn",
    )(qt, k_cmp, v_cmp_t, ks, vst, kw, vwt, gates_t, jnp.asarray(ov_t, BF16), jnp.asarray(key_blk, BF16))
    return _out_ln(o, w_o, x, gate, ln_g, ln_b)


def _sb_attn_kernel(qt_ref, k_ref, vt_ref, u_ref, o_ref, r_sc, acc_sc, *, tq, tk):
    d, s_len = vt_ref.shape[1], vt_ref.shape[2]
    u = u_ref[...]
    nd = tq // tk

    def weights(z, strict):
        lk = -(jnp.maximum(z, 0.0) + jnp.log2(1.0 + jnp.exp2(-jnp.abs(z))))
        if strict is not None:
            lk = jnp.where(strict, lk, 0.0)
        hi, lo = _split_bf16(lk)
        between = r_sc[...] + _dot(u, hi) + _dot(u, lo)
        a = jnp.exp2(lk + z + between)
        if strict is not None:
            a = jnp.where(strict, a, 0.0)
        r_sc[...] += jnp.sum(lk, axis=0, keepdims=True)
        return a.astype(BF16)

    for qi in range(s_len // tq):
        qt = qt_ref[0, :, qi * tq:(qi + 1) * tq]
        r_sc[...] = jnp.zeros(r_sc.shape, F32)
        acc_sc[...] = jnp.zeros(acc_sc.shape, F32)
        last = (qi + 1) * nd - 1
        tiles = [last - i for i in range(nd)] + ([qi * nd - 1] if qi > 0 else [])
        qpos = qi * tq + lax.broadcasted_iota(jnp.int32, (1, tq), 1)

        def scores(kt):
            return _dot(k_ref[0, kt * tk:(kt + 1) * tk, :], qt)

        z, pending = scores(tiles[0]), None
        for i, kt in enumerate(tiles):
            nxt = tiles[i + 1] if i + 1 < len(tiles) else max(qi * nd - 2, 0)
            z_next = scores(nxt)
            if pending is not None:
                acc_sc[...] += _dot(vt_ref[0, :, pending[0] * tk:(pending[0] + 1) * tk], pending[1])
            strict = (kt * tk + lax.broadcasted_iota(jnp.int32, (tk, 1), 0) < qpos) if kt >= qi * nd else None
            pending = (kt, weights(z, strict))
            z = z_next
        acc_sc[...] += _dot(vt_ref[0, :, pending[0] * tk:(pending[0] + 1) * tk], pending[1])

        if qi * nd - 2 >= 0:
            def scores_dyn(kt):
                return _dot(k_ref[0, pl.ds(pl.multiple_of(kt * tk, tk), tk), :], qt)

            def cond(carry):
                kt, _ = carry
                return (kt >= 0) & (jnp.max(r_sc[...]) > SB_EXIT_LOG2)

            def body(carry):
                kt, z = carry
                z_next = scores_dyn(jnp.maximum(kt - 1, 0))
                a = weights(z, None)
                acc_sc[...] += _dot(vt_ref[0, :, pl.ds(pl.multiple_of(kt * tk, tk), tk)], a)
                return kt - 1, z_next

            lax.while_loop(cond, body, (qi * nd - 2, z))
        o_ref[0, qi * tq:(qi + 1) * tq, :] = acc_sc[...].T.astype(o_ref.dtype)


def _sb_layer(x, shift, scale_v, gate, w_qkv, w_o, ln_g, ln_b):
    B, S, D = x.shape
    H, d = SB_HEADS, SB_DH
    hw = H * d
    w = w_qkv[:, hw:2 * hw].astype(BF16)
    wt = jnp.concatenate([w_qkv[:, :hw], w_qkv[:, 2 * hw:]], axis=1).T.astype(BF16)
    segs = [(0, hw, "plain", LOG2E * d ** -0.5, True), (0, hw, "plain", 1.0, False), (hw, hw, "plain", 1.0, True)]
    qt, k, vt = _mod_proj(x, shift, scale_v, w, wt, segs, [BF16] * 3)
    tq = min(512, S)
    tk = min(256, S)
    tri = np.triu(np.ones((tk, tk), np.float32), 1)
    o = pl.pallas_call(
        functools.partial(_sb_attn_kernel, tq=tq, tk=tk),
        grid=(B, H),
        in_specs=[pl.BlockSpec((1, d, S), lambda b, h: (b, h, 0)),
                  pl.BlockSpec((1, S, d), lambda b, h: (b, 0, h)),
                  pl.BlockSpec((1, d, S), lambda b, h: (b, h, 0)),
                  pl.BlockSpec((tk, tk), lambda b, h: (0, 0))],
        out_specs=pl.BlockSpec((1, S, d), lambda b, h: (b, 0, h)),
        out_shape=jax.ShapeDtypeStruct((B, S, hw), BF16),
        scratch_shapes=[pltpu.VMEM((1, tq), F32), pltpu.VMEM((d, tq), F32)],
        compiler_params=_cp(("arbitrary", "arbitrary"), 32),
        name="sb_attn",
    )(qt, k, vt, jnp.asarray(tri, BF16))
    return _out_ln(o, w_o, x, gate, ln_g, ln_b)


def _router_kernel(x_ref, sc_ref, sh_ref, whi_ref, wlo_ref, b_ref, tri_ref, hb_ref, info_ref, cnt_ref,
                   base_sc):
    first = (pl.program_id(0) == 0) & (pl.program_id(1) == 0)

    @pl.when(first)
    def _():
        base_sc[...] = jnp.zeros(base_sc.shape, F32)

    h = x_ref[0] * (1.0 + sc_ref[0]) + sh_ref[0]
    h_hi, h_lo = _split_bf16(h)
    hb_ref[0] = _pack_pairs(h)
    logit = (_nt_dot(whi_ref[...], h_hi) + _nt_dot(whi_ref[...], h_lo) + _nt_dot(wlo_ref[...], h_hi)
             + b_ref[...])
    tm = logit.shape[1]
    epg = MOE_EPG
    row = lax.broadcasted_iota(jnp.int32, (epg, 1), 0)
    row_f = row.astype(F32)

    def first_max(vals):
        mx = jnp.max(vals, axis=0, keepdims=True)
        idx = jnp.min(jnp.where(vals == mx, row_f, float(epg)), axis=0, keepdims=True)
        return mx, idx

    gl = jnp.where(row < MOE_GROUPS, logit[0:epg], -jnp.inf)
    gmax, g_idx = first_max(gl)
    g_w = 1.0 / jnp.sum(jnp.exp(gl - gmax), axis=0, keepdims=True)
    el = logit[epg:2 * epg]
    for g in range(1, MOE_GROUPS):
        el = jnp.where(g_idx == float(g), logit[epg * (g + 1):epg * (g + 2)], el)
    ep = jnp.exp(el - jnp.max(el, axis=0, keepdims=True))
    prob = ep / jnp.sum(ep, axis=0, keepdims=True)
    p1, i1 = first_max(prob)
    p2, i2 = first_max(jnp.where(row_f == i1, -1.0, prob))
    e1 = g_idx * float(epg) + i1
    e2 = g_idx * float(epg) + i2
    w1 = g_w * p1 / (p1 + p2)
    w2 = g_w * p2 / (p1 + p2)

    erow = lax.broadcasted_iota(jnp.int32, (MOE_E, 1), 0).astype(F32)
    hot1 = erow == e1
    hot2 = erow == e2
    cnt = jnp.where(hot1 | hot2, 1.0, 0.0)
    before = base_sc[:, 0:1] + _dot(cnt.astype(BF16), tri_ref[...])
    r1 = jnp.sum(jnp.where(hot1, before, 0.0), axis=0, keepdims=True)
    r2 = jnp.sum(jnp.where(hot2, before, 0.0), axis=0, keepdims=True)
    base_sc[...] += jnp.sum(cnt, axis=1, keepdims=True)
    cnt_ref[...] = base_sc[...]

    irow = lax.broadcasted_iota(jnp.int32, (INFO_ROWS, 1), 0)
    info = jnp.zeros((INFO_ROWS, tm), F32)
    for k, val in enumerate((e1, e2, r1, r2, w1, w2)):
        info = jnp.where(irow == k, val, info)
    info_ref[0] = info


def _ffn_kernel(be_ref, nu_ref, x_ref, wg_ref, wu_ref, wd_ref, y_ref, wg_sc, wu_sc, wd_sc):
    i = pl.program_id(0)
    new_expert = (i == 0) | (be_ref[i] != be_ref[jnp.maximum(i - 1, 0)])

    @pl.when(new_expert & (i < nu_ref[0]))
    def _():
        wg_sc[...] = wg_ref[0, 0].astype(BF16)
        wu_sc[...] = wu_ref[0, 0].astype(BF16)
        wd_sc[...] = wd_ref[0, 0].astype(BF16)

    @pl.when(i < nu_ref[0])
    def _():
        xb = _unpack_pairs(x_ref[...]).astype(BF16)
        g = _dot(xb, wg_sc[...])
        u = _dot(xb, wu_sc[...])
        a = (g * jax.nn.sigmoid(g) * u).astype(BF16)
        y_ref[...] = _pack_pairs(_dot(a, wd_sc[...]))

    @pl.when(i >= nu_ref[0])
    def _():
        y_ref[...] = jnp.zeros(y_ref.shape, y_ref.dtype)


def _sc_gather(table, idx):
    n, d = idx.shape[0], table.shape[1]
    workers = SC_CORES * SC_SUBCORES
    per_w = n // workers
    assert per_w * workers == n and per_w % SC_CHUNK == 0
    mesh = plsc.VectorSubcoreMesh(core_axis_name="c", subcore_axis_name="s")

    @functools.partial(
        pl.kernel, mesh=mesh,
        out_type=jax.ShapeDtypeStruct((n, d), table.dtype),
        scratch_types=[pltpu.VMEM((SC_CHUNK,), jnp.int32),
                       pltpu.VMEM((SC_CHUNK, d), table.dtype),
                       pltpu.SemaphoreType.DMA],
        name="sc_gather")
    def gather(table_hbm, idx_hbm, out_hbm, idx_v, rows_v, sem):
        wid = lax.axis_index("s") * SC_CORES + lax.axis_index("c")

        @pl.loop(0, per_w // SC_CHUNK)
        def _(j):
            base = pl.multiple_of(wid * per_w + j * SC_CHUNK, SC_CHUNK)
            pltpu.sync_copy(idx_hbm.at[pl.ds(base, SC_CHUNK)], idx_v)
            pltpu.async_copy(table_hbm.at[idx_v], rows_v, sem).wait()
            pltpu.sync_copy(rows_v, out_hbm.at[pl.ds(base, SC_CHUNK)])

    return gather(table, idx)


def _sc_dispatch(src, dest0, dest1, n_out):
    n, d = src.shape
    workers = SC_CORES * SC_SUBCORES
    per_w = n // workers
    assert per_w * workers == n and per_w % SC_CHUNK == 0
    mesh = plsc.VectorSubcoreMesh(core_axis_name="c", subcore_axis_name="s")

    @functools.partial(
        pl.kernel, mesh=mesh,
        out_type=jax.ShapeDtypeStruct((n_out, d), src.dtype),
        scratch_types=[pltpu.VMEM((SC_CHUNK,), jnp.int32),
                       pltpu.VMEM((SC_CHUNK,), jnp.int32),
                       pltpu.VMEM((SC_CHUNK, d), src.dtype),
                       pltpu.SemaphoreType.DMA],
        name="sc_dispatch")
    def dispatch(src_hbm, d0_hbm, d1_hbm, out_hbm, i0_v, i1_v, rows_v, sem):
        wid = lax.axis_index("s") * SC_CORES + lax.axis_index("c")

        @pl.loop(0, per_w // SC_CHUNK)
        def _(j):
            base = pl.multiple_of(wid * per_w + j * SC_CHUNK, SC_CHUNK)
            pltpu.sync_copy(d0_hbm.at[pl.ds(base, SC_CHUNK)], i0_v)
            pltpu.sync_copy(d1_hbm.at[pl.ds(base, SC_CHUNK)], i1_v)
            pltpu.sync_copy(src_hbm.at[pl.ds(base, SC_CHUNK)], rows_v)
            pltpu.async_copy(rows_v, out_hbm.at[i0_v], sem).wait()
            pltpu.async_copy(rows_v, out_hbm.at[i1_v], sem).wait()

    return dispatch(src, dest0, dest1)


def _moe_out_kernel(x_ref, y1_ref, y2_ref, info_ref, gate_ref, g_ref, b_ref, o_ref):
    info = info_ref[0]
    ts = info.shape[1]
    info_tok = jnp.concatenate([info, jnp.zeros((LANES - INFO_ROWS, ts), F32)], axis=0).T
    y = info_tok[:, 4:5] * _unpack_pairs(y1_ref[0, 0]) + info_tok[:, 5:6] * _unpack_pairs(y2_ref[0, 0])
    z = ALPHA * x_ref[0] + gate_ref[0] * y
    o_ref[0] = _layer_norm(z, g_ref[...], b_ref[...])


def _moe_out(x, y12, info, gate, g, b, ts=512):
    B, S, D = x.shape
    ts = min(ts, S)
    tok = pl.BlockSpec((1, ts, D), lambda bi, i: (bi, i, 0))
    return pl.pallas_call(
        _moe_out_kernel,
        grid=(B, S // ts),
        in_specs=[tok,
                  pl.BlockSpec((1, 1, ts, D // 2), lambda bi, i: (0, bi, i, 0)),
                  pl.BlockSpec((1, 1, ts, D // 2), lambda bi, i: (1, bi, i, 0)),
                  pl.BlockSpec((1, INFO_ROWS, ts), lambda bi, i: (bi, 0, i)),
                  pl.BlockSpec((1, 1, D), lambda bi, i: (bi, 0, 0)),
                  pl.BlockSpec((1, D), lambda bi, i: (0, 0)),
                  pl.BlockSpec((1, D), lambda bi, i: (0, 0))],
        out_specs=tok,
        out_shape=jax.ShapeDtypeStruct((B, S, D), F32),
        compiler_params=_cp(("arbitrary", "arbitrary"), 40),
        name="moe_out",
    )(x, y12, y12, info, gate, g.reshape(1, D), b.reshape(1, D))


def _moe_layer(x, shift, scale_v, gate, w_grp, b_grp, w_rt, b_rt, layer, w_gate, w_up, w_down, ln_g, ln_b):
    B, S, D = x.shape
    T = B * S
    E, R, F = MOE_E, MOE_ROWS, w_gate.shape[-1]
    g0 = MOE_EPG
    w_r = jnp.zeros((ROUTER_ROWS, D), F32).at[:MOE_GROUPS].set(w_grp.T).at[g0:g0 + E].set(w_rt.T)
    w_hi = w_r.astype(BF16)
    w_lo = (w_r - w_hi.astype(F32)).astype(BF16)
    b_r = jnp.zeros((ROUTER_ROWS, 1), F32).at[:MOE_GROUPS, 0].set(b_grp).at[g0:g0 + E, 0].set(b_rt)
    tm = min(256, S)
    tri = jnp.asarray(np.triu(np.ones((tm, tm), np.float32), 1), BF16)
    hb, info, cnt = pl.pallas_call(
        _router_kernel,
        grid=(B, S // tm),
        in_specs=[pl.BlockSpec((1, tm, D), lambda b, i: (b, i, 0)),
                  pl.BlockSpec((1, 1, D), lambda b, i: (b, 0, 0)),
                  pl.BlockSpec((1, 1, D), lambda b, i: (b, 0, 0)),
                  pl.BlockSpec((ROUTER_ROWS, D), lambda b, i: (0, 0)),
                  pl.BlockSpec((ROUTER_ROWS, D), lambda b, i: (0, 0)),
                  pl.BlockSpec((ROUTER_ROWS, 1), lambda b, i: (0, 0)),
                  pl.BlockSpec((tm, tm), lambda b, i: (0, 0))],
        out_specs=[pl.BlockSpec((1, tm, D // 2), lambda b, i: (b, i, 0)),
                   pl.BlockSpec((1, INFO_ROWS, tm), lambda b, i: (b, 0, i)),
                   pl.BlockSpec((E, LANES), lambda b, i: (0, 0))],
        out_shape=[jax.ShapeDtypeStruct((B, S, D // 2), jnp.uint32),
                   jax.ShapeDtypeStruct((B, INFO_ROWS, S), F32),
                   jax.ShapeDtypeStruct((E, LANES), F32)],
        scratch_shapes=[pltpu.VMEM((E, LANES), F32)],
        compiler_params=_cp(("arbitrary", "arbitrary"), 32),
        name="moe_router",
    )(x, scale_v, shift, w_hi, w_lo, b_r, tri)

    counts = cnt[:, 0].astype(jnp.int32)
    padded = (counts + R - 1) // R * R
    ends = jnp.cumsum(padded)
    starts = ends - padded
    expert_ids = jnp.arange(E, dtype=jnp.int32)

    def row_of(expert, rank):
        start = jnp.sum(jnp.where(expert.astype(jnp.int32)[..., None] == expert_ids, starts, 0), axis=-1)
        return (start + rank.astype(jnp.int32)).reshape(T)

    dest0 = row_of(info[:, 0, :], info[:, 2, :])
    dest1 = row_of(info[:, 1, :], info[:, 3, :])
    n_blk = (T * 2) // R + E
    P = n_blk * R
    blk_exp = jnp.minimum(jnp.sum(jnp.arange(n_blk)[:, None] * R >= ends[None, :], axis=1), E - 1).astype(jnp.int32)
    n_used = (ends[-1] // R).astype(jnp.int32).reshape(1)
    x_rows = _sc_dispatch(hb.reshape(T, D // 2), dest0, dest1, P)

    y_rows = pl.pallas_call(
        _ffn_kernel,
        grid_spec=pltpu.PrefetchScalarGridSpec(
            num_scalar_prefetch=2,
            grid=(n_blk,),
            in_specs=[pl.BlockSpec((R, D // 2), lambda i, be, nu: (i, 0)),
                      pl.BlockSpec((1, 1, D, F), lambda i, be, nu: (layer, be[i], 0, 0)),
                      pl.BlockSpec((1, 1, D, F), lambda i, be, nu: (layer, be[i], 0, 0)),
                      pl.BlockSpec((1, 1, F, D), lambda i, be, nu: (layer, be[i], 0, 0))],
            out_specs=pl.BlockSpec((R, D // 2), lambda i, be, nu: (i, 0)),
            scratch_shapes=[pltpu.VMEM((D, F), BF16), pltpu.VMEM((D, F), BF16), pltpu.VMEM((F, D), BF16)]),
        out_shape=jax.ShapeDtypeStruct((P, D // 2), jnp.uint32),
        compiler_params=_cp(("arbitrary",), 48),
        name="moe_ffn",
    )(blk_exp, n_used, x_rows, w_gate, w_up, w_down)

    y12 = _sc_gather(y_rows, jnp.concatenate([dest0, dest1])).reshape(2, B, S, D // 2)
    return _moe_out(x, y12, info, gate, ln_g, ln_b)


def kernel(x, c, positions, ada_w, ada_b, ln_g, ln_b, mla_w_in, mla_g_q, mla_w_uq, mla_g_kv, mla_w_ukv, mla_w_o, nsa_w_in, nsa_pe_k, nsa_pe_v, nsa_w_ck1, nsa_w_ck2, nsa_w_cv1, nsa_w_cv2, nsa_w_o, sb_w_qkv, sb_w_o, moe_w_grp, moe_b_grp, moe_w_rt, moe_b_rt, moe_w_gate, moe_w_up, moe_w_down):
    D = x.shape[-1]
    depth = ada_w.shape[0]
    cos_mla, sin_mla = _rope_tables(positions, MLA_DR)
    cos_nsa, sin_nsa = _rope_tables(positions, NSA_DH)
    mods = _ada_all(c, ada_w, ada_b)

    def mod(i, sub):
        m = mods[2 * i + sub][:, None, :]
        return m[..., :D], m[..., D:2 * D], m[..., 2 * D:]

    for i in range(depth):
        kind, j = i % N_MIXERS, i // N_MIXERS
        shift, scale_v, gate = mod(i, 0)
        if kind == 0:
            x = _mla_layer(x, shift, scale_v, gate, cos_mla, sin_mla, mla_w_in[j], mla_g_q[j], mla_w_uq[j],
                           mla_g_kv[j], mla_w_ukv[j], mla_w_o[j], ln_g[i, 0], ln_b[i, 0])
        elif kind == 1:
            x = _nsa_layer(x, shift, scale_v, gate, cos_nsa, sin_nsa, nsa_w_in[j], nsa_pe_k[j], nsa_pe_v[j],
                           nsa_w_ck1[j], nsa_w_ck2[j], nsa_w_cv1[j], nsa_w_cv2[j], nsa_w_o[j],
                           ln_g[i, 0], ln_b[i, 0])
        else:
            x = _sb_layer(x, shift, scale_v, gate, sb_w_qkv[j], sb_w_o[j], ln_g[i, 0], ln_b[i, 0])
        shift, scale_v, gate = mod(i, 1)
        x = _moe_layer(x, shift, scale_v, gate, moe_w_grp[i], moe_b_grp[i], moe_w_rt[i], moe_b_rt[i],
                       i, moe_w_gate, moe_w_up, moe_w_down, ln_g[i, 1], ln_b[i, 1])
    return x
```

```python
import functools

import numpy as np
import jax
import jax.numpy as jnp
from jax import lax
from jax.experimental import pallas as pl
from jax.experimental.pallas import tpu as pltpu
from jax.experimental.pallas import tpu_sc as plsc

BF16 = jnp.bfloat16
F32 = jnp.float32

DEPTH = 4
N_MIXERS = 3
ROPE_THETA = 10000.0
LN_EPS = 1e-5
RMS_EPS = 1e-6
NEG = -1e30
LOG2E = 1.4426950408889634
ALPHA = (2 * DEPTH) ** 0.25

MLA_HEADS, MLA_DN, MLA_DR, MLA_DV = 8, 128, 64, 128
MLA_QR, MLA_KVR = 256, 256

NSA_HEADS, NSA_KVH, NSA_DH = 8, 2, 128
NSA_G = NSA_HEADS // NSA_KVH
NSA_CMP_BLOCK, NSA_CMP_STRIDE = 32, 16
NSA_SLC_BLOCK, NSA_TOPK = 64, 8
SLC_SHIFT = 6
NSA_WINDOW = 512
NSA_BONUS = 1e3
GATE_ROWS = 16

SB_HEADS, SB_DH = 8, 128
SB_EXIT_LOG2 = -160.0

MOE_GROUPS, MOE_EPG = 4, 8
EPG_SHIFT = 3
MOE_E = MOE_GROUPS * MOE_EPG
MOE_ROWS = 256
FFN_LEAD = 3
FFN_SLOTS = FFN_LEAD + 1
ROUTER_ROWS = 64
INFO_ROWS = 8

SC_CORES, SC_SUBCORES = 2, 16
SC_CHUNK = 64

LANES = 128
VMEM_BYTES_V7X = 64 * 1024 * 1024


def _cp(sem, vmem_mb):
    assert vmem_mb * 2**20 < VMEM_BYTES_V7X
    return pltpu.CompilerParams(dimension_semantics=sem, vmem_limit_bytes=vmem_mb * 2**20)


def _nt_dot(a, b):
    return lax.dot_general(a, b, (((1,), (1,)), ((), ())), preferred_element_type=F32)


def _dot(a, b):
    return jnp.dot(a, b, preferred_element_type=F32)


def _split_bf16(a):
    hi = a.astype(BF16)
    lo = (a - hi.astype(F32)).astype(BF16)
    return hi, lo


def _pack_pairs(a):
    n = a.shape[1] // 2
    hi = pltpu.bitcast(a[:, :n].astype(BF16).astype(F32), jnp.uint32)
    lo = pltpu.bitcast(a[:, n:].astype(BF16).astype(F32), jnp.uint32)
    return hi | (lo >> 16)


def _unpack_pairs(u):
    hi = pltpu.bitcast(u & jnp.uint32(0xFFFF0000), F32)
    lo = pltpu.bitcast(u << 16, F32)
    return jnp.concatenate([hi, lo], axis=1)


def _layer_norm(z, g, b):
    mu = jnp.mean(z, axis=-1, keepdims=True)
    d = z - mu
    var = jnp.mean(d * d, axis=-1, keepdims=True)
    return d * lax.rsqrt(var + LN_EPS) * g + b


def _ada_kernel(c_ref, w_ref, b_ref, o_ref):
    c = c_ref[...]
    s_hi, s_lo = _split_bf16(c * jax.nn.sigmoid(c))
    w_hi, w_lo = _split_bf16(w_ref[0])
    o_ref[0] = _dot(s_hi, w_hi) + _dot(s_lo, w_hi) + _dot(s_hi, w_lo) + b_ref[0]


def _ada_all(c, ada_w, ada_b):
    L, two, D, D3 = ada_w.shape
    B = c.shape[0]
    n = L * two
    tn = 1024
    return pl.pallas_call(
        _ada_kernel,
        grid=(n, D3 // tn),
        in_specs=[pl.BlockSpec((B, D), lambda i, j: (0, 0)),
                  pl.BlockSpec((1, D, tn), lambda i, j: (i, 0, j)),
                  pl.BlockSpec((1, 1, tn), lambda i, j: (i, 0, j))],
        out_specs=pl.BlockSpec((1, B, tn), lambda i, j: (i, 0, j)),
        out_shape=jax.ShapeDtypeStruct((n, B, D3), F32),
        compiler_params=_cp(("arbitrary", "arbitrary"), 32),
        name="ada",
    )(c, ada_w.reshape(n, D, D3), ada_b.reshape(n, 1, D3))


def _res_ln_kernel(x_ref, y_ref, gate_ref, g_ref, b_ref, o_ref):
    z = ALPHA * x_ref[0] + gate_ref[0] * y_ref[0]
    o_ref[0] = _layer_norm(z, g_ref[...], b_ref[...])


def _res_ln(x, y, gate, g, b, ts=512):
    B, S, D = x.shape
    ts = min(ts, S)
    tok = pl.BlockSpec((1, ts, D), lambda bi, i: (bi, i, 0))
    return pl.pallas_call(
        _res_ln_kernel,
        grid=(B, S // ts),
        in_specs=[tok, tok,
                  pl.BlockSpec((1, 1, D), lambda bi, i: (bi, 0, 0)),
                  pl.BlockSpec((1, D), lambda bi, i: (0, 0)),
                  pl.BlockSpec((1, D), lambda bi, i: (0, 0))],
        out_specs=tok,
        out_shape=jax.ShapeDtypeStruct((B, S, D), F32),
        compiler_params=_cp(("arbitrary", "arbitrary"), 32),
        name="res_ln",
    )(x, y, gate, g.reshape(1, D), b.reshape(1, D))


def _out_ln_kernel(o_ref, w_ref, x_ref, gate_ref, g_ref, b_ref, out_ref):
    y = _dot(o_ref[0], w_ref[...])
    z = ALPHA * x_ref[0] + gate_ref[0] * y
    out_ref[0] = _layer_norm(z, g_ref[...], b_ref[...])


def _out_ln(o, w_o, x, gate, g, b, ts=512):
    B, S, D = x.shape
    K = o.shape[-1]
    ts = min(ts, S)
    return pl.pallas_call(
        _out_ln_kernel,
        grid=(B, S // ts),
        in_specs=[pl.BlockSpec((1, ts, K), lambda bi, i: (bi, i, 0)),
                  pl.BlockSpec((K, D), lambda bi, i: (0, 0)),
                  pl.BlockSpec((1, ts, D), lambda bi, i: (bi, i, 0)),
                  pl.BlockSpec((1, 1, D), lambda bi, i: (bi, 0, 0)),
                  pl.BlockSpec((1, D), lambda bi, i: (0, 0)),
                  pl.BlockSpec((1, D), lambda bi, i: (0, 0))],
        out_specs=pl.BlockSpec((1, ts, D), lambda bi, i: (bi, i, 0)),
        out_shape=jax.ShapeDtypeStruct((B, S, D), F32),
        compiler_params=_cp(("arbitrary", "arbitrary"), 40),
        name="out_ln",
    )(o, w_o.astype(BF16), x, gate, g.reshape(1, D), b.reshape(1, D))


def _rope_tables(positions, dim):
    inv_freq = 1.0 / (ROPE_THETA ** (jnp.arange(0, dim, 2, dtype=F32) / dim))
    ang = positions.astype(F32)[..., None] * inv_freq
    return jnp.cos(ang), jnp.sin(ang)


def _rot_half_cols(w, half):
    return jnp.concatenate([-w[..., half:], w[..., :half]], axis=-1)


def _mla_front_kernel(x_ref, sc_ref, sh_ref, win_ref, gq_ref, gkv_ref, wqt_ref, wk_ref, wvt_ref, t_ref, tt_ref,
                      qt_ref, k_ref, vt_ref, *, scale):
    hb = (x_ref[0] * (1.0 + sc_ref[0]) + sh_ref[0]).astype(BF16)
    r = _dot(hb, win_ref[...])
    tab = t_ref[0]
    tab_t = tt_ref[0]
    low = lax.broadcasted_iota(jnp.int32, tab.shape, 1) < MLA_DR

    def rms(cx, g):
        return (cx * lax.rsqrt(jnp.mean(cx * cx, axis=-1, keepdims=True) + RMS_EPS) * g).astype(BF16)

    cq = rms(r[:, :MLA_QR], gq_ref[...])
    ckv = rms(r[:, MLA_QR:MLA_QR + MLA_KVR], gkv_ref[...])
    t = r[:, MLA_QR + MLA_KVR:] * tab
    k_rope = jnp.where(low, t + pltpu.roll(t, MLA_DR, axis=1), 0.0).astype(BF16)
    zeros = jnp.zeros((MLA_DR, hb.shape[0]), BF16)
    for h in range(MLA_HEADS):
        qh = _nt_dot(wqt_ref[h], cq)
        tq = qh[MLA_DN:] * tab_t
        qt_ref[0, h, :MLA_DN, :] = (qh[:MLA_DN] * scale).astype(BF16)
        qt_ref[0, h, MLA_DN:MLA_DN + MLA_DR, :] = ((tq[:MLA_DR] + tq[MLA_DR:]) * scale).astype(BF16)
        qt_ref[0, h, MLA_DN + MLA_DR:, :] = zeros
        k_ref[0, h, :, :MLA_DN] = _dot(ckv, wk_ref[h]).astype(BF16)
        k_ref[0, h, :, MLA_DN:] = k_rope
        vt_ref[0, h] = _nt_dot(wvt_ref[h], ckv).astype(BF16)


def _flash_update(s, m_sc, l_sc, acc_sc, vt):
    m_prev = m_sc[...]
    m_new = jnp.maximum(m_prev, jnp.max(s, axis=0, keepdims=True))
    p = jnp.exp2(s - m_new)
    a = jnp.exp2(m_prev - m_new)
    l_sc[...] = a * l_sc[...] + jnp.sum(p, axis=0, keepdims=True)
    acc_sc[...] = a * acc_sc[...] + _dot(vt, p.astype(BF16))
    m_sc[...] = m_new


def _flash_causal_kernel(qt_ref, k_ref, vt_ref, o_ref, *, t):
    dv, s_len = vt_ref.shape[2], vt_ref.shape[3]
    pairs = [(qi, kt) for qi in range(s_len // t) for kt in range(qi + 1)]
    kpos = lax.broadcasted_iota(jnp.int32, (t, 1), 0)
    qpos = lax.broadcasted_iota(jnp.int32, (1, t), 1)
    diag = kpos <= qpos

    def scores(j):
        qi, kt = pairs[j]
        return _dot(k_ref[0, 0, kt * t:(kt + 1) * t, :], qt_ref[0, 0, :, qi * t:(qi + 1) * t])

    def value_matmul(pending, l, acc):
        qi, kt, p, a = pending
        acc = a * acc + _dot(vt_ref[0, 0, :, kt * t:(kt + 1) * t], p)
        if kt == qi:
            o_ref[0, qi * t:(qi + 1) * t, :] = (acc / l).T.astype(o_ref.dtype)
        return acc

    s_cur, pending = scores(0), None
    m = l = acc = None
    for j, (qi, kt) in enumerate(pairs):
        s_next = scores(j + 1) if j + 1 < len(pairs) else None
        if pending is not None:
            acc = value_matmul(pending, l, acc)
        if kt == 0:
            m, l, acc = jnp.full((1, t), NEG, F32), jnp.zeros((1, t), F32), jnp.zeros((dv, t), F32)
        s = jnp.where(diag, s_cur, NEG) if kt == qi else s_cur
        m_new = jnp.maximum(m, jnp.max(s, axis=0, keepdims=True))
        p = jnp.exp2(s - m_new)
        a = jnp.exp2(m - m_new)
        l = a * l + jnp.sum(p, axis=0, keepdims=True)
        m = m_new
        pending = (qi, kt, p.astype(BF16), a)
        s_cur = s_next
    value_matmul(pending, l, acc)


def _mla_layer(x, shift, scale_v, gate, cos, sin, w_in, g_q, w_uq, g_kv, w_ukv, w_o, ln_g, ln_b):
    B, S, D = x.shape
    H, dn, dr, dv = MLA_HEADS, MLA_DN, MLA_DR, MLA_DV
    half = dr // 2
    kr = w_in[:, MLA_QR + MLA_KVR:]
    win_p = jnp.concatenate([w_in, _rot_half_cols(kr, half)], axis=1).astype(BF16)
    wq = w_uq.reshape(MLA_QR, H, dn + dr)
    wq = jnp.concatenate([wq, _rot_half_cols(wq[..., dn:], half)], axis=-1)
    wqt = jnp.transpose(wq, (1, 2, 0)).astype(BF16)
    wkv = w_ukv.reshape(MLA_KVR, H, dn + dv)
    wk = jnp.transpose(wkv[..., :dn], (1, 0, 2)).astype(BF16)
    wvt = jnp.transpose(wkv[..., dn:], (1, 2, 0)).astype(BF16)
    tab = jnp.concatenate([cos, cos, sin, sin], axis=-1)
    tab_t = jnp.swapaxes(tab, 1, 2)
    ts = min(512, S)
    nw = win_p.shape[1]
    hd = dn + 2 * dr
    qt, k, vt = pl.pallas_call(
        functools.partial(_mla_front_kernel, scale=LOG2E * (dn + dr) ** -0.5),
        grid=(B, S // ts),
        in_specs=[pl.BlockSpec((1, ts, D), lambda b, i: (b, i, 0)),
                  pl.BlockSpec((1, 1, D), lambda b, i: (b, 0, 0)),
                  pl.BlockSpec((1, 1, D), lambda b, i: (b, 0, 0)),
                  pl.BlockSpec((D, nw), lambda b, i: (0, 0)),
                  pl.BlockSpec((1, MLA_QR), lambda b, i: (0, 0)),
                  pl.BlockSpec((1, MLA_KVR), lambda b, i: (0, 0)),
                  pl.BlockSpec((H, hd, MLA_QR), lambda b, i: (0, 0, 0)),
                  pl.BlockSpec((H, MLA_KVR, dn), lambda b, i: (0, 0, 0)),
                  pl.BlockSpec((H, dv, MLA_KVR), lambda b, i: (0, 0, 0)),
                  pl.BlockSpec((1, ts, LANES), lambda b, i: (b, i, 0)),
                  pl.BlockSpec((1, LANES, ts), lambda b, i: (b, 0, i))],
        out_specs=[pl.BlockSpec((1, H, hd, ts), lambda b, i: (b, 0, 0, i)),
                   pl.BlockSpec((1, H, ts, hd), lambda b, i: (b, 0, i, 0)),
                   pl.BlockSpec((1, H, dv, ts), lambda b, i: (b, 0, 0, i))],
        out_shape=[jax.ShapeDtypeStruct((B, H, hd, S), BF16),
                   jax.ShapeDtypeStruct((B, H, S, hd), BF16),
                   jax.ShapeDtypeStruct((B, H, dv, S), BF16)],
        compiler_params=_cp(("arbitrary", "arbitrary"), 48),
        name="mla_front",
    )(x, scale_v, shift, win_p, g_q.reshape(1, -1), g_kv.reshape(1, -1), wqt, wk, wvt, tab, tab_t)

    t = min(512, S)
    o = pl.pallas_call(
        functools.partial(_flash_causal_kernel, t=t),
        grid=(B, H),
        in_specs=[pl.BlockSpec((1, 1, hd, S), lambda b, h: (b, h, 0, 0)),
                  pl.BlockSpec((1, 1, S, hd), lambda b, h: (b, h, 0, 0)),
                  pl.BlockSpec((1, 1, dv, S), lambda b, h: (b, h, 0, 0))],
        out_specs=pl.BlockSpec((1, S, dv), lambda b, h: (b, 0, h)),
        out_shape=jax.ShapeDtypeStruct((B, S, H * dv), BF16),
        compiler_params=_cp(("arbitrary", "arbitrary"), 48),
        name="mla_attn",
    )(qt, k, vt)
    return _out_ln(o, w_o, x, gate, ln_g, ln_b)


def _proj_kernel(*refs, segs, chunk, has_rope):
    x_ref, sc_ref, sh_ref, w_ref, wt_ref = refs[:5]
    n_in = 9 if has_rope else 5
    outs = refs[n_in:]
    hb = (x_ref[0] * (1.0 + sc_ref[0]) + sh_ref[0]).astype(BF16)
    if has_rope:
        cos, sin = refs[5][0], refs[6][0]
        cos_t, sin_t = refs[7][0], refs[8][0]
    half = LANES // 2
    for (c0, width, kind, scale, transposed), o_ref in zip(segs, outs):
        for j in range(0, width, chunk):
            cw = min(chunk, width - j)
            if transposed:
                r = _nt_dot(wt_ref[c0 + j:c0 + j + cw, :], hb)
            else:
                r = _dot(hb, w_ref[:, c0 + j:c0 + j + cw])
            if kind == "rope":
                parts = []
                for t in range(0, cw, LANES):
                    if transposed:
                        u = r[t:t + LANES]
                        parts.append(u * cos_t + pltpu.roll(u, half, axis=0) * sin_t)
                    else:
                        u = r[:, t:t + LANES]
                        parts.append(u * cos + pltpu.roll(u, half, axis=1) * sin)
                r = jnp.concatenate(parts, axis=0 if transposed else 1) if len(parts) > 1 else parts[0]
            elif kind == "sigmoid":
                r = jax.nn.sigmoid(r)
            if scale != 1.0:
                r = r * scale
            if transposed:
                o_ref[0, j:j + cw, :] = r.astype(o_ref.dtype)
            else:
                o_ref[0, :, j:j + cw] = r.astype(o_ref.dtype)


def _mod_proj(x, shift, scale_v, w, wt, segs, out_dtypes, rope=None, ts=512, chunk=256):
    B, S, D = x.shape
    ts = min(ts, S)
    in_specs = [pl.BlockSpec((1, ts, D), lambda b, i: (b, i, 0)),
                pl.BlockSpec((1, 1, D), lambda b, i: (b, 0, 0)),
                pl.BlockSpec((1, 1, D), lambda b, i: (b, 0, 0)),
                pl.BlockSpec(w.shape, lambda b, i: (0, 0)),
                pl.BlockSpec(wt.shape, lambda b, i: (0, 0))]
    args = [x, scale_v, shift, w, wt]
    if rope is not None:
        cos, sin = rope
        in_specs += [pl.BlockSpec((1, ts, LANES), lambda b, i: (b, i, 0))] * 2
        in_specs += [pl.BlockSpec((1, LANES, ts), lambda b, i: (b, 0, i))] * 2
        args += [cos, sin, jnp.swapaxes(cos, 1, 2), jnp.swapaxes(sin, 1, 2)]
    out_specs, out_shape = [], []
    for sg, dt in zip(segs, out_dtypes):
        if sg[4]:
            out_specs.append(pl.BlockSpec((1, sg[1], ts), lambda b, i: (b, 0, i)))
            out_shape.append(jax.ShapeDtypeStruct((B, sg[1], S), dt))
        else:
            out_specs.append(pl.BlockSpec((1, ts, sg[1]), lambda b, i: (b, i, 0)))
            out_shape.append(jax.ShapeDtypeStruct((B, S, sg[1]), dt))
    return pl.pallas_call(
        functools.partial(_proj_kernel, segs=tuple(segs), chunk=chunk, has_rope=rope is not None),
        grid=(B, S // ts),
        in_specs=in_specs,
        out_specs=out_specs,
        out_shape=out_shape,
        compiler_params=_cp(("arbitrary", "arbitrary"), 48),
        name="mod_proj",
    )(*args)


def _nsa_cmp_kernel(kc_ref, vc_ref, pek_ref, pev_ref, wk1_ref, wk2_ref, wv1_ref, wv2t_ref, ko_ref, vto_ref):
    half = NSA_CMP_BLOCK // 2
    nchunk = kc_ref.shape[1]

    def hidden(src, pe_ref, w1_ref, kh):
        a = jnp.zeros((nchunk, w1_ref.shape[2]), F32)
        bm = jnp.zeros((nchunk, w1_ref.shape[2]), F32)
        for l in range(half):
            c0 = (l * NSA_KVH + kh) * NSA_DH
            t = src[0, :, c0:c0 + NSA_DH].astype(F32)
            a = a + _dot((t + pe_ref[l:l + 1, :]).astype(BF16), w1_ref[l])
            bm = bm + _dot((t + pe_ref[half + l:half + l + 1, :]).astype(BF16), w1_ref[half + l])
        hid = a + pltpu.roll(bm, nchunk - 1, axis=0)
        return (hid * jax.nn.sigmoid(hid)).astype(BF16)

    rowi = lax.broadcasted_iota(jnp.int32, (nchunk, 1), 0)
    coli = lax.broadcasted_iota(jnp.int32, (1, nchunk), 1)
    for kh in range(NSA_KVH):
        out = _dot(hidden(kc_ref, pek_ref, wk1_ref, kh), wk2_ref[...])
        ko_ref[0, kh] = jnp.where(rowi < nchunk - 1, out, 0.0).astype(ko_ref.dtype)
        out_t = _nt_dot(wv2t_ref[...], hidden(vc_ref, pev_ref, wv1_ref, kh))
        vto_ref[0, kh] = jnp.where(coli < nchunk - 1, out_t, 0.0).astype(vto_ref.dtype)


def _nsa_attn_kernel(qt_ref, kc_ref, vct_ref, ks_ref, vst_ref, kw_ref, vwt_ref, gt_ref, ovt_ref, e_ref, o_ref,
                     q4_sc, m_sc, l_sc, acc_sc, *, tq, tk, span, n_slc):
    qi = pl.program_id(2)
    G = NSA_G
    for g in range(G):
        q4_sc[:, g * tq:(g + 1) * tq] = qt_ref[0, g * NSA_DH:(g + 1) * NSA_DH, :]
    q4 = q4_sc[...]
    qpos = qi * tq + lax.broadcasted_iota(jnp.int32, (1, tq), 1)

    def tile_g(a):
        return jnp.concatenate([a] * G, axis=1)

    qpos4 = tile_g(qpos)

    kstart = pl.multiple_of(jnp.maximum(qi * tq + tq - span, 0), tq)
    s_win = _dot(kw_ref[0, pl.ds(kstart, span), :], q4)

    n_cmp = kc_ref.shape[2]
    nid = lax.broadcasted_iota(jnp.int32, (n_cmp, 1), 0)
    cmask = (nid * NSA_CMP_STRIDE + (NSA_CMP_BLOCK - 1) <= qpos4) & (nid < n_cmp - 1)
    s = jnp.where(cmask, _dot(kc_ref[0, 0], q4), NEG)
    p = jnp.where(cmask, jnp.exp2(s - jnp.max(s, axis=0, keepdims=True)), 0.0)
    l = jnp.sum(p, axis=0, keepdims=True)
    p_cmp = p / jnp.where(l > 0.0, l, 1.0)
    o_cmp = _dot(vct_ref[0, 0], p_cmp.astype(BF16))

    diff = qpos - (kstart + lax.broadcasted_iota(jnp.int32, (span, 1), 0))
    s_win = s_win + tile_g(jnp.where((diff >= 0) & (diff < NSA_WINDOW), 0.0, NEG))
    p = jnp.exp2(s_win - jnp.max(s_win, axis=0, keepdims=True))
    o_win = _dot(vwt_ref[0, :, pl.ds(kstart, span)], p.astype(BF16)) / jnp.sum(p, axis=0, keepdims=True)

    psum = p_cmp[:, 0:tq]
    for g in range(1, G):
        psum = psum + p_cmp[:, g * tq:(g + 1) * tq]
    p_hi, p_lo = _split_bf16(psum)
    imp = _dot(ovt_ref[...], p_hi) + _dot(ovt_ref[...], p_lo)
    nb = ovt_ref.shape[0]
    blk = lax.broadcasted_iota(jnp.int32, (nb, 1), 0)
    cur = jnp.right_shift(qpos, SLC_SHIFT)
    forced = (blk == 0) | (blk == cur) | (blk == cur - 1)
    valid = (blk * NSA_SLC_BLOCK <= qpos) & (blk < n_slc)
    work = jnp.where(valid, imp + jnp.where(forced, NSA_BONUS, 0.0), -jnp.inf)
    beaten = jnp.zeros((nb, tq), F32)
    for j in range(n_slc):
        wj = work[j:j + 1, :]
        beaten = beaten + jnp.where((wj > work) | ((wj == work) & (blk > j)), 1.0, 0.0)
    sel = jnp.where(beaten < float(min(NSA_TOPK, n_slc)), 1.0, 0.0).astype(BF16)

    m_sc[...] = jnp.full(m_sc.shape, NEG, F32)
    l_sc[...] = jnp.zeros(l_sc.shape, F32)
    acc_sc[...] = jnp.zeros(acc_sc.shape, F32)
    n_kt = ((qi + 1) * tq + tk - 1) // tk

    def slc_tiles(n):
        def scores(kt):
            return _dot(ks_ref[0, kt * tk:(kt + 1) * tk, :], q4)

        def value_matmul(kt, p, a):
            acc_sc[...] = a * acc_sc[...] + _dot(vst_ref[0, :, kt * tk:(kt + 1) * tk], p)

        s, pending = scores(0), None
        chosen = _dot(e_ref[0:n * tk, :], sel)
        kpos = lax.broadcasted_iota(jnp.int32, (n * tk, 1), 0)
        bias_all = jnp.where((chosen > 0.5) & (kpos <= qpos), 0.0, NEG)
        for kt in range(n):
            s_next = scores(kt + 1) if kt + 1 < n else None
            if pending is not None:
                value_matmul(*pending)
            s = s + tile_g(bias_all[kt * tk:(kt + 1) * tk])
            m_prev = m_sc[...]
            m_new = jnp.maximum(m_prev, jnp.max(s, axis=0, keepdims=True))
            p = jnp.exp2(s - m_new)
            a = jnp.exp2(m_prev - m_new)
            l_sc[...] = a * l_sc[...] + jnp.sum(p, axis=0, keepdims=True)
            m_sc[...] = m_new
            pending = (kt, p.astype(BF16), a)
            s = s_next
        value_matmul(*pending)

    for n in range(1, (ks_ref.shape[1] + tk - 1) // tk + 1):
        pl.when(n_kt == n)(functools.partial(slc_tiles, n))
    o_slc = acc_sc[...] / l_sc[...]

    gt = gt_ref[0]
    for g in range(G):
        cols = slice(g * tq, (g + 1) * tq)
        o = (gt[3 * g:3 * g + 1] * o_cmp[:, cols] + gt[3 * g + 1:3 * g + 2] * o_slc[:, cols]
             + gt[3 * g + 2:3 * g + 3] * o_win[:, cols])
        o_ref[0, :, g * NSA_DH:(g + 1) * NSA_DH] = o.T.astype(o_ref.dtype)


def _nsa_layer(x, shift, scale_v, gate, cos, sin, w_in, pe_k, pe_v, w_ck1, w_ck2, w_cv1, w_cv2, w_o,
               ln_g, ln_b):
    B, S, D = x.shape
    H, KH, Dh, G = NSA_HEADS, NSA_KVH, NSA_DH, NSA_G
    kvw = KH * Dh
    cuts = [int(v) for v in np.cumsum([H * Dh] + [kvw] * 6)]
    wq, wkc, wvc, wks, wvs, wkw, wvw, wgl = jnp.split(w_in, cuts, axis=1)
    wgl = jnp.pad(wgl.reshape(D, KH, G * 3), ((0, 0), (0, 0), (0, GATE_ROWS - G * 3))).reshape(D, KH * GATE_ROWS)
    w = jnp.concatenate([wkc, wks, wkw, wvc], axis=1).astype(BF16)
    wt = jnp.concatenate([wq, wvs, wvw, wgl], axis=1).T.astype(BF16)
    segs = [(0, H * Dh, "rope", LOG2E * Dh ** -0.5, True),
            (0, kvw, "rope", 1.0, False), (kvw, kvw, "rope", 1.0, False), (2 * kvw, kvw, "rope", 1.0, False),
            (3 * kvw, kvw, "plain", 1.0, False),
            (H * Dh, kvw, "plain", 1.0, True), (H * Dh + kvw, kvw, "plain", 1.0, True),
            (H * Dh + 2 * kvw, KH * GATE_ROWS, "sigmoid", 1.0, True)]
    rope = (jnp.concatenate([cos, cos], axis=-1), jnp.concatenate([-sin, sin], axis=-1))
    qt, kc, ks, kw, vc, vst, vwt, gates_t = _mod_proj(
        x, shift, scale_v, w, wt, segs, [BF16] * 7 + [F32], rope=rope)

    st = NSA_CMP_STRIDE
    nchunk = S // st
    kc2 = kc.reshape(B, nchunk, st * kvw)
    vc2 = vc.reshape(B, nchunk, st * kvw)
    hid = w_ck1.shape[-1]
    full = lambda shape: pl.BlockSpec(shape, lambda b: (0,) * len(shape))
    k_cmp, v_cmp_t = pl.pallas_call(
        _nsa_cmp_kernel,
        grid=(B,),
        in_specs=[pl.BlockSpec((1, nchunk, st * kvw), lambda b: (b, 0, 0)),
                  pl.BlockSpec((1, nchunk, st * kvw), lambda b: (b, 0, 0)),
                  full((NSA_CMP_BLOCK, Dh)), full((NSA_CMP_BLOCK, Dh)),
                  full((NSA_CMP_BLOCK, Dh, hid)), full((hid, Dh)),
                  full((NSA_CMP_BLOCK, Dh, hid)), full((Dh, hid))],
        out_specs=[pl.BlockSpec((1, KH, nchunk, Dh), lambda b: (b, 0, 0, 0)),
                   pl.BlockSpec((1, KH, Dh, nchunk), lambda b: (b, 0, 0, 0))],
        out_shape=[jax.ShapeDtypeStruct((B, KH, nchunk, Dh), BF16),
                   jax.ShapeDtypeStruct((B, KH, Dh, nchunk), BF16)],
        compiler_params=_cp(("arbitrary",), 32),
        name="nsa_cmp",
    )(kc2, vc2, pe_k, pe_v, w_ck1.astype(BF16), w_ck2.astype(BF16), w_cv1.astype(BF16), w_cv2.T.astype(BF16))

    n_slc = S // NSA_SLC_BLOCK
    nb = 32
    assert n_slc <= nb and nchunk % 8 == 0
    cmp_start = np.arange(nchunk) * st
    slc_start = np.arange(n_slc) * NSA_SLC_BLOCK
    ov = np.clip(np.minimum(cmp_start[:, None] + NSA_CMP_BLOCK, slc_start[None, :] + NSA_SLC_BLOCK)
                 - np.maximum(cmp_start[:, None], slc_start[None, :]), 0, None).astype(np.float32) / NSA_CMP_BLOCK
    ov[nchunk - 1:] = 0.0
    ov_t = np.zeros((nb, nchunk), np.float32)
    ov_t[:n_slc] = ov.T
    key_blk = np.zeros((S, nb), np.float32)
    key_blk[np.arange(S), np.arange(S) // NSA_SLC_BLOCK] = 1.0

    tq = min(128, S)
    tk = min(256, S)
    span = min(NSA_WINDOW + tq, S)
    k_spec = pl.BlockSpec((1, S, Dh), lambda b, kh, i: (b, 0, kh))
    vt_spec = pl.BlockSpec((1, Dh, S), lambda b, kh, i: (b, kh, 0))
    o = pl.pallas_call(
        functools.partial(_nsa_attn_kernel, tq=tq, tk=tk, span=span, n_slc=n_slc),
        grid=(B, KH, S // tq),
        in_specs=[pl.BlockSpec((1, G * Dh, tq), lambda b, kh, i: (b, kh, i)),
                  pl.BlockSpec((1, 1, nchunk, Dh), lambda b, kh, i: (b, kh, 0, 0)),
                  pl.BlockSpec((1, 1, Dh, nchunk), lambda b, kh, i: (b, kh, 0, 0)),
                  k_spec, vt_spec, k_spec, vt_spec,
                  pl.BlockSpec((1, GATE_ROWS, tq), lambda b, kh, i: (b, kh, i)),
                  pl.BlockSpec((nb, nchunk), lambda b, kh, i: (0, 0)),
                  pl.BlockSpec((S, nb), lambda b, kh, i: (0, 0))],
        out_specs=pl.BlockSpec((1, tq, G * Dh), lambda b, kh, i: (b, i, kh)),
        out_shape=jax.ShapeDtypeStruct((B, S, H * Dh), BF16),
        scratch_shapes=[pltpu.VMEM((Dh, G * tq), BF16), pltpu.VMEM((1, G * tq), F32),
                        pltpu.VMEM((1, G * tq), F32), pltpu.VMEM((Dh, G * tq), F32)],
        compiler_params=_cp(("arbitrary", "arbitrary", "arbitrary"), 32),
        name="nsa_attn",
    )(qt, k_cmp, v_cmp_t, ks, vst, kw, vwt, gates_t, jnp.asarray(ov_t, BF16), jnp.asarray(key_blk, BF16))
    return _out_ln(o, w_o, x, gate, ln_g, ln_b)


def _sb_attn_kernel(qt_ref, k_ref, vt_ref, u_ref, o_ref, r_sc, acc_sc, *, tq, tk):
    d, s_len = vt_ref.shape[1], vt_ref.shape[2]
    u = u_ref[...]
    nd = tq // tk

    def weights(z, strict):
        lk = -(jnp.maximum(z, 0.0) + jnp.log2(1.0 + jnp.exp2(-jnp.abs(z))))
        if strict is not None:
            lk = jnp.where(strict, lk, 0.0)
        hi, lo = _split_bf16(lk)
        between = r_sc[...] + _dot(u, hi) + _dot(u, lo)
        a = jnp.exp2(lk + z + between)
        if strict is not None:
            a = jnp.where(strict, a, 0.0)
        r_sc[...] += jnp.sum(lk, axis=0, keepdims=True)
        return a.astype(BF16)

    for qi in range(s_len // tq):
        qt = qt_ref[0, :, qi * tq:(qi + 1) * tq]
        r_sc[...] = jnp.zeros(r_sc.shape, F32)
        acc_sc[...] = jnp.zeros(acc_sc.shape, F32)
        last = (qi + 1) * nd - 1
        tiles = [last - i for i in range(nd)] + ([qi * nd - 1] if qi > 0 else [])
        qpos = qi * tq + lax.broadcasted_iota(jnp.int32, (1, tq), 1)

        def scores(kt):
            return _dot(k_ref[0, kt * tk:(kt + 1) * tk, :], qt)

        z, pending = scores(tiles[0]), None
        for i, kt in enumerate(tiles):
            nxt = tiles[i + 1] if i + 1 < len(tiles) else max(qi * nd - 2, 0)
            z_next = scores(nxt)
            if pending is not None:
                acc_sc[...] += _dot(vt_ref[0, :, pending[0] * tk:(pending[0] + 1) * tk], pending[1])
            strict = (kt * tk + lax.broadcasted_iota(jnp.int32, (tk, 1), 0) < qpos) if kt >= qi * nd else None
            pending = (kt, weights(z, strict))
            z = z_next
        acc_sc[...] += _dot(vt_ref[0, :, pending[0] * tk:(pending[0] + 1) * tk], pending[1])

        if qi * nd - 2 >= 0:
            def scores_dyn(kt):
                return _dot(k_ref[0, pl.ds(pl.multiple_of(kt * tk, tk), tk), :], qt)

            def cond(carry):
                kt, _ = carry
                return (kt >= 0) & (jnp.max(r_sc[...]) > SB_EXIT_LOG2)

            def body(carry):
                kt, z = carry
                z_next = scores_dyn(jnp.maximum(kt - 1, 0))
                a = weights(z, None)
                acc_sc[...] += _dot(vt_ref[0, :, pl.ds(pl.multiple_of(kt * tk, tk), tk)], a)
                return kt - 1, z_next

            lax.while_loop(cond, body, (qi * nd - 2, z))
        o_ref[0, qi * tq:(qi + 1) * tq, :] = acc_sc[...].T.astype(o_ref.dtype)


def _sb_layer(x, shift, scale_v, gate, w_qkv, w_o, ln_g, ln_b):
    B, S, D = x.shape
    H, d = SB_HEADS, SB_DH
    hw = H * d
    w = w_qkv[:, hw:2 * hw].astype(BF16)
    wt = jnp.concatenate([w_qkv[:, :hw], w_qkv[:, 2 * hw:]], axis=1).T.astype(BF16)
    segs = [(0, hw, "plain", LOG2E * d ** -0.5, True), (0, hw, "plain", 1.0, False), (hw, hw, "plain", 1.0, True)]
    qt, k, vt = _mod_proj(x, shift, scale_v, w, wt, segs, [BF16] * 3)
    tq = min(512, S)
    tk = min(256, S)
    tri = np.triu(np.ones((tk, tk), np.float32), 1)
    o = pl.pallas_call(
        functools.partial(_sb_attn_kernel, tq=tq, tk=tk),
        grid=(B, H),
        in_specs=[pl.BlockSpec((1, d, S), lambda b, h: (b, h, 0)),
                  pl.BlockSpec((1, S, d), lambda b, h: (b, 0, h)),
                  pl.BlockSpec((1, d, S), lambda b, h: (b, h, 0)),
                  pl.BlockSpec((tk, tk), lambda b, h: (0, 0))],
        out_specs=pl.BlockSpec((1, S, d), lambda b, h: (b, 0, h)),
        out_shape=jax.ShapeDtypeStruct((B, S, hw), BF16),
        scratch_shapes=[pltpu.VMEM((1, tq), F32), pltpu.VMEM((d, tq), F32)],
        compiler_params=_cp(("arbitrary", "arbitrary"), 32),
        name="sb_attn",
    )(qt, k, vt, jnp.asarray(tri, BF16))
    return _out_ln(o, w_o, x, gate, ln_g, ln_b)


def _router_kernel(x_ref, sc_ref, sh_ref, whi_ref, wlo_ref, b_ref, tri_ref, hb_ref, info_ref, cnt_ref,
                   base_sc):
    first = (pl.program_id(0) == 0) & (pl.program_id(1) == 0)

    @pl.when(first)
    def _():
        base_sc[...] = jnp.zeros(base_sc.shape, F32)

    h = x_ref[0] * (1.0 + sc_ref[0]) + sh_ref[0]
    h_hi, h_lo = _split_bf16(h)
    hb_ref[0] = _pack_pairs(h)
    logit = (_nt_dot(whi_ref[...], h_hi) + _nt_dot(whi_ref[...], h_lo) + _nt_dot(wlo_ref[...], h_hi)
             + b_ref[...])
    tm = logit.shape[1]
    epg = MOE_EPG
    row = lax.broadcasted_iota(jnp.int32, (epg, 1), 0)
    row_f = row.astype(F32)

    def first_max(vals):
        mx = jnp.max(vals, axis=0, keepdims=True)
        idx = jnp.min(jnp.where(vals == mx, row_f, float(epg)), axis=0, keepdims=True)
        return mx, idx

    gl = jnp.where(row < MOE_GROUPS, logit[0:epg], -jnp.inf)
    gmax, g_idx = first_max(gl)
    g_w = 1.0 / jnp.sum(jnp.exp(gl - gmax), axis=0, keepdims=True)
    el = logit[epg:2 * epg]
    for g in range(1, MOE_GROUPS):
        el = jnp.where(g_idx == float(g), logit[epg * (g + 1):epg * (g + 2)], el)
    ep = jnp.exp(el - jnp.max(el, axis=0, keepdims=True))
    prob = ep / jnp.sum(ep, axis=0, keepdims=True)
    p1, i1 = first_max(prob)
    p2, i2 = first_max(jnp.where(row_f == i1, -1.0, prob))
    e1 = g_idx * float(epg) + i1
    e2 = g_idx * float(epg) + i2
    w1 = g_w * p1 / (p1 + p2)
    w2 = g_w * p2 / (p1 + p2)

    erow = lax.broadcasted_iota(jnp.int32, (MOE_E, 1), 0).astype(F32)
    hot1 = erow == e1
    hot2 = erow == e2
    cnt = jnp.where(hot1 | hot2, 1.0, 0.0)
    before = base_sc[:, 0:1] + _dot(cnt.astype(BF16), tri_ref[...])
    r1 = jnp.sum(jnp.where(hot1, before, 0.0), axis=0, keepdims=True)
    r2 = jnp.sum(jnp.where(hot2, before, 0.0), axis=0, keepdims=True)
    base_sc[...] += jnp.sum(cnt, axis=1, keepdims=True)
    cnt_ref[...] = base_sc[...]

    irow = lax.broadcasted_iota(jnp.int32, (INFO_ROWS, 1), 0)
    info = jnp.zeros((INFO_ROWS, tm), F32)
    for k, val in enumerate((e1, e2, r1, r2, w1, w2)):
        info = jnp.where(irow == k, val, info)
    info_ref[0] = info


def _ffn_kernel(be_ref, seg_ref, nu_ref, x_ref, wg_ref, wu_ref, wd_ref, y_ref, wg_sc, wu_sc, wd_sc):
    i = pl.program_id(0)
    n_blk = be_ref.shape[0]

    def stage(blk, src_ref, dst_sc):
        r = jnp.clip(blk, 0, n_blk - 1)
        first = (blk >= 0) & (blk < nu_ref[0]) & ((r == 0) | (be_ref[r] != be_ref[jnp.maximum(r - 1, 0)]))

        @pl.when(first)
        def _():
            dst_sc[seg_ref[r] % FFN_SLOTS] = src_ref[0, 0].astype(BF16)

    stage(i, wg_ref, wg_sc)
    stage(i - 1, wu_ref, wu_sc)
    stage(i - 2, wd_ref, wd_sc)
    blk = i - FFN_LEAD

    @pl.when((blk >= 0) & (blk < nu_ref[0]))
    def _():
        slot = seg_ref[jnp.maximum(blk, 0)] % FFN_SLOTS
        xb = _unpack_pairs(x_ref[...]).astype(BF16)
        g = _dot(xb, wg_sc[slot])
        u = _dot(xb, wu_sc[slot])
        a = (g * jax.nn.sigmoid(g) * u).astype(BF16)
        y_ref[...] = _pack_pairs(_dot(a, wd_sc[slot]))

    @pl.when(blk >= nu_ref[0])
    def _():
        y_ref[...] = jnp.zeros(y_ref.shape, y_ref.dtype)


def _sc_gather(table, idx):
    n, d = idx.shape[0], table.shape[1]
    workers = SC_CORES * SC_SUBCORES
    per_w = n // workers
    assert per_w * workers == n and per_w % (2 * SC_CHUNK) == 0
    mesh = plsc.VectorSubcoreMesh(core_axis_name="c", subcore_axis_name="s")

    @functools.partial(
        pl.kernel, mesh=mesh,
        out_type=jax.ShapeDtypeStruct((n, d), table.dtype),
        scratch_types=[pltpu.VMEM((SC_CHUNK,), jnp.int32), pltpu.VMEM((SC_CHUNK,), jnp.int32),
                       pltpu.VMEM((SC_CHUNK, d), table.dtype), pltpu.VMEM((SC_CHUNK, d), table.dtype)]
                      + [pltpu.SemaphoreType.DMA] * 4,
        name="sc_gather")
    def gather(table_hbm, idx_hbm, out_hbm, idx_a, idx_b, rows_a, rows_b, sem_a, sem_b, sem_c, sem_d):
        wid = lax.axis_index("s") * SC_CORES + lax.axis_index("c")

        @pl.loop(0, per_w // (2 * SC_CHUNK))
        def _(j):
            base_a = pl.multiple_of(wid * per_w + j * (2 * SC_CHUNK), SC_CHUNK)
            base_b = pl.multiple_of(base_a + SC_CHUNK, SC_CHUNK)
            pltpu.sync_copy(idx_hbm.at[pl.ds(base_a, SC_CHUNK)], idx_a)
            pltpu.sync_copy(idx_hbm.at[pl.ds(base_b, SC_CHUNK)], idx_b)
            get_a = pltpu.async_copy(table_hbm.at[idx_a], rows_a, sem_a)
            get_b = pltpu.async_copy(table_hbm.at[idx_b], rows_b, sem_b)
            get_a.wait()
            put_a = pltpu.async_copy(rows_a, out_hbm.at[pl.ds(base_a, SC_CHUNK)], sem_c)
            get_b.wait()
            put_b = pltpu.async_copy(rows_b, out_hbm.at[pl.ds(base_b, SC_CHUNK)], sem_d)
            put_a.wait()
            put_b.wait()

    return gather(table, idx)


def _sc_dispatch(src, dest0, dest1, n_out):
    n, d = src.shape
    workers = SC_CORES * SC_SUBCORES
    per_w = n // workers
    assert per_w * workers == n and per_w % SC_CHUNK == 0
    mesh = plsc.VectorSubcoreMesh(core_axis_name="c", subcore_axis_name="s")

    @functools.partial(
        pl.kernel, mesh=mesh,
        out_type=jax.ShapeDtypeStruct((n_out, d), src.dtype),
        scratch_types=[pltpu.VMEM((SC_CHUNK,), jnp.int32),
                       pltpu.VMEM((SC_CHUNK,), jnp.int32),
                       pltpu.VMEM((SC_CHUNK, d), src.dtype)]
                      + [pltpu.SemaphoreType.DMA] * 3,
        name="sc_dispatch")
    def dispatch(src_hbm, d0_hbm, d1_hbm, out_hbm, i0_v, i1_v, rows_v, sem_a, sem_b, sem_c):
        wid = lax.axis_index("s") * SC_CORES + lax.axis_index("c")

        @pl.loop(0, per_w // SC_CHUNK)
        def _(j):
            base = pl.multiple_of(wid * per_w + j * SC_CHUNK, SC_CHUNK)
            get_0 = pltpu.async_copy(d0_hbm.at[pl.ds(base, SC_CHUNK)], i0_v, sem_a)
            get_1 = pltpu.async_copy(d1_hbm.at[pl.ds(base, SC_CHUNK)], i1_v, sem_b)
            get_r = pltpu.async_copy(src_hbm.at[pl.ds(base, SC_CHUNK)], rows_v, sem_c)
            get_0.wait()
            get_1.wait()
            get_r.wait()
            put_0 = pltpu.async_copy(rows_v, out_hbm.at[i0_v], sem_a)
            put_1 = pltpu.async_copy(rows_v, out_hbm.at[i1_v], sem_b)
            put_0.wait()
            put_1.wait()

    return dispatch(src, dest0, dest1)


def _moe_out_kernel(x_ref, y1_ref, y2_ref, info_ref, gate_ref, g_ref, b_ref, o_ref):
    info = info_ref[0]
    ts = info.shape[1]
    info_tok = jnp.concatenate([info, jnp.zeros((LANES - INFO_ROWS, ts), F32)], axis=0).T
    y = info_tok[:, 4:5] * _unpack_pairs(y1_ref[0, 0]) + info_tok[:, 5:6] * _unpack_pairs(y2_ref[0, 0])
    z = ALPHA * x_ref[0] + gate_ref[0] * y
    o_ref[0] = _layer_norm(z, g_ref[...], b_ref[...])


def _moe_out(x, y12, info, gate, g, b, ts=512):
    B, S, D = x.shape
    ts = min(ts, S)
    tok = pl.BlockSpec((1, ts, D), lambda bi, i: (bi, i, 0))
    return pl.pallas_call(
        _moe_out_kernel,
        grid=(B, S // ts),
        in_specs=[tok,
                  pl.BlockSpec((1, 1, ts, D // 2), lambda bi, i: (0, bi, i, 0)),
                  pl.BlockSpec((1, 1, ts, D // 2), lambda bi, i: (1, bi, i, 0)),
                  pl.BlockSpec((1, INFO_ROWS, ts), lambda bi, i: (bi, 0, i)),
                  pl.BlockSpec((1, 1, D), lambda bi, i: (bi, 0, 0)),
                  pl.BlockSpec((1, D), lambda bi, i: (0, 0)),
                  pl.BlockSpec((1, D), lambda bi, i: (0, 0))],
        out_specs=tok,
        out_shape=jax.ShapeDtypeStruct((B, S, D), F32),
        compiler_params=_cp(("arbitrary", "arbitrary"), 40),
        name="moe_out",
    )(x, y12, y12, info, gate, g.reshape(1, D), b.reshape(1, D))


def _moe_layer(x, shift, scale_v, gate, w_grp, b_grp, w_rt, b_rt, layer, w_gate, w_up, w_down, ln_g, ln_b):
    B, S, D = x.shape
    T = B * S
    E, R, F = MOE_E, MOE_ROWS, w_gate.shape[-1]
    g0 = MOE_EPG
    w_r = jnp.zeros((ROUTER_ROWS, D), F32).at[:MOE_GROUPS].set(w_grp.T).at[g0:g0 + E].set(w_rt.T)
    w_hi = w_r.astype(BF16)
    w_lo = (w_r - w_hi.astype(F32)).astype(BF16)
    b_r = jnp.zeros((ROUTER_ROWS, 1), F32).at[:MOE_GROUPS, 0].set(b_grp).at[g0:g0 + E, 0].set(b_rt)
    tm = min(256, S)
    tri = jnp.asarray(np.triu(np.ones((tm, tm), np.float32), 1), BF16)
    hb, info, cnt = pl.pallas_call(
        _router_kernel,
        grid=(B, S // tm),
        in_specs=[pl.BlockSpec((1, tm, D), lambda b, i: (b, i, 0)),
                  pl.BlockSpec((1, 1, D), lambda b, i: (b, 0, 0)),
                  pl.BlockSpec((1, 1, D), lambda b, i: (b, 0, 0)),
                  pl.BlockSpec((ROUTER_ROWS, D), lambda b, i: (0, 0)),
                  pl.BlockSpec((ROUTER_ROWS, D), lambda b, i: (0, 0)),
                  pl.BlockSpec((ROUTER_ROWS, 1), lambda b, i: (0, 0)),
                  pl.BlockSpec((tm, tm), lambda b, i: (0, 0))],
        out_specs=[pl.BlockSpec((1, tm, D // 2), lambda b, i: (b, i, 0)),
                   pl.BlockSpec((1, INFO_ROWS, tm), lambda b, i: (b, 0, i)),
                   pl.BlockSpec((E, LANES), lambda b, i: (0, 0))],
        out_shape=[jax.ShapeDtypeStruct((B, S, D // 2), jnp.uint32),
                   jax.ShapeDtypeStruct((B, INFO_ROWS, S), F32),
                   jax.ShapeDtypeStruct((E, LANES), F32)],
        scratch_shapes=[pltpu.VMEM((E, LANES), F32)],
        compiler_params=_cp(("arbitrary", "arbitrary"), 32),
        name="moe_router",
    )(x, scale_v, shift, w_hi, w_lo, b_r, tri)

    counts = cnt[:, 0].astype(jnp.int32)
    padded = (counts + R - 1) // R * R
    ends = jnp.cumsum(padded)
    starts = ends - padded
    expert_ids = jnp.arange(E, dtype=jnp.int32)

    def row_of(expert, rank):
        start = jnp.sum(jnp.where(expert.astype(jnp.int32)[..., None] == expert_ids, starts, 0), axis=-1)
        return (start + rank.astype(jnp.int32)).reshape(T)

    dest0 = row_of(info[:, 0, :], info[:, 2, :])
    dest1 = row_of(info[:, 1, :], info[:, 3, :])
    n_blk = (T * 2) // R + E
    P = n_blk * R
    blk_exp = jnp.minimum(jnp.sum(jnp.arange(n_blk)[:, None] * R >= ends[None, :], axis=1), E - 1).astype(jnp.int32)
    n_used = (ends[-1] // R).astype(jnp.int32).reshape(1)
    x_rows = _sc_dispatch(hb.reshape(T, D // 2), dest0, dest1, P)

    seg = jnp.concatenate([jnp.zeros((1,), jnp.int32),
                           jnp.cumsum((blk_exp[1:] != blk_exp[:-1]).astype(jnp.int32))])

    def blk_at(offset):
        return lambda i: jnp.clip(i - offset, 0, n_blk - 1)

    rows_spec = pl.BlockSpec((R, D // 2), lambda i, be, sg, nu: (blk_at(FFN_LEAD)(i), 0))
    y_rows = pl.pallas_call(
        _ffn_kernel,
        grid_spec=pltpu.PrefetchScalarGridSpec(
            num_scalar_prefetch=3,
            grid=(n_blk + FFN_LEAD,),
            in_specs=[rows_spec,
                      pl.BlockSpec((1, 1, D, F), lambda i, be, sg, nu: (layer, be[blk_at(0)(i)], 0, 0)),
                      pl.BlockSpec((1, 1, D, F), lambda i, be, sg, nu: (layer, be[blk_at(1)(i)], 0, 0)),
                      pl.BlockSpec((1, 1, F, D), lambda i, be, sg, nu: (layer, be[blk_at(2)(i)], 0, 0))],
            out_specs=rows_spec,
            scratch_shapes=[pltpu.VMEM((FFN_SLOTS, D, F), BF16), pltpu.VMEM((FFN_SLOTS, D, F), BF16),
                            pltpu.VMEM((FFN_SLOTS, F, D), BF16)]),
        out_shape=jax.ShapeDtypeStruct((P, D // 2), jnp.uint32),
        compiler_params=_cp(("arbitrary",), 48),
        name="moe_ffn",
    )(blk_exp, seg, n_used, x_rows, w_gate, w_up, w_down)

    y12 = _sc_gather(y_rows, jnp.concatenate([dest0, dest1])).reshape(2, B, S, D // 2)
    return _moe_out(x, y12, info, gate, ln_g, ln_b)


def kernel(x, c, positions, ada_w, ada_b, ln_g, ln_b, mla_w_in, mla_g_q, mla_w_uq, mla_g_kv, mla_w_ukv, mla_w_o, nsa_w_in, nsa_pe_k, nsa_pe_v, nsa_w_ck1, nsa_w_ck2, nsa_w_cv1, nsa_w_cv2, nsa_w_o, sb_w_qkv, sb_w_o, moe_w_grp, moe_b_grp, moe_w_rt, moe_b_rt, moe_w_gate, moe_w_up, moe_w_down):
    D = x.shape[-1]
    depth = ada_w.shape[0]
    cos_mla, sin_mla = _rope_tables(positions, MLA_DR)
    cos_nsa, sin_nsa = _rope_tables(positions, NSA_DH)
    mods = _ada_all(c, ada_w, ada_b)

    def mod(i, sub):
        m = mods[2 * i + sub][:, None, :]
        return m[..., :D], m[..., D:2 * D], m[..., 2 * D:]

    for i in range(depth):
        kind, j = i % N_MIXERS, i // N_MIXERS
        shift, scale_v, gate = mod(i, 0)
        if kind == 0:
            x = _mla_layer(x, shift, scale_v, gate, cos_mla, sin_mla, mla_w_in[j], mla_g_q[j], mla_w_uq[j],
                           mla_g_kv[j], mla_w_ukv[j], mla_w_o[j], ln_g[i, 0], ln_b[i, 0])
        elif kind == 1:
            x = _nsa_layer(x, shift, scale_v, gate, cos_nsa, sin_nsa, nsa_w_in[j], nsa_pe_k[j], nsa_pe_v[j],
                           nsa_w_ck1[j], nsa_w_ck2[j], nsa_w_cv1[j], nsa_w_cv2[j], nsa_w_o[j],
                           ln_g[i, 0], ln_b[i, 0])
        else:
            x = _sb_layer(x, shift, scale_v, gate, sb_w_qkv[j], sb_w_o[j], ln_g[i, 0], ln_b[i, 0])
        shift, scale_v, gate = mod(i, 1)
        x = _moe_layer(x, shift, scale_v, gate, moe_w_grp[i], moe_b_grp[i], moe_w_rt[i], moe_b_rt[i],
                       i, moe_w_gate, moe_w_up, moe_w_down, ln_g[i, 1], ln_b[i, 1])
    return x
```

```python
import functools

import numpy as np
import jax
import jax.numpy as jnp
from jax import lax
from jax.experimental import pallas as pl
from jax.experimental.pallas import tpu as pltpu
from jax.experimental.pallas import tpu_sc as plsc

BF16 = jnp.bfloat16
F32 = jnp.float32

DEPTH = 4
N_MIXERS = 3
ROPE_THETA = 10000.0
LN_EPS = 1e-5
RMS_EPS = 1e-6
NEG = -1e30
LOG2E = 1.4426950408889634
ALPHA = (2 * DEPTH) ** 0.25

MLA_HEADS, MLA_DN, MLA_DR, MLA_DV = 8, 128, 64, 128
MLA_QR, MLA_KVR = 256, 256

NSA_HEADS, NSA_KVH, NSA_DH = 8, 2, 128
NSA_G = NSA_HEADS // NSA_KVH
NSA_CMP_BLOCK, NSA_CMP_STRIDE = 32, 16
NSA_SLC_BLOCK, NSA_TOPK = 64, 8
SLC_SHIFT = 6
NSA_WINDOW = 512
NSA_BONUS = 1e3
GATE_ROWS = 16

SB_HEADS, SB_DH = 8, 128
SB_EXIT_LOG2 = -160.0

MOE_GROUPS, MOE_EPG = 4, 8
EPG_SHIFT = 3
MOE_E = MOE_GROUPS * MOE_EPG
MOE_ROWS = 256
FFN_LEAD = 3
FFN_SLOTS = FFN_LEAD + 2
ROUTER_ROWS = 64
INFO_ROWS = 8

SC_CORES, SC_SUBCORES = 2, 16
SC_CHUNK = 64

LANES = 128
VMEM_BYTES_V7X = 64 * 1024 * 1024


def _cp(sem, vmem_mb):
    assert vmem_mb * 2**20 < VMEM_BYTES_V7X
    return pltpu.CompilerParams(dimension_semantics=sem, vmem_limit_bytes=vmem_mb * 2**20)


def _nt_dot(a, b):
    return lax.dot_general(a, b, (((1,), (1,)), ((), ())), preferred_element_type=F32)


def _dot(a, b):
    return jnp.dot(a, b, preferred_element_type=F32)


def _split_bf16(a):
    hi = a.astype(BF16)
    lo = (a - hi.astype(F32)).astype(BF16)
    return hi, lo


def _pack_pairs(a):
    n = a.shape[1] // 2
    hi = pltpu.bitcast(a[:, :n].astype(BF16).astype(F32), jnp.uint32)
    lo = pltpu.bitcast(a[:, n:].astype(BF16).astype(F32), jnp.uint32)
    return hi | (lo >> 16)


def _unpack_pairs(u):
    hi = pltpu.bitcast(u & jnp.uint32(0xFFFF0000), F32)
    lo = pltpu.bitcast(u << 16, F32)
    return jnp.concatenate([hi, lo], axis=1)


def _layer_norm(z, g, b):
    mu = jnp.mean(z, axis=-1, keepdims=True)
    d = z - mu
    var = jnp.mean(d * d, axis=-1, keepdims=True)
    return d * lax.rsqrt(var + LN_EPS) * g + b


def _ada_kernel(c_ref, w_ref, b_ref, o_ref):
    c = c_ref[...]
    s_hi, s_lo = _split_bf16(c * jax.nn.sigmoid(c))
    w_hi, w_lo = _split_bf16(w_ref[0])
    o_ref[0] = _dot(s_hi, w_hi) + _dot(s_lo, w_hi) + _dot(s_hi, w_lo) + b_ref[0]


def _ada_all(c, ada_w, ada_b):
    L, two, D, D3 = ada_w.shape
    B = c.shape[0]
    n = L * two
    tn = 1024
    return pl.pallas_call(
        _ada_kernel,
        grid=(n, D3 // tn),
        in_specs=[pl.BlockSpec((B, D), lambda i, j: (0, 0)),
                  pl.BlockSpec((1, D, tn), lambda i, j: (i, 0, j)),
                  pl.BlockSpec((1, 1, tn), lambda i, j: (i, 0, j))],
        out_specs=pl.BlockSpec((1, B, tn), lambda i, j: (i, 0, j)),
        out_shape=jax.ShapeDtypeStruct((n, B, D3), F32),
        compiler_params=_cp(("arbitrary", "arbitrary"), 32),
        name="ada",
    )(c, ada_w.reshape(n, D, D3), ada_b.reshape(n, 1, D3))


def _res_ln_kernel(x_ref, y_ref, gate_ref, g_ref, b_ref, o_ref):
    z = ALPHA * x_ref[0] + gate_ref[0] * y_ref[0]
    o_ref[0] = _layer_norm(z, g_ref[...], b_ref[...])


def _res_ln(x, y, gate, g, b, ts=512):
    B, S, D = x.shape
    ts = min(ts, S)
    tok = pl.BlockSpec((1, ts, D), lambda bi, i: (bi, i, 0))
    return pl.pallas_call(
        _res_ln_kernel,
        grid=(B, S // ts),
        in_specs=[tok, tok,
                  pl.BlockSpec((1, 1, D), lambda bi, i: (bi, 0, 0)),
                  pl.BlockSpec((1, D), lambda bi, i: (0, 0)),
                  pl.BlockSpec((1, D), lambda bi, i: (0, 0))],
        out_specs=tok,
        out_shape=jax.ShapeDtypeStruct((B, S, D), F32),
        compiler_params=_cp(("arbitrary", "arbitrary"), 32),
        name="res_ln",
    )(x, y, gate, g.reshape(1, D), b.reshape(1, D))


def _out_ln_kernel(o_ref, w_ref, x_ref, gate_ref, g_ref, b_ref, out_ref):
    y = _dot(o_ref[0], w_ref[...])
    z = ALPHA * x_ref[0] + gate_ref[0] * y
    out_ref[0] = _layer_norm(z, g_ref[...], b_ref[...])


def _out_ln(o, w_o, x, gate, g, b, ts=512):
    B, S, D = x.shape
    K = o.shape[-1]
    ts = min(ts, S)
    return pl.pallas_call(
        _out_ln_kernel,
        grid=(B, S // ts),
        in_specs=[pl.BlockSpec((1, ts, K), lambda bi, i: (bi, i, 0)),
                  pl.BlockSpec((K, D), lambda bi, i: (0, 0)),
                  pl.BlockSpec((1, ts, D), lambda bi, i: (bi, i, 0)),
                  pl.BlockSpec((1, 1, D), lambda bi, i: (bi, 0, 0)),
                  pl.BlockSpec((1, D), lambda bi, i: (0, 0)),
                  pl.BlockSpec((1, D), lambda bi, i: (0, 0))],
        out_specs=pl.BlockSpec((1, ts, D), lambda bi, i: (bi, i, 0)),
        out_shape=jax.ShapeDtypeStruct((B, S, D), F32),
        compiler_params=_cp(("arbitrary", "arbitrary"), 40),
        name="out_ln",
    )(o, w_o.astype(BF16), x, gate, g.reshape(1, D), b.reshape(1, D))


def _rope_tables(positions, dim):
    inv_freq = 1.0 / (ROPE_THETA ** (jnp.arange(0, dim, 2, dtype=F32) / dim))
    ang = positions.astype(F32)[..., None] * inv_freq
    return jnp.cos(ang), jnp.sin(ang)


def _rot_half_cols(w, half):
    return jnp.concatenate([-w[..., half:], w[..., :half]], axis=-1)


def _mla_front_kernel(x_ref, sc_ref, sh_ref, win_ref, gq_ref, gkv_ref, wqt_ref, wk_ref, wvt_ref, t_ref, tt_ref,
                      qt_ref, k_ref, vt_ref, *, scale):
    hb = (x_ref[0] * (1.0 + sc_ref[0]) + sh_ref[0]).astype(BF16)
    r = _dot(hb, win_ref[...])
    tab = t_ref[0]
    tab_t = tt_ref[0]
    low = lax.broadcasted_iota(jnp.int32, tab.shape, 1) < MLA_DR

    def rms(cx, g):
        return (cx * lax.rsqrt(jnp.mean(cx * cx, axis=-1, keepdims=True) + RMS_EPS) * g).astype(BF16)

    cq = rms(r[:, :MLA_QR], gq_ref[...])
    ckv = rms(r[:, MLA_QR:MLA_QR + MLA_KVR], gkv_ref[...])
    t = r[:, MLA_QR + MLA_KVR:] * tab
    k_rope = jnp.where(low, t + pltpu.roll(t, MLA_DR, axis=1), 0.0).astype(BF16)
    zeros = jnp.zeros((MLA_DR, hb.shape[0]), BF16)
    for h in range(MLA_HEADS):
        qh = _nt_dot(wqt_ref[h], cq)
        tq = qh[MLA_DN:] * tab_t
        qt_ref[0, h, :MLA_DN, :] = (qh[:MLA_DN] * scale).astype(BF16)
        qt_ref[0, h, MLA_DN:MLA_DN + MLA_DR, :] = ((tq[:MLA_DR] + tq[MLA_DR:]) * scale).astype(BF16)
        qt_ref[0, h, MLA_DN + MLA_DR:, :] = zeros
        k_ref[0, h, :, :MLA_DN] = _dot(ckv, wk_ref[h]).astype(BF16)
        k_ref[0, h, :, MLA_DN:] = k_rope
        vt_ref[0, h] = _nt_dot(wvt_ref[h], ckv).astype(BF16)


def _flash_update(s, m_sc, l_sc, acc_sc, vt):
    m_prev = m_sc[...]
    m_new = jnp.maximum(m_prev, jnp.max(s, axis=0, keepdims=True))
    p = jnp.exp2(s - m_new)
    a = jnp.exp2(m_prev - m_new)
    l_sc[...] = a * l_sc[...] + jnp.sum(p, axis=0, keepdims=True)
    acc_sc[...] = a * acc_sc[...] + _dot(vt, p.astype(BF16))
    m_sc[...] = m_new


def _flash_causal_kernel(qt_ref, k_ref, vt_ref, o_ref, *, t):
    dv, s_len = vt_ref.shape[2], vt_ref.shape[3]
    pairs = [(qi, kt) for qi in range(s_len // t) for kt in range(qi + 1)]
    kpos = lax.broadcasted_iota(jnp.int32, (t, 1), 0)
    qpos = lax.broadcasted_iota(jnp.int32, (1, t), 1)
    diag = kpos <= qpos

    def scores(j):
        qi, kt = pairs[j]
        return _dot(k_ref[0, 0, kt * t:(kt + 1) * t, :], qt_ref[0, 0, :, qi * t:(qi + 1) * t])

    def value_matmul(pending, l, acc):
        qi, kt, p, a = pending
        acc = a * acc + _dot(vt_ref[0, 0, :, kt * t:(kt + 1) * t], p)
        if kt == qi:
            o_ref[0, qi * t:(qi + 1) * t, :] = (acc / l).T.astype(o_ref.dtype)
        return acc

    s_cur, pending = scores(0), None
    m = l = acc = None
    for j, (qi, kt) in enumerate(pairs):
        s_next = scores(j + 1) if j + 1 < len(pairs) else None
        if pending is not None:
            acc = value_matmul(pending, l, acc)
        if kt == 0:
            m, l, acc = jnp.full((1, t), NEG, F32), jnp.zeros((1, t), F32), jnp.zeros((dv, t), F32)
        s = jnp.where(diag, s_cur, NEG) if kt == qi else s_cur
        m_new = jnp.maximum(m, jnp.max(s, axis=0, keepdims=True))
        p = jnp.exp2(s - m_new)
        a = jnp.exp2(m - m_new)
        l = a * l + jnp.sum(p, axis=0, keepdims=True)
        m = m_new
        pending = (qi, kt, p.astype(BF16), a)
        s_cur = s_next
    value_matmul(pending, l, acc)


def _mla_layer(x, shift, scale_v, gate, cos, sin, w_in, g_q, w_uq, g_kv, w_ukv, w_o, ln_g, ln_b):
    B, S, D = x.shape
    H, dn, dr, dv = MLA_HEADS, MLA_DN, MLA_DR, MLA_DV
    half = dr // 2
    kr = w_in[:, MLA_QR + MLA_KVR:]
    win_p = jnp.concatenate([w_in, _rot_half_cols(kr, half)], axis=1).astype(BF16)
    wq = w_uq.reshape(MLA_QR, H, dn + dr)
    wq = jnp.concatenate([wq, _rot_half_cols(wq[..., dn:], half)], axis=-1)
    wqt = jnp.transpose(wq, (1, 2, 0)).astype(BF16)
    wkv = w_ukv.reshape(MLA_KVR, H, dn + dv)
    wk = jnp.transpose(wkv[..., :dn], (1, 0, 2)).astype(BF16)
    wvt = jnp.transpose(wkv[..., dn:], (1, 2, 0)).astype(BF16)
    tab = jnp.concatenate([cos, cos, sin, sin], axis=-1)
    tab_t = jnp.swapaxes(tab, 1, 2)
    ts = min(512, S)
    nw = win_p.shape[1]
    hd = dn + 2 * dr
    qt, k, vt = pl.pallas_call(
        functools.partial(_mla_front_kernel, scale=LOG2E * (dn + dr) ** -0.5),
        grid=(B, S // ts),
        in_specs=[pl.BlockSpec((1, ts, D), lambda b, i: (b, i, 0)),
                  pl.BlockSpec((1, 1, D), lambda b, i: (b, 0, 0)),
                  pl.BlockSpec((1, 1, D), lambda b, i: (b, 0, 0)),
                  pl.BlockSpec((D, nw), lambda b, i: (0, 0)),
                  pl.BlockSpec((1, MLA_QR), lambda b, i: (0, 0)),
                  pl.BlockSpec((1, MLA_KVR), lambda b, i: (0, 0)),
                  pl.BlockSpec((H, hd, MLA_QR), lambda b, i: (0, 0, 0)),
                  pl.BlockSpec((H, MLA_KVR, dn), lambda b, i: (0, 0, 0)),
                  pl.BlockSpec((H, dv, MLA_KVR), lambda b, i: (0, 0, 0)),
                  pl.BlockSpec((1, ts, LANES), lambda b, i: (b, i, 0)),
                  pl.BlockSpec((1, LANES, ts), lambda b, i: (b, 0, i))],
        out_specs=[pl.BlockSpec((1, H, hd, ts), lambda b, i: (b, 0, 0, i)),
                   pl.BlockSpec((1, H, ts, hd), lambda b, i: (b, 0, i, 0)),
                   pl.BlockSpec((1, H, dv, ts), lambda b, i: (b, 0, 0, i))],
        out_shape=[jax.ShapeDtypeStruct((B, H, hd, S), BF16),
                   jax.ShapeDtypeStruct((B, H, S, hd), BF16),
                   jax.ShapeDtypeStruct((B, H, dv, S), BF16)],
        compiler_params=_cp(("arbitrary", "arbitrary"), 48),
        name="mla_front",
    )(x, scale_v, shift, win_p, g_q.reshape(1, -1), g_kv.reshape(1, -1), wqt, wk, wvt, tab, tab_t)

    t = min(512, S)
    o = pl.pallas_call(
        functools.partial(_flash_causal_kernel, t=t),
        grid=(B, H),
        in_specs=[pl.BlockSpec((1, 1, hd, S), lambda b, h: (b, h, 0, 0)),
                  pl.BlockSpec((1, 1, S, hd), lambda b, h: (b, h, 0, 0)),
                  pl.BlockSpec((1, 1, dv, S), lambda b, h: (b, h, 0, 0))],
        out_specs=pl.BlockSpec((1, S, dv), lambda b, h: (b, 0, h)),
        out_shape=jax.ShapeDtypeStruct((B, S, H * dv), BF16),
        compiler_params=_cp(("arbitrary", "arbitrary"), 48),
        name="mla_attn",
    )(qt, k, vt)
    return _out_ln(o, w_o, x, gate, ln_g, ln_b)


def _proj_kernel(*refs, segs, chunk, has_rope):
    x_ref, sc_ref, sh_ref, w_ref, wt_ref = refs[:5]
    n_in = 9 if has_rope else 5
    outs = refs[n_in:]
    hb = (x_ref[0] * (1.0 + sc_ref[0]) + sh_ref[0]).astype(BF16)
    if has_rope:
        cos, sin = refs[5][0], refs[6][0]
        cos_t, sin_t = refs[7][0], refs[8][0]
    half = LANES // 2
    for (c0, width, kind, scale, transposed), o_ref in zip(segs, outs):
        for j in range(0, width, chunk):
            cw = min(chunk, width - j)
            if transposed:
                r = _nt_dot(wt_ref[c0 + j:c0 + j + cw, :], hb)
            else:
                r = _dot(hb, w_ref[:, c0 + j:c0 + j + cw])
            if kind == "rope":
                parts = []
                for t in range(0, cw, LANES):
                    if transposed:
                        u = r[t:t + LANES]
                        parts.append(u * cos_t + pltpu.roll(u, half, axis=0) * sin_t)
                    else:
                        u = r[:, t:t + LANES]
                        parts.append(u * cos + pltpu.roll(u, half, axis=1) * sin)
                r = jnp.concatenate(parts, axis=0 if transposed else 1) if len(parts) > 1 else parts[0]
            elif kind == "sigmoid":
                r = jax.nn.sigmoid(r)
            if scale != 1.0:
                r = r * scale
            if transposed:
                o_ref[0, j:j + cw, :] = r.astype(o_ref.dtype)
            else:
                o_ref[0, :, j:j + cw] = r.astype(o_ref.dtype)


def _mod_proj(x, shift, scale_v, w, wt, segs, out_dtypes, rope=None, ts=512, chunk=256):
    B, S, D = x.shape
    ts = min(ts, S)
    in_specs = [pl.BlockSpec((1, ts, D), lambda b, i: (b, i, 0)),
                pl.BlockSpec((1, 1, D), lambda b, i: (b, 0, 0)),
                pl.BlockSpec((1, 1, D), lambda b, i: (b, 0, 0)),
                pl.BlockSpec(w.shape, lambda b, i: (0, 0)),
                pl.BlockSpec(wt.shape, lambda b, i: (0, 0))]
    args = [x, scale_v, shift, w, wt]
    if rope is not None:
        cos, sin = rope
        in_specs += [pl.BlockSpec((1, ts, LANES), lambda b, i: (b, i, 0))] * 2
        in_specs += [pl.BlockSpec((1, LANES, ts), lambda b, i: (b, 0, i))] * 2
        args += [cos, sin, jnp.swapaxes(cos, 1, 2), jnp.swapaxes(sin, 1, 2)]
    out_specs, out_shape = [], []
    for sg, dt in zip(segs, out_dtypes):
        if sg[4]:
            out_specs.append(pl.BlockSpec((1, sg[1], ts), lambda b, i: (b, 0, i)))
            out_shape.append(jax.ShapeDtypeStruct((B, sg[1], S), dt))
        else:
            out_specs.append(pl.BlockSpec((1, ts, sg[1]), lambda b, i: (b, i, 0)))
            out_shape.append(jax.ShapeDtypeStruct((B, S, sg[1]), dt))
    return pl.pallas_call(
        functools.partial(_proj_kernel, segs=tuple(segs), chunk=chunk, has_rope=rope is not None),
        grid=(B, S // ts),
        in_specs=in_specs,
        out_specs=out_specs,
        out_shape=out_shape,
        compiler_params=_cp(("arbitrary", "arbitrary"), 48),
        name="mod_proj",
    )(*args)


def _nsa_cmp_kernel(kc_ref, vc_ref, pek_ref, pev_ref, wk1_ref, wk2_ref, wv1_ref, wv2t_ref, ko_ref, vto_ref):
    half = NSA_CMP_BLOCK // 2
    nchunk = kc_ref.shape[1]

    def hidden(src, pe_ref, w1_ref, kh):
        a = jnp.zeros((nchunk, w1_ref.shape[2]), F32)
        bm = jnp.zeros((nchunk, w1_ref.shape[2]), F32)
        for l in range(half):
            c0 = (l * NSA_KVH + kh) * NSA_DH
            t = src[0, :, c0:c0 + NSA_DH].astype(F32)
            a = a + _dot((t + pe_ref[l:l + 1, :]).astype(BF16), w1_ref[l])
            bm = bm + _dot((t + pe_ref[half + l:half + l + 1, :]).astype(BF16), w1_ref[half + l])
        hid = a + pltpu.roll(bm, nchunk - 1, axis=0)
        return (hid * jax.nn.sigmoid(hid)).astype(BF16)

    rowi = lax.broadcasted_iota(jnp.int32, (nchunk, 1), 0)
    coli = lax.broadcasted_iota(jnp.int32, (1, nchunk), 1)
    for kh in range(NSA_KVH):
        out = _dot(hidden(kc_ref, pek_ref, wk1_ref, kh), wk2_ref[...])
        ko_ref[0, kh] = jnp.where(rowi < nchunk - 1, out, 0.0).astype(ko_ref.dtype)
        out_t = _nt_dot(wv2t_ref[...], hidden(vc_ref, pev_ref, wv1_ref, kh))
        vto_ref[0, kh] = jnp.where(coli < nchunk - 1, out_t, 0.0).astype(vto_ref.dtype)


def _nsa_attn_kernel(qt_ref, kc_ref, vct_ref, ks_ref, vst_ref, kw_ref, vwt_ref, gt_ref, ovt_ref, e_ref, o_ref,
                     q4_sc, m_sc, l_sc, acc_sc, *, tq, tk, span, n_slc):
    qi = pl.program_id(2)
    G = NSA_G
    for g in range(G):
        q4_sc[:, g * tq:(g + 1) * tq] = qt_ref[0, g * NSA_DH:(g + 1) * NSA_DH, :]
    q4 = q4_sc[...]
    qpos = qi * tq + lax.broadcasted_iota(jnp.int32, (1, tq), 1)

    def tile_g(a):
        return jnp.concatenate([a] * G, axis=1)

    qpos4 = tile_g(qpos)

    kstart = pl.multiple_of(jnp.maximum(qi * tq + tq - span, 0), tq)
    s_win = _dot(kw_ref[0, pl.ds(kstart, span), :], q4)

    n_cmp = kc_ref.shape[2]
    nid = lax.broadcasted_iota(jnp.int32, (n_cmp, 1), 0)
    cmask = (nid * NSA_CMP_STRIDE + (NSA_CMP_BLOCK - 1) <= qpos4) & (nid < n_cmp - 1)
    s = jnp.where(cmask, _dot(kc_ref[0, 0], q4), NEG)
    p = jnp.where(cmask, jnp.exp2(s - jnp.max(s, axis=0, keepdims=True)), 0.0)
    l = jnp.sum(p, axis=0, keepdims=True)
    p_cmp = p / jnp.where(l > 0.0, l, 1.0)
    o_cmp = _dot(vct_ref[0, 0], p_cmp.astype(BF16))

    diff = qpos - (kstart + lax.broadcasted_iota(jnp.int32, (span, 1), 0))
    s_win = s_win + tile_g(jnp.where((diff >= 0) & (diff < NSA_WINDOW), 0.0, NEG))
    p = jnp.exp2(s_win - jnp.max(s_win, axis=0, keepdims=True))
    o_win = _dot(vwt_ref[0, :, pl.ds(kstart, span)], p.astype(BF16)) / jnp.sum(p, axis=0, keepdims=True)

    psum = p_cmp[:, 0:tq]
    for g in range(1, G):
        psum = psum + p_cmp[:, g * tq:(g + 1) * tq]
    p_hi, p_lo = _split_bf16(psum)
    imp = _dot(ovt_ref[...], p_hi) + _dot(ovt_ref[...], p_lo)
    nb = ovt_ref.shape[0]
    blk = lax.broadcasted_iota(jnp.int32, (nb, 1), 0)
    cur = jnp.right_shift(qpos, SLC_SHIFT)
    forced = (blk == 0) | (blk == cur) | (blk == cur - 1)
    valid = (blk * NSA_SLC_BLOCK <= qpos) & (blk < n_slc)
    work = jnp.where(valid, imp + jnp.where(forced, NSA_BONUS, 0.0), -jnp.inf)
    beaten = jnp.zeros((nb, tq), F32)
    for j in range(n_slc):
        wj = work[j:j + 1, :]
        beaten = beaten + jnp.where((wj > work) | ((wj == work) & (blk > j)), 1.0, 0.0)
    sel = jnp.where(beaten < float(min(NSA_TOPK, n_slc)), 1.0, 0.0).astype(BF16)

    m_sc[...] = jnp.full(m_sc.shape, NEG, F32)
    l_sc[...] = jnp.zeros(l_sc.shape, F32)
    acc_sc[...] = jnp.zeros(acc_sc.shape, F32)
    n_kt = ((qi + 1) * tq + tk - 1) // tk

    def slc_tiles(n):
        def scores(kt):
            return _dot(ks_ref[0, kt * tk:(kt + 1) * tk, :], q4)

        def value_matmul(kt, p, a):
            acc_sc[...] = a * acc_sc[...] + _dot(vst_ref[0, :, kt * tk:(kt + 1) * tk], p)

        s, pending = scores(0), None
        chosen = _dot(e_ref[0:n * tk, :], sel)
        kpos = lax.broadcasted_iota(jnp.int32, (n * tk, 1), 0)
        bias_all = jnp.where((chosen > 0.5) & (kpos <= qpos), 0.0, NEG)
        for kt in range(n):
            s_next = scores(kt + 1) if kt + 1 < n else None
            if pending is not None:
                value_matmul(*pending)
            s = s + tile_g(bias_all[kt * tk:(kt + 1) * tk])
            m_prev = m_sc[...]
            m_new = jnp.maximum(m_prev, jnp.max(s, axis=0, keepdims=True))
            p = jnp.exp2(s - m_new)
            a = jnp.exp2(m_prev - m_new)
            l_sc[...] = a * l_sc[...] + jnp.sum(p, axis=0, keepdims=True)
            m_sc[...] = m_new
            pending = (kt, p.astype(BF16), a)
            s = s_next
        value_matmul(*pending)

    for n in range(1, (ks_ref.shape[1] + tk - 1) // tk + 1):
        pl.when(n_kt == n)(functools.partial(slc_tiles, n))
    o_slc = acc_sc[...] / l_sc[...]

    gt = gt_ref[0]
    for g in range(G):
        cols = slice(g * tq, (g + 1) * tq)
        o = (gt[3 * g:3 * g + 1] * o_cmp[:, cols] + gt[3 * g + 1:3 * g + 2] * o_slc[:, cols]
             + gt[3 * g + 2:3 * g + 3] * o_win[:, cols])
        o_ref[0, :, g * NSA_DH:(g + 1) * NSA_DH] = o.T.astype(o_ref.dtype)


def _nsa_layer(x, shift, scale_v, gate, cos, sin, w_in, pe_k, pe_v, w_ck1, w_ck2, w_cv1, w_cv2, w_o,
               ln_g, ln_b):
    B, S, D = x.shape
    H, KH, Dh, G = NSA_HEADS, NSA_KVH, NSA_DH, NSA_G
    kvw = KH * Dh
    cuts = [int(v) for v in np.cumsum([H * Dh] + [kvw] * 6)]
    wq, wkc, wvc, wks, wvs, wkw, wvw, wgl = jnp.split(w_in, cuts, axis=1)
    wgl = jnp.pad(wgl.reshape(D, KH, G * 3), ((0, 0), (0, 0), (0, GATE_ROWS - G * 3))).reshape(D, KH * GATE_ROWS)
    w = jnp.concatenate([wkc, wks, wkw, wvc], axis=1).astype(BF16)
    wt = jnp.concatenate([wq, wvs, wvw, wgl], axis=1).T.astype(BF16)
    segs = [(0, H * Dh, "rope", LOG2E * Dh ** -0.5, True),
            (0, kvw, "rope", 1.0, False), (kvw, kvw, "rope", 1.0, False), (2 * kvw, kvw, "rope", 1.0, False),
            (3 * kvw, kvw, "plain", 1.0, False),
            (H * Dh, kvw, "plain", 1.0, True), (H * Dh + kvw, kvw, "plain", 1.0, True),
            (H * Dh + 2 * kvw, KH * GATE_ROWS, "sigmoid", 1.0, True)]
    rope = (jnp.concatenate([cos, cos], axis=-1), jnp.concatenate([-sin, sin], axis=-1))
    qt, kc, ks, kw, vc, vst, vwt, gates_t = _mod_proj(
        x, shift, scale_v, w, wt, segs, [BF16] * 7 + [F32], rope=rope)

    st = NSA_CMP_STRIDE
    nchunk = S // st
    kc2 = kc.reshape(B, nchunk, st * kvw)
    vc2 = vc.reshape(B, nchunk, st * kvw)
    hid = w_ck1.shape[-1]
    full = lambda shape: pl.BlockSpec(shape, lambda b: (0,) * len(shape))
    k_cmp, v_cmp_t = pl.pallas_call(
        _nsa_cmp_kernel,
        grid=(B,),
        in_specs=[pl.BlockSpec((1, nchunk, st * kvw), lambda b: (b, 0, 0)),
                  pl.BlockSpec((1, nchunk, st * kvw), lambda b: (b, 0, 0)),
                  full((NSA_CMP_BLOCK, Dh)), full((NSA_CMP_BLOCK, Dh)),
                  full((NSA_CMP_BLOCK, Dh, hid)), full((hid, Dh)),
                  full((NSA_CMP_BLOCK, Dh, hid)), full((Dh, hid))],
        out_specs=[pl.BlockSpec((1, KH, nchunk, Dh), lambda b: (b, 0, 0, 0)),
                   pl.BlockSpec((1, KH, Dh, nchunk), lambda b: (b, 0, 0, 0))],
        out_shape=[jax.ShapeDtypeStruct((B, KH, nchunk, Dh), BF16),
                   jax.ShapeDtypeStruct((B, KH, Dh, nchunk), BF16)],
        compiler_params=_cp(("arbitrary",), 32),
        name="nsa_cmp",
    )(kc2, vc2, pe_k, pe_v, w_ck1.astype(BF16), w_ck2.astype(BF16), w_cv1.astype(BF16), w_cv2.T.astype(BF16))

    n_slc = S // NSA_SLC_BLOCK
    nb = 32
    assert n_slc <= nb and nchunk % 8 == 0
    cmp_start = np.arange(nchunk) * st
    slc_start = np.arange(n_slc) * NSA_SLC_BLOCK
    ov = np.clip(np.minimum(cmp_start[:, None] + NSA_CMP_BLOCK, slc_start[None, :] + NSA_SLC_BLOCK)
                 - np.maximum(cmp_start[:, None], slc_start[None, :]), 0, None).astype(np.float32) / NSA_CMP_BLOCK
    ov[nchunk - 1:] = 0.0
    ov_t = np.zeros((nb, nchunk), np.float32)
    ov_t[:n_slc] = ov.T
    key_blk = np.zeros((S, nb), np.float32)
    key_blk[np.arange(S), np.arange(S) // NSA_SLC_BLOCK] = 1.0

    tq = min(256, S)
    tk = min(256, S)
    span = min(NSA_WINDOW + tq, S)
    k_spec = pl.BlockSpec((1, S, Dh), lambda b, kh, i: (b, 0, kh))
    vt_spec = pl.BlockSpec((1, Dh, S), lambda b, kh, i: (b, kh, 0))
    o = pl.pallas_call(
        functools.partial(_nsa_attn_kernel, tq=tq, tk=tk, span=span, n_slc=n_slc),
        grid=(B, KH, S // tq),
        in_specs=[pl.BlockSpec((1, G * Dh, tq), lambda b, kh, i: (b, kh, i)),
                  pl.BlockSpec((1, 1, nchunk, Dh), lambda b, kh, i: (b, kh, 0, 0)),
                  pl.BlockSpec((1, 1, Dh, nchunk), lambda b, kh, i: (b, kh, 0, 0)),
                  k_spec, vt_spec, k_spec, vt_spec,
                  pl.BlockSpec((1, GATE_ROWS, tq), lambda b, kh, i: (b, kh, i)),
                  pl.BlockSpec((nb, nchunk), lambda b, kh, i: (0, 0)),
                  pl.BlockSpec((S, nb), lambda b, kh, i: (0, 0))],
        out_specs=pl.BlockSpec((1, tq, G * Dh), lambda b, kh, i: (b, i, kh)),
        out_shape=jax.ShapeDtypeStruct((B, S, H * Dh), BF16),
        scratch_shapes=[pltpu.VMEM((Dh, G * tq), BF16), pltpu.VMEM((1, G * tq), F32),
                        pltpu.VMEM((1, G * tq), F32), pltpu.VMEM((Dh, G * tq), F32)],
        compiler_params=_cp(("arbitrary", "arbitrary", "arbitrary"), 32),
        name="nsa_attn",
    )(qt, k_cmp, v_cmp_t, ks, vst, kw, vwt, gates_t, jnp.asarray(ov_t, BF16), jnp.asarray(key_blk, BF16))
    return _out_ln(o, w_o, x, gate, ln_g, ln_b)


def _sb_attn_kernel(qt_ref, k_ref, vt_ref, u_ref, o_ref, r_sc, acc_sc, *, tq, tk):
    d, s_len = vt_ref.shape[1], vt_ref.shape[2]
    u = u_ref[...]
    nd = tq // tk

    def weights(z, strict):
        lk = -(jnp.maximum(z, 0.0) + jnp.log2(1.0 + jnp.exp2(-jnp.abs(z))))
        if strict is not None:
            lk = jnp.where(strict, lk, 0.0)
        between = r_sc[...] + _dot(u, lk.astype(BF16))
        a = jnp.exp2(lk + z + between)
        if strict is not None:
            a = jnp.where(strict, a, 0.0)
        r_sc[...] += jnp.sum(lk, axis=0, keepdims=True)
        return a.astype(BF16)

    for qi in range(s_len // tq):
        qt = qt_ref[0, :, qi * tq:(qi + 1) * tq]
        r_sc[...] = jnp.zeros(r_sc.shape, F32)
        acc_sc[...] = jnp.zeros(acc_sc.shape, F32)
        last = (qi + 1) * nd - 1
        tiles = [last - i for i in range(nd)] + ([qi * nd - 1] if qi > 0 else [])
        qpos = qi * tq + lax.broadcasted_iota(jnp.int32, (1, tq), 1)

        def scores(kt):
            return _dot(k_ref[0, kt * tk:(kt + 1) * tk, :], qt)

        z, pending = scores(tiles[0]), None
        for i, kt in enumerate(tiles):
            nxt = tiles[i + 1] if i + 1 < len(tiles) else max(qi * nd - 2, 0)
            z_next = scores(nxt)
            if pending is not None:
                acc_sc[...] += _dot(vt_ref[0, :, pending[0] * tk:(pending[0] + 1) * tk], pending[1])
            strict = (kt * tk + lax.broadcasted_iota(jnp.int32, (tk, 1), 0) < qpos) if kt >= qi * nd else None
            pending = (kt, weights(z, strict))
            z = z_next
        acc_sc[...] += _dot(vt_ref[0, :, pending[0] * tk:(pending[0] + 1) * tk], pending[1])

        if qi * nd - 2 >= 0:
            def scores_dyn(kt):
                return _dot(k_ref[0, pl.ds(pl.multiple_of(kt * tk, tk), tk), :], qt)

            def cond(carry):
                kt, _ = carry
                return (kt >= 0) & (jnp.max(r_sc[...]) > SB_EXIT_LOG2)

            def body(carry):
                kt, z = carry
                z_next = scores_dyn(jnp.maximum(kt - 1, 0))
                a = weights(z, None)
                acc_sc[...] += _dot(vt_ref[0, :, pl.ds(pl.multiple_of(kt * tk, tk), tk)], a)
                return kt - 1, z_next

            lax.while_loop(cond, body, (qi * nd - 2, z))
        o_ref[0, qi * tq:(qi + 1) * tq, :] = acc_sc[...].T.astype(o_ref.dtype)


def _sb_layer(x, shift, scale_v, gate, w_qkv, w_o, ln_g, ln_b):
    B, S, D = x.shape
    H, d = SB_HEADS, SB_DH
    hw = H * d
    w = w_qkv[:, hw:2 * hw].astype(BF16)
    wt = jnp.concatenate([w_qkv[:, :hw], w_qkv[:, 2 * hw:]], axis=1).T.astype(BF16)
    segs = [(0, hw, "plain", LOG2E * d ** -0.5, True), (0, hw, "plain", 1.0, False), (hw, hw, "plain", 1.0, True)]
    qt, k, vt = _mod_proj(x, shift, scale_v, w, wt, segs, [BF16] * 3)
    tq = min(512, S)
    tk = min(256, S)
    tri = np.triu(np.ones((tk, tk), np.float32), 1)
    o = pl.pallas_call(
        functools.partial(_sb_attn_kernel, tq=tq, tk=tk),
        grid=(B, H),
        in_specs=[pl.BlockSpec((1, d, S), lambda b, h: (b, h, 0)),
                  pl.BlockSpec((1, S, d), lambda b, h: (b, 0, h)),
                  pl.BlockSpec((1, d, S), lambda b, h: (b, h, 0)),
                  pl.BlockSpec((tk, tk), lambda b, h: (0, 0))],
        out_specs=pl.BlockSpec((1, S, d), lambda b, h: (b, 0, h)),
        out_shape=jax.ShapeDtypeStruct((B, S, hw), BF16),
        scratch_shapes=[pltpu.VMEM((1, tq), F32), pltpu.VMEM((d, tq), F32)],
        compiler_params=_cp(("arbitrary", "arbitrary"), 32),
        name="sb_attn",
    )(qt, k, vt, jnp.asarray(tri, BF16))
    return _out_ln(o, w_o, x, gate, ln_g, ln_b)


def _router_kernel(x_ref, sc_ref, sh_ref, whi_ref, wlo_ref, b_ref, tri_ref, hb_ref, info_ref, cnt_ref,
                   base_sc):
    first = (pl.program_id(0) == 0) & (pl.program_id(1) == 0)

    @pl.when(first)
    def _():
        base_sc[...] = jnp.zeros(base_sc.shape, F32)

    h = x_ref[0] * (1.0 + sc_ref[0]) + sh_ref[0]
    h_hi, h_lo = _split_bf16(h)
    hb_ref[0] = _pack_pairs(h)
    logit = (_nt_dot(whi_ref[...], h_hi) + _nt_dot(whi_ref[...], h_lo) + _nt_dot(wlo_ref[...], h_hi)
             + b_ref[...])
    tm = logit.shape[1]
    epg = MOE_EPG
    row = lax.broadcasted_iota(jnp.int32, (epg, 1), 0)
    row_f = row.astype(F32)

    def first_max(vals):
        mx = jnp.max(vals, axis=0, keepdims=True)
        idx = jnp.min(jnp.where(vals == mx, row_f, float(epg)), axis=0, keepdims=True)
        return mx, idx

    gl = jnp.where(row < MOE_GROUPS, logit[0:epg], -jnp.inf)
    gmax, g_idx = first_max(gl)
    g_w = 1.0 / jnp.sum(jnp.exp(gl - gmax), axis=0, keepdims=True)
    el = logit[epg:2 * epg]
    for g in range(1, MOE_GROUPS):
        el = jnp.where(g_idx == float(g), logit[epg * (g + 1):epg * (g + 2)], el)
    ep = jnp.exp(el - jnp.max(el, axis=0, keepdims=True))
    prob = ep / jnp.sum(ep, axis=0, keepdims=True)
    p1, i1 = first_max(prob)
    p2, i2 = first_max(jnp.where(row_f == i1, -1.0, prob))
    e1 = g_idx * float(epg) + i1
    e2 = g_idx * float(epg) + i2
    w1 = g_w * p1 / (p1 + p2)
    w2 = g_w * p2 / (p1 + p2)

    erow = lax.broadcasted_iota(jnp.int32, (MOE_E, 1), 0).astype(F32)
    hot1 = erow == e1
    hot2 = erow == e2
    cnt = jnp.where(hot1 | hot2, 1.0, 0.0)
    before = base_sc[:, 0:1] + _dot(cnt.astype(BF16), tri_ref[...])
    r1 = jnp.sum(jnp.where(hot1, before, 0.0), axis=0, keepdims=True)
    r2 = jnp.sum(jnp.where(hot2, before, 0.0), axis=0, keepdims=True)
    base_sc[...] += jnp.sum(cnt, axis=1, keepdims=True)
    cnt_ref[...] = base_sc[...]

    irow = lax.broadcasted_iota(jnp.int32, (INFO_ROWS, 1), 0)
    info = jnp.zeros((INFO_ROWS, tm), F32)
    for k, val in enumerate((e1, e2, r1, r2, w1, w2)):
        info = jnp.where(irow == k, val, info)
    info_ref[0] = info


def _ffn_kernel(be_ref, seg_ref, nv_ref, nu_ref, x_ref, wg_ref, wu_ref, wd_ref, y_ref, wg_sc, wu_sc, wd_sc, act_sc):
    i = pl.program_id(0)
    n_blk = be_ref.shape[0]

    def stage(blk, src_ref, dst_sc):
        r = jnp.clip(blk, 0, n_blk - 1)
        first = (blk >= 0) & (blk < nu_ref[0]) & ((r == 0) | (be_ref[r] != be_ref[jnp.maximum(r - 1, 0)]))

        @pl.when(first)
        def _():
            dst_sc[seg_ref[r] % FFN_SLOTS] = src_ref[0, 0].astype(BF16)

    stage(i, wg_ref, wg_sc)
    stage(i - 1, wu_ref, wu_sc)
    stage(i - 2, wd_ref, wd_sc)
    blk = i - FFN_LEAD
    done = blk - 1
    has_new = (blk >= 0) & (blk < nu_ref[0])
    has_done = (done >= 0) & (done < nu_ref[0])
    slot_new = seg_ref[jnp.clip(blk, 0, n_blk - 1)] % FFN_SLOTS
    slot_done = seg_ref[jnp.clip(done, 0, n_blk - 1)] % FFN_SLOTS

    def rows_in():
        live = lax.broadcasted_iota(jnp.int32, (x_ref.shape[0], 1), 0) < nv_ref[jnp.clip(blk, 0, n_blk - 1)]
        return jnp.where(live, _unpack_pairs(x_ref[...]), 0.0).astype(BF16)

    act_in, act_out = (i + 1) % 2, i % 2

    @pl.when(has_new & has_done)
    def _():
        xb = rows_in()
        g = _dot(xb, wg_sc[slot_new])
        u = _dot(xb, wu_sc[slot_new])
        y_ref[...] = _pack_pairs(_dot(act_sc[act_in], wd_sc[slot_done]))
        act_sc[act_out] = (g * jax.nn.sigmoid(g) * u).astype(BF16)

    @pl.when(has_new & jnp.logical_not(has_done))
    def _():
        xb = rows_in()
        g = _dot(xb, wg_sc[slot_new])
        u = _dot(xb, wu_sc[slot_new])
        act_sc[act_out] = (g * jax.nn.sigmoid(g) * u).astype(BF16)

    @pl.when(has_done & jnp.logical_not(has_new))
    def _():
        y_ref[...] = _pack_pairs(_dot(act_sc[act_in], wd_sc[slot_done]))

    @pl.when((done >= nu_ref[0]) & (done < n_blk))
    def _():
        y_ref[...] = jnp.zeros(y_ref.shape, y_ref.dtype)


def _sc_gather(table, idx):
    n, d = idx.shape[0], table.shape[1]
    workers = SC_CORES * SC_SUBCORES
    per_w = n // workers
    assert per_w * workers == n and per_w % (2 * SC_CHUNK) == 0
    mesh = plsc.VectorSubcoreMesh(core_axis_name="c", subcore_axis_name="s")

    @functools.partial(
        pl.kernel, mesh=mesh,
        out_type=jax.ShapeDtypeStruct((n, d), table.dtype),
        scratch_types=[pltpu.VMEM((SC_CHUNK,), jnp.int32), pltpu.VMEM((SC_CHUNK,), jnp.int32),
                       pltpu.VMEM((SC_CHUNK, d), table.dtype), pltpu.VMEM((SC_CHUNK, d), table.dtype)]
                      + [pltpu.SemaphoreType.DMA] * 4,
        name="sc_gather")
    def gather(table_hbm, idx_hbm, out_hbm, idx_a, idx_b, rows_a, rows_b, sem_a, sem_b, sem_c, sem_d):
        wid = lax.axis_index("s") * SC_CORES + lax.axis_index("c")

        @pl.loop(0, per_w // (2 * SC_CHUNK))
        def _(j):
            base_a = pl.multiple_of(wid * per_w + j * (2 * SC_CHUNK), SC_CHUNK)
            base_b = pl.multiple_of(base_a + SC_CHUNK, SC_CHUNK)
            pltpu.sync_copy(idx_hbm.at[pl.ds(base_a, SC_CHUNK)], idx_a)
            pltpu.sync_copy(idx_hbm.at[pl.ds(base_b, SC_CHUNK)], idx_b)
            get_a = pltpu.async_copy(table_hbm.at[idx_a], rows_a, sem_a)
            get_b = pltpu.async_copy(table_hbm.at[idx_b], rows_b, sem_b)
            get_a.wait()
            put_a = pltpu.async_copy(rows_a, out_hbm.at[pl.ds(base_a, SC_CHUNK)], sem_c)
            get_b.wait()
            put_b = pltpu.async_copy(rows_b, out_hbm.at[pl.ds(base_b, SC_CHUNK)], sem_d)
            put_a.wait()
            put_b.wait()

    return gather(table, idx)


def _sc_dispatch(src, dest0, dest1, n_out):
    n, d = src.shape
    workers = SC_CORES * SC_SUBCORES
    per_w = n // workers
    assert per_w * workers == n and per_w % SC_CHUNK == 0
    mesh = plsc.VectorSubcoreMesh(core_axis_name="c", subcore_axis_name="s")

    @functools.partial(
        pl.kernel, mesh=mesh,
        out_type=jax.ShapeDtypeStruct((n_out, d), src.dtype),
        scratch_types=[pltpu.VMEM((SC_CHUNK,), jnp.int32),
                       pltpu.VMEM((SC_CHUNK,), jnp.int32),
                       pltpu.VMEM((SC_CHUNK, d), src.dtype)]
                      + [pltpu.SemaphoreType.DMA] * 3,
        name="sc_dispatch")
    def dispatch(src_hbm, d0_hbm, d1_hbm, out_hbm, i0_v, i1_v, rows_v, sem_a, sem_b, sem_c):
        wid = lax.axis_index("s") * SC_CORES + lax.axis_index("c")

        @pl.loop(0, per_w // SC_CHUNK)
        def _(j):
            base = pl.multiple_of(wid * per_w + j * SC_CHUNK, SC_CHUNK)
            get_0 = pltpu.async_copy(d0_hbm.at[pl.ds(base, SC_CHUNK)], i0_v, sem_a)
            get_1 = pltpu.async_copy(d1_hbm.at[pl.ds(base, SC_CHUNK)], i1_v, sem_b)
            get_r = pltpu.async_copy(src_hbm.at[pl.ds(base, SC_CHUNK)], rows_v, sem_c)
            get_0.wait()
            get_1.wait()
            get_r.wait()
            put_0 = pltpu.async_copy(rows_v, out_hbm.at[i0_v], sem_a)
            put_1 = pltpu.async_copy(rows_v, out_hbm.at[i1_v], sem_b)
            put_0.wait()
            put_1.wait()

    return dispatch(src, dest0, dest1)


def _moe_out_kernel(x_ref, y1_ref, y2_ref, info_ref, gate_ref, g_ref, b_ref, o_ref):
    info = info_ref[0]
    ts = info.shape[1]
    info_tok = jnp.concatenate([info, jnp.zeros((LANES - INFO_ROWS, ts), F32)], axis=0).T
    y = info_tok[:, 4:5] * _unpack_pairs(y1_ref[0, 0]) + info_tok[:, 5:6] * _unpack_pairs(y2_ref[0, 0])
    z = ALPHA * x_ref[0] + gate_ref[0] * y
    o_ref[0] = _layer_norm(z, g_ref[...], b_ref[...])


def _moe_out(x, y12, info, gate, g, b, ts=512):
    B, S, D = x.shape
    ts = min(ts, S)
    tok = pl.BlockSpec((1, ts, D), lambda bi, i: (bi, i, 0))
    return pl.pallas_call(
        _moe_out_kernel,
        grid=(B, S // ts),
        in_specs=[tok,
                  pl.BlockSpec((1, 1, ts, D // 2), lambda bi, i: (0, bi, i, 0)),
                  pl.BlockSpec((1, 1, ts, D // 2), lambda bi, i: (1, bi, i, 0)),
                  pl.BlockSpec((1, INFO_ROWS, ts), lambda bi, i: (bi, 0, i)),
                  pl.BlockSpec((1, 1, D), lambda bi, i: (bi, 0, 0)),
                  pl.BlockSpec((1, D), lambda bi, i: (0, 0)),
                  pl.BlockSpec((1, D), lambda bi, i: (0, 0))],
        out_specs=tok,
        out_shape=jax.ShapeDtypeStruct((B, S, D), F32),
        compiler_params=_cp(("arbitrary", "arbitrary"), 40),
        name="moe_out",
    )(x, y12, y12, info, gate, g.reshape(1, D), b.reshape(1, D))


def _moe_layer(x, shift, scale_v, gate, w_grp, b_grp, w_rt, b_rt, layer, w_gate, w_up, w_down, ln_g, ln_b):
    B, S, D = x.shape
    T = B * S
    E, R, F = MOE_E, MOE_ROWS, w_gate.shape[-1]
    g0 = MOE_EPG
    w_r = jnp.zeros((ROUTER_ROWS, D), F32).at[:MOE_GROUPS].set(w_grp.T).at[g0:g0 + E].set(w_rt.T)
    w_hi = w_r.astype(BF16)
    w_lo = (w_r - w_hi.astype(F32)).astype(BF16)
    b_r = jnp.zeros((ROUTER_ROWS, 1), F32).at[:MOE_GROUPS, 0].set(b_grp).at[g0:g0 + E, 0].set(b_rt)
    tm = min(256, S)
    tri = jnp.asarray(np.triu(np.ones((tm, tm), np.float32), 1), BF16)
    hb, info, cnt = pl.pallas_call(
        _router_kernel,
        grid=(B, S // tm),
        in_specs=[pl.BlockSpec((1, tm, D), lambda b, i: (b, i, 0)),
                  pl.BlockSpec((1, 1, D), lambda b, i: (b, 0, 0)),
                  pl.BlockSpec((1, 1, D), lambda b, i: (b, 0, 0)),
                  pl.BlockSpec((ROUTER_ROWS, D), lambda b, i: (0, 0)),
                  pl.BlockSpec((ROUTER_ROWS, D), lambda b, i: (0, 0)),
                  pl.BlockSpec((ROUTER_ROWS, 1), lambda b, i: (0, 0)),
                  pl.BlockSpec((tm, tm), lambda b, i: (0, 0))],
        out_specs=[pl.BlockSpec((1, tm, D // 2), lambda b, i: (b, i, 0)),
                   pl.BlockSpec((1, INFO_ROWS, tm), lambda b, i: (b, 0, i)),
                   pl.BlockSpec((E, LANES), lambda b, i: (0, 0))],
        out_shape=[jax.ShapeDtypeStruct((B, S, D // 2), jnp.uint32),
                   jax.ShapeDtypeStruct((B, INFO_ROWS, S), F32),
                   jax.ShapeDtypeStruct((E, LANES), F32)],
        scratch_shapes=[pltpu.VMEM((E, LANES), F32)],
        compiler_params=_cp(("arbitrary", "arbitrary"), 32),
        name="moe_router",
    )(x, scale_v, shift, w_hi, w_lo, b_r, tri)

    counts = cnt[:, 0].astype(jnp.int32)
    padded = (counts + R - 1) // R * R
    ends = jnp.cumsum(padded)
    starts = ends - padded
    expert_ids = jnp.arange(E, dtype=jnp.int32)

    def row_of(expert, rank):
        start = jnp.sum(jnp.where(expert.astype(jnp.int32)[..., None] == expert_ids, starts, 0), axis=-1)
        return (start + rank.astype(jnp.int32)).reshape(T)

    dest0 = row_of(info[:, 0, :], info[:, 2, :])
    dest1 = row_of(info[:, 1, :], info[:, 3, :])
    n_blk = (T * 2) // R + E
    P = n_blk * R
    blk_exp = jnp.minimum(jnp.sum(jnp.arange(n_blk)[:, None] * R >= ends[None, :], axis=1), E - 1).astype(jnp.int32)
    n_used = (ends[-1] // R).astype(jnp.int32).reshape(1)
    x_rows = _sc_dispatch(hb.reshape(T, D // 2), dest0, dest1, P)

    seg = jnp.concatenate([jnp.zeros((1,), jnp.int32),
                           jnp.cumsum((blk_exp[1:] != blk_exp[:-1]).astype(jnp.int32))])

    live = jnp.clip(counts[blk_exp] - (jnp.arange(n_blk, dtype=jnp.int32) * R - starts[blk_exp]), 0, R).astype(jnp.int32)

    def blk_at(offset):
        return lambda i: jnp.clip(i - offset, 0, n_blk - 1)

    y_rows = pl.pallas_call(
        _ffn_kernel,
        grid_spec=pltpu.PrefetchScalarGridSpec(
            num_scalar_prefetch=4,
            grid=(n_blk + FFN_LEAD + 1,),
            in_specs=[pl.BlockSpec((R, D // 2), lambda i, be, sg, nv, nu: (blk_at(FFN_LEAD)(i), 0)),
                      pl.BlockSpec((1, 1, D, F), lambda i, be, sg, nv, nu: (layer, be[blk_at(0)(i)], 0, 0)),
                      pl.BlockSpec((1, 1, D, F), lambda i, be, sg, nv, nu: (layer, be[blk_at(1)(i)], 0, 0)),
                      pl.BlockSpec((1, 1, F, D), lambda i, be, sg, nv, nu: (layer, be[blk_at(2)(i)], 0, 0))],
            out_specs=pl.BlockSpec((R, D // 2), lambda i, be, sg, nv, nu: (blk_at(FFN_LEAD + 1)(i), 0)),
            scratch_shapes=[pltpu.VMEM((FFN_SLOTS, D, F), BF16), pltpu.VMEM((FFN_SLOTS, D, F), BF16),
                            pltpu.VMEM((FFN_SLOTS, F, D), BF16), pltpu.VMEM((2, R, F), BF16)]),
        out_shape=jax.ShapeDtypeStruct((P, D // 2), jnp.uint32),
        compiler_params=_cp(("arbitrary",), 48),
        name="moe_ffn",
    )(blk_exp, seg, live, n_used, x_rows, w_gate, w_up, w_down)

    y12 = _sc_gather(y_rows, jnp.concatenate([dest0, dest1])).reshape(2, B, S, D // 2)
    return _moe_out(x, y12, info, gate, ln_g, ln_b)


def kernel(x, c, positions, ada_w, ada_b, ln_g, ln_b, mla_w_in, mla_g_q, mla_w_uq, mla_g_kv, mla_w_ukv, mla_w_o, nsa_w_in, nsa_pe_k, nsa_pe_v, nsa_w_ck1, nsa_w_ck2, nsa_w_cv1, nsa_w_cv2, nsa_w_o, sb_w_qkv, sb_w_o, moe_w_grp, moe_b_grp, moe_w_rt, moe_b_rt, moe_w_gate, moe_w_up, moe_w_down):
    D = x.shape[-1]
    depth = ada_w.shape[0]
    cos_mla, sin_mla = _rope_tables(positions, MLA_DR)
    cos_nsa, sin_nsa = _rope_tables(positions, NSA_DH)
    mods = _ada_all(c, ada_w, ada_b)

    def mod(i, sub):
        m = mods[2 * i + sub][:, None, :]
        return m[..., :D], m[..., D:2 * D], m[..., 2 * D:]

    for i in range(depth):
        kind, j = i % N_MIXERS, i // N_MIXERS
        shift, scale_v, gate = mod(i, 0)
        if kind == 0:
            x = _mla_layer(x, shift, scale_v, gate, cos_mla, sin_mla, mla_w_in[j], mla_g_q[j], mla_w_uq[j],
                           mla_g_kv[j], mla_w_ukv[j], mla_w_o[j], ln_g[i, 0], ln_b[i, 0])
        elif kind == 1:
            x = _nsa_layer(x, shift, scale_v, gate, cos_nsa, sin_nsa, nsa_w_in[j], nsa_pe_k[j], nsa_pe_v[j],
                           nsa_w_ck1[j], nsa_w_ck2[j], nsa_w_cv1[j], nsa_w_cv2[j], nsa_w_o[j],
                           ln_g[i, 0], ln_b[i, 0])
        else:
            x = _sb_layer(x, shift, scale_v, gate, sb_w_qkv[j], sb_w_o[j], ln_g[i, 0], ln_b[i, 0])
        shift, scale_v, gate = mod(i, 1)
        x = _moe_layer(x, shift, scale_v, gate, moe_w_grp[i], moe_b_grp[i], moe_w_rt[i], moe_b_rt[i],
                       i, moe_w_gate, moe_w_up, moe_w_down, ln_g[i, 1], ln_b[i, 1])
    return x
```

```python
import functools

import numpy as np
import jax
import jax.numpy as jnp
from jax import lax
from jax.experimental import pallas as pl
from jax.experimental.pallas import tpu as pltpu
from jax.experimental.pallas import tpu_sc as plsc

BF16 = jnp.bfloat16
F32 = jnp.float32

DEPTH = 4
N_MIXERS = 3
ROPE_THETA = 10000.0
LN_EPS = 1e-5
RMS_EPS = 1e-6
NEG = -1e30
LOG2E = 1.4426950408889634
ALPHA = (2 * DEPTH) ** 0.25

MLA_HEADS, MLA_DN, MLA_DR, MLA_DV = 8, 128, 64, 128
MLA_QR, MLA_KVR = 256, 256

NSA_HEADS, NSA_KVH, NSA_DH = 8, 2, 128
NSA_G = NSA_HEADS // NSA_KVH
NSA_CMP_BLOCK, NSA_CMP_STRIDE = 32, 16
NSA_SLC_BLOCK, NSA_TOPK = 64, 8
SLC_SHIFT = 6
NSA_WINDOW = 512
NSA_BONUS = 1e3
GATE_ROWS = 16

SB_HEADS, SB_DH = 8, 128
SB_EXIT_LOG2 = -160.0

MOE_GROUPS, MOE_EPG = 4, 8
EPG_SHIFT = 3
MOE_E = MOE_GROUPS * MOE_EPG
MOE_ROWS = 256
FFN_LEAD = 3
FFN_SLOTS = FFN_LEAD + 2
ROUTER_ROWS = 64
INFO_ROWS = 8

SC_CORES, SC_SUBCORES = 2, 16
SC_CHUNK = 64

LANES = 128
VMEM_BYTES_V7X = 64 * 1024 * 1024


def _cp(sem, vmem_mb):
    assert vmem_mb * 2**20 < VMEM_BYTES_V7X
    return pltpu.CompilerParams(dimension_semantics=sem, vmem_limit_bytes=vmem_mb * 2**20)


def _nt_dot(a, b):
    return lax.dot_general(a, b, (((1,), (1,)), ((), ())), preferred_element_type=F32)


def _dot(a, b):
    return jnp.dot(a, b, preferred_element_type=F32)


def _split_bf16(a):
    hi = a.astype(BF16)
    lo = (a - hi.astype(F32)).astype(BF16)
    return hi, lo


def _pack_pairs(a):
    n = a.shape[1] // 2
    hi = pltpu.bitcast(a[:, :n].astype(BF16).astype(F32), jnp.uint32)
    lo = pltpu.bitcast(a[:, n:].astype(BF16).astype(F32), jnp.uint32)
    return hi | (lo >> 16)


def _unpack_pairs(u):
    hi = pltpu.bitcast(u & jnp.uint32(0xFFFF0000), F32)
    lo = pltpu.bitcast(u << 16, F32)
    return jnp.concatenate([hi, lo], axis=1)


def _layer_norm(z, g, b):
    mu = jnp.mean(z, axis=-1, keepdims=True)
    d = z - mu
    var = jnp.mean(d * d, axis=-1, keepdims=True)
    return d * lax.rsqrt(var + LN_EPS) * g + b


def _ada_kernel(c_ref, w_ref, b_ref, o_ref):
    c = c_ref[...]
    s_hi, s_lo = _split_bf16(c * jax.nn.sigmoid(c))
    w_hi, w_lo = _split_bf16(w_ref[0])
    o_ref[0] = _dot(s_hi, w_hi) + _dot(s_lo, w_hi) + _dot(s_hi, w_lo) + b_ref[0]


def _ada_all(c, ada_w, ada_b):
    L, two, D, D3 = ada_w.shape
    B = c.shape[0]
    n = L * two
    tn = 1024
    return pl.pallas_call(
        _ada_kernel,
        grid=(n, D3 // tn),
        in_specs=[pl.BlockSpec((B, D), lambda i, j: (0, 0)),
                  pl.BlockSpec((1, D, tn), lambda i, j: (i, 0, j)),
                  pl.BlockSpec((1, 1, tn), lambda i, j: (i, 0, j))],
        out_specs=pl.BlockSpec((1, B, tn), lambda i, j: (i, 0, j)),
        out_shape=jax.ShapeDtypeStruct((n, B, D3), F32),
        compiler_params=_cp(("arbitrary", "arbitrary"), 32),
        name="ada",
    )(c, ada_w.reshape(n, D, D3), ada_b.reshape(n, 1, D3))


def _res_ln_kernel(x_ref, y_ref, gate_ref, g_ref, b_ref, o_ref):
    z = ALPHA * x_ref[0] + gate_ref[0] * y_ref[0]
    o_ref[0] = _layer_norm(z, g_ref[...], b_ref[...])


def _res_ln(x, y, gate, g, b, ts=512):
    B, S, D = x.shape
    ts = min(ts, S)
    tok = pl.BlockSpec((1, ts, D), lambda bi, i: (bi, i, 0))
    return pl.pallas_call(
        _res_ln_kernel,
        grid=(B, S // ts),
        in_specs=[tok, tok,
                  pl.BlockSpec((1, 1, D), lambda bi, i: (bi, 0, 0)),
                  pl.BlockSpec((1, D), lambda bi, i: (0, 0)),
                  pl.BlockSpec((1, D), lambda bi, i: (0, 0))],
        out_specs=tok,
        out_shape=jax.ShapeDtypeStruct((B, S, D), F32),
        compiler_params=_cp(("arbitrary", "arbitrary"), 32),
        name="res_ln",
    )(x, y, gate, g.reshape(1, D), b.reshape(1, D))


def _out_ln_kernel(o_ref, w_ref, x_ref, gate_ref, g_ref, b_ref, out_ref):
    y = _dot(o_ref[0], w_ref[...])
    z = ALPHA * x_ref[0] + gate_ref[0] * y
    out_ref[0] = _layer_norm(z, g_ref[...], b_ref[...])


def _out_ln(o, w_o, x, gate, g, b, ts=512):
    B, S, D = x.shape
    K = o.shape[-1]
    ts = min(ts, S)
    return pl.pallas_call(
        _out_ln_kernel,
        grid=(B, S // ts),
        in_specs=[pl.BlockSpec((1, ts, K), lambda bi, i: (bi, i, 0)),
                  pl.BlockSpec((K, D), lambda bi, i: (0, 0)),
                  pl.BlockSpec((1, ts, D), lambda bi, i: (bi, i, 0)),
                  pl.BlockSpec((1, 1, D), lambda bi, i: (bi, 0, 0)),
                  pl.BlockSpec((1, D), lambda bi, i: (0, 0)),
                  pl.BlockSpec((1, D), lambda bi, i: (0, 0))],
        out_specs=pl.BlockSpec((1, ts, D), lambda bi, i: (bi, i, 0)),
        out_shape=jax.ShapeDtypeStruct((B, S, D), F32),
        compiler_params=_cp(("arbitrary", "arbitrary"), 40),
        name="out_ln",
    )(o, w_o.astype(BF16), x, gate, g.reshape(1, D), b.reshape(1, D))


def _rope_tables(positions, dim):
    inv_freq = 1.0 / (ROPE_THETA ** (jnp.arange(0, dim, 2, dtype=F32) / dim))
    ang = positions.astype(F32)[..., None] * inv_freq
    return jnp.cos(ang), jnp.sin(ang)


def _rot_half_cols(w, half):
    return jnp.concatenate([-w[..., half:], w[..., :half]], axis=-1)


def _mla_front_kernel(x_ref, sc_ref, sh_ref, win_ref, gq_ref, gkv_ref, wqt_ref, wk_ref, wvt_ref, t_ref, tt_ref,
                      qt_ref, k_ref, vt_ref, *, scale):
    hb = (x_ref[0] * (1.0 + sc_ref[0]) + sh_ref[0]).astype(BF16)
    r = _dot(hb, win_ref[...])
    tab = t_ref[0]
    tab_t = tt_ref[0]
    low = lax.broadcasted_iota(jnp.int32, tab.shape, 1) < MLA_DR

    def rms(cx, g):
        return (cx * lax.rsqrt(jnp.mean(cx * cx, axis=-1, keepdims=True) + RMS_EPS) * g).astype(BF16)

    cq = rms(r[:, :MLA_QR], gq_ref[...])
    ckv = rms(r[:, MLA_QR:MLA_QR + MLA_KVR], gkv_ref[...])
    t = r[:, MLA_QR + MLA_KVR:] * tab
    k_rope = jnp.where(low, t + pltpu.roll(t, MLA_DR, axis=1), 0.0).astype(BF16)
    zeros = jnp.zeros((MLA_DR, hb.shape[0]), BF16)
    for h in range(MLA_HEADS):
        qh = _nt_dot(wqt_ref[h], cq)
        tq = qh[MLA_DN:] * tab_t
        qt_ref[0, h, :MLA_DN, :] = (qh[:MLA_DN] * scale).astype(BF16)
        qt_ref[0, h, MLA_DN:MLA_DN + MLA_DR, :] = ((tq[:MLA_DR] + tq[MLA_DR:]) * scale).astype(BF16)
        qt_ref[0, h, MLA_DN + MLA_DR:, :] = zeros
        k_ref[0, h, :, :MLA_DN] = _dot(ckv, wk_ref[h]).astype(BF16)
        k_ref[0, h, :, MLA_DN:] = k_rope
        vt_ref[0, h] = _nt_dot(wvt_ref[h], ckv).astype(BF16)


def _flash_update(s, m_sc, l_sc, acc_sc, vt):
    m_prev = m_sc[...]
    m_new = jnp.maximum(m_prev, jnp.max(s, axis=0, keepdims=True))
    p = jnp.exp2(s - m_new)
    a = jnp.exp2(m_prev - m_new)
    l_sc[...] = a * l_sc[...] + jnp.sum(p, axis=0, keepdims=True)
    acc_sc[...] = a * acc_sc[...] + _dot(vt, p.astype(BF16))
    m_sc[...] = m_new


def _flash_causal_kernel(qt_ref, k_ref, vt_ref, o_ref, *, t):
    dv, s_len = vt_ref.shape[2], vt_ref.shape[3]
    pairs = [(qi, kt) for qi in range(s_len // t) for kt in range(qi + 1)]
    kpos = lax.broadcasted_iota(jnp.int32, (t, 1), 0)
    qpos = lax.broadcasted_iota(jnp.int32, (1, t), 1)
    diag = kpos <= qpos

    def scores(j):
        qi, kt = pairs[j]
        return _dot(k_ref[0, 0, kt * t:(kt + 1) * t, :], qt_ref[0, 0, :, qi * t:(qi + 1) * t])

    def value_matmul(pending, l, acc):
        qi, kt, p, a = pending
        acc = a * acc + _dot(vt_ref[0, 0, :, kt * t:(kt + 1) * t], p)
        if kt == qi:
            o_ref[0, qi * t:(qi + 1) * t, :] = (acc / l).T.astype(o_ref.dtype)
        return acc

    s_cur, pending = scores(0), None
    m = l = acc = None
    for j, (qi, kt) in enumerate(pairs):
        s_next = scores(j + 1) if j + 1 < len(pairs) else None
        if pending is not None:
            acc = value_matmul(pending, l, acc)
        if kt == 0:
            m, l, acc = jnp.full((1, t), NEG, F32), jnp.zeros((1, t), F32), jnp.zeros((dv, t), F32)
        s = jnp.where(diag, s_cur, NEG) if kt == qi else s_cur
        m_new = jnp.maximum(m, jnp.max(s, axis=0, keepdims=True))
        p = jnp.exp2(s - m_new)
        a = jnp.exp2(m - m_new)
        l = a * l + jnp.sum(p, axis=0, keepdims=True)
        m = m_new
        pending = (qi, kt, p.astype(BF16), a)
        s_cur = s_next
    value_matmul(pending, l, acc)


def _mla_layer(x, shift, scale_v, gate, cos, sin, w_in, g_q, w_uq, g_kv, w_ukv, w_o, ln_g, ln_b):
    B, S, D = x.shape
    H, dn, dr, dv = MLA_HEADS, MLA_DN, MLA_DR, MLA_DV
    half = dr // 2
    kr = w_in[:, MLA_QR + MLA_KVR:]
    win_p = jnp.concatenate([w_in, _rot_half_cols(kr, half)], axis=1).astype(BF16)
    wq = w_uq.reshape(MLA_QR, H, dn + dr)
    wq = jnp.concatenate([wq, _rot_half_cols(wq[..., dn:], half)], axis=-1)
    wqt = jnp.transpose(wq, (1, 2, 0)).astype(BF16)
    wkv = w_ukv.reshape(MLA_KVR, H, dn + dv)
    wk = jnp.transpose(wkv[..., :dn], (1, 0, 2)).astype(BF16)
    wvt = jnp.transpose(wkv[..., dn:], (1, 2, 0)).astype(BF16)
    tab = jnp.concatenate([cos, cos, sin, sin], axis=-1)
    tab_t = jnp.swapaxes(tab, 1, 2)
    ts = min(512, S)
    nw = win_p.shape[1]
    hd = dn + 2 * dr
    qt, k, vt = pl.pallas_call(
        functools.partial(_mla_front_kernel, scale=LOG2E * (dn + dr) ** -0.5),
        grid=(B, S // ts),
        in_specs=[pl.BlockSpec((1, ts, D), lambda b, i: (b, i, 0)),
                  pl.BlockSpec((1, 1, D), lambda b, i: (b, 0, 0)),
                  pl.BlockSpec((1, 1, D), lambda b, i: (b, 0, 0)),
                  pl.BlockSpec((D, nw), lambda b, i: (0, 0)),
                  pl.BlockSpec((1, MLA_QR), lambda b, i: (0, 0)),
                  pl.BlockSpec((1, MLA_KVR), lambda b, i: (0, 0)),
                  pl.BlockSpec((H, hd, MLA_QR), lambda b, i: (0, 0, 0)),
                  pl.BlockSpec((H, MLA_KVR, dn), lambda b, i: (0, 0, 0)),
                  pl.BlockSpec((H, dv, MLA_KVR), lambda b, i: (0, 0, 0)),
                  pl.BlockSpec((1, ts, LANES), lambda b, i: (b, i, 0)),
                  pl.BlockSpec((1, LANES, ts), lambda b, i: (b, 0, i))],
        out_specs=[pl.BlockSpec((1, H, hd, ts), lambda b, i: (b, 0, 0, i)),
                   pl.BlockSpec((1, H, ts, hd), lambda b, i: (b, 0, i, 0)),
                   pl.BlockSpec((1, H, dv, ts), lambda b, i: (b, 0, 0, i))],
        out_shape=[jax.ShapeDtypeStruct((B, H, hd, S), BF16),
                   jax.ShapeDtypeStruct((B, H, S, hd), BF16),
                   jax.ShapeDtypeStruct((B, H, dv, S), BF16)],
        compiler_params=_cp(("arbitrary", "arbitrary"), 48),
        name="mla_front",
    )(x, scale_v, shift, win_p, g_q.reshape(1, -1), g_kv.reshape(1, -1), wqt, wk, wvt, tab, tab_t)

    t = min(512, S)
    o = pl.pallas_call(
        functools.partial(_flash_causal_kernel, t=t),
        grid=(B, H),
        in_specs=[pl.BlockSpec((1, 1, hd, S), lambda b, h: (b, h, 0, 0)),
                  pl.BlockSpec((1, 1, S, hd), lambda b, h: (b, h, 0, 0)),
                  pl.BlockSpec((1, 1, dv, S), lambda b, h: (b, h, 0, 0))],
        out_specs=pl.BlockSpec((1, S, dv), lambda b, h: (b, 0, h)),
        out_shape=jax.ShapeDtypeStruct((B, S, H * dv), BF16),
        compiler_params=_cp(("arbitrary", "arbitrary"), 48),
        name="mla_attn",
    )(qt, k, vt)
    return _out_ln(o, w_o, x, gate, ln_g, ln_b)


def _proj_kernel(*refs, segs, chunk, has_rope):
    x_ref, sc_ref, sh_ref, w_ref, wt_ref = refs[:5]
    n_in = 9 if has_rope else 5
    outs = refs[n_in:]
    hb = (x_ref[0] * (1.0 + sc_ref[0]) + sh_ref[0]).astype(BF16)
    if has_rope:
        cos, sin = refs[5][0], refs[6][0]
        cos_t, sin_t = refs[7][0], refs[8][0]
    half = LANES // 2
    for (c0, width, kind, scale, transposed), o_ref in zip(segs, outs):
        for j in range(0, width, chunk):
            cw = min(chunk, width - j)
            if transposed:
                r = _nt_dot(wt_ref[c0 + j:c0 + j + cw, :], hb)
            else:
                r = _dot(hb, w_ref[:, c0 + j:c0 + j + cw])
            if kind == "rope":
                parts = []
                for t in range(0, cw, LANES):
                    if transposed:
                        u = r[t:t + LANES]
                        parts.append(u * cos_t + pltpu.roll(u, half, axis=0) * sin_t)
                    else:
                        u = r[:, t:t + LANES]
                        parts.append(u * cos + pltpu.roll(u, half, axis=1) * sin)
                r = jnp.concatenate(parts, axis=0 if transposed else 1) if len(parts) > 1 else parts[0]
            elif kind == "sigmoid":
                r = jax.nn.sigmoid(r)
            if scale != 1.0:
                r = r * scale
            if transposed:
                o_ref[0, j:j + cw, :] = r.astype(o_ref.dtype)
            else:
                o_ref[0, :, j:j + cw] = r.astype(o_ref.dtype)


def _mod_proj(x, shift, scale_v, w, wt, segs, out_dtypes, rope=None, ts=512, chunk=256):
    B, S, D = x.shape
    ts = min(ts, S)
    in_specs = [pl.BlockSpec((1, ts, D), lambda b, i: (b, i, 0)),
                pl.BlockSpec((1, 1, D), lambda b, i: (b, 0, 0)),
                pl.BlockSpec((1, 1, D), lambda b, i: (b, 0, 0)),
                pl.BlockSpec(w.shape, lambda b, i: (0, 0)),
                pl.BlockSpec(wt.shape, lambda b, i: (0, 0))]
    args = [x, scale_v, shift, w, wt]
    if rope is not None:
        cos, sin = rope
        in_specs += [pl.BlockSpec((1, ts, LANES), lambda b, i: (b, i, 0))] * 2
        in_specs += [pl.BlockSpec((1, LANES, ts), lambda b, i: (b, 0, i))] * 2
        args += [cos, sin, jnp.swapaxes(cos, 1, 2), jnp.swapaxes(sin, 1, 2)]
    out_specs, out_shape = [], []
    for sg, dt in zip(segs, out_dtypes):
        if sg[4]:
            out_specs.append(pl.BlockSpec((1, sg[1], ts), lambda b, i: (b, 0, i)))
            out_shape.append(jax.ShapeDtypeStruct((B, sg[1], S), dt))
        else:
            out_specs.append(pl.BlockSpec((1, ts, sg[1]), lambda b, i: (b, i, 0)))
            out_shape.append(jax.ShapeDtypeStruct((B, S, sg[1]), dt))
    return pl.pallas_call(
        functools.partial(_proj_kernel, segs=tuple(segs), chunk=chunk, has_rope=rope is not None),
        grid=(B, S // ts),
        in_specs=in_specs,
        out_specs=out_specs,
        out_shape=out_shape,
        compiler_params=_cp(("arbitrary", "arbitrary"), 48),
        name="mod_proj",
    )(*args)


def _nsa_cmp_kernel(kc_ref, vc_ref, pek_ref, pev_ref, wk1_ref, wk2_ref, wv1_ref, wv2t_ref, ko_ref, vto_ref):
    half = NSA_CMP_BLOCK // 2
    nchunk = kc_ref.shape[1]

    def hidden(src, pe_ref, w1_ref, kh):
        a = jnp.zeros((nchunk, w1_ref.shape[2]), F32)
        bm = jnp.zeros((nchunk, w1_ref.shape[2]), F32)
        for l in range(half):
            c0 = (l * NSA_KVH + kh) * NSA_DH
            t = src[0, :, c0:c0 + NSA_DH].astype(F32)
            a = a + _dot((t + pe_ref[l:l + 1, :]).astype(BF16), w1_ref[l])
            bm = bm + _dot((t + pe_ref[half + l:half + l + 1, :]).astype(BF16), w1_ref[half + l])
        hid = a + pltpu.roll(bm, nchunk - 1, axis=0)
        return (hid * jax.nn.sigmoid(hid)).astype(BF16)

    rowi = lax.broadcasted_iota(jnp.int32, (nchunk, 1), 0)
    coli = lax.broadcasted_iota(jnp.int32, (1, nchunk), 1)
    for kh in range(NSA_KVH):
        out = _dot(hidden(kc_ref, pek_ref, wk1_ref, kh), wk2_ref[...])
        ko_ref[0, kh] = jnp.where(rowi < nchunk - 1, out, 0.0).astype(ko_ref.dtype)
        out_t = _nt_dot(wv2t_ref[...], hidden(vc_ref, pev_ref, wv1_ref, kh))
        vto_ref[0, kh] = jnp.where(coli < nchunk - 1, out_t, 0.0).astype(vto_ref.dtype)


def _nsa_attn_kernel(qt_ref, kc_ref, vct_ref, ks_ref, vst_ref, kw_ref, vwt_ref, gt_ref, ovt_ref, e_ref, o_ref,
                     q4_sc, m_sc, l_sc, acc_sc, *, tq, tk, span, n_slc):
    qi = pl.program_id(2)
    G = NSA_G
    for g in range(G):
        q4_sc[:, g * tq:(g + 1) * tq] = qt_ref[0, g * NSA_DH:(g + 1) * NSA_DH, :]
    q4 = q4_sc[...]
    qpos = qi * tq + lax.broadcasted_iota(jnp.int32, (1, tq), 1)

    def tile_g(a):
        return jnp.concatenate([a] * G, axis=1)

    qpos4 = tile_g(qpos)

    kstart = pl.multiple_of(jnp.maximum(qi * tq + tq - span, 0), tq)
    s_win = _dot(kw_ref[0, pl.ds(kstart, span), :], q4)

    n_cmp = kc_ref.shape[2]
    nid = lax.broadcasted_iota(jnp.int32, (n_cmp, 1), 0)
    cmask = (nid * NSA_CMP_STRIDE + (NSA_CMP_BLOCK - 1) <= qpos4) & (nid < n_cmp - 1)
    s = jnp.where(cmask, _dot(kc_ref[0, 0], q4), NEG)
    p = jnp.where(cmask, jnp.exp2(s - jnp.max(s, axis=0, keepdims=True)), 0.0)
    l = jnp.sum(p, axis=0, keepdims=True)
    p_cmp = p / jnp.where(l > 0.0, l, 1.0)
    o_cmp = _dot(vct_ref[0, 0], p_cmp.astype(BF16))

    diff = qpos - (kstart + lax.broadcasted_iota(jnp.int32, (span, 1), 0))
    s_win = s_win + tile_g(jnp.where((diff >= 0) & (diff < NSA_WINDOW), 0.0, NEG))
    p = jnp.exp2(s_win - jnp.max(s_win, axis=0, keepdims=True))
    o_win = _dot(vwt_ref[0, :, pl.ds(kstart, span)], p.astype(BF16)) / jnp.sum(p, axis=0, keepdims=True)

    psum = p_cmp[:, 0:tq]
    for g in range(1, G):
        psum = psum + p_cmp[:, g * tq:(g + 1) * tq]
    p_hi, p_lo = _split_bf16(psum)
    imp = _dot(ovt_ref[...], p_hi) + _dot(ovt_ref[...], p_lo)
    nb = ovt_ref.shape[0]
    blk = lax.broadcasted_iota(jnp.int32, (nb, 1), 0)
    cur = jnp.right_shift(qpos, SLC_SHIFT)
    forced = (blk == 0) | (blk == cur) | (blk == cur - 1)
    valid = (blk * NSA_SLC_BLOCK <= qpos) & (blk < n_slc)
    work = jnp.where(valid, imp + jnp.where(forced, NSA_BONUS, 0.0), -jnp.inf)
    beaten = jnp.zeros((nb, tq), F32)
    for j in range(n_slc):
        wj = work[j:j + 1, :]
        beaten = beaten + jnp.where((wj > work) | ((wj == work) & (blk > j)), 1.0, 0.0)
    sel = jnp.where(beaten < float(min(NSA_TOPK, n_slc)), 1.0, 0.0).astype(BF16)

    m_sc[...] = jnp.full(m_sc.shape, NEG, F32)
    l_sc[...] = jnp.zeros(l_sc.shape, F32)
    acc_sc[...] = jnp.zeros(acc_sc.shape, F32)
    n_kt = ((qi + 1) * tq + tk - 1) // tk

    def slc_tiles(n):
        def scores(kt):
            return _dot(ks_ref[0, kt * tk:(kt + 1) * tk, :], q4)

        def value_matmul(kt, p, a):
            acc_sc[...] = a * acc_sc[...] + _dot(vst_ref[0, :, kt * tk:(kt + 1) * tk], p)

        s, pending = scores(0), None
        chosen = _dot(e_ref[0:n * tk, :], sel)
        kpos = lax.broadcasted_iota(jnp.int32, (n * tk, 1), 0)
        bias_all = jnp.where((chosen > 0.5) & (kpos <= qpos), 0.0, NEG)
        for kt in range(n):
            s_next = scores(kt + 1) if kt + 1 < n else None
            if pending is not None:
                value_matmul(*pending)
            s = s + tile_g(bias_all[kt * tk:(kt + 1) * tk])
            m_prev = m_sc[...]
            m_new = jnp.maximum(m_prev, jnp.max(s, axis=0, keepdims=True))
            p = jnp.exp2(s - m_new)
            a = jnp.exp2(m_prev - m_new)
            l_sc[...] = a * l_sc[...] + jnp.sum(p, axis=0, keepdims=True)
            m_sc[...] = m_new
            pending = (kt, p.astype(BF16), a)
            s = s_next
        value_matmul(*pending)

    for n in range(1, (ks_ref.shape[1] + tk - 1) // tk + 1):
        pl.when(n_kt == n)(functools.partial(slc_tiles, n))
    o_slc = acc_sc[...] / l_sc[...]

    gt = gt_ref[0]
    for g in range(G):
        cols = slice(g * tq, (g + 1) * tq)
        o = (gt[3 * g:3 * g + 1] * o_cmp[:, cols] + gt[3 * g + 1:3 * g + 2] * o_slc[:, cols]
             + gt[3 * g + 2:3 * g + 3] * o_win[:, cols])
        o_ref[0, :, g * NSA_DH:(g + 1) * NSA_DH] = o.T.astype(o_ref.dtype)


def _nsa_layer(x, shift, scale_v, gate, cos, sin, w_in, pe_k, pe_v, w_ck1, w_ck2, w_cv1, w_cv2, w_o,
               ln_g, ln_b):
    B, S, D = x.shape
    H, KH, Dh, G = NSA_HEADS, NSA_KVH, NSA_DH, NSA_G
    kvw = KH * Dh
    cuts = [int(v) for v in np.cumsum([H * Dh] + [kvw] * 6)]
    wq, wkc, wvc, wks, wvs, wkw, wvw, wgl = jnp.split(w_in, cuts, axis=1)
    wgl = jnp.pad(wgl.reshape(D, KH, G * 3), ((0, 0), (0, 0), (0, GATE_ROWS - G * 3))).reshape(D, KH * GATE_ROWS)
    w = jnp.concatenate([wkc, wks, wkw, wvc], axis=1).astype(BF16)
    wt = jnp.concatenate([wq, wvs, wvw, wgl], axis=1).T.astype(BF16)
    segs = [(0, H * Dh, "rope", LOG2E * Dh ** -0.5, True),
            (0, kvw, "rope", 1.0, False), (kvw, kvw, "rope", 1.0, False), (2 * kvw, kvw, "rope", 1.0, False),
            (3 * kvw, kvw, "plain", 1.0, False),
            (H * Dh, kvw, "plain", 1.0, True), (H * Dh + kvw, kvw, "plain", 1.0, True),
            (H * Dh + 2 * kvw, KH * GATE_ROWS, "sigmoid", 1.0, True)]
    rope = (jnp.concatenate([cos, cos], axis=-1), jnp.concatenate([-sin, sin], axis=-1))
    qt, kc, ks, kw, vc, vst, vwt, gates_t = _mod_proj(
        x, shift, scale_v, w, wt, segs, [BF16] * 7 + [F32], rope=rope)

    st = NSA_CMP_STRIDE
    nchunk = S // st
    kc2 = kc.reshape(B, nchunk, st * kvw)
    vc2 = vc.reshape(B, nchunk, st * kvw)
    hid = w_ck1.shape[-1]
    full = lambda shape: pl.BlockSpec(shape, lambda b: (0,) * len(shape))
    k_cmp, v_cmp_t = pl.pallas_call(
        _nsa_cmp_kernel,
        grid=(B,),
        in_specs=[pl.BlockSpec((1, nchunk, st * kvw), lambda b: (b, 0, 0)),
                  pl.BlockSpec((1, nchunk, st * kvw), lambda b: (b, 0, 0)),
                  full((NSA_CMP_BLOCK, Dh)), full((NSA_CMP_BLOCK, Dh)),
                  full((NSA_CMP_BLOCK, Dh, hid)), full((hid, Dh)),
                  full((NSA_CMP_BLOCK, Dh, hid)), full((Dh, hid))],
        out_specs=[pl.BlockSpec((1, KH, nchunk, Dh), lambda b: (b, 0, 0, 0)),
                   pl.BlockSpec((1, KH, Dh, nchunk), lambda b: (b, 0, 0, 0))],
        out_shape=[jax.ShapeDtypeStruct((B, KH, nchunk, Dh), BF16),
                   jax.ShapeDtypeStruct((B, KH, Dh, nchunk), BF16)],
        compiler_params=_cp(("arbitrary",), 32),
        name="nsa_cmp",
    )(kc2, vc2, pe_k, pe_v, w_ck1.astype(BF16), w_ck2.astype(BF16), w_cv1.astype(BF16), w_cv2.T.astype(BF16))

    n_slc = S // NSA_SLC_BLOCK
    nb = 32
    assert n_slc <= nb and nchunk % 8 == 0
    cmp_start = np.arange(nchunk) * st
    slc_start = np.arange(n_slc) * NSA_SLC_BLOCK
    ov = np.clip(np.minimum(cmp_start[:, None] + NSA_CMP_BLOCK, slc_start[None, :] + NSA_SLC_BLOCK)
                 - np.maximum(cmp_start[:, None], slc_start[None, :]), 0, None).astype(np.float32) / NSA_CMP_BLOCK
    ov[nchunk - 1:] = 0.0
    ov_t = np.zeros((nb, nchunk), np.float32)
    ov_t[:n_slc] = ov.T
    key_blk = np.zeros((S, nb), np.float32)
    key_blk[np.arange(S), np.arange(S) // NSA_SLC_BLOCK] = 1.0

    tq = min(256, S)
    tk = min(256, S)
    span = min(NSA_WINDOW + tq, S)
    k_spec = pl.BlockSpec((1, S, Dh), lambda b, kh, i: (b, 0, kh))
    vt_spec = pl.BlockSpec((1, Dh, S), lambda b, kh, i: (b, kh, 0))
    o = pl.pallas_call(
        functools.partial(_nsa_attn_kernel, tq=tq, tk=tk, span=span, n_slc=n_slc),
        grid=(B, KH, S // tq),
        in_specs=[pl.BlockSpec((1, G * Dh, tq), lambda b, kh, i: (b, kh, i)),
                  pl.BlockSpec((1, 1, nchunk, Dh), lambda b, kh, i: (b, kh, 0, 0)),
                  pl.BlockSpec((1, 1, Dh, nchunk), lambda b, kh, i: (b, kh, 0, 0)),
                  k_spec, vt_spec, k_spec, vt_spec,
                  pl.BlockSpec((1, GATE_ROWS, tq), lambda b, kh, i: (b, kh, i)),
                  pl.BlockSpec((nb, nchunk), lambda b, kh, i: (0, 0)),
                  pl.BlockSpec((S, nb), lambda b, kh, i: (0, 0))],
        out_specs=pl.BlockSpec((1, tq, G * Dh), lambda b, kh, i: (b, i, kh)),
        out_shape=jax.ShapeDtypeStruct((B, S, H * Dh), BF16),
        scratch_shapes=[pltpu.VMEM((Dh, G * tq), BF16), pltpu.VMEM((1, G * tq), F32),
                        pltpu.VMEM((1, G * tq), F32), pltpu.VMEM((Dh, G * tq), F32)],
        compiler_params=_cp(("arbitrary", "arbitrary", "arbitrary"), 32),
        name="nsa_attn",
    )(qt, k_cmp, v_cmp_t, ks, vst, kw, vwt, gates_t, jnp.asarray(ov_t, BF16), jnp.asarray(key_blk, BF16))
    return _out_ln(o, w_o, x, gate, ln_g, ln_b)


def _sb_attn_kernel(qt_ref, k_ref, vt_ref, u_ref, o_ref, r_sc, acc_sc, *, tq, tk):
    d, s_len = vt_ref.shape[1], vt_ref.shape[2]
    u = u_ref[...]
    nd = tq // tk

    def weights(z, strict):
        lk = -(jnp.maximum(z, 0.0) + jnp.log2(1.0 + jnp.exp2(-jnp.abs(z))))
        if strict is not None:
            lk = jnp.where(strict, lk, 0.0)
        between = r_sc[...] + _dot(u, lk.astype(BF16))
        a = jnp.exp2(lk + z + between)
        if strict is not None:
            a = jnp.where(strict, a, 0.0)
        r_sc[...] += jnp.sum(lk, axis=0, keepdims=True)
        return a.astype(BF16)

    for qi in range(s_len // tq):
        qt = qt_ref[0, :, qi * tq:(qi + 1) * tq]
        r_sc[...] = jnp.zeros(r_sc.shape, F32)
        acc_sc[...] = jnp.zeros(acc_sc.shape, F32)
        last = (qi + 1) * nd - 1
        tiles = [last - i for i in range(nd)] + ([qi * nd - 1] if qi > 0 else [])
        qpos = qi * tq + lax.broadcasted_iota(jnp.int32, (1, tq), 1)

        def scores(kt):
            return _dot(k_ref[0, kt * tk:(kt + 1) * tk, :], qt)

        z, pending = scores(tiles[0]), None
        for i, kt in enumerate(tiles):
            nxt = tiles[i + 1] if i + 1 < len(tiles) else max(qi * nd - 2, 0)
            z_next = scores(nxt)
            if pending is not None:
                acc_sc[...] += _dot(vt_ref[0, :, pending[0] * tk:(pending[0] + 1) * tk], pending[1])
            strict = (kt * tk + lax.broadcasted_iota(jnp.int32, (tk, 1), 0) < qpos) if kt >= qi * nd else None
            pending = (kt, weights(z, strict))
            z = z_next
        acc_sc[...] += _dot(vt_ref[0, :, pending[0] * tk:(pending[0] + 1) * tk], pending[1])

        if qi * nd - 2 >= 0:
            def scores_dyn(kt):
                return _dot(k_ref[0, pl.ds(pl.multiple_of(kt * tk, tk), tk), :], qt)

            def cond(carry):
                kt, _ = carry
                return (kt >= 0) & (jnp.max(r_sc[...]) > SB_EXIT_LOG2)

            def body(carry):
                kt, z = carry
                z_next = scores_dyn(jnp.maximum(kt - 1, 0))
                a = weights(z, None)
                acc_sc[...] += _dot(vt_ref[0, :, pl.ds(pl.multiple_of(kt * tk, tk), tk)], a)
                return kt - 1, z_next

            lax.while_loop(cond, body, (qi * nd - 2, z))
        o_ref[0, qi * tq:(qi + 1) * tq, :] = acc_sc[...].T.astype(o_ref.dtype)


def _sb_layer(x, shift, scale_v, gate, w_qkv, w_o, ln_g, ln_b):
    B, S, D = x.shape
    H, d = SB_HEADS, SB_DH
    hw = H * d
    w = w_qkv[:, hw:2 * hw].astype(BF16)
    wt = jnp.concatenate([w_qkv[:, :hw], w_qkv[:, 2 * hw:]], axis=1).T.astype(BF16)
    segs = [(0, hw, "plain", LOG2E * d ** -0.5, True), (0, hw, "plain", 1.0, False), (hw, hw, "plain", 1.0, True)]
    qt, k, vt = _mod_proj(x, shift, scale_v, w, wt, segs, [BF16] * 3)
    tq = min(512, S)
    tk = min(256, S)
    tri = np.triu(np.ones((tk, tk), np.float32), 1)
    o = pl.pallas_call(
        functools.partial(_sb_attn_kernel, tq=tq, tk=tk),
        grid=(B, H),
        in_specs=[pl.BlockSpec((1, d, S), lambda b, h: (b, h, 0)),
                  pl.BlockSpec((1, S, d), lambda b, h: (b, 0, h)),
                  pl.BlockSpec((1, d, S), lambda b, h: (b, h, 0)),
                  pl.BlockSpec((tk, tk), lambda b, h: (0, 0))],
        out_specs=pl.BlockSpec((1, S, d), lambda b, h: (b, 0, h)),
        out_shape=jax.ShapeDtypeStruct((B, S, hw), BF16),
        scratch_shapes=[pltpu.VMEM((1, tq), F32), pltpu.VMEM((d, tq), F32)],
        compiler_params=_cp(("arbitrary", "arbitrary"), 32),
        name="sb_attn",
    )(qt, k, vt, jnp.asarray(tri, BF16))
    return _out_ln(o, w_o, x, gate, ln_g, ln_b)


def _router_kernel(x_ref, sc_ref, sh_ref, whi_ref, wlo_ref, b_ref, tri_ref, hb_ref, info_ref, cnt_ref,
                   base_sc):
    first = (pl.program_id(0) == 0) & (pl.program_id(1) == 0)

    @pl.when(first)
    def _():
        base_sc[...] = jnp.zeros(base_sc.shape, F32)

    h = x_ref[0] * (1.0 + sc_ref[0]) + sh_ref[0]
    h_hi, h_lo = _split_bf16(h)
    hb_ref[0] = _pack_pairs(h_hi)
    logit = (_nt_dot(whi_ref[...], h_hi) + _nt_dot(whi_ref[...], h_lo) + _nt_dot(wlo_ref[...], h_hi)
             + b_ref[...])
    tm = logit.shape[1]
    epg = MOE_EPG
    row = lax.broadcasted_iota(jnp.int32, (epg, 1), 0)
    row_f = row.astype(F32)

    def first_max(vals):
        mx = jnp.max(vals, axis=0, keepdims=True)
        idx = jnp.min(jnp.where(vals == mx, row_f, float(epg)), axis=0, keepdims=True)
        return mx, idx

    gl = jnp.where(row < MOE_GROUPS, logit[0:epg], -jnp.inf)
    gmax, g_idx = first_max(gl)
    g_w = 1.0 / jnp.sum(jnp.exp(gl - gmax), axis=0, keepdims=True)
    el = logit[epg:2 * epg]
    for g in range(1, MOE_GROUPS):
        el = jnp.where(g_idx == float(g), logit[epg * (g + 1):epg * (g + 2)], el)
    ep = jnp.exp(el - jnp.max(el, axis=0, keepdims=True))
    prob = ep / jnp.sum(ep, axis=0, keepdims=True)
    p1, i1 = first_max(prob)
    p2, i2 = first_max(jnp.where(row_f == i1, -1.0, prob))
    e1 = g_idx * float(epg) + i1
    e2 = g_idx * float(epg) + i2
    w1 = g_w * p1 / (p1 + p2)
    w2 = g_w * p2 / (p1 + p2)

    erow = lax.broadcasted_iota(jnp.int32, (MOE_E, 1), 0).astype(F32)
    hot1 = erow == e1
    hot2 = erow == e2
    cnt = jnp.where(hot1 | hot2, 1.0, 0.0)
    before = base_sc[:, 0:1] + _dot(cnt.astype(BF16), tri_ref[...])
    r1 = jnp.sum(jnp.where(hot1, before, 0.0), axis=0, keepdims=True)
    r2 = jnp.sum(jnp.where(hot2, before, 0.0), axis=0, keepdims=True)
    base_sc[...] += jnp.sum(cnt, axis=1, keepdims=True)
    cnt_ref[...] = base_sc[...]

    irow = lax.broadcasted_iota(jnp.int32, (INFO_ROWS, 1), 0)
    info = jnp.zeros((INFO_ROWS, tm), F32)
    for k, val in enumerate((e1, e2, r1, r2, w1, w2)):
        info = jnp.where(irow == k, val, info)
    info_ref[0] = info


def _ffn_kernel(be_ref, seg_ref, nv_ref, nu_ref, x_ref, wg_ref, wu_ref, wd_ref, y_ref, wg_sc, wu_sc, wd_sc, act_sc):
    i = pl.program_id(0)
    n_blk = be_ref.shape[0]

    def stage(blk, src_ref, dst_sc):
        r = jnp.clip(blk, 0, n_blk - 1)
        first = (blk >= 0) & (blk < nu_ref[0]) & ((r == 0) | (be_ref[r] != be_ref[jnp.maximum(r - 1, 0)]))

        @pl.when(first)
        def _():
            dst_sc[seg_ref[r] % FFN_SLOTS] = src_ref[0, 0].astype(BF16)

    stage(i, wg_ref, wg_sc)
    stage(i - 1, wu_ref, wu_sc)
    stage(i - 2, wd_ref, wd_sc)
    blk = i - FFN_LEAD
    done = blk - 1
    has_new = (blk >= 0) & (blk < nu_ref[0])
    has_done = (done >= 0) & (done < nu_ref[0])
    slot_new = seg_ref[jnp.clip(blk, 0, n_blk - 1)] % FFN_SLOTS
    slot_done = seg_ref[jnp.clip(done, 0, n_blk - 1)] % FFN_SLOTS

    def rows_in():
        live = lax.broadcasted_iota(jnp.int32, (x_ref.shape[0], 1), 0) < nv_ref[jnp.clip(blk, 0, n_blk - 1)]
        return jnp.where(live, _unpack_pairs(x_ref[...]), 0.0).astype(BF16)

    act_in, act_out = (i + 1) % 2, i % 2

    @pl.when(has_new & has_done)
    def _():
        xb = rows_in()
        g = _dot(xb, wg_sc[slot_new])
        u = _dot(xb, wu_sc[slot_new])
        y_ref[...] = _pack_pairs(_dot(act_sc[act_in], wd_sc[slot_done]))
        act_sc[act_out] = (g * jax.nn.sigmoid(g) * u).astype(BF16)

    @pl.when(has_new & jnp.logical_not(has_done))
    def _():
        xb = rows_in()
        g = _dot(xb, wg_sc[slot_new])
        u = _dot(xb, wu_sc[slot_new])
        act_sc[act_out] = (g * jax.nn.sigmoid(g) * u).astype(BF16)

    @pl.when(has_done & jnp.logical_not(has_new))
    def _():
        y_ref[...] = _pack_pairs(_dot(act_sc[act_in], wd_sc[slot_done]))

    @pl.when((done >= nu_ref[0]) & (done < n_blk))
    def _():
        y_ref[...] = jnp.zeros(y_ref.shape, y_ref.dtype)


def _sc_gather(table, idx):
    n, d = idx.shape[0], table.shape[1]
    workers = SC_CORES * SC_SUBCORES
    per_w = n // workers
    assert per_w * workers == n and per_w % (2 * SC_CHUNK) == 0
    mesh = plsc.VectorSubcoreMesh(core_axis_name="c", subcore_axis_name="s")

    @functools.partial(
        pl.kernel, mesh=mesh,
        out_type=jax.ShapeDtypeStruct((n, d), table.dtype),
        scratch_types=[pltpu.VMEM((SC_CHUNK,), jnp.int32), pltpu.VMEM((SC_CHUNK,), jnp.int32),
                       pltpu.VMEM((SC_CHUNK, d), table.dtype), pltpu.VMEM((SC_CHUNK, d), table.dtype)]
                      + [pltpu.SemaphoreType.DMA] * 4,
        name="sc_gather")
    def gather(table_hbm, idx_hbm, out_hbm, idx_a, idx_b, rows_a, rows_b, sem_a, sem_b, sem_c, sem_d):
        wid = lax.axis_index("s") * SC_CORES + lax.axis_index("c")

        @pl.loop(0, per_w // (2 * SC_CHUNK))
        def _(j):
            base_a = pl.multiple_of(wid * per_w + j * (2 * SC_CHUNK), SC_CHUNK)
            base_b = pl.multiple_of(base_a + SC_CHUNK, SC_CHUNK)
            pltpu.sync_copy(idx_hbm.at[pl.ds(base_a, SC_CHUNK)], idx_a)
            pltpu.sync_copy(idx_hbm.at[pl.ds(base_b, SC_CHUNK)], idx_b)
            get_a = pltpu.async_copy(table_hbm.at[idx_a], rows_a, sem_a)
            get_b = pltpu.async_copy(table_hbm.at[idx_b], rows_b, sem_b)
            get_a.wait()
            put_a = pltpu.async_copy(rows_a, out_hbm.at[pl.ds(base_a, SC_CHUNK)], sem_c)
            get_b.wait()
            put_b = pltpu.async_copy(rows_b, out_hbm.at[pl.ds(base_b, SC_CHUNK)], sem_d)
            put_a.wait()
            put_b.wait()

    return gather(table, idx)


def _sc_dispatch(src, dest0, dest1, n_out):
    n, d = src.shape
    workers = SC_CORES * SC_SUBCORES
    per_w = n // workers
    assert per_w * workers == n and per_w % SC_CHUNK == 0
    mesh = plsc.VectorSubcoreMesh(core_axis_name="c", subcore_axis_name="s")

    @functools.partial(
        pl.kernel, mesh=mesh,
        out_type=jax.ShapeDtypeStruct((n_out, d), src.dtype),
        scratch_types=[pltpu.VMEM((SC_CHUNK,), jnp.int32),
                       pltpu.VMEM((SC_CHUNK,), jnp.int32),
                       pltpu.VMEM((SC_CHUNK, d), src.dtype)]
                      + [pltpu.SemaphoreType.DMA] * 3,
        name="sc_dispatch")
    def dispatch(src_hbm, d0_hbm, d1_hbm, out_hbm, i0_v, i1_v, rows_v, sem_a, sem_b, sem_c):
        wid = lax.axis_index("s") * SC_CORES + lax.axis_index("c")

        @pl.loop(0, per_w // SC_CHUNK)
        def _(j):
            base = pl.multiple_of(wid * per_w + j * SC_CHUNK, SC_CHUNK)
            get_0 = pltpu.async_copy(d0_hbm.at[pl.ds(base, SC_CHUNK)], i0_v, sem_a)
            get_1 = pltpu.async_copy(d1_hbm.at[pl.ds(base, SC_CHUNK)], i1_v, sem_b)
            get_r = pltpu.async_copy(src_hbm.at[pl.ds(base, SC_CHUNK)], rows_v, sem_c)
            get_0.wait()
            get_1.wait()
            get_r.wait()
            put_0 = pltpu.async_copy(rows_v, out_hbm.at[i0_v], sem_a)
            put_1 = pltpu.async_copy(rows_v, out_hbm.at[i1_v], sem_b)
            put_0.wait()
            put_1.wait()

    return dispatch(src, dest0, dest1)


def _moe_out_kernel(x_ref, y1_ref, y2_ref, info_ref, gate_ref, g_ref, b_ref, o_ref):
    info = info_ref[0]
    ts = info.shape[1]
    info_tok = jnp.concatenate([info, jnp.zeros((LANES - INFO_ROWS, ts), F32)], axis=0).T
    y = info_tok[:, 4:5] * _unpack_pairs(y1_ref[0, 0]) + info_tok[:, 5:6] * _unpack_pairs(y2_ref[0, 0])
    z = ALPHA * x_ref[0] + gate_ref[0] * y
    o_ref[0] = _layer_norm(z, g_ref[...], b_ref[...])


def _moe_out(x, y12, info, gate, g, b, ts=512):
    B, S, D = x.shape
    ts = min(ts, S)
    tok = pl.BlockSpec((1, ts, D), lambda bi, i: (bi, i, 0))
    return pl.pallas_call(
        _moe_out_kernel,
        grid=(B, S // ts),
        in_specs=[tok,
                  pl.BlockSpec((1, 1, ts, D // 2), lambda bi, i: (0, bi, i, 0)),
                  pl.BlockSpec((1, 1, ts, D // 2), lambda bi, i: (1, bi, i, 0)),
                  pl.BlockSpec((1, INFO_ROWS, ts), lambda bi, i: (bi, 0, i)),
                  pl.BlockSpec((1, 1, D), lambda bi, i: (bi, 0, 0)),
                  pl.BlockSpec((1, D), lambda bi, i: (0, 0)),
                  pl.BlockSpec((1, D), lambda bi, i: (0, 0))],
        out_specs=tok,
        out_shape=jax.ShapeDtypeStruct((B, S, D), F32),
        compiler_params=_cp(("arbitrary", "arbitrary"), 40),
        name="moe_out",
    )(x, y12, y12, info, gate, g.reshape(1, D), b.reshape(1, D))


def _moe_layer(x, shift, scale_v, gate, w_grp, b_grp, w_rt, b_rt, layer, w_gate, w_up, w_down, ln_g, ln_b):
    B, S, D = x.shape
    T = B * S
    E, R, F = MOE_E, MOE_ROWS, w_gate.shape[-1]
    g0 = MOE_EPG
    w_r = jnp.zeros((ROUTER_ROWS, D), F32).at[:MOE_GROUPS].set(w_grp.T).at[g0:g0 + E].set(w_rt.T)
    w_hi = w_r.astype(BF16)
    w_lo = (w_r - w_hi.astype(F32)).astype(BF16)
    b_r = jnp.zeros((ROUTER_ROWS, 1), F32).at[:MOE_GROUPS, 0].set(b_grp).at[g0:g0 + E, 0].set(b_rt)
    tm = min(256, S)
    tri = jnp.asarray(np.triu(np.ones((tm, tm), np.float32), 1), BF16)
    hb, info, cnt = pl.pallas_call(
        _router_kernel,
        grid=(B, S // tm),
        in_specs=[pl.BlockSpec((1, tm, D), lambda b, i: (b, i, 0)),
                  pl.BlockSpec((1, 1, D), lambda b, i: (b, 0, 0)),
                  pl.BlockSpec((1, 1, D), lambda b, i: (b, 0, 0)),
                  pl.BlockSpec((ROUTER_ROWS, D), lambda b, i: (0, 0)),
                  pl.BlockSpec((ROUTER_ROWS, D), lambda b, i: (0, 0)),
                  pl.BlockSpec((ROUTER_ROWS, 1), lambda b, i: (0, 0)),
                  pl.BlockSpec((tm, tm), lambda b, i: (0, 0))],
        out_specs=[pl.BlockSpec((1, tm, D // 2), lambda b, i: (b, i, 0)),
                   pl.BlockSpec((1, INFO_ROWS, tm), lambda b, i: (b, 0, i)),
                   pl.BlockSpec((E, LANES), lambda b, i: (0, 0))],
        out_shape=[jax.ShapeDtypeStruct((B, S, D // 2), jnp.uint32),
                   jax.ShapeDtypeStruct((B, INFO_ROWS, S), F32),
                   jax.ShapeDtypeStruct((E, LANES), F32)],
        scratch_shapes=[pltpu.VMEM((E, LANES), F32)],
        compiler_params=_cp(("arbitrary", "arbitrary"), 32),
        name="moe_router",
    )(x, scale_v, shift, w_hi, w_lo, b_r, tri)

    counts = cnt[:, 0].astype(jnp.int32)
    padded = (counts + R - 1) // R * R
    ends = jnp.cumsum(padded)
    starts = ends - padded
    expert_ids = jnp.arange(E, dtype=jnp.int32)

    def per_expert(table, expert):
        return jnp.sum(jnp.where(expert[..., None] == expert_ids, table, 0), axis=-1)

    def row_of(expert, rank):
        return (per_expert(starts, expert.astype(jnp.int32)) + rank.astype(jnp.int32)).reshape(T)

    dest0 = row_of(info[:, 0, :], info[:, 2, :])
    dest1 = row_of(info[:, 1, :], info[:, 3, :])
    n_blk = (T * 2) // R + E
    P = n_blk * R
    blk_exp = jnp.minimum(jnp.sum(jnp.arange(n_blk)[:, None] * R >= ends[None, :], axis=1), E - 1).astype(jnp.int32)
    n_used = (ends[-1] // R).astype(jnp.int32).reshape(1)
    x_rows = _sc_dispatch(hb.reshape(T, D // 2), dest0, dest1, P)

    seg = jnp.concatenate([jnp.zeros((1,), jnp.int32),
                           jnp.cumsum((blk_exp[1:] != blk_exp[:-1]).astype(jnp.int32))])

    live = jnp.clip(per_expert(counts, blk_exp) + per_expert(starts, blk_exp) - jnp.arange(n_blk, dtype=jnp.int32) * R,
                    0, R).astype(jnp.int32)

    def blk_at(offset):
        return lambda i: jnp.clip(i - offset, 0, n_blk - 1)

    y_rows = pl.pallas_call(
        _ffn_kernel,
        grid_spec=pltpu.PrefetchScalarGridSpec(
            num_scalar_prefetch=4,
            grid=(n_blk + FFN_LEAD + 1,),
            in_specs=[pl.BlockSpec((R, D // 2), lambda i, be, sg, nv, nu: (blk_at(FFN_LEAD)(i), 0)),
                      pl.BlockSpec((1, 1, D, F), lambda i, be, sg, nv, nu: (layer, be[blk_at(0)(i)], 0, 0)),
                      pl.BlockSpec((1, 1, D, F), lambda i, be, sg, nv, nu: (layer, be[blk_at(1)(i)], 0, 0)),
                      pl.BlockSpec((1, 1, F, D), lambda i, be, sg, nv, nu: (layer, be[blk_at(2)(i)], 0, 0))],
            out_specs=pl.BlockSpec((R, D // 2), lambda i, be, sg, nv, nu: (blk_at(FFN_LEAD + 1)(i), 0)),
            scratch_shapes=[pltpu.VMEM((FFN_SLOTS, D, F), BF16), pltpu.VMEM((FFN_SLOTS, D, F), BF16),
                            pltpu.VMEM((FFN_SLOTS, F, D), BF16), pltpu.VMEM((2, R, F), BF16)]),
        out_shape=jax.ShapeDtypeStruct((P, D // 2), jnp.uint32),
        compiler_params=_cp(("arbitrary",), 48),
        name="moe_ffn",
    )(blk_exp, seg, live, n_used, x_rows, w_gate, w_up, w_down)

    y12 = _sc_gather(y_rows, jnp.concatenate([dest0, dest1])).reshape(2, B, S, D // 2)
    return _moe_out(x, y12, info, gate, ln_g, ln_b)


def kernel(x, c, positions, ada_w, ada_b, ln_g, ln_b, mla_w_in, mla_g_q, mla_w_uq, mla_g_kv, mla_w_ukv, mla_w_o, nsa_w_in, nsa_pe_k, nsa_pe_v, nsa_w_ck1, nsa_w_ck2, nsa_w_cv1, nsa_w_cv2, nsa_w_o, sb_w_qkv, sb_w_o, moe_w_grp, moe_b_grp, moe_w_rt, moe_b_rt, moe_w_gate, moe_w_up, moe_w_down):
    D = x.shape[-1]
    depth = ada_w.shape[0]
    cos_mla, sin_mla = _rope_tables(positions, MLA_DR)
    cos_nsa, sin_nsa = _rope_tables(positions, NSA_DH)
    mods = _ada_all(c, ada_w, ada_b)

    def mod(i, sub):
        m = mods[2 * i + sub][:, None, :]
        return m[..., :D], m[..., D:2 * D], m[..., 2 * D:]

    for i in range(depth):
        kind, j = i % N_MIXERS, i // N_MIXERS
        shift, scale_v, gate = mod(i, 0)
        if kind == 0:
            x = _mla_layer(x, shift, scale_v, gate, cos_mla, sin_mla, mla_w_in[j], mla_g_q[j], mla_w_uq[j],
                           mla_g_kv[j], mla_w_ukv[j], mla_w_o[j], ln_g[i, 0], ln_b[i, 0])
        elif kind == 1:
            x = _nsa_layer(x, shift, scale_v, gate, cos_nsa, sin_nsa, nsa_w_in[j], nsa_pe_k[j], nsa_pe_v[j],
                           nsa_w_ck1[j], nsa_w_ck2[j], nsa_w_cv1[j], nsa_w_cv2[j], nsa_w_o[j],
                           ln_g[i, 0], ln_b[i, 0])
        else:
            x = _sb_layer(x, shift, scale_v, gate, sb_w_qkv[j], sb_w_o[j], ln_g[i, 0], ln_b[i, 0])
        shift, scale_v, gate = mod(i, 1)
        x = _moe_layer(x, shift, scale_v, gate, moe_w_grp[i], moe_b_grp[i], moe_w_rt[i], moe_b_rt[i],
                       i, moe_w_gate, moe_w_up, moe_w_down, ln_g[i, 1], ln_b[i, 1])
    return x
```

```python
import functools

import numpy as np
import jax
import jax.numpy as jnp
from jax import lax
from jax.experimental import pallas as pl
from jax.experimental.pallas import tpu as pltpu
from jax.experimental.pallas import tpu_sc as plsc

BF16 = jnp.bfloat16
F32 = jnp.float32

DEPTH = 4
N_MIXERS = 3
ROPE_THETA = 10000.0
LN_EPS = 1e-5
RMS_EPS = 1e-6
NEG = -1e30
LOG2E = 1.4426950408889634
ALPHA = (2 * DEPTH) ** 0.25

MLA_HEADS, MLA_DN, MLA_DR, MLA_DV = 8, 128, 64, 128
MLA_QR, MLA_KVR = 256, 256

NSA_HEADS, NSA_KVH, NSA_DH = 8, 2, 128
NSA_G = NSA_HEADS // NSA_KVH
NSA_CMP_BLOCK, NSA_CMP_STRIDE = 32, 16
NSA_SLC_BLOCK, NSA_TOPK = 64, 8
SLC_SHIFT = 6
NSA_WINDOW = 512
NSA_BONUS = 1e3
GATE_ROWS = 16

SB_HEADS, SB_DH = 8, 128
SB_EXIT_LOG2 = -160.0

MOE_GROUPS, MOE_EPG = 4, 8
EPG_SHIFT = 3
MOE_E = MOE_GROUPS * MOE_EPG
MOE_ROWS = 256
FFN_LEAD = 3
FFN_SLOTS = FFN_LEAD + 2
OUT_LN_ROWS = 256
ROUTER_ROWS = 64
INFO_ROWS = 8

SC_CORES, SC_SUBCORES = 2, 16
SC_CHUNK = 64

LANES = 128
VMEM_BYTES_V7X = 64 * 1024 * 1024


def _cp(sem, vmem_mb):
    assert vmem_mb * 2**20 < VMEM_BYTES_V7X
    return pltpu.CompilerParams(dimension_semantics=sem, vmem_limit_bytes=vmem_mb * 2**20)


def _nt_dot(a, b):
    return lax.dot_general(a, b, (((1,), (1,)), ((), ())), preferred_element_type=F32)


def _dot(a, b):
    return jnp.dot(a, b, preferred_element_type=F32)


def _split_bf16(a):
    hi = a.astype(BF16)
    lo = (a - hi.astype(F32)).astype(BF16)
    return hi, lo


def _pack_pairs(a):
    n = a.shape[1] // 2
    hi = pltpu.bitcast(a[:, :n].astype(BF16).astype(F32), jnp.uint32)
    lo = pltpu.bitcast(a[:, n:].astype(BF16).astype(F32), jnp.uint32)
    return hi | (lo >> 16)


def _unpack_pairs(u):
    hi = pltpu.bitcast(u & jnp.uint32(0xFFFF0000), F32)
    lo = pltpu.bitcast(u << 16, F32)
    return jnp.concatenate([hi, lo], axis=1)


def _layer_norm(z, g, b):
    mu = jnp.mean(z, axis=-1, keepdims=True)
    d = z - mu
    var = jnp.mean(d * d, axis=-1, keepdims=True)
    return d * lax.rsqrt(var + LN_EPS) * g + b


def _ada_kernel(c_ref, w_ref, b_ref, o_ref):
    c = c_ref[...]
    s_hi, s_lo = _split_bf16(c * jax.nn.sigmoid(c))
    w_hi, w_lo = _split_bf16(w_ref[0])
    o_ref[0] = _dot(s_hi, w_hi) + _dot(s_lo, w_hi) + _dot(s_hi, w_lo) + b_ref[0]


def _ada_all(c, ada_w, ada_b):
    L, two, D, D3 = ada_w.shape
    B = c.shape[0]
    n = L * two
    tn = 1024
    return pl.pallas_call(
        _ada_kernel,
        grid=(n, D3 // tn),
        in_specs=[pl.BlockSpec((B, D), lambda i, j: (0, 0)),
                  pl.BlockSpec((1, D, tn), lambda i, j: (i, 0, j)),
                  pl.BlockSpec((1, 1, tn), lambda i, j: (i, 0, j))],
        out_specs=pl.BlockSpec((1, B, tn), lambda i, j: (i, 0, j)),
        out_shape=jax.ShapeDtypeStruct((n, B, D3), F32),
        compiler_params=_cp(("arbitrary", "arbitrary"), 32),
        name="ada",
    )(c, ada_w.reshape(n, D, D3), ada_b.reshape(n, 1, D3))


def _route_tokens(x1, sc, sh, whi, wlo, bcol, tri, base_sc):
    h = x1 * (1.0 + sc) + sh
    h_hi, h_lo = _split_bf16(h)
    logit = _nt_dot(whi, h_hi) + _nt_dot(whi, h_lo) + _nt_dot(wlo, h_hi) + bcol
    tm = logit.shape[1]
    epg = MOE_EPG
    row = lax.broadcasted_iota(jnp.int32, (epg, 1), 0)
    row_f = row.astype(F32)

    def first_max(vals):
        mx = jnp.max(vals, axis=0, keepdims=True)
        idx = jnp.min(jnp.where(vals == mx, row_f, float(epg)), axis=0, keepdims=True)
        return mx, idx

    gl = jnp.where(row < MOE_GROUPS, logit[0:epg], -jnp.inf)
    gmax, g_idx = first_max(gl)
    g_w = 1.0 / jnp.sum(jnp.exp(gl - gmax), axis=0, keepdims=True)
    el = logit[epg:2 * epg]
    for g in range(1, MOE_GROUPS):
        el = jnp.where(g_idx == float(g), logit[epg * (g + 1):epg * (g + 2)], el)
    ep = jnp.exp(el - jnp.max(el, axis=0, keepdims=True))
    prob = ep / jnp.sum(ep, axis=0, keepdims=True)
    p1, i1 = first_max(prob)
    p2, i2 = first_max(jnp.where(row_f == i1, -1.0, prob))
    e1 = g_idx * float(epg) + i1
    e2 = g_idx * float(epg) + i2
    w1 = g_w * p1 / (p1 + p2)
    w2 = g_w * p2 / (p1 + p2)

    erow = lax.broadcasted_iota(jnp.int32, (MOE_E, 1), 0).astype(F32)
    hot1 = erow == e1
    hot2 = erow == e2
    cnt = jnp.where(hot1 | hot2, 1.0, 0.0)
    before = base_sc[:, 0:1] + _dot(cnt.astype(BF16), tri)
    r1 = jnp.sum(jnp.where(hot1, before, 0.0), axis=0, keepdims=True)
    r2 = jnp.sum(jnp.where(hot2, before, 0.0), axis=0, keepdims=True)
    base_sc[...] += jnp.sum(cnt, axis=1, keepdims=True)

    irow = lax.broadcasted_iota(jnp.int32, (INFO_ROWS, 1), 0)
    info = jnp.zeros((INFO_ROWS, tm), F32)
    for k, val in enumerate((e1, e2, r1, r2, w1, w2)):
        info = jnp.where(irow == k, val, info)
    return _pack_pairs(h_hi), info


def _out_ln_kernel(o_ref, w_ref, x_ref, gate_ref, g_ref, b_ref, sc_ref, sh_ref, whi_ref, wlo_ref, br_ref, tri_ref,
                   out_ref, hb_ref, info_ref, cnt_ref, base_sc):
    first = (pl.program_id(0) == 0) & (pl.program_id(1) == 0)

    @pl.when(first)
    def _():
        base_sc[...] = jnp.zeros(base_sc.shape, F32)

    ts = o_ref.shape[1]
    rows = min(OUT_LN_ROWS, ts)
    ys = [_dot(o_ref[0, 0:rows, :], w_ref[...])]
    for c in range(ts // rows):
        if c + 1 < ts // rows:
            ys.append(_dot(o_ref[0, (c + 1) * rows:(c + 2) * rows, :], w_ref[...]))
        sl = slice(c * rows, (c + 1) * rows)
        z = ALPHA * x_ref[0, sl, :] + gate_ref[0] * ys[c]
        x1 = _layer_norm(z, g_ref[...], b_ref[...])
        out_ref[0, sl, :] = x1
        packed, info = _route_tokens(x1, sc_ref[0], sh_ref[0], whi_ref[...], wlo_ref[...], br_ref[...],
                                     tri_ref[...], base_sc)
        hb_ref[0, sl, :] = packed
        info_ref[0, :, sl] = info
    cnt_ref[...] = base_sc[...]


def _out_ln(o, w_o, x, gate, g, b, route, ts=1024):
    B, S, D = x.shape
    K = o.shape[-1]
    ts = min(ts, S)
    scale2, shift2, w_grp, b_grp, w_rt, b_rt = route
    E, g0 = MOE_E, MOE_EPG
    w_r = jnp.zeros((ROUTER_ROWS, D), F32).at[:MOE_GROUPS].set(w_grp.T).at[g0:g0 + E].set(w_rt.T)
    w_hi = w_r.astype(BF16)
    w_lo = (w_r - w_hi.astype(F32)).astype(BF16)
    b_r = jnp.zeros((ROUTER_ROWS, 1), F32).at[:MOE_GROUPS, 0].set(b_grp).at[g0:g0 + E, 0].set(b_rt)
    tm = min(OUT_LN_ROWS, ts)
    tri = jnp.asarray(np.triu(np.ones((tm, tm), np.float32), 1), BF16)
    const = lambda shape: pl.BlockSpec(shape, lambda bi, i: (0,) * len(shape))
    per_batch = pl.BlockSpec((1, 1, D), lambda bi, i: (bi, 0, 0))
    return pl.pallas_call(
        _out_ln_kernel,
        grid=(B, S // ts),
        in_specs=[pl.BlockSpec((1, ts, K), lambda bi, i: (bi, i, 0)),
                  const((K, D)),
                  pl.BlockSpec((1, ts, D), lambda bi, i: (bi, i, 0)),
                  per_batch, const((1, D)), const((1, D)),
                  per_batch, per_batch,
                  const((ROUTER_ROWS, D)), const((ROUTER_ROWS, D)), const((ROUTER_ROWS, 1)), const((tm, tm))],
        out_specs=[pl.BlockSpec((1, ts, D), lambda bi, i: (bi, i, 0)),
                   pl.BlockSpec((1, ts, D // 2), lambda bi, i: (bi, i, 0)),
                   pl.BlockSpec((1, INFO_ROWS, ts), lambda bi, i: (bi, 0, i)),
                   const((E, LANES))],
        out_shape=[jax.ShapeDtypeStruct((B, S, D), F32),
                   jax.ShapeDtypeStruct((B, S, D // 2), jnp.uint32),
                   jax.ShapeDtypeStruct((B, INFO_ROWS, S), F32),
                   jax.ShapeDtypeStruct((E, LANES), F32)],
        scratch_shapes=[pltpu.VMEM((E, LANES), F32)],
        compiler_params=_cp(("arbitrary", "arbitrary"), 48),
        name="out_ln_route",
    )(o, w_o.astype(BF16), x, gate, g.reshape(1, D), b.reshape(1, D), scale2, shift2, w_hi, w_lo, b_r, tri)


def _rope_tables(positions, dim):
    inv_freq = 1.0 / (ROPE_THETA ** (jnp.arange(0, dim, 2, dtype=F32) / dim))
    ang = positions.astype(F32)[..., None] * inv_freq
    return jnp.cos(ang), jnp.sin(ang)


def _rot_half_cols(w, half):
    return jnp.concatenate([-w[..., half:], w[..., :half]], axis=-1)


def _mla_front_kernel(x_ref, sc_ref, sh_ref, win_ref, gq_ref, gkv_ref, wqt_ref, wk_ref, wvt_ref, t_ref, tt_ref,
                      qt_ref, k_ref, vt_ref, *, scale):
    hb = (x_ref[0] * (1.0 + sc_ref[0]) + sh_ref[0]).astype(BF16)
    r = _dot(hb, win_ref[...])
    tab = t_ref[0]
    tab_t = tt_ref[0]
    low = lax.broadcasted_iota(jnp.int32, tab.shape, 1) < MLA_DR

    def rms(cx, g):
        return (cx * lax.rsqrt(jnp.mean(cx * cx, axis=-1, keepdims=True) + RMS_EPS) * g).astype(BF16)

    cq = rms(r[:, :MLA_QR], gq_ref[...])
    ckv = rms(r[:, MLA_QR:MLA_QR + MLA_KVR], gkv_ref[...])
    t = r[:, MLA_QR + MLA_KVR:] * tab
    k_rope = jnp.where(low, t + pltpu.roll(t, MLA_DR, axis=1), 0.0).astype(BF16)
    zeros = jnp.zeros((MLA_DR, hb.shape[0]), BF16)
    for h in range(MLA_HEADS):
        qh = _nt_dot(wqt_ref[h], cq)
        tq = qh[MLA_DN:] * tab_t
        qt_ref[0, h, :MLA_DN, :] = (qh[:MLA_DN] * scale).astype(BF16)
        qt_ref[0, h, MLA_DN:MLA_DN + MLA_DR, :] = ((tq[:MLA_DR] + tq[MLA_DR:]) * scale).astype(BF16)
        qt_ref[0, h, MLA_DN + MLA_DR:, :] = zeros
        k_ref[0, h, :, :MLA_DN] = _dot(ckv, wk_ref[h]).astype(BF16)
        k_ref[0, h, :, MLA_DN:] = k_rope
        vt_ref[0, h] = _nt_dot(wvt_ref[h], ckv).astype(BF16)


def _flash_causal_kernel(qt_ref, k_ref, vt_ref, o_ref, *, t):
    dv, s_len = vt_ref.shape[2], vt_ref.shape[3]
    pairs = [(qi, kt) for qi in range(s_len // t) for kt in range(qi + 1)]
    kpos = lax.broadcasted_iota(jnp.int32, (t, 1), 0)
    qpos = lax.broadcasted_iota(jnp.int32, (1, t), 1)
    diag = kpos <= qpos

    def scores(j):
        qi, kt = pairs[j]
        return _dot(k_ref[0, 0, kt * t:(kt + 1) * t, :], qt_ref[0, 0, :, qi * t:(qi + 1) * t])

    def value_matmul(pending, l, acc):
        qi, kt, p, a = pending
        acc = a * acc + _dot(vt_ref[0, 0, :, kt * t:(kt + 1) * t], p)
        if kt == qi:
            o_ref[0, qi * t:(qi + 1) * t, :] = (acc / l).T.astype(o_ref.dtype)
        return acc

    s_cur, pending = scores(0), None
    m = l = acc = None
    for j, (qi, kt) in enumerate(pairs):
        s_next = scores(j + 1) if j + 1 < len(pairs) else None
        if pending is not None:
            acc = value_matmul(pending, l, acc)
        if kt == 0:
            m, l, acc = jnp.full((1, t), NEG, F32), jnp.zeros((1, t), F32), jnp.zeros((dv, t), F32)
        s = jnp.where(diag, s_cur, NEG) if kt == qi else s_cur
        m_new = jnp.maximum(m, jnp.max(s, axis=0, keepdims=True))
        p = jnp.exp2(s - m_new)
        a = jnp.exp2(m - m_new)
        l = a * l + jnp.sum(p, axis=0, keepdims=True)
        m = m_new
        pending = (qi, kt, p.astype(BF16), a)
        s_cur = s_next
    value_matmul(pending, l, acc)


def _mla_layer(x, shift, scale_v, gate, cos, sin, w_in, g_q, w_uq, g_kv, w_ukv, w_o, ln_g, ln_b, route):
    B, S, D = x.shape
    H, dn, dr, dv = MLA_HEADS, MLA_DN, MLA_DR, MLA_DV
    half = dr // 2
    kr = w_in[:, MLA_QR + MLA_KVR:]
    win_p = jnp.concatenate([w_in, _rot_half_cols(kr, half)], axis=1).astype(BF16)
    wq = w_uq.reshape(MLA_QR, H, dn + dr)
    wq = jnp.concatenate([wq, _rot_half_cols(wq[..., dn:], half)], axis=-1)
    wqt = jnp.transpose(wq, (1, 2, 0)).astype(BF16)
    wkv = w_ukv.reshape(MLA_KVR, H, dn + dv)
    wk = jnp.transpose(wkv[..., :dn], (1, 0, 2)).astype(BF16)
    wvt = jnp.transpose(wkv[..., dn:], (1, 2, 0)).astype(BF16)
    tab = jnp.concatenate([cos, cos, sin, sin], axis=-1)
    tab_t = jnp.swapaxes(tab, 1, 2)
    ts = min(512, S)
    nw = win_p.shape[1]
    hd = dn + 2 * dr
    qt, k, vt = pl.pallas_call(
        functools.partial(_mla_front_kernel, scale=LOG2E * (dn + dr) ** -0.5),
        grid=(B, S // ts),
        in_specs=[pl.BlockSpec((1, ts, D), lambda b, i: (b, i, 0)),
                  pl.BlockSpec((1, 1, D), lambda b, i: (b, 0, 0)),
                  pl.BlockSpec((1, 1, D), lambda b, i: (b, 0, 0)),
                  pl.BlockSpec((D, nw), lambda b, i: (0, 0)),
                  pl.BlockSpec((1, MLA_QR), lambda b, i: (0, 0)),
                  pl.BlockSpec((1, MLA_KVR), lambda b, i: (0, 0)),
                  pl.BlockSpec((H, hd, MLA_QR), lambda b, i: (0, 0, 0)),
                  pl.BlockSpec((H, MLA_KVR, dn), lambda b, i: (0, 0, 0)),
                  pl.BlockSpec((H, dv, MLA_KVR), lambda b, i: (0, 0, 0)),
                  pl.BlockSpec((1, ts, LANES), lambda b, i: (b, i, 0)),
                  pl.BlockSpec((1, LANES, ts), lambda b, i: (b, 0, i))],
        out_specs=[pl.BlockSpec((1, H, hd, ts), lambda b, i: (b, 0, 0, i)),
                   pl.BlockSpec((1, H, ts, hd), lambda b, i: (b, 0, i, 0)),
                   pl.BlockSpec((1, H, dv, ts), lambda b, i: (b, 0, 0, i))],
        out_shape=[jax.ShapeDtypeStruct((B, H, hd, S), BF16),
                   jax.ShapeDtypeStruct((B, H, S, hd), BF16),
                   jax.ShapeDtypeStruct((B, H, dv, S), BF16)],
        compiler_params=_cp(("arbitrary", "arbitrary"), 48),
        name="mla_front",
    )(x, scale_v, shift, win_p, g_q.reshape(1, -1), g_kv.reshape(1, -1), wqt, wk, wvt, tab, tab_t)

    t = min(512, S)
    o = pl.pallas_call(
        functools.partial(_flash_causal_kernel, t=t),
        grid=(B, H),
        in_specs=[pl.BlockSpec((1, 1, hd, S), lambda b, h: (b, h, 0, 0)),
                  pl.BlockSpec((1, 1, S, hd), lambda b, h: (b, h, 0, 0)),
                  pl.BlockSpec((1, 1, dv, S), lambda b, h: (b, h, 0, 0))],
        out_specs=pl.BlockSpec((1, S, dv), lambda b, h: (b, 0, h)),
        out_shape=jax.ShapeDtypeStruct((B, S, H * dv), BF16),
        compiler_params=_cp(("arbitrary", "arbitrary"), 48),
        name="mla_attn",
    )(qt, k, vt)
    return _out_ln(o, w_o, x, gate, ln_g, ln_b, route)


def _proj_kernel(*refs, segs, chunk, has_rope):
    x_ref, sc_ref, sh_ref, w_ref, wt_ref = refs[:5]
    n_in = 9 if has_rope else 5
    outs = refs[n_in:]
    hb = (x_ref[0] * (1.0 + sc_ref[0]) + sh_ref[0]).astype(BF16)
    if has_rope:
        cos, sin = refs[5][0], refs[6][0]
        cos_t, sin_t = refs[7][0], refs[8][0]
    half = LANES // 2
    for (c0, width, kind, scale, transposed), o_ref in zip(segs, outs):
        for j in range(0, width, chunk):
            cw = min(chunk, width - j)
            if transposed:
                r = _nt_dot(wt_ref[c0 + j:c0 + j + cw, :], hb)
            else:
                r = _dot(hb, w_ref[:, c0 + j:c0 + j + cw])
            if kind == "rope":
                parts = []
                for t in range(0, cw, LANES):
                    if transposed:
                        u = r[t:t + LANES]
                        parts.append(u * cos_t + pltpu.roll(u, half, axis=0) * sin_t)
                    else:
                        u = r[:, t:t + LANES]
                        parts.append(u * cos + pltpu.roll(u, half, axis=1) * sin)
                r = jnp.concatenate(parts, axis=0 if transposed else 1) if len(parts) > 1 else parts[0]
            elif kind == "sigmoid":
                r = jax.nn.sigmoid(r)
            if scale != 1.0:
                r = r * scale
            if transposed:
                o_ref[0, j:j + cw, :] = r.astype(o_ref.dtype)
            else:
                o_ref[0, :, j:j + cw] = r.astype(o_ref.dtype)


def _mod_proj(x, shift, scale_v, w, wt, segs, out_dtypes, rope=None, ts=512, chunk=256):
    B, S, D = x.shape
    ts = min(ts, S)
    in_specs = [pl.BlockSpec((1, ts, D), lambda b, i: (b, i, 0)),
                pl.BlockSpec((1, 1, D), lambda b, i: (b, 0, 0)),
                pl.BlockSpec((1, 1, D), lambda b, i: (b, 0, 0)),
                pl.BlockSpec(w.shape, lambda b, i: (0, 0)),
                pl.BlockSpec(wt.shape, lambda b, i: (0, 0))]
    args = [x, scale_v, shift, w, wt]
    if rope is not None:
        cos, sin = rope
        in_specs += [pl.BlockSpec((1, ts, LANES), lambda b, i: (b, i, 0))] * 2
        in_specs += [pl.BlockSpec((1, LANES, ts), lambda b, i: (b, 0, i))] * 2
        args += [cos, sin, jnp.swapaxes(cos, 1, 2), jnp.swapaxes(sin, 1, 2)]
    out_specs, out_shape = [], []
    for sg, dt in zip(segs, out_dtypes):
        if sg[4]:
            out_specs.append(pl.BlockSpec((1, sg[1], ts), lambda b, i: (b, 0, i)))
            out_shape.append(jax.ShapeDtypeStruct((B, sg[1], S), dt))
        else:
            out_specs.append(pl.BlockSpec((1, ts, sg[1]), lambda b, i: (b, i, 0)))
            out_shape.append(jax.ShapeDtypeStruct((B, S, sg[1]), dt))
    return pl.pallas_call(
        functools.partial(_proj_kernel, segs=tuple(segs), chunk=chunk, has_rope=rope is not None),
        grid=(B, S // ts),
        in_specs=in_specs,
        out_specs=out_specs,
        out_shape=out_shape,
        compiler_params=_cp(("arbitrary", "arbitrary"), 48),
        name="mod_proj",
    )(*args)


def _nsa_cmp_kernel(kc_ref, vc_ref, pek_ref, pev_ref, wk1_ref, wk2_ref, wv1_ref, wv2t_ref, ko_ref, vto_ref):
    half = NSA_CMP_BLOCK // 2
    nchunk = kc_ref.shape[1]

    def hidden(src, pe_ref, w1_ref, kh):
        a = jnp.zeros((nchunk, w1_ref.shape[2]), F32)
        bm = jnp.zeros((nchunk, w1_ref.shape[2]), F32)
        for l in range(half):
            c0 = (l * NSA_KVH + kh) * NSA_DH
            t = src[0, :, c0:c0 + NSA_DH].astype(F32)
            a = a + _dot((t + pe_ref[l:l + 1, :]).astype(BF16), w1_ref[l])
            bm = bm + _dot((t + pe_ref[half + l:half + l + 1, :]).astype(BF16), w1_ref[half + l])
        hid = a + pltpu.roll(bm, nchunk - 1, axis=0)
        return (hid * jax.nn.sigmoid(hid)).astype(BF16)

    rowi = lax.broadcasted_iota(jnp.int32, (nchunk, 1), 0)
    coli = lax.broadcasted_iota(jnp.int32, (1, nchunk), 1)
    for kh in range(NSA_KVH):
        out = _dot(hidden(kc_ref, pek_ref, wk1_ref, kh), wk2_ref[...])
        ko_ref[0, kh] = jnp.where(rowi < nchunk - 1, out, 0.0).astype(ko_ref.dtype)
        out_t = _nt_dot(wv2t_ref[...], hidden(vc_ref, pev_ref, wv1_ref, kh))
        vto_ref[0, kh] = jnp.where(coli < nchunk - 1, out_t, 0.0).astype(vto_ref.dtype)


def _nsa_attn_kernel(qt_ref, kc_ref, vct_ref, ks_ref, vst_ref, kw_ref, vwt_ref, gt_ref, ovt_ref, e_ref, o_ref,
                     q4_sc, m_sc, l_sc, acc_sc, *, tq, tk, span, n_slc):
    qi = pl.program_id(2)
    G = NSA_G
    for g in range(G):
        q4_sc[:, g * tq:(g + 1) * tq] = qt_ref[0, g * NSA_DH:(g + 1) * NSA_DH, :]
    q4 = q4_sc[...]
    qpos = qi * tq + lax.broadcasted_iota(jnp.int32, (1, tq), 1)

    def tile_g(a):
        return jnp.concatenate([a] * G, axis=1)

    qpos4 = tile_g(qpos)

    kstart = pl.multiple_of(jnp.maximum(qi * tq + tq - span, 0), tq)
    s_win = _dot(kw_ref[0, pl.ds(kstart, span), :], q4)

    n_cmp = kc_ref.shape[2]
    nid = lax.broadcasted_iota(jnp.int32, (n_cmp, 1), 0)
    cmask = (nid * NSA_CMP_STRIDE + (NSA_CMP_BLOCK - 1) <= qpos4) & (nid < n_cmp - 1)
    s = jnp.where(cmask, _dot(kc_ref[0, 0], q4), NEG)
    p = jnp.where(cmask, jnp.exp2(s - jnp.max(s, axis=0, keepdims=True)), 0.0)
    l = jnp.sum(p, axis=0, keepdims=True)
    p_cmp = p / jnp.where(l > 0.0, l, 1.0)
    o_cmp = _dot(vct_ref[0, 0], p_cmp.astype(BF16))

    diff = qpos - (kstart + lax.broadcasted_iota(jnp.int32, (span, 1), 0))
    s_win = s_win + tile_g(jnp.where((diff >= 0) & (diff < NSA_WINDOW), 0.0, NEG))
    p = jnp.exp2(s_win - jnp.max(s_win, axis=0, keepdims=True))
    o_win = _dot(vwt_ref[0, :, pl.ds(kstart, span)], p.astype(BF16)) / jnp.sum(p, axis=0, keepdims=True)

    psum = p_cmp[:, 0:tq]
    for g in range(1, G):
        psum = psum + p_cmp[:, g * tq:(g + 1) * tq]
    p_hi, p_lo = _split_bf16(psum)
    imp = _dot(ovt_ref[...], p_hi) + _dot(ovt_ref[...], p_lo)
    nb = ovt_ref.shape[0]
    blk = lax.broadcasted_iota(jnp.int32, (nb, 1), 0)
    cur = jnp.right_shift(qpos, SLC_SHIFT)
    forced = (blk == 0) | (blk == cur) | (blk == cur - 1)
    valid = (blk * NSA_SLC_BLOCK <= qpos) & (blk < n_slc)
    work = jnp.where(valid, imp + jnp.where(forced, NSA_BONUS, 0.0), -jnp.inf)
    beaten = jnp.zeros((nb, tq), F32)
    for j in range(n_slc):
        wj = work[j:j + 1, :]
        beaten = beaten + jnp.where((wj > work) | ((wj == work) & (blk > j)), 1.0, 0.0)
    sel = jnp.where(beaten < float(min(NSA_TOPK, n_slc)), 1.0, 0.0).astype(BF16)

    m_sc[...] = jnp.full(m_sc.shape, NEG, F32)
    l_sc[...] = jnp.zeros(l_sc.shape, F32)
    acc_sc[...] = jnp.zeros(acc_sc.shape, F32)
    n_kt = ((qi + 1) * tq + tk - 1) // tk

    def slc_tiles(n):
        def scores(kt):
            return _dot(ks_ref[0, kt * tk:(kt + 1) * tk, :], q4)

        def value_matmul(kt, p, a):
            acc_sc[...] = a * acc_sc[...] + _dot(vst_ref[0, :, kt * tk:(kt + 1) * tk], p)

        s, pending = scores(0), None
        chosen = _dot(e_ref[0:n * tk, :], sel)
        kpos = lax.broadcasted_iota(jnp.int32, (n * tk, 1), 0)
        bias_all = jnp.where((chosen > 0.5) & (kpos <= qpos), 0.0, NEG)
        for kt in range(n):
            s_next = scores(kt + 1) if kt + 1 < n else None
            if pending is not None:
                value_matmul(*pending)
            s = s + tile_g(bias_all[kt * tk:(kt + 1) * tk])
            m_prev = m_sc[...]
            m_new = jnp.maximum(m_prev, jnp.max(s, axis=0, keepdims=True))
            p = jnp.exp2(s - m_new)
            a = jnp.exp2(m_prev - m_new)
            l_sc[...] = a * l_sc[...] + jnp.sum(p, axis=0, keepdims=True)
            m_sc[...] = m_new
            pending = (kt, p.astype(BF16), a)
            s = s_next
        value_matmul(*pending)

    for n in range(1, (ks_ref.shape[1] + tk - 1) // tk + 1):
        pl.when(n_kt == n)(functools.partial(slc_tiles, n))
    o_slc = acc_sc[...] / l_sc[...]

    gt = gt_ref[0]
    for g in range(G):
        cols = slice(g * tq, (g + 1) * tq)
        o = (gt[3 * g:3 * g + 1] * o_cmp[:, cols] + gt[3 * g + 1:3 * g + 2] * o_slc[:, cols]
             + gt[3 * g + 2:3 * g + 3] * o_win[:, cols])
        o_ref[0, :, g * NSA_DH:(g + 1) * NSA_DH] = o.T.astype(o_ref.dtype)


def _nsa_layer(x, shift, scale_v, gate, cos, sin, w_in, pe_k, pe_v, w_ck1, w_ck2, w_cv1, w_cv2, w_o,
               ln_g, ln_b, route):
    B, S, D = x.shape
    H, KH, Dh, G = NSA_HEADS, NSA_KVH, NSA_DH, NSA_G
    kvw = KH * Dh
    cuts = [int(v) for v in np.cumsum([H * Dh] + [kvw] * 6)]
    wq, wkc, wvc, wks, wvs, wkw, wvw, wgl = jnp.split(w_in, cuts, axis=1)
    wgl = jnp.pad(wgl.reshape(D, KH, G * 3), ((0, 0), (0, 0), (0, GATE_ROWS - G * 3))).reshape(D, KH * GATE_ROWS)
    w = jnp.concatenate([wkc, wks, wkw, wvc], axis=1).astype(BF16)
    wt = jnp.concatenate([wq, wvs, wvw, wgl], axis=1).T.astype(BF16)
    segs = [(0, H * Dh, "rope", LOG2E * Dh ** -0.5, True),
            (0, kvw, "rope", 1.0, False), (kvw, kvw, "rope", 1.0, False), (2 * kvw, kvw, "rope", 1.0, False),
            (3 * kvw, kvw, "plain", 1.0, False),
            (H * Dh, kvw, "plain", 1.0, True), (H * Dh + kvw, kvw, "plain", 1.0, True),
            (H * Dh + 2 * kvw, KH * GATE_ROWS, "sigmoid", 1.0, True)]
    rope = (jnp.concatenate([cos, cos], axis=-1), jnp.concatenate([-sin, sin], axis=-1))
    qt, kc, ks, kw, vc, vst, vwt, gates_t = _mod_proj(
        x, shift, scale_v, w, wt, segs, [BF16] * 7 + [F32], rope=rope)

    st = NSA_CMP_STRIDE
    nchunk = S // st
    kc2 = kc.reshape(B, nchunk, st * kvw)
    vc2 = vc.reshape(B, nchunk, st * kvw)
    hid = w_ck1.shape[-1]
    full = lambda shape: pl.BlockSpec(shape, lambda b: (0,) * len(shape))
    k_cmp, v_cmp_t = pl.pallas_call(
        _nsa_cmp_kernel,
        grid=(B,),
        in_specs=[pl.BlockSpec((1, nchunk, st * kvw), lambda b: (b, 0, 0)),
                  pl.BlockSpec((1, nchunk, st * kvw), lambda b: (b, 0, 0)),
                  full((NSA_CMP_BLOCK, Dh)), full((NSA_CMP_BLOCK, Dh)),
                  full((NSA_CMP_BLOCK, Dh, hid)), full((hid, Dh)),
                  full((NSA_CMP_BLOCK, Dh, hid)), full((Dh, hid))],
        out_specs=[pl.BlockSpec((1, KH, nchunk, Dh), lambda b: (b, 0, 0, 0)),
                   pl.BlockSpec((1, KH, Dh, nchunk), lambda b: (b, 0, 0, 0))],
        out_shape=[jax.ShapeDtypeStruct((B, KH, nchunk, Dh), BF16),
                   jax.ShapeDtypeStruct((B, KH, Dh, nchunk), BF16)],
        compiler_params=_cp(("arbitrary",), 32),
        name="nsa_cmp",
    )(kc2, vc2, pe_k, pe_v, w_ck1.astype(BF16), w_ck2.astype(BF16), w_cv1.astype(BF16), w_cv2.T.astype(BF16))

    n_slc = S // NSA_SLC_BLOCK
    nb = 32
    assert n_slc <= nb and nchunk % 8 == 0
    cmp_start = np.arange(nchunk) * st
    slc_start = np.arange(n_slc) * NSA_SLC_BLOCK
    ov = np.clip(np.minimum(cmp_start[:, None] + NSA_CMP_BLOCK, slc_start[None, :] + NSA_SLC_BLOCK)
                 - np.maximum(cmp_start[:, None], slc_start[None, :]), 0, None).astype(np.float32) / NSA_CMP_BLOCK
    ov[nchunk - 1:] = 0.0
    ov_t = np.zeros((nb, nchunk), np.float32)
    ov_t[:n_slc] = ov.T
    key_blk = np.zeros((S, nb), np.float32)
    key_blk[np.arange(S), np.arange(S) // NSA_SLC_BLOCK] = 1.0

    tq = min(256, S)
    tk = min(256, S)
    span = min(NSA_WINDOW + tq, S)
    k_spec = pl.BlockSpec((1, S, Dh), lambda b, kh, i: (b, 0, kh))
    vt_spec = pl.BlockSpec((1, Dh, S), lambda b, kh, i: (b, kh, 0))
    o = pl.pallas_call(
        functools.partial(_nsa_attn_kernel, tq=tq, tk=tk, span=span, n_slc=n_slc),
        grid=(B, KH, S // tq),
        in_specs=[pl.BlockSpec((1, G * Dh, tq), lambda b, kh, i: (b, kh, i)),
                  pl.BlockSpec((1, 1, nchunk, Dh), lambda b, kh, i: (b, kh, 0, 0)),
                  pl.BlockSpec((1, 1, Dh, nchunk), lambda b, kh, i: (b, kh, 0, 0)),
                  k_spec, vt_spec, k_spec, vt_spec,
                  pl.BlockSpec((1, GATE_ROWS, tq), lambda b, kh, i: (b, kh, i)),
                  pl.BlockSpec((nb, nchunk), lambda b, kh, i: (0, 0)),
                  pl.BlockSpec((S, nb), lambda b, kh, i: (0, 0))],
        out_specs=pl.BlockSpec((1, tq, G * Dh), lambda b, kh, i: (b, i, kh)),
        out_shape=jax.ShapeDtypeStruct((B, S, H * Dh), BF16),
        scratch_shapes=[pltpu.VMEM((Dh, G * tq), BF16), pltpu.VMEM((1, G * tq), F32),
                        pltpu.VMEM((1, G * tq), F32), pltpu.VMEM((Dh, G * tq), F32)],
        compiler_params=_cp(("arbitrary", "arbitrary", "arbitrary"), 32),
        name="nsa_attn",
    )(qt, k_cmp, v_cmp_t, ks, vst, kw, vwt, gates_t, jnp.asarray(ov_t, BF16), jnp.asarray(key_blk, BF16))
    return _out_ln(o, w_o, x, gate, ln_g, ln_b, route)


def _sb_attn_kernel(qt_ref, k_ref, vt_ref, u_ref, o_ref, r_sc, acc_sc, *, tq, tk):
    d, s_len = vt_ref.shape[1], vt_ref.shape[2]
    u = u_ref[...]
    nd = tq // tk

    def weights(z, strict):
        lk = -(jnp.maximum(z, 0.0) + jnp.log2(1.0 + jnp.exp2(-jnp.abs(z))))
        if strict is not None:
            lk = jnp.where(strict, lk, 0.0)
        between = r_sc[...] + _dot(u, lk.astype(BF16))
        a = jnp.exp2(lk + z + between)
        if strict is not None:
            a = jnp.where(strict, a, 0.0)
        r_sc[...] += jnp.sum(lk, axis=0, keepdims=True)
        return a.astype(BF16)

    for qi in range(s_len // tq):
        qt = qt_ref[0, :, qi * tq:(qi + 1) * tq]
        r_sc[...] = jnp.zeros(r_sc.shape, F32)
        acc_sc[...] = jnp.zeros(acc_sc.shape, F32)
        last = (qi + 1) * nd - 1
        tiles = [last - i for i in range(nd)] + ([qi * nd - 1] if qi > 0 else [])
        qpos = qi * tq + lax.broadcasted_iota(jnp.int32, (1, tq), 1)

        def scores(kt):
            return _dot(k_ref[0, kt * tk:(kt + 1) * tk, :], qt)

        z, pending = scores(tiles[0]), None
        for i, kt in enumerate(tiles):
            nxt = tiles[i + 1] if i + 1 < len(tiles) else max(qi * nd - 2, 0)
            z_next = scores(nxt)
            if pending is not None:
                acc_sc[...] += _dot(vt_ref[0, :, pending[0] * tk:(pending[0] + 1) * tk], pending[1])
            strict = (kt * tk + lax.broadcasted_iota(jnp.int32, (tk, 1), 0) < qpos) if kt >= qi * nd else None
            pending = (kt, weights(z, strict))
            z = z_next
        acc_sc[...] += _dot(vt_ref[0, :, pending[0] * tk:(pending[0] + 1) * tk], pending[1])

        if qi * nd - 2 >= 0:
            def scores_dyn(kt):
                return _dot(k_ref[0, pl.ds(pl.multiple_of(kt * tk, tk), tk), :], qt)

            def cond(carry):
                kt, _ = carry
                return (kt >= 0) & (jnp.max(r_sc[...]) > SB_EXIT_LOG2)

            def body(carry):
                kt, z = carry
                z_next = scores_dyn(jnp.maximum(kt - 1, 0))
                a = weights(z, None)
                acc_sc[...] += _dot(vt_ref[0, :, pl.ds(pl.multiple_of(kt * tk, tk), tk)], a)
                return kt - 1, z_next

            lax.while_loop(cond, body, (qi * nd - 2, z))
        o_ref[0, qi * tq:(qi + 1) * tq, :] = acc_sc[...].T.astype(o_ref.dtype)


def _sb_layer(x, shift, scale_v, gate, w_qkv, w_o, ln_g, ln_b, route):
    B, S, D = x.shape
    H, d = SB_HEADS, SB_DH
    hw = H * d
    w = w_qkv[:, hw:2 * hw].astype(BF16)
    wt = jnp.concatenate([w_qkv[:, :hw], w_qkv[:, 2 * hw:]], axis=1).T.astype(BF16)
    segs = [(0, hw, "plain", LOG2E * d ** -0.5, True), (0, hw, "plain", 1.0, False), (hw, hw, "plain", 1.0, True)]
    qt, k, vt = _mod_proj(x, shift, scale_v, w, wt, segs, [BF16] * 3)
    tq = min(512, S)
    tk = min(256, S)
    tri = np.triu(np.ones((tk, tk), np.float32), 1)
    o = pl.pallas_call(
        functools.partial(_sb_attn_kernel, tq=tq, tk=tk),
        grid=(B, H),
        in_specs=[pl.BlockSpec((1, d, S), lambda b, h: (b, h, 0)),
                  pl.BlockSpec((1, S, d), lambda b, h: (b, 0, h)),
                  pl.BlockSpec((1, d, S), lambda b, h: (b, h, 0)),
                  pl.BlockSpec((tk, tk), lambda b, h: (0, 0))],
        out_specs=pl.BlockSpec((1, S, d), lambda b, h: (b, 0, h)),
        out_shape=jax.ShapeDtypeStruct((B, S, hw), BF16),
        scratch_shapes=[pltpu.VMEM((1, tq), F32), pltpu.VMEM((d, tq), F32)],
        compiler_params=_cp(("arbitrary", "arbitrary"), 32),
        name="sb_attn",
    )(qt, k, vt, jnp.asarray(tri, BF16))
    return _out_ln(o, w_o, x, gate, ln_g, ln_b, route)


def _ffn_kernel(be_ref, seg_ref, nv_ref, nu_ref, x_ref, wg_ref, wu_ref, wd_ref, y_ref, wg_sc, wu_sc, wd_sc, act_sc):
    i = pl.program_id(0)
    n_blk = be_ref.shape[0]

    def stage(blk, src_ref, dst_sc):
        r = jnp.clip(blk, 0, n_blk - 1)
        first = (blk >= 0) & (blk < nu_ref[0]) & ((r == 0) | (be_ref[r] != be_ref[jnp.maximum(r - 1, 0)]))

        @pl.when(first)
        def _():
            dst_sc[seg_ref[r] % FFN_SLOTS] = src_ref[0, 0].astype(BF16)

    stage(i, wg_ref, wg_sc)
    stage(i - 1, wu_ref, wu_sc)
    stage(i - 2, wd_ref, wd_sc)
    blk = i - FFN_LEAD
    done = blk - 1
    has_new = (blk >= 0) & (blk < nu_ref[0])
    has_done = (done >= 0) & (done < nu_ref[0])
    slot_new = seg_ref[jnp.clip(blk, 0, n_blk - 1)] % FFN_SLOTS
    slot_done = seg_ref[jnp.clip(done, 0, n_blk - 1)] % FFN_SLOTS

    def rows_in():
        live = lax.broadcasted_iota(jnp.int32, (x_ref.shape[0], 1), 0) < nv_ref[jnp.clip(blk, 0, n_blk - 1)]
        return jnp.where(live, _unpack_pairs(x_ref[...]), 0.0).astype(BF16)

    act_in, act_out = (i + 1) % 2, i % 2

    @pl.when(has_new & has_done)
    def _():
        xb = rows_in()
        g = _dot(xb, wg_sc[slot_new])
        u = _dot(xb, wu_sc[slot_new])
        y_ref[...] = _pack_pairs(_dot(act_sc[act_in], wd_sc[slot_done]))
        act_sc[act_out] = (g * jax.nn.sigmoid(g) * u).astype(BF16)

    @pl.when(has_new & jnp.logical_not(has_done))
    def _():
        xb = rows_in()
        g = _dot(xb, wg_sc[slot_new])
        u = _dot(xb, wu_sc[slot_new])
        act_sc[act_out] = (g * jax.nn.sigmoid(g) * u).astype(BF16)

    @pl.when(has_done & jnp.logical_not(has_new))
    def _():
        y_ref[...] = _pack_pairs(_dot(act_sc[act_in], wd_sc[slot_done]))

    @pl.when((done >= nu_ref[0]) & (done < n_blk))
    def _():
        y_ref[...] = jnp.zeros(y_ref.shape, y_ref.dtype)


def _sc_gather(table, idx):
    n, d = idx.shape[0], table.shape[1]
    workers = SC_CORES * SC_SUBCORES
    per_w = n // workers
    assert per_w * workers == n and per_w % (2 * SC_CHUNK) == 0
    mesh = plsc.VectorSubcoreMesh(core_axis_name="c", subcore_axis_name="s")

    @functools.partial(
        pl.kernel, mesh=mesh,
        out_type=jax.ShapeDtypeStruct((n, d), table.dtype),
        scratch_types=[pltpu.VMEM((SC_CHUNK,), jnp.int32), pltpu.VMEM((SC_CHUNK,), jnp.int32),
                       pltpu.VMEM((SC_CHUNK, d), table.dtype), pltpu.VMEM((SC_CHUNK, d), table.dtype)]
                      + [pltpu.SemaphoreType.DMA] * 4,
        name="sc_gather")
    def gather(table_hbm, idx_hbm, out_hbm, idx_a, idx_b, rows_a, rows_b, sem_a, sem_b, sem_c, sem_d):
        wid = lax.axis_index("s") * SC_CORES + lax.axis_index("c")

        @pl.loop(0, per_w // (2 * SC_CHUNK))
        def _(j):
            base_a = pl.multiple_of(wid * per_w + j * (2 * SC_CHUNK), SC_CHUNK)
            base_b = pl.multiple_of(base_a + SC_CHUNK, SC_CHUNK)
            pltpu.sync_copy(idx_hbm.at[pl.ds(base_a, SC_CHUNK)], idx_a)
            pltpu.sync_copy(idx_hbm.at[pl.ds(base_b, SC_CHUNK)], idx_b)
            get_a = pltpu.async_copy(table_hbm.at[idx_a], rows_a, sem_a)
            get_b = pltpu.async_copy(table_hbm.at[idx_b], rows_b, sem_b)
            get_a.wait()
            put_a = pltpu.async_copy(rows_a, out_hbm.at[pl.ds(base_a, SC_CHUNK)], sem_c)
            get_b.wait()
            put_b = pltpu.async_copy(rows_b, out_hbm.at[pl.ds(base_b, SC_CHUNK)], sem_d)
            put_a.wait()
            put_b.wait()

    return gather(table, idx)


def _sc_dispatch(src, dest0, dest1, n_out):
    n, d = src.shape
    workers = SC_CORES * SC_SUBCORES
    per_w = n // workers
    assert per_w * workers == n and per_w % SC_CHUNK == 0
    mesh = plsc.VectorSubcoreMesh(core_axis_name="c", subcore_axis_name="s")

    @functools.partial(
        pl.kernel, mesh=mesh,
        out_type=jax.ShapeDtypeStruct((n_out, d), src.dtype),
        scratch_types=[pltpu.VMEM((SC_CHUNK,), jnp.int32),
                       pltpu.VMEM((SC_CHUNK,), jnp.int32),
                       pltpu.VMEM((SC_CHUNK, d), src.dtype)]
                      + [pltpu.SemaphoreType.DMA] * 3,
        name="sc_dispatch")
    def dispatch(src_hbm, d0_hbm, d1_hbm, out_hbm, i0_v, i1_v, rows_v, sem_a, sem_b, sem_c):
        wid = lax.axis_index("s") * SC_CORES + lax.axis_index("c")

        @pl.loop(0, per_w // SC_CHUNK)
        def _(j):
            base = pl.multiple_of(wid * per_w + j * SC_CHUNK, SC_CHUNK)
            get_0 = pltpu.async_copy(d0_hbm.at[pl.ds(base, SC_CHUNK)], i0_v, sem_a)
            get_1 = pltpu.async_copy(d1_hbm.at[pl.ds(base, SC_CHUNK)], i1_v, sem_b)
            get_r = pltpu.async_copy(src_hbm.at[pl.ds(base, SC_CHUNK)], rows_v, sem_c)
            get_0.wait()
            get_1.wait()
            get_r.wait()
            put_0 = pltpu.async_copy(rows_v, out_hbm.at[i0_v], sem_a)
            put_1 = pltpu.async_copy(rows_v, out_hbm.at[i1_v], sem_b)
            put_0.wait()
            put_1.wait()

    return dispatch(src, dest0, dest1)


def _moe_out_kernel(x_ref, y1_ref, y2_ref, info_ref, gate_ref, g_ref, b_ref, o_ref):
    info = info_ref[0]
    ts = info.shape[1]
    info_tok = jnp.concatenate([info, jnp.zeros((LANES - INFO_ROWS, ts), F32)], axis=0).T
    y = info_tok[:, 4:5] * _unpack_pairs(y1_ref[0, 0]) + info_tok[:, 5:6] * _unpack_pairs(y2_ref[0, 0])
    z = ALPHA * x_ref[0] + gate_ref[0] * y
    o_ref[0] = _layer_norm(z, g_ref[...], b_ref[...])


def _moe_out(x, y12, info, gate, g, b, ts=512):
    B, S, D = x.shape
    ts = min(ts, S)
    tok = pl.BlockSpec((1, ts, D), lambda bi, i: (bi, i, 0))
    return pl.pallas_call(
        _moe_out_kernel,
        grid=(B, S // ts),
        in_specs=[tok,
                  pl.BlockSpec((1, 1, ts, D // 2), lambda bi, i: (0, bi, i, 0)),
                  pl.BlockSpec((1, 1, ts, D // 2), lambda bi, i: (1, bi, i, 0)),
                  pl.BlockSpec((1, INFO_ROWS, ts), lambda bi, i: (bi, 0, i)),
                  pl.BlockSpec((1, 1, D), lambda bi, i: (bi, 0, 0)),
                  pl.BlockSpec((1, D), lambda bi, i: (0, 0)),
                  pl.BlockSpec((1, D), lambda bi, i: (0, 0))],
        out_specs=tok,
        out_shape=jax.ShapeDtypeStruct((B, S, D), F32),
        compiler_params=_cp(("arbitrary", "arbitrary"), 40),
        name="moe_out",
    )(x, y12, y12, info, gate, g.reshape(1, D), b.reshape(1, D))


def _moe_layer(x, routed, gate, layer, w_gate, w_up, w_down, ln_g, ln_b):
    B, S, D = x.shape
    T = B * S
    E, R, F = MOE_E, MOE_ROWS, w_gate.shape[-1]
    hb, info, cnt = routed

    counts = cnt[:, 0].astype(jnp.int32)
    padded = (counts + R - 1) // R * R
    ends = jnp.cumsum(padded)
    starts = ends - padded
    expert_ids = jnp.arange(E, dtype=jnp.int32)

    def per_expert(table, expert):
        return jnp.sum(jnp.where(expert[..., None] == expert_ids, table, 0), axis=-1)

    def row_of(expert, rank):
        return (per_expert(starts, expert.astype(jnp.int32)) + rank.astype(jnp.int32)).reshape(T)

    dest0 = row_of(info[:, 0, :], info[:, 2, :])
    dest1 = row_of(info[:, 1, :], info[:, 3, :])
    n_blk = (T * 2) // R + E
    P = n_blk * R
    blk_exp = jnp.minimum(jnp.sum(jnp.arange(n_blk)[:, None] * R >= ends[None, :], axis=1), E - 1).astype(jnp.int32)
    n_used = (ends[-1] // R).astype(jnp.int32).reshape(1)
    x_rows = _sc_dispatch(hb.reshape(T, D // 2), dest0, dest1, P)

    seg = jnp.concatenate([jnp.zeros((1,), jnp.int32),
                           jnp.cumsum((blk_exp[1:] != blk_exp[:-1]).astype(jnp.int32))])

    live = jnp.clip(per_expert(counts, blk_exp) + per_expert(starts, blk_exp) - jnp.arange(n_blk, dtype=jnp.int32) * R,
                    0, R).astype(jnp.int32)

    def blk_at(offset):
        return lambda i: jnp.clip(i - offset, 0, n_blk - 1)

    y_rows = pl.pallas_call(
        _ffn_kernel,
        grid_spec=pltpu.PrefetchScalarGridSpec(
            num_scalar_prefetch=4,
            grid=(n_blk + FFN_LEAD + 1,),
            in_specs=[pl.BlockSpec((R, D // 2), lambda i, be, sg, nv, nu: (blk_at(FFN_LEAD)(i), 0)),
                      pl.BlockSpec((1, 1, D, F), lambda i, be, sg, nv, nu: (layer, be[blk_at(0)(i)], 0, 0)),
                      pl.BlockSpec((1, 1, D, F), lambda i, be, sg, nv, nu: (layer, be[blk_at(1)(i)], 0, 0)),
                      pl.BlockSpec((1, 1, F, D), lambda i, be, sg, nv, nu: (layer, be[blk_at(2)(i)], 0, 0))],
            out_specs=pl.BlockSpec((R, D // 2), lambda i, be, sg, nv, nu: (blk_at(FFN_LEAD + 1)(i), 0)),
            scratch_shapes=[pltpu.VMEM((FFN_SLOTS, D, F), BF16), pltpu.VMEM((FFN_SLOTS, D, F), BF16),
                            pltpu.VMEM((FFN_SLOTS, F, D), BF16), pltpu.VMEM((2, R, F), BF16)]),
        out_shape=jax.ShapeDtypeStruct((P, D // 2), jnp.uint32),
        compiler_params=_cp(("arbitrary",), 48),
        name="moe_ffn",
    )(blk_exp, seg, live, n_used, x_rows, w_gate, w_up, w_down)

    y12 = _sc_gather(y_rows, jnp.concatenate([dest0, dest1])).reshape(2, B, S, D // 2)
    return _moe_out(x, y12, info, gate, ln_g, ln_b)


def kernel(x, c, positions, ada_w, ada_b, ln_g, ln_b, mla_w_in, mla_g_q, mla_w_uq, mla_g_kv, mla_w_ukv, mla_w_o, nsa_w_in, nsa_pe_k, nsa_pe_v, nsa_w_ck1, nsa_w_ck2, nsa_w_cv1, nsa_w_cv2, nsa_w_o, sb_w_qkv, sb_w_o, moe_w_grp, moe_b_grp, moe_w_rt, moe_b_rt, moe_w_gate, moe_w_up, moe_w_down):
    D = x.shape[-1]
    depth = ada_w.shape[0]
    cos_mla, sin_mla = _rope_tables(positions, MLA_DR)
    cos_nsa, sin_nsa = _rope_tables(positions, NSA_DH)
    mods = _ada_all(c, ada_w, ada_b)

    def mod(i, sub):
        m = mods[2 * i + sub][:, None, :]
        return m[..., :D], m[..., D:2 * D], m[..., 2 * D:]

    for i in range(depth):
        kind, j = i % N_MIXERS, i // N_MIXERS
        shift, scale_v, gate = mod(i, 0)
        shift2, scale2, gate2 = mod(i, 1)
        route = (scale2, shift2, moe_w_grp[i], moe_b_grp[i], moe_w_rt[i], moe_b_rt[i])
        if kind == 0:
            x, *routed = _mla_layer(x, shift, scale_v, gate, cos_mla, sin_mla, mla_w_in[j], mla_g_q[j], mla_w_uq[j],
                                    mla_g_kv[j], mla_w_ukv[j], mla_w_o[j], ln_g[i, 0], ln_b[i, 0], route)
        elif kind == 1:
            x, *routed = _nsa_layer(x, shift, scale_v, gate, cos_nsa, sin_nsa, nsa_w_in[j], nsa_pe_k[j], nsa_pe_v[j],
                                    nsa_w_ck1[j], nsa_w_ck2[j], nsa_w_cv1[j], nsa_w_cv2[j], nsa_w_o[j],
                                    ln_g[i, 0], ln_b[i, 0], route)
        else:
            x, *routed = _sb_layer(x, shift, scale_v, gate, sb_w_qkv[j], sb_w_o[j], ln_g[i, 0], ln_b[i, 0], route)
        x = _moe_layer(x, routed, gate2, i, moe_w_gate, moe_w_up, moe_w_down, ln_g[i, 1], ln_b[i, 1])
    return x
```

```python
import functools

import numpy as np
import jax
import jax.numpy as jnp
from jax import lax
from jax.experimental import pallas as pl
from jax.experimental.pallas import tpu as pltpu
from jax.experimental.pallas import tpu_sc as plsc

BF16 = jnp.bfloat16
F32 = jnp.float32

DEPTH = 4
N_MIXERS = 3
ROPE_THETA = 10000.0
LN_EPS = 1e-5
RMS_EPS = 1e-6
NEG = -1e30
LOG2E = 1.4426950408889634
ALPHA = (2 * DEPTH) ** 0.25

MLA_HEADS, MLA_DN, MLA_DR, MLA_DV = 8, 128, 64, 128
MLA_QR, MLA_KVR = 256, 256

NSA_HEADS, NSA_KVH, NSA_DH = 8, 2, 128
NSA_G = NSA_HEADS // NSA_KVH
NSA_CMP_BLOCK, NSA_CMP_STRIDE = 32, 16
NSA_SLC_BLOCK, NSA_TOPK = 64, 8
SLC_SHIFT = 6
NSA_WINDOW = 512
NSA_BONUS = 1e3
GATE_ROWS = 16

SB_HEADS, SB_DH = 8, 128
SB_EXIT_LOG2 = -160.0

MOE_GROUPS, MOE_EPG = 4, 8
EPG_SHIFT = 3
MOE_E = MOE_GROUPS * MOE_EPG
MOE_ROWS = 256
FFN_LEAD = 3
FFN_SLOTS = FFN_LEAD + 2
OUT_LN_ROWS = 256
MOE_OUT_CHUNKS = 4
ROUTER_ROWS = 64
INFO_ROWS = 8

SC_CORES, SC_SUBCORES = 2, 16
SC_CHUNK = 64

LANES = 128
VMEM_BYTES_V7X = 64 * 1024 * 1024


def _cp(sem, vmem_mb):
    assert vmem_mb * 2**20 < VMEM_BYTES_V7X
    return pltpu.CompilerParams(dimension_semantics=sem, vmem_limit_bytes=vmem_mb * 2**20)


def _nt_dot(a, b):
    return lax.dot_general(a, b, (((1,), (1,)), ((), ())), preferred_element_type=F32)


def _dot(a, b):
    return jnp.dot(a, b, preferred_element_type=F32)


def _split_bf16(a):
    hi = a.astype(BF16)
    lo = (a - hi.astype(F32)).astype(BF16)
    return hi, lo


def _pack_pairs(a):
    n = a.shape[1] // 2
    hi = pltpu.bitcast(a[:, :n].astype(BF16).astype(F32), jnp.uint32)
    lo = pltpu.bitcast(a[:, n:].astype(BF16).astype(F32), jnp.uint32)
    return hi | (lo >> 16)


def _unpack_pairs(u):
    hi = pltpu.bitcast(u & jnp.uint32(0xFFFF0000), F32)
    lo = pltpu.bitcast(u << 16, F32)
    return jnp.concatenate([hi, lo], axis=1)


def _layer_norm(z, g, b):
    mu = jnp.mean(z, axis=-1, keepdims=True)
    d = z - mu
    var = jnp.mean(d * d, axis=-1, keepdims=True)
    return d * lax.rsqrt(var + LN_EPS) * g + b


def _ada_kernel(c_ref, w_ref, b_ref, o_ref):
    c = c_ref[...]
    s_hi, s_lo = _split_bf16(c * jax.nn.sigmoid(c))
    w_hi, w_lo = _split_bf16(w_ref[0])
    o_ref[0] = _dot(s_hi, w_hi) + _dot(s_lo, w_hi) + _dot(s_hi, w_lo) + b_ref[0]


def _ada_all(c, ada_w, ada_b):
    L, two, D, D3 = ada_w.shape
    B = c.shape[0]
    n = L * two
    tn = 1024
    return pl.pallas_call(
        _ada_kernel,
        grid=(n, D3 // tn),
        in_specs=[pl.BlockSpec((B, D), lambda i, j: (0, 0)),
                  pl.BlockSpec((1, D, tn), lambda i, j: (i, 0, j)),
                  pl.BlockSpec((1, 1, tn), lambda i, j: (i, 0, j))],
        out_specs=pl.BlockSpec((1, B, tn), lambda i, j: (i, 0, j)),
        out_shape=jax.ShapeDtypeStruct((n, B, D3), F32),
        compiler_params=_cp(("arbitrary", "arbitrary"), 32),
        name="ada",
    )(c, ada_w.reshape(n, D, D3), ada_b.reshape(n, 1, D3))


def _route_tokens(x1, sc, sh, whi, wlo, bcol, tri, base_sc):
    h = x1 * (1.0 + sc) + sh
    h_hi, h_lo = _split_bf16(h)
    logit = _nt_dot(whi, h_hi) + _nt_dot(whi, h_lo) + _nt_dot(wlo, h_hi) + bcol
    tm = logit.shape[1]
    epg = MOE_EPG
    row = lax.broadcasted_iota(jnp.int32, (epg, 1), 0)
    row_f = row.astype(F32)

    def first_max(vals):
        mx = jnp.max(vals, axis=0, keepdims=True)
        idx = jnp.min(jnp.where(vals == mx, row_f, float(epg)), axis=0, keepdims=True)
        return mx, idx

    gl = jnp.where(row < MOE_GROUPS, logit[0:epg], -jnp.inf)
    gmax, g_idx = first_max(gl)
    g_w = 1.0 / jnp.sum(jnp.exp(gl - gmax), axis=0, keepdims=True)
    el = logit[epg:2 * epg]
    for g in range(1, MOE_GROUPS):
        el = jnp.where(g_idx == float(g), logit[epg * (g + 1):epg * (g + 2)], el)
    ep = jnp.exp(el - jnp.max(el, axis=0, keepdims=True))
    prob = ep / jnp.sum(ep, axis=0, keepdims=True)
    p1, i1 = first_max(prob)
    p2, i2 = first_max(jnp.where(row_f == i1, -1.0, prob))
    e1 = g_idx * float(epg) + i1
    e2 = g_idx * float(epg) + i2
    w1 = g_w * p1 / (p1 + p2)
    w2 = g_w * p2 / (p1 + p2)

    erow = lax.broadcasted_iota(jnp.int32, (MOE_E, 1), 0).astype(F32)
    hot1 = erow == e1
    hot2 = erow == e2
    cnt = jnp.where(hot1 | hot2, 1.0, 0.0)
    before = base_sc[:, 0:1] + _dot(cnt.astype(BF16), tri)
    r1 = jnp.sum(jnp.where(hot1, before, 0.0), axis=0, keepdims=True)
    r2 = jnp.sum(jnp.where(hot2, before, 0.0), axis=0, keepdims=True)
    base_sc[...] += jnp.sum(cnt, axis=1, keepdims=True)

    irow = lax.broadcasted_iota(jnp.int32, (INFO_ROWS, 1), 0)
    info = jnp.zeros((INFO_ROWS, tm), F32)
    for k, val in enumerate((e1, e2, r1, r2, w1, w2)):
        info = jnp.where(irow == k, val, info)
    return _pack_pairs(h_hi), info


def _out_ln_kernel(o_ref, w_ref, x_ref, gate_ref, g_ref, b_ref, sc_ref, sh_ref, whi_ref, wlo_ref, br_ref, tri_ref,
                   out_ref, hb_ref, info_ref, cnt_ref, base_sc):
    first = (pl.program_id(0) == 0) & (pl.program_id(1) == 0)

    @pl.when(first)
    def _():
        base_sc[...] = jnp.zeros(base_sc.shape, F32)

    ts = o_ref.shape[1]
    rows = min(OUT_LN_ROWS, ts)
    ys = [_dot(o_ref[0, 0:rows, :], w_ref[...])]
    for c in range(ts // rows):
        if c + 1 < ts // rows:
            ys.append(_dot(o_ref[0, (c + 1) * rows:(c + 2) * rows, :], w_ref[...]))
        sl = slice(c * rows, (c + 1) * rows)
        z = ALPHA * x_ref[0, sl, :] + gate_ref[0] * ys[c]
        x1 = _layer_norm(z, g_ref[...], b_ref[...])
        out_ref[0, sl, :] = x1
        packed, info = _route_tokens(x1, sc_ref[0], sh_ref[0], whi_ref[...], wlo_ref[...], br_ref[...],
                                     tri_ref[...], base_sc)
        hb_ref[0, sl, :] = packed
        info_ref[0, :, sl] = info
    cnt_ref[...] = base_sc[...]


def _out_ln(o, w_o, x, gate, g, b, route, ts=1024):
    B, S, D = x.shape
    K = o.shape[-1]
    ts = min(ts, S)
    scale2, shift2, w_grp, b_grp, w_rt, b_rt = route
    E, g0 = MOE_E, MOE_EPG
    w_r = jnp.zeros((ROUTER_ROWS, D), F32).at[:MOE_GROUPS].set(w_grp.T).at[g0:g0 + E].set(w_rt.T)
    w_hi = w_r.astype(BF16)
    w_lo = (w_r - w_hi.astype(F32)).astype(BF16)
    b_r = jnp.zeros((ROUTER_ROWS, 1), F32).at[:MOE_GROUPS, 0].set(b_grp).at[g0:g0 + E, 0].set(b_rt)
    tm = min(OUT_LN_ROWS, ts)
    tri = jnp.asarray(np.triu(np.ones((tm, tm), np.float32), 1), BF16)
    const = lambda shape: pl.BlockSpec(shape, lambda bi, i: (0,) * len(shape))
    per_batch = pl.BlockSpec((1, 1, D), lambda bi, i: (bi, 0, 0))
    return pl.pallas_call(
        _out_ln_kernel,
        grid=(B, S // ts),
        in_specs=[pl.BlockSpec((1, ts, K), lambda bi, i: (bi, i, 0)),
                  const((K, D)),
                  pl.BlockSpec((1, ts, D), lambda bi, i: (bi, i, 0)),
                  per_batch, const((1, D)), const((1, D)),
                  per_batch, per_batch,
                  const((ROUTER_ROWS, D)), const((ROUTER_ROWS, D)), const((ROUTER_ROWS, 1)), const((tm, tm))],
        out_specs=[pl.BlockSpec((1, ts, D), lambda bi, i: (bi, i, 0)),
                   pl.BlockSpec((1, ts, D // 2), lambda bi, i: (bi, i, 0)),
                   pl.BlockSpec((1, INFO_ROWS, ts), lambda bi, i: (bi, 0, i)),
                   const((E, LANES))],
        out_shape=[jax.ShapeDtypeStruct((B, S, D), F32),
                   jax.ShapeDtypeStruct((B, S, D // 2), jnp.uint32),
                   jax.ShapeDtypeStruct((B, INFO_ROWS, S), F32),
                   jax.ShapeDtypeStruct((E, LANES), F32)],
        scratch_shapes=[pltpu.VMEM((E, LANES), F32)],
        compiler_params=_cp(("arbitrary", "arbitrary"), 48),
        name="out_ln_route",
    )(o, w_o.astype(BF16), x, gate, g.reshape(1, D), b.reshape(1, D), scale2, shift2, w_hi, w_lo, b_r, tri)


def _rope_tables(positions, dim):
    inv_freq = 1.0 / (ROPE_THETA ** (jnp.arange(0, dim, 2, dtype=F32) / dim))
    ang = positions.astype(F32)[..., None] * inv_freq
    return jnp.cos(ang), jnp.sin(ang)


def _rot_half_cols(w, half):
    return jnp.concatenate([-w[..., half:], w[..., :half]], axis=-1)


def _mla_front_kernel(x_ref, sc_ref, sh_ref, win_ref, gq_ref, gkv_ref, wqt_ref, wk_ref, wvt_ref, t_ref, tt_ref,
                      qt_ref, k_ref, vt_ref, *, scale):
    hb = (x_ref[0] * (1.0 + sc_ref[0]) + sh_ref[0]).astype(BF16)
    r = _dot(hb, win_ref[...])
    tab = t_ref[0]
    tab_t = tt_ref[0]
    low = lax.broadcasted_iota(jnp.int32, tab.shape, 1) < MLA_DR

    def rms(cx, g):
        return (cx * lax.rsqrt(jnp.mean(cx * cx, axis=-1, keepdims=True) + RMS_EPS) * g).astype(BF16)

    cq = rms(r[:, :MLA_QR], gq_ref[...])
    ckv = rms(r[:, MLA_QR:MLA_QR + MLA_KVR], gkv_ref[...])
    t = r[:, MLA_QR + MLA_KVR:] * tab
    k_rope = jnp.where(low, t + pltpu.roll(t, MLA_DR, axis=1), 0.0).astype(BF16)
    zeros = jnp.zeros((MLA_DR, hb.shape[0]), BF16)
    for h in range(MLA_HEADS):
        qh = _nt_dot(wqt_ref[h], cq)
        tq = qh[MLA_DN:] * tab_t
        qt_ref[0, h, :MLA_DN, :] = (qh[:MLA_DN] * scale).astype(BF16)
        qt_ref[0, h, MLA_DN:MLA_DN + MLA_DR, :] = ((tq[:MLA_DR] + tq[MLA_DR:]) * scale).astype(BF16)
        qt_ref[0, h, MLA_DN + MLA_DR:, :] = zeros
        k_ref[0, h, :, :MLA_DN] = _dot(ckv, wk_ref[h]).astype(BF16)
        k_ref[0, h, :, MLA_DN:] = k_rope
        vt_ref[0, h] = _nt_dot(wvt_ref[h], ckv).astype(BF16)


def _flash_causal_kernel(qt_ref, k_ref, vt_ref, o_ref, *, t):
    dv, s_len = vt_ref.shape[2], vt_ref.shape[3]
    pairs = [(qi, kt) for qi in range(s_len // t) for kt in range(qi + 1)]
    kpos = lax.broadcasted_iota(jnp.int32, (t, 1), 0)
    qpos = lax.broadcasted_iota(jnp.int32, (1, t), 1)
    diag = kpos <= qpos

    def scores(j):
        qi, kt = pairs[j]
        return _dot(k_ref[0, 0, kt * t:(kt + 1) * t, :], qt_ref[0, 0, :, qi * t:(qi + 1) * t])

    def value_matmul(pending, l, acc):
        qi, kt, p, a = pending
        acc = a * acc + _dot(vt_ref[0, 0, :, kt * t:(kt + 1) * t], p)
        if kt == qi:
            o_ref[0, qi * t:(qi + 1) * t, :] = (acc / l).T.astype(o_ref.dtype)
        return acc

    s_cur, pending = scores(0), None
    m = l = acc = None
    for j, (qi, kt) in enumerate(pairs):
        s_next = scores(j + 1) if j + 1 < len(pairs) else None
        if pending is not None:
            acc = value_matmul(pending, l, acc)
        if kt == 0:
            m, l, acc = jnp.full((1, t), NEG, F32), jnp.zeros((1, t), F32), jnp.zeros((dv, t), F32)
        s = jnp.where(diag, s_cur, NEG) if kt == qi else s_cur
        m_new = jnp.maximum(m, jnp.max(s, axis=0, keepdims=True))
        p = jnp.exp2(s - m_new)
        a = jnp.exp2(m - m_new)
        l = a * l + jnp.sum(p, axis=0, keepdims=True)
        m = m_new
        pending = (qi, kt, p.astype(BF16), a)
        s_cur = s_next
    value_matmul(pending, l, acc)


def _mla_layer(x, shift, scale_v, gate, cos, sin, w_in, g_q, w_uq, g_kv, w_ukv, w_o, ln_g, ln_b, route):
    B, S, D = x.shape
    H, dn, dr, dv = MLA_HEADS, MLA_DN, MLA_DR, MLA_DV
    half = dr // 2
    kr = w_in[:, MLA_QR + MLA_KVR:]
    win_p = jnp.concatenate([w_in, _rot_half_cols(kr, half)], axis=1).astype(BF16)
    wq = w_uq.reshape(MLA_QR, H, dn + dr)
    wq = jnp.concatenate([wq, _rot_half_cols(wq[..., dn:], half)], axis=-1)
    wqt = jnp.transpose(wq, (1, 2, 0)).astype(BF16)
    wkv = w_ukv.reshape(MLA_KVR, H, dn + dv)
    wk = jnp.transpose(wkv[..., :dn], (1, 0, 2)).astype(BF16)
    wvt = jnp.transpose(wkv[..., dn:], (1, 2, 0)).astype(BF16)
    tab = jnp.concatenate([cos, cos, sin, sin], axis=-1)
    tab_t = jnp.swapaxes(tab, 1, 2)
    ts = min(512, S)
    nw = win_p.shape[1]
    hd = dn + 2 * dr
    qt, k, vt = pl.pallas_call(
        functools.partial(_mla_front_kernel, scale=LOG2E * (dn + dr) ** -0.5),
        grid=(B, S // ts),
        in_specs=[pl.BlockSpec((1, ts, D), lambda b, i: (b, i, 0)),
                  pl.BlockSpec((1, 1, D), lambda b, i: (b, 0, 0)),
                  pl.BlockSpec((1, 1, D), lambda b, i: (b, 0, 0)),
                  pl.BlockSpec((D, nw), lambda b, i: (0, 0)),
                  pl.BlockSpec((1, MLA_QR), lambda b, i: (0, 0)),
                  pl.BlockSpec((1, MLA_KVR), lambda b, i: (0, 0)),
                  pl.BlockSpec((H, hd, MLA_QR), lambda b, i: (0, 0, 0)),
                  pl.BlockSpec((H, MLA_KVR, dn), lambda b, i: (0, 0, 0)),
                  pl.BlockSpec((H, dv, MLA_KVR), lambda b, i: (0, 0, 0)),
                  pl.BlockSpec((1, ts, LANES), lambda b, i: (b, i, 0)),
                  pl.BlockSpec((1, LANES, ts), lambda b, i: (b, 0, i))],
        out_specs=[pl.BlockSpec((1, H, hd, ts), lambda b, i: (b, 0, 0, i)),
                   pl.BlockSpec((1, H, ts, hd), lambda b, i: (b, 0, i, 0)),
                   pl.BlockSpec((1, H, dv, ts), lambda b, i: (b, 0, 0, i))],
        out_shape=[jax.ShapeDtypeStruct((B, H, hd, S), BF16),
                   jax.ShapeDtypeStruct((B, H, S, hd), BF16),
                   jax.ShapeDtypeStruct((B, H, dv, S), BF16)],
        compiler_params=_cp(("arbitrary", "arbitrary"), 48),
        name="mla_front",
    )(x, scale_v, shift, win_p, g_q.reshape(1, -1), g_kv.reshape(1, -1), wqt, wk, wvt, tab, tab_t)

    t = min(512, S)
    o = pl.pallas_call(
        functools.partial(_flash_causal_kernel, t=t),
        grid=(B, H),
        in_specs=[pl.BlockSpec((1, 1, hd, S), lambda b, h: (b, h, 0, 0)),
                  pl.BlockSpec((1, 1, S, hd), lambda b, h: (b, h, 0, 0)),
                  pl.BlockSpec((1, 1, dv, S), lambda b, h: (b, h, 0, 0))],
        out_specs=pl.BlockSpec((1, S, dv), lambda b, h: (b, 0, h)),
        out_shape=jax.ShapeDtypeStruct((B, S, H * dv), BF16),
        compiler_params=_cp(("arbitrary", "arbitrary"), 48),
        name="mla_attn",
    )(qt, k, vt)
    return _out_ln(o, w_o, x, gate, ln_g, ln_b, route)


def _proj_kernel(*refs, segs, chunk, has_rope):
    x_ref, sc_ref, sh_ref, w_ref, wt_ref = refs[:5]
    n_in = 9 if has_rope else 5
    outs = refs[n_in:]
    hb = (x_ref[0] * (1.0 + sc_ref[0]) + sh_ref[0]).astype(BF16)
    if has_rope:
        cos, sin = refs[5][0], refs[6][0]
        cos_t, sin_t = refs[7][0], refs[8][0]
    half = LANES // 2
    for (c0, width, kind, scale, transposed), o_ref in zip(segs, outs):
        for j in range(0, width, chunk):
            cw = min(chunk, width - j)
            if transposed:
                r = _nt_dot(wt_ref[c0 + j:c0 + j + cw, :], hb)
            else:
                r = _dot(hb, w_ref[:, c0 + j:c0 + j + cw])
            if kind == "rope":
                parts = []
                for t in range(0, cw, LANES):
                    if transposed:
                        u = r[t:t + LANES]
                        parts.append(u * cos_t + pltpu.roll(u, half, axis=0) * sin_t)
                    else:
                        u = r[:, t:t + LANES]
                        parts.append(u * cos + pltpu.roll(u, half, axis=1) * sin)
                r = jnp.concatenate(parts, axis=0 if transposed else 1) if len(parts) > 1 else parts[0]
            elif kind == "sigmoid":
                r = jax.nn.sigmoid(r)
            if scale != 1.0:
                r = r * scale
            if transposed:
                o_ref[0, j:j + cw, :] = r.astype(o_ref.dtype)
            else:
                o_ref[0, :, j:j + cw] = r.astype(o_ref.dtype)


def _mod_proj(x, shift, scale_v, w, wt, segs, out_dtypes, rope=None, ts=512, chunk=256):
    B, S, D = x.shape
    ts = min(ts, S)
    in_specs = [pl.BlockSpec((1, ts, D), lambda b, i: (b, i, 0)),
                pl.BlockSpec((1, 1, D), lambda b, i: (b, 0, 0)),
                pl.BlockSpec((1, 1, D), lambda b, i: (b, 0, 0)),
                pl.BlockSpec(w.shape, lambda b, i: (0, 0)),
                pl.BlockSpec(wt.shape, lambda b, i: (0, 0))]
    args = [x, scale_v, shift, w, wt]
    if rope is not None:
        cos, sin = rope
        in_specs += [pl.BlockSpec((1, ts, LANES), lambda b, i: (b, i, 0))] * 2
        in_specs += [pl.BlockSpec((1, LANES, ts), lambda b, i: (b, 0, i))] * 2
        args += [cos, sin, jnp.swapaxes(cos, 1, 2), jnp.swapaxes(sin, 1, 2)]
    out_specs, out_shape = [], []
    for sg, dt in zip(segs, out_dtypes):
        if sg[4]:
            out_specs.append(pl.BlockSpec((1, sg[1], ts), lambda b, i: (b, 0, i)))
            out_shape.append(jax.ShapeDtypeStruct((B, sg[1], S), dt))
        else:
            out_specs.append(pl.BlockSpec((1, ts, sg[1]), lambda b, i: (b, i, 0)))
            out_shape.append(jax.ShapeDtypeStruct((B, S, sg[1]), dt))
    return pl.pallas_call(
        functools.partial(_proj_kernel, segs=tuple(segs), chunk=chunk, has_rope=rope is not None),
        grid=(B, S // ts),
        in_specs=in_specs,
        out_specs=out_specs,
        out_shape=out_shape,
        compiler_params=_cp(("arbitrary", "arbitrary"), 48),
        name="mod_proj",
    )(*args)


def _nsa_cmp_kernel(kc_ref, vc_ref, pek_ref, pev_ref, wk1_ref, wk2_ref, wv1_ref, wv2t_ref, ko_ref, vto_ref):
    half = NSA_CMP_BLOCK // 2
    nchunk = kc_ref.shape[1]

    def hidden(src, pe_ref, w1_ref, kh):
        a = jnp.zeros((nchunk, w1_ref.shape[2]), F32)
        bm = jnp.zeros((nchunk, w1_ref.shape[2]), F32)
        for l in range(half):
            c0 = (l * NSA_KVH + kh) * NSA_DH
            t = src[0, :, c0:c0 + NSA_DH].astype(F32)
            a = a + _dot((t + pe_ref[l:l + 1, :]).astype(BF16), w1_ref[l])
            bm = bm + _dot((t + pe_ref[half + l:half + l + 1, :]).astype(BF16), w1_ref[half + l])
        hid = a + pltpu.roll(bm, nchunk - 1, axis=0)
        return (hid * jax.nn.sigmoid(hid)).astype(BF16)

    rowi = lax.broadcasted_iota(jnp.int32, (nchunk, 1), 0)
    coli = lax.broadcasted_iota(jnp.int32, (1, nchunk), 1)
    for kh in range(NSA_KVH):
        out = _dot(hidden(kc_ref, pek_ref, wk1_ref, kh), wk2_ref[...])
        ko_ref[0, kh] = jnp.where(rowi < nchunk - 1, out, 0.0).astype(ko_ref.dtype)
        out_t = _nt_dot(wv2t_ref[...], hidden(vc_ref, pev_ref, wv1_ref, kh))
        vto_ref[0, kh] = jnp.where(coli < nchunk - 1, out_t, 0.0).astype(vto_ref.dtype)


def _nsa_attn_kernel(qt_ref, kc_ref, vct_ref, ks_ref, vst_ref, kw_ref, vwt_ref, gt_ref, ovt_ref, e_ref, o_ref,
                     q4_sc, m_sc, l_sc, acc_sc, *, tq, tk, span, n_slc):
    qi = pl.program_id(2)
    G = NSA_G
    for g in range(G):
        q4_sc[:, g * tq:(g + 1) * tq] = qt_ref[0, g * NSA_DH:(g + 1) * NSA_DH, :]
    q4 = q4_sc[...]
    qpos = qi * tq + lax.broadcasted_iota(jnp.int32, (1, tq), 1)

    def tile_g(a):
        return jnp.concatenate([a] * G, axis=1)

    qpos4 = tile_g(qpos)

    kstart = pl.multiple_of(jnp.maximum(qi * tq + tq - span, 0), tq)
    s_win = _dot(kw_ref[0, pl.ds(kstart, span), :], q4)

    n_cmp = kc_ref.shape[2]
    nid = lax.broadcasted_iota(jnp.int32, (n_cmp, 1), 0)
    cmask = (nid * NSA_CMP_STRIDE + (NSA_CMP_BLOCK - 1) <= qpos4) & (nid < n_cmp - 1)
    s = jnp.where(cmask, _dot(kc_ref[0, 0], q4), NEG)
    p = jnp.where(cmask, jnp.exp2(s - jnp.max(s, axis=0, keepdims=True)), 0.0)
    l = jnp.sum(p, axis=0, keepdims=True)
    p_cmp = p / jnp.where(l > 0.0, l, 1.0)
    o_cmp = _dot(vct_ref[0, 0], p_cmp.astype(BF16))

    diff = qpos - (kstart + lax.broadcasted_iota(jnp.int32, (span, 1), 0))
    s_win = s_win + tile_g(jnp.where((diff >= 0) & (diff < NSA_WINDOW), 0.0, NEG))
    p = jnp.exp2(s_win - jnp.max(s_win, axis=0, keepdims=True))
    o_win = _dot(vwt_ref[0, :, pl.ds(kstart, span)], p.astype(BF16)) / jnp.sum(p, axis=0, keepdims=True)

    psum = p_cmp[:, 0:tq]
    for g in range(1, G):
        psum = psum + p_cmp[:, g * tq:(g + 1) * tq]
    p_hi, p_lo = _split_bf16(psum)
    imp = _dot(ovt_ref[...], p_hi) + _dot(ovt_ref[...], p_lo)
    nb = ovt_ref.shape[0]
    blk = lax.broadcasted_iota(jnp.int32, (nb, 1), 0)
    cur = jnp.right_shift(qpos, SLC_SHIFT)
    forced = (blk == 0) | (blk == cur) | (blk == cur - 1)
    valid = (blk * NSA_SLC_BLOCK <= qpos) & (blk < n_slc)
    work = jnp.where(valid, imp + jnp.where(forced, NSA_BONUS, 0.0), -jnp.inf)
    beaten = jnp.zeros((nb, tq), F32)
    for j in range(n_slc):
        wj = work[j:j + 1, :]
        beaten = beaten + jnp.where((wj > work) | ((wj == work) & (blk > j)), 1.0, 0.0)
    sel = jnp.where(beaten < float(min(NSA_TOPK, n_slc)), 1.0, 0.0).astype(BF16)

    m_sc[...] = jnp.full(m_sc.shape, NEG, F32)
    l_sc[...] = jnp.zeros(l_sc.shape, F32)
    acc_sc[...] = jnp.zeros(acc_sc.shape, F32)
    n_kt = ((qi + 1) * tq + tk - 1) // tk

    def slc_tiles(n):
        def scores(kt):
            return _dot(ks_ref[0, kt * tk:(kt + 1) * tk, :], q4)

        def value_matmul(kt, p, a):
            acc_sc[...] = a * acc_sc[...] + _dot(vst_ref[0, :, kt * tk:(kt + 1) * tk], p)

        s, pending = scores(0), None
        chosen = _dot(e_ref[0:n * tk, :], sel)
        kpos = lax.broadcasted_iota(jnp.int32, (n * tk, 1), 0)
        bias_all = jnp.where((chosen > 0.5) & (kpos <= qpos), 0.0, NEG)
        for kt in range(n):
            s_next = scores(kt + 1) if kt + 1 < n else None
            if pending is not None:
                value_matmul(*pending)
            s = s + tile_g(bias_all[kt * tk:(kt + 1) * tk])
            m_prev = m_sc[...]
            m_new = jnp.maximum(m_prev, jnp.max(s, axis=0, keepdims=True))
            p = jnp.exp2(s - m_new)
            a = jnp.exp2(m_prev - m_new)
            l_sc[...] = a * l_sc[...] + jnp.sum(p, axis=0, keepdims=True)
            m_sc[...] = m_new
            pending = (kt, p.astype(BF16), a)
            s = s_next
        value_matmul(*pending)

    for n in range(1, (ks_ref.shape[1] + tk - 1) // tk + 1):
        pl.when(n_kt == n)(functools.partial(slc_tiles, n))
    o_slc = acc_sc[...] / l_sc[...]

    gt = gt_ref[0]
    for g in range(G):
        cols = slice(g * tq, (g + 1) * tq)
        o = (gt[3 * g:3 * g + 1] * o_cmp[:, cols] + gt[3 * g + 1:3 * g + 2] * o_slc[:, cols]
             + gt[3 * g + 2:3 * g + 3] * o_win[:, cols])
        o_ref[0, :, g * NSA_DH:(g + 1) * NSA_DH] = o.T.astype(o_ref.dtype)


def _nsa_layer(x, shift, scale_v, gate, cos, sin, w_in, pe_k, pe_v, w_ck1, w_ck2, w_cv1, w_cv2, w_o,
               ln_g, ln_b, route):
    B, S, D = x.shape
    H, KH, Dh, G = NSA_HEADS, NSA_KVH, NSA_DH, NSA_G
    kvw = KH * Dh
    cuts = [int(v) for v in np.cumsum([H * Dh] + [kvw] * 6)]
    wq, wkc, wvc, wks, wvs, wkw, wvw, wgl = jnp.split(w_in, cuts, axis=1)
    wgl = jnp.pad(wgl.reshape(D, KH, G * 3), ((0, 0), (0, 0), (0, GATE_ROWS - G * 3))).reshape(D, KH * GATE_ROWS)
    w = jnp.concatenate([wkc, wks, wkw, wvc], axis=1).astype(BF16)
    wt = jnp.concatenate([wq, wvs, wvw, wgl], axis=1).T.astype(BF16)
    segs = [(0, H * Dh, "rope", LOG2E * Dh ** -0.5, True),
            (0, kvw, "rope", 1.0, False), (kvw, kvw, "rope", 1.0, False), (2 * kvw, kvw, "rope", 1.0, False),
            (3 * kvw, kvw, "plain", 1.0, False),
            (H * Dh, kvw, "plain", 1.0, True), (H * Dh + kvw, kvw, "plain", 1.0, True),
            (H * Dh + 2 * kvw, KH * GATE_ROWS, "sigmoid", 1.0, True)]
    rope = (jnp.concatenate([cos, cos], axis=-1), jnp.concatenate([-sin, sin], axis=-1))
    qt, kc, ks, kw, vc, vst, vwt, gates_t = _mod_proj(
        x, shift, scale_v, w, wt, segs, [BF16] * 7 + [F32], rope=rope)

    st = NSA_CMP_STRIDE
    nchunk = S // st
    kc2 = kc.reshape(B, nchunk, st * kvw)
    vc2 = vc.reshape(B, nchunk, st * kvw)
    hid = w_ck1.shape[-1]
    full = lambda shape: pl.BlockSpec(shape, lambda b: (0,) * len(shape))
    k_cmp, v_cmp_t = pl.pallas_call(
        _nsa_cmp_kernel,
        grid=(B,),
        in_specs=[pl.BlockSpec((1, nchunk, st * kvw), lambda b: (b, 0, 0)),
                  pl.BlockSpec((1, nchunk, st * kvw), lambda b: (b, 0, 0)),
                  full((NSA_CMP_BLOCK, Dh)), full((NSA_CMP_BLOCK, Dh)),
                  full((NSA_CMP_BLOCK, Dh, hid)), full((hid, Dh)),
                  full((NSA_CMP_BLOCK, Dh, hid)), full((Dh, hid))],
        out_specs=[pl.BlockSpec((1, KH, nchunk, Dh), lambda b: (b, 0, 0, 0)),
                   pl.BlockSpec((1, KH, Dh, nchunk), lambda b: (b, 0, 0, 0))],
        out_shape=[jax.ShapeDtypeStruct((B, KH, nchunk, Dh), BF16),
                   jax.ShapeDtypeStruct((B, KH, Dh, nchunk), BF16)],
        compiler_params=_cp(("arbitrary",), 32),
        name="nsa_cmp",
    )(kc2, vc2, pe_k, pe_v, w_ck1.astype(BF16), w_ck2.astype(BF16), w_cv1.astype(BF16), w_cv2.T.astype(BF16))

    n_slc = S // NSA_SLC_BLOCK
    nb = 32
    assert n_slc <= nb and nchunk % 8 == 0
    cmp_start = np.arange(nchunk) * st
    slc_start = np.arange(n_slc) * NSA_SLC_BLOCK
    ov = np.clip(np.minimum(cmp_start[:, None] + NSA_CMP_BLOCK, slc_start[None, :] + NSA_SLC_BLOCK)
                 - np.maximum(cmp_start[:, None], slc_start[None, :]), 0, None).astype(np.float32) / NSA_CMP_BLOCK
    ov[nchunk - 1:] = 0.0
    ov_t = np.zeros((nb, nchunk), np.float32)
    ov_t[:n_slc] = ov.T
    key_blk = np.zeros((S, nb), np.float32)
    key_blk[np.arange(S), np.arange(S) // NSA_SLC_BLOCK] = 1.0

    tq = min(256, S)
    tk = min(256, S)
    span = min(NSA_WINDOW + tq, S)
    k_spec = pl.BlockSpec((1, S, Dh), lambda b, kh, i: (b, 0, kh))
    vt_spec = pl.BlockSpec((1, Dh, S), lambda b, kh, i: (b, kh, 0))
    o = pl.pallas_call(
        functools.partial(_nsa_attn_kernel, tq=tq, tk=tk, span=span, n_slc=n_slc),
        grid=(B, KH, S // tq),
        in_specs=[pl.BlockSpec((1, G * Dh, tq), lambda b, kh, i: (b, kh, i)),
                  pl.BlockSpec((1, 1, nchunk, Dh), lambda b, kh, i: (b, kh, 0, 0)),
                  pl.BlockSpec((1, 1, Dh, nchunk), lambda b, kh, i: (b, kh, 0, 0)),
                  k_spec, vt_spec, k_spec, vt_spec,
                  pl.BlockSpec((1, GATE_ROWS, tq), lambda b, kh, i: (b, kh, i)),
                  pl.BlockSpec((nb, nchunk), lambda b, kh, i: (0, 0)),
                  pl.BlockSpec((S, nb), lambda b, kh, i: (0, 0))],
        out_specs=pl.BlockSpec((1, tq, G * Dh), lambda b, kh, i: (b, i, kh)),
        out_shape=jax.ShapeDtypeStruct((B, S, H * Dh), BF16),
        scratch_shapes=[pltpu.VMEM((Dh, G * tq), BF16), pltpu.VMEM((1, G * tq), F32),
                        pltpu.VMEM((1, G * tq), F32), pltpu.VMEM((Dh, G * tq), F32)],
        compiler_params=_cp(("arbitrary", "arbitrary", "arbitrary"), 32),
        name="nsa_attn",
    )(qt, k_cmp, v_cmp_t, ks, vst, kw, vwt, gates_t, jnp.asarray(ov_t, BF16), jnp.asarray(key_blk, BF16))
    return _out_ln(o, w_o, x, gate, ln_g, ln_b, route)


def _sb_attn_kernel(qt_ref, k_ref, vt_ref, u_ref, o_ref, r_sc, acc_sc, *, tq, tk):
    d, s_len = vt_ref.shape[1], vt_ref.shape[2]
    u = u_ref[...]
    nd = tq // tk

    def weights(z, strict):
        lk = -(jnp.maximum(z, 0.0) + jnp.log2(1.0 + jnp.exp2(-jnp.abs(z))))
        if strict is not None:
            lk = jnp.where(strict, lk, 0.0)
        between = r_sc[...] + _dot(u, lk.astype(BF16))
        a = jnp.exp2(lk + z + between)
        if strict is not None:
            a = jnp.where(strict, a, 0.0)
        r_sc[...] += jnp.sum(lk, axis=0, keepdims=True)
        return a.astype(BF16)

    for qi in range(s_len // tq):
        qt = qt_ref[0, :, qi * tq:(qi + 1) * tq]
        r_sc[...] = jnp.zeros(r_sc.shape, F32)
        acc_sc[...] = jnp.zeros(acc_sc.shape, F32)
        last = (qi + 1) * nd - 1
        tiles = [last - i for i in range(nd)] + ([qi * nd - 1] if qi > 0 else [])
        qpos = qi * tq + lax.broadcasted_iota(jnp.int32, (1, tq), 1)

        def scores(kt):
            return _dot(k_ref[0, kt * tk:(kt + 1) * tk, :], qt)

        z, pending = scores(tiles[0]), None
        for i, kt in enumerate(tiles):
            nxt = tiles[i + 1] if i + 1 < len(tiles) else max(qi * nd - 2, 0)
            z_next = scores(nxt)
            if pending is not None:
                acc_sc[...] += _dot(vt_ref[0, :, pending[0] * tk:(pending[0] + 1) * tk], pending[1])
            strict = (kt * tk + lax.broadcasted_iota(jnp.int32, (tk, 1), 0) < qpos) if kt >= qi * nd else None
            pending = (kt, weights(z, strict))
            z = z_next
        acc_sc[...] += _dot(vt_ref[0, :, pending[0] * tk:(pending[0] + 1) * tk], pending[1])

        if qi * nd - 2 >= 0:
            def scores_dyn(kt):
                return _dot(k_ref[0, pl.ds(pl.multiple_of(kt * tk, tk), tk), :], qt)

            def cond(carry):
                kt, _ = carry
                return (kt >= 0) & (jnp.max(r_sc[...]) > SB_EXIT_LOG2)

            def body(carry):
                kt, z = carry
                z_next = scores_dyn(jnp.maximum(kt - 1, 0))
                a = weights(z, None)
                acc_sc[...] += _dot(vt_ref[0, :, pl.ds(pl.multiple_of(kt * tk, tk), tk)], a)
                return kt - 1, z_next

            lax.while_loop(cond, body, (qi * nd - 2, z))
        o_ref[0, qi * tq:(qi + 1) * tq, :] = acc_sc[...].T.astype(o_ref.dtype)


def _sb_layer(x, shift, scale_v, gate, w_qkv, w_o, ln_g, ln_b, route):
    B, S, D = x.shape
    H, d = SB_HEADS, SB_DH
    hw = H * d
    w = w_qkv[:, hw:2 * hw].astype(BF16)
    wt = jnp.concatenate([w_qkv[:, :hw], w_qkv[:, 2 * hw:]], axis=1).T.astype(BF16)
    segs = [(0, hw, "plain", LOG2E * d ** -0.5, True), (0, hw, "plain", 1.0, False), (hw, hw, "plain", 1.0, True)]
    qt, k, vt = _mod_proj(x, shift, scale_v, w, wt, segs, [BF16] * 3)
    tq = min(512, S)
    tk = min(256, S)
    tri = np.triu(np.ones((tk, tk), np.float32), 1)
    o = pl.pallas_call(
        functools.partial(_sb_attn_kernel, tq=tq, tk=tk),
        grid=(B, H),
        in_specs=[pl.BlockSpec((1, d, S), lambda b, h: (b, h, 0)),
                  pl.BlockSpec((1, S, d), lambda b, h: (b, 0, h)),
                  pl.BlockSpec((1, d, S), lambda b, h: (b, h, 0)),
                  pl.BlockSpec((tk, tk), lambda b, h: (0, 0))],
        out_specs=pl.BlockSpec((1, S, d), lambda b, h: (b, 0, h)),
        out_shape=jax.ShapeDtypeStruct((B, S, hw), BF16),
        scratch_shapes=[pltpu.VMEM((1, tq), F32), pltpu.VMEM((d, tq), F32)],
        compiler_params=_cp(("arbitrary", "arbitrary"), 32),
        name="sb_attn",
    )(qt, k, vt, jnp.asarray(tri, BF16))
    return _out_ln(o, w_o, x, gate, ln_g, ln_b, route)


def _ffn_kernel(be_ref, seg_ref, nv_ref, nu_ref, x_ref, wg_ref, wu_ref, wd_ref, y_ref, wg_sc, wu_sc, wd_sc, act_sc):
    i = pl.program_id(0)
    n_blk = be_ref.shape[0]

    def stage(blk, src_ref, dst_sc):
        r = jnp.clip(blk, 0, n_blk - 1)
        first = (blk >= 0) & (blk < nu_ref[0]) & ((r == 0) | (be_ref[r] != be_ref[jnp.maximum(r - 1, 0)]))

        @pl.when(first)
        def _():
            dst_sc[seg_ref[r] % FFN_SLOTS] = src_ref[0, 0].astype(BF16)

    stage(i, wg_ref, wg_sc)
    stage(i - 1, wu_ref, wu_sc)
    stage(i - 2, wd_ref, wd_sc)
    blk = i - FFN_LEAD
    done = blk - 1
    has_new = (blk >= 0) & (blk < nu_ref[0])
    has_done = (done >= 0) & (done < nu_ref[0])
    slot_new = seg_ref[jnp.clip(blk, 0, n_blk - 1)] % FFN_SLOTS
    slot_done = seg_ref[jnp.clip(done, 0, n_blk - 1)] % FFN_SLOTS

    def rows_in():
        live = lax.broadcasted_iota(jnp.int32, (x_ref.shape[0], 1), 0) < nv_ref[jnp.clip(blk, 0, n_blk - 1)]
        return jnp.where(live, _unpack_pairs(x_ref[...]), 0.0).astype(BF16)

    act_in, act_out = (i + 1) % 2, i % 2

    @pl.when(has_new & has_done)
    def _():
        xb = rows_in()
        g = _dot(xb, wg_sc[slot_new])
        u = _dot(xb, wu_sc[slot_new])
        y_ref[...] = _pack_pairs(_dot(act_sc[act_in], wd_sc[slot_done]))
        act_sc[act_out] = (g * jax.nn.sigmoid(g) * u).astype(BF16)

    @pl.when(has_new & jnp.logical_not(has_done))
    def _():
        xb = rows_in()
        g = _dot(xb, wg_sc[slot_new])
        u = _dot(xb, wu_sc[slot_new])
        act_sc[act_out] = (g * jax.nn.sigmoid(g) * u).astype(BF16)

    @pl.when(has_done & jnp.logical_not(has_new))
    def _():
        y_ref[...] = _pack_pairs(_dot(act_sc[act_in], wd_sc[slot_done]))

    @pl.when((done >= nu_ref[0]) & (done < n_blk))
    def _():
        y_ref[...] = jnp.zeros(y_ref.shape, y_ref.dtype)


def _sc_gather(table, idx):
    n, d = idx.shape[0], table.shape[1]
    workers = SC_CORES * SC_SUBCORES
    per_w = n // workers
    assert per_w * workers == n and per_w % (2 * SC_CHUNK) == 0
    mesh = plsc.VectorSubcoreMesh(core_axis_name="c", subcore_axis_name="s")

    @functools.partial(
        pl.kernel, mesh=mesh,
        out_type=jax.ShapeDtypeStruct((n, d), table.dtype),
        scratch_types=[pltpu.VMEM((SC_CHUNK,), jnp.int32), pltpu.VMEM((SC_CHUNK,), jnp.int32),
                       pltpu.VMEM((SC_CHUNK, d), table.dtype), pltpu.VMEM((SC_CHUNK, d), table.dtype)]
                      + [pltpu.SemaphoreType.DMA] * 4,
        name="sc_gather")
    def gather(table_hbm, idx_hbm, out_hbm, idx_a, idx_b, rows_a, rows_b, sem_a, sem_b, sem_c, sem_d):
        wid = lax.axis_index("s") * SC_CORES + lax.axis_index("c")

        @pl.loop(0, per_w // (2 * SC_CHUNK))
        def _(j):
            base_a = pl.multiple_of(wid * per_w + j * (2 * SC_CHUNK), SC_CHUNK)
            base_b = pl.multiple_of(base_a + SC_CHUNK, SC_CHUNK)
            pltpu.sync_copy(idx_hbm.at[pl.ds(base_a, SC_CHUNK)], idx_a)
            pltpu.sync_copy(idx_hbm.at[pl.ds(base_b, SC_CHUNK)], idx_b)
            get_a = pltpu.async_copy(table_hbm.at[idx_a], rows_a, sem_a)
            get_b = pltpu.async_copy(table_hbm.at[idx_b], rows_b, sem_b)
            get_a.wait()
            put_a = pltpu.async_copy(rows_a, out_hbm.at[pl.ds(base_a, SC_CHUNK)], sem_c)
            get_b.wait()
            put_b = pltpu.async_copy(rows_b, out_hbm.at[pl.ds(base_b, SC_CHUNK)], sem_d)
            put_a.wait()
            put_b.wait()

    return gather(table, idx)


def _sc_dispatch(src, dest0, dest1, n_out):
    n, d = src.shape
    workers = SC_CORES * SC_SUBCORES
    per_w = n // workers
    assert per_w * workers == n and per_w % SC_CHUNK == 0
    mesh = plsc.VectorSubcoreMesh(core_axis_name="c", subcore_axis_name="s")

    @functools.partial(
        pl.kernel, mesh=mesh,
        out_type=jax.ShapeDtypeStruct((n_out, d), src.dtype),
        scratch_types=[pltpu.VMEM((SC_CHUNK,), jnp.int32),
                       pltpu.VMEM((SC_CHUNK,), jnp.int32),
                       pltpu.VMEM((SC_CHUNK, d), src.dtype)]
                      + [pltpu.SemaphoreType.DMA] * 3,
        name="sc_dispatch")
    def dispatch(src_hbm, d0_hbm, d1_hbm, out_hbm, i0_v, i1_v, rows_v, sem_a, sem_b, sem_c):
        wid = lax.axis_index("s") * SC_CORES + lax.axis_index("c")

        @pl.loop(0, per_w // SC_CHUNK)
        def _(j):
            base = pl.multiple_of(wid * per_w + j * SC_CHUNK, SC_CHUNK)
            get_0 = pltpu.async_copy(d0_hbm.at[pl.ds(base, SC_CHUNK)], i0_v, sem_a)
            get_1 = pltpu.async_copy(d1_hbm.at[pl.ds(base, SC_CHUNK)], i1_v, sem_b)
            get_r = pltpu.async_copy(src_hbm.at[pl.ds(base, SC_CHUNK)], rows_v, sem_c)
            get_0.wait()
            get_1.wait()
            get_r.wait()
            put_0 = pltpu.async_copy(rows_v, out_hbm.at[i0_v], sem_a)
            put_1 = pltpu.async_copy(rows_v, out_hbm.at[i1_v], sem_b)
            put_0.wait()
            put_1.wait()

    return dispatch(src, dest0, dest1)


def _moe_out_kernel(x_ref, y1_ref, y2_ref, info_ref, gate_ref, g_ref, b_ref, *rest, has_partial):
    o_ref = rest[1] if has_partial else rest[0]
    info = info_ref[0]
    ts = info.shape[1]
    info_tok = jnp.concatenate([info, jnp.zeros((LANES - INFO_ROWS, ts), F32)], axis=0).T
    y = info_tok[:, 4:5] * _unpack_pairs(y1_ref[0, 0]) + info_tok[:, 5:6] * _unpack_pairs(y2_ref[0, 0])
    z = ALPHA * x_ref[0] + gate_ref[0] * y
    o_ref[0] = _layer_norm(z, g_ref[...], b_ref[...])


def _moe_out(x, y12, info, gate, g, b, b0, partial_out, ts=512):
    B, S, D = x.shape
    nb = y12.shape[1]
    ts = min(ts, S)
    tok = pl.BlockSpec((1, ts, D), lambda bi, i: (b0 + bi, i, 0))
    args = [x, y12, y12, info, gate, g.reshape(1, D), b.reshape(1, D)]
    in_specs = [tok,
                pl.BlockSpec((1, 1, ts, D // 2), lambda bi, i: (0, bi, i, 0)),
                pl.BlockSpec((1, 1, ts, D // 2), lambda bi, i: (1, bi, i, 0)),
                pl.BlockSpec((1, INFO_ROWS, ts), lambda bi, i: (b0 + bi, 0, i)),
                pl.BlockSpec((1, 1, D), lambda bi, i: (b0 + bi, 0, 0)),
                pl.BlockSpec((1, D), lambda bi, i: (0, 0)),
                pl.BlockSpec((1, D), lambda bi, i: (0, 0))]
    aliases = {}
    if partial_out is not None:
        args.append(partial_out)
        in_specs.append(pl.BlockSpec(memory_space=pl.ANY))
        aliases = {len(args) - 1: 0}
    return pl.pallas_call(
        functools.partial(_moe_out_kernel, has_partial=partial_out is not None),
        grid=(nb, S // ts),
        in_specs=in_specs,
        out_specs=tok,
        out_shape=jax.ShapeDtypeStruct((B, S, D), F32),
        input_output_aliases=aliases,
        compiler_params=_cp(("arbitrary", "arbitrary"), 40),
        name="moe_out",
    )(*args)


def _moe_layer(x, routed, gate, layer, w_gate, w_up, w_down, ln_g, ln_b):
    B, S, D = x.shape
    T = B * S
    E, R, F = MOE_E, MOE_ROWS, w_gate.shape[-1]
    hb, info, cnt = routed

    counts = cnt[:, 0].astype(jnp.int32)
    padded = (counts + R - 1) // R * R
    ends = jnp.cumsum(padded)
    starts = ends - padded
    expert_ids = jnp.arange(E, dtype=jnp.int32)

    def per_expert(table, expert):
        return jnp.sum(jnp.where(expert[..., None] == expert_ids, table, 0), axis=-1)

    def row_of(expert, rank):
        return (per_expert(starts, expert.astype(jnp.int32)) + rank.astype(jnp.int32)).reshape(T)

    dest0 = row_of(info[:, 0, :], info[:, 2, :])
    dest1 = row_of(info[:, 1, :], info[:, 3, :])
    n_blk = (T * 2) // R + E
    P = n_blk * R
    blk_exp = jnp.minimum(jnp.sum(jnp.arange(n_blk)[:, None] * R >= ends[None, :], axis=1), E - 1).astype(jnp.int32)
    n_used = (ends[-1] // R).astype(jnp.int32).reshape(1)
    x_rows = _sc_dispatch(hb.reshape(T, D // 2), dest0, dest1, P)

    seg = jnp.concatenate([jnp.zeros((1,), jnp.int32),
                           jnp.cumsum((blk_exp[1:] != blk_exp[:-1]).astype(jnp.int32))])

    live = jnp.clip(per_expert(counts, blk_exp) + per_expert(starts, blk_exp) - jnp.arange(n_blk, dtype=jnp.int32) * R,
                    0, R).astype(jnp.int32)

    def blk_at(offset):
        return lambda i: jnp.clip(i - offset, 0, n_blk - 1)

    y_rows = pl.pallas_call(
        _ffn_kernel,
        grid_spec=pltpu.PrefetchScalarGridSpec(
            num_scalar_prefetch=4,
            grid=(n_blk + FFN_LEAD + 1,),
            in_specs=[pl.BlockSpec((R, D // 2), lambda i, be, sg, nv, nu: (blk_at(FFN_LEAD)(i), 0)),
                      pl.BlockSpec((1, 1, D, F), lambda i, be, sg, nv, nu: (layer, be[blk_at(0)(i)], 0, 0)),
                      pl.BlockSpec((1, 1, D, F), lambda i, be, sg, nv, nu: (layer, be[blk_at(1)(i)], 0, 0)),
                      pl.BlockSpec((1, 1, F, D), lambda i, be, sg, nv, nu: (layer, be[blk_at(2)(i)], 0, 0))],
            out_specs=pl.BlockSpec((R, D // 2), lambda i, be, sg, nv, nu: (blk_at(FFN_LEAD + 1)(i), 0)),
            scratch_shapes=[pltpu.VMEM((FFN_SLOTS, D, F), BF16), pltpu.VMEM((FFN_SLOTS, D, F), BF16),
                            pltpu.VMEM((FFN_SLOTS, F, D), BF16), pltpu.VMEM((2, R, F), BF16)]),
        out_shape=jax.ShapeDtypeStruct((P, D // 2), jnp.uint32),
        compiler_params=_cp(("arbitrary",), 48),
        name="moe_ffn",
    )(blk_exp, seg, live, n_used, x_rows, w_gate, w_up, w_down)

    nb = B // MOE_OUT_CHUNKS if B % MOE_OUT_CHUNKS == 0 else B
    out = None
    for b0 in range(0, B, nb):
        rows = slice(b0 * S, (b0 + nb) * S)
        y12 = _sc_gather(y_rows, jnp.concatenate([dest0[rows], dest1[rows]])).reshape(2, nb, S, D // 2)
        out = _moe_out(x, y12, info, gate, ln_g, ln_b, b0, out)
    return out


def kernel(x, c, positions, ada_w, ada_b, ln_g, ln_b, mla_w_in, mla_g_q, mla_w_uq, mla_g_kv, mla_w_ukv, mla_w_o, nsa_w_in, nsa_pe_k, nsa_pe_v, nsa_w_ck1, nsa_w_ck2, nsa_w_cv1, nsa_w_cv2, nsa_w_o, sb_w_qkv, sb_w_o, moe_w_grp, moe_b_grp, moe_w_rt, moe_b_rt, moe_w_gate, moe_w_up, moe_w_down):
    D = x.shape[-1]
    depth = ada_w.shape[0]
    cos_mla, sin_mla = _rope_tables(positions, MLA_DR)
    cos_nsa, sin_nsa = _rope_tables(positions, NSA_DH)
    mods = _ada_all(c, ada_w, ada_b)

    def mod(i, sub):
        m = mods[2 * i + sub][:, None, :]
        return m[..., :D], m[..., D:2 * D], m[..., 2 * D:]

    for i in range(depth):
        kind, j = i % N_MIXERS, i // N_MIXERS
        shift, scale_v, gate = mod(i, 0)
        shift2, scale2, gate2 = mod(i, 1)
        route = (scale2, shift2, moe_w_grp[i], moe_b_grp[i], moe_w_rt[i], moe_b_rt[i])
        if kind == 0:
            x, *routed = _mla_layer(x, shift, scale_v, gate, cos_mla, sin_mla, mla_w_in[j], mla_g_q[j], mla_w_uq[j],
                                    mla_g_kv[j], mla_w_ukv[j], mla_w_o[j], ln_g[i, 0], ln_b[i, 0], route)
        elif kind == 1:
            x, *routed = _nsa_layer(x, shift, scale_v, gate, cos_nsa, sin_nsa, nsa_w_in[j], nsa_pe_k[j], nsa_pe_v[j],
                                    nsa_w_ck1[j], nsa_w_ck2[j], nsa_w_cv1[j], nsa_w_cv2[j], nsa_w_o[j],
                                    ln_g[i, 0], ln_b[i, 0], route)
        else:
            x, *routed = _sb_layer(x, shift, scale_v, gate, sb_w_qkv[j], sb_w_o[j], ln_g[i, 0], ln_b[i, 0], route)
        x = _moe_layer(x, routed, gate2, i, moe_w_gate, moe_w_up, moe_w_down, ln_g[i, 1], ln_b[i, 1])
    return x
```

```python
import functools

import numpy as np
import jax
import jax.numpy as jnp
from jax import lax
from jax.experimental import pallas as pl
from jax.experimental.pallas import tpu as pltpu
from jax.experimental.pallas import tpu_sc as plsc

BF16 = jnp.bfloat16
F32 = jnp.float32

DEPTH = 4
N_MIXERS = 3
ROPE_THETA = 10000.0
LN_EPS = 1e-5
RMS_EPS = 1e-6
NEG = -1e30
LOG2E = 1.4426950408889634
ALPHA = (2 * DEPTH) ** 0.25

MLA_HEADS, MLA_DN, MLA_DR, MLA_DV = 8, 128, 64, 128
MLA_QR, MLA_KVR = 256, 256

NSA_HEADS, NSA_KVH, NSA_DH = 8, 2, 128
NSA_G = NSA_HEADS // NSA_KVH
NSA_CMP_BLOCK, NSA_CMP_STRIDE = 32, 16
NSA_SLC_BLOCK, NSA_TOPK = 64, 8
SLC_SHIFT = 6
NSA_WINDOW = 512
NSA_BONUS = 1e3
GATE_ROWS = 16

SB_HEADS, SB_DH = 8, 128
SB_EXIT_LOG2 = -160.0

MOE_GROUPS, MOE_EPG = 4, 8
EPG_SHIFT = 3
MOE_E = MOE_GROUPS * MOE_EPG
MOE_ROWS = 256
FFN_LEAD = 3
FFN_SLOTS = FFN_LEAD + 2
OUT_LN_ROWS = 256
ROUTER_ROWS = 64
INFO_ROWS = 8

SC_CORES, SC_SUBCORES = 2, 16
SC_CHUNK = 64

LANES = 128
VMEM_BYTES_V7X = 64 * 1024 * 1024


def _cp(sem, vmem_mb):
    assert vmem_mb * 2**20 < VMEM_BYTES_V7X
    return pltpu.CompilerParams(dimension_semantics=sem, vmem_limit_bytes=vmem_mb * 2**20)


def _nt_dot(a, b):
    return lax.dot_general(a, b, (((1,), (1,)), ((), ())), preferred_element_type=F32)


def _dot(a, b):
    return jnp.dot(a, b, preferred_element_type=F32)


def _split_bf16(a):
    hi = a.astype(BF16)
    lo = (a - hi.astype(F32)).astype(BF16)
    return hi, lo


def _pack_pairs(a):
    n = a.shape[1] // 2
    hi = pltpu.bitcast(a[:, :n].astype(BF16).astype(F32), jnp.uint32)
    lo = pltpu.bitcast(a[:, n:].astype(BF16).astype(F32), jnp.uint32)
    return hi | (lo >> 16)


def _unpack_pairs(u):
    hi = pltpu.bitcast(u & jnp.uint32(0xFFFF0000), F32)
    lo = pltpu.bitcast(u << 16, F32)
    return jnp.concatenate([hi, lo], axis=1)


def _layer_norm(z, g, b):
    mu = jnp.mean(z, axis=-1, keepdims=True)
    d = z - mu
    var = jnp.mean(d * d, axis=-1, keepdims=True)
    return d * lax.rsqrt(var + LN_EPS) * g + b


def _ada_kernel(c_ref, w_ref, b_ref, o_ref):
    c = c_ref[...]
    s_hi, s_lo = _split_bf16(c * jax.nn.sigmoid(c))
    w_hi, w_lo = _split_bf16(w_ref[0])
    o_ref[0] = _dot(s_hi, w_hi) + _dot(s_lo, w_hi) + _dot(s_hi, w_lo) + b_ref[0]


def _ada_all(c, ada_w, ada_b):
    L, two, D, D3 = ada_w.shape
    B = c.shape[0]
    n = L * two
    tn = 1024
    return pl.pallas_call(
        _ada_kernel,
        grid=(n, D3 // tn),
        in_specs=[pl.BlockSpec((B, D), lambda i, j: (0, 0)),
                  pl.BlockSpec((1, D, tn), lambda i, j: (i, 0, j)),
                  pl.BlockSpec((1, 1, tn), lambda i, j: (i, 0, j))],
        out_specs=pl.BlockSpec((1, B, tn), lambda i, j: (i, 0, j)),
        out_shape=jax.ShapeDtypeStruct((n, B, D3), F32),
        compiler_params=_cp(("arbitrary", "arbitrary"), 32),
        name="ada",
    )(c, ada_w.reshape(n, D, D3), ada_b.reshape(n, 1, D3))


def _route_tokens(x1, sc, sh, whi, wlo, bcol, tri, base_sc):
    h = x1 * (1.0 + sc) + sh
    h_hi, h_lo = _split_bf16(h)
    logit = _nt_dot(whi, h_hi) + _nt_dot(whi, h_lo) + _nt_dot(wlo, h_hi) + bcol
    tm = logit.shape[1]
    epg = MOE_EPG
    row = lax.broadcasted_iota(jnp.int32, (epg, 1), 0)
    row_f = row.astype(F32)

    def first_max(vals):
        mx = jnp.max(vals, axis=0, keepdims=True)
        idx = jnp.min(jnp.where(vals == mx, row_f, float(epg)), axis=0, keepdims=True)
        return mx, idx

    gl = jnp.where(row < MOE_GROUPS, logit[0:epg], -jnp.inf)
    gmax, g_idx = first_max(gl)
    g_w = 1.0 / jnp.sum(jnp.exp(gl - gmax), axis=0, keepdims=True)
    el = logit[epg:2 * epg]
    for g in range(1, MOE_GROUPS):
        el = jnp.where(g_idx == float(g), logit[epg * (g + 1):epg * (g + 2)], el)
    ep = jnp.exp(el - jnp.max(el, axis=0, keepdims=True))
    prob = ep / jnp.sum(ep, axis=0, keepdims=True)
    p1, i1 = first_max(prob)
    p2, i2 = first_max(jnp.where(row_f == i1, -1.0, prob))
    e1 = g_idx * float(epg) + i1
    e2 = g_idx * float(epg) + i2
    w1 = g_w * p1 / (p1 + p2)
    w2 = g_w * p2 / (p1 + p2)

    erow = lax.broadcasted_iota(jnp.int32, (MOE_E, 1), 0).astype(F32)
    hot1 = erow == e1
    hot2 = erow == e2
    cnt = jnp.where(hot1 | hot2, 1.0, 0.0)
    before = base_sc[:, 0:1] + _dot(cnt.astype(BF16), tri)
    r1 = jnp.sum(jnp.where(hot1, before, 0.0), axis=0, keepdims=True)
    r2 = jnp.sum(jnp.where(hot2, before, 0.0), axis=0, keepdims=True)
    base_sc[...] += jnp.sum(cnt, axis=1, keepdims=True)

    irow = lax.broadcasted_iota(jnp.int32, (INFO_ROWS, 1), 0)
    info = jnp.zeros((INFO_ROWS, tm), F32)
    for k, val in enumerate((e1, e2, r1, r2, w1, w2)):
        info = jnp.where(irow == k, val, info)
    return _pack_pairs(h_hi), info


def _out_ln_kernel(o_ref, w_ref, x_ref, gate_ref, g_ref, b_ref, sc_ref, sh_ref, whi_ref, wlo_ref, br_ref, tri_ref,
                   out_ref, hb_ref, info_ref, cnt_ref, base_sc):
    first = (pl.program_id(0) == 0) & (pl.program_id(1) == 0)

    @pl.when(first)
    def _():
        base_sc[...] = jnp.zeros(base_sc.shape, F32)

    ts = o_ref.shape[1]
    rows = min(OUT_LN_ROWS, ts)
    ys = [_dot(o_ref[0, 0:rows, :], w_ref[...])]
    for c in range(ts // rows):
        if c + 1 < ts // rows:
            ys.append(_dot(o_ref[0, (c + 1) * rows:(c + 2) * rows, :], w_ref[...]))
        sl = slice(c * rows, (c + 1) * rows)
        z = ALPHA * x_ref[0, sl, :] + gate_ref[0] * ys[c]
        x1 = _layer_norm(z, g_ref[...], b_ref[...])
        out_ref[0, sl, :] = x1
        packed, info = _route_tokens(x1, sc_ref[0], sh_ref[0], whi_ref[...], wlo_ref[...], br_ref[...],
                                     tri_ref[...], base_sc)
        hb_ref[0, sl, :] = packed
        info_ref[0, :, sl] = info
    cnt_ref[...] = base_sc[...]


def _out_ln(o, w_o, x, gate, g, b, route, ts=1024):
    B, S, D = x.shape
    K = o.shape[-1]
    ts = min(ts, S)
    scale2, shift2, w_grp, b_grp, w_rt, b_rt = route
    E, g0 = MOE_E, MOE_EPG
    w_r = jnp.zeros((ROUTER_ROWS, D), F32).at[:MOE_GROUPS].set(w_grp.T).at[g0:g0 + E].set(w_rt.T)
    w_hi = w_r.astype(BF16)
    w_lo = (w_r - w_hi.astype(F32)).astype(BF16)
    b_r = jnp.zeros((ROUTER_ROWS, 1), F32).at[:MOE_GROUPS, 0].set(b_grp).at[g0:g0 + E, 0].set(b_rt)
    tm = min(OUT_LN_ROWS, ts)
    tri = jnp.asarray(np.triu(np.ones((tm, tm), np.float32), 1), BF16)
    const = lambda shape: pl.BlockSpec(shape, lambda bi, i: (0,) * len(shape))
    per_batch = pl.BlockSpec((1, 1, D), lambda bi, i: (bi, 0, 0))
    return pl.pallas_call(
        _out_ln_kernel,
        grid=(B, S // ts),
        in_specs=[pl.BlockSpec((1, ts, K), lambda bi, i: (bi, i, 0)),
                  const((K, D)),
                  pl.BlockSpec((1, ts, D), lambda bi, i: (bi, i, 0)),
                  per_batch, const((1, D)), const((1, D)),
                  per_batch, per_batch,
                  const((ROUTER_ROWS, D)), const((ROUTER_ROWS, D)), const((ROUTER_ROWS, 1)), const((tm, tm))],
        out_specs=[pl.BlockSpec((1, ts, D), lambda bi, i: (bi, i, 0)),
                   pl.BlockSpec((1, ts, D // 2), lambda bi, i: (bi, i, 0)),
                   pl.BlockSpec((1, INFO_ROWS, ts), lambda bi, i: (bi, 0, i)),
                   const((E, LANES))],
        out_shape=[jax.ShapeDtypeStruct((B, S, D), F32),
                   jax.ShapeDtypeStruct((B, S, D // 2), jnp.uint32),
                   jax.ShapeDtypeStruct((B, INFO_ROWS, S), F32),
                   jax.ShapeDtypeStruct((E, LANES), F32)],
        scratch_shapes=[pltpu.VMEM((E, LANES), F32)],
        compiler_params=_cp(("arbitrary", "arbitrary"), 48),
        name="out_ln_route",
    )(o, w_o.astype(BF16), x, gate, g.reshape(1, D), b.reshape(1, D), scale2, shift2, w_hi, w_lo, b_r, tri)


def _rope_tables(positions, dim):
    inv_freq = 1.0 / (ROPE_THETA ** (jnp.arange(0, dim, 2, dtype=F32) / dim))
    ang = positions.astype(F32)[..., None] * inv_freq
    return jnp.cos(ang), jnp.sin(ang)


def _rot_half_cols(w, half):
    return jnp.concatenate([-w[..., half:], w[..., :half]], axis=-1)


def _mla_front_kernel(x_ref, sc_ref, sh_ref, win_ref, gq_ref, gkv_ref, wqt_ref, wk_ref, wvt_ref, t_ref, tt_ref,
                      qt_ref, k_ref, vt_ref, *, scale):
    hb = (x_ref[0] * (1.0 + sc_ref[0]) + sh_ref[0]).astype(BF16)
    r = _dot(hb, win_ref[...])
    tab = t_ref[0]
    tab_t = tt_ref[0]
    low = lax.broadcasted_iota(jnp.int32, tab.shape, 1) < MLA_DR

    def rms(cx, g):
        return (cx * lax.rsqrt(jnp.mean(cx * cx, axis=-1, keepdims=True) + RMS_EPS) * g).astype(BF16)

    cq = rms(r[:, :MLA_QR], gq_ref[...])
    ckv = rms(r[:, MLA_QR:MLA_QR + MLA_KVR], gkv_ref[...])
    t = r[:, MLA_QR + MLA_KVR:] * tab
    k_rope = jnp.where(low, t + pltpu.roll(t, MLA_DR, axis=1), 0.0).astype(BF16)
    zeros = jnp.zeros((MLA_DR, hb.shape[0]), BF16)
    hd = MLA_DN + 2 * MLA_DR
    q_all = _nt_dot(wqt_ref[...], cq)
    k_all = _dot(ckv, wk_ref[...])
    vt_all = _nt_dot(wvt_ref[...], ckv)
    for h in range(MLA_HEADS):
        qh = q_all[h * hd:(h + 1) * hd]
        tq = qh[MLA_DN:] * tab_t
        qt_ref[0, h, :MLA_DN, :] = (qh[:MLA_DN] * scale).astype(BF16)
        qt_ref[0, h, MLA_DN:MLA_DN + MLA_DR, :] = ((tq[:MLA_DR] + tq[MLA_DR:]) * scale).astype(BF16)
        qt_ref[0, h, MLA_DN + MLA_DR:, :] = zeros
        k_ref[0, h, :, :MLA_DN] = k_all[:, h * MLA_DN:(h + 1) * MLA_DN].astype(BF16)
        k_ref[0, h, :, MLA_DN:] = k_rope
        vt_ref[0, h] = vt_all[h * MLA_DV:(h + 1) * MLA_DV].astype(BF16)


def _flash_causal_kernel(qt_ref, k_ref, vt_ref, o_ref, *, t):
    dv, s_len = vt_ref.shape[2], vt_ref.shape[3]
    pairs = [(qi, kt) for qi in range(s_len // t) for kt in range(qi + 1)]
    kpos = lax.broadcasted_iota(jnp.int32, (t, 1), 0)
    qpos = lax.broadcasted_iota(jnp.int32, (1, t), 1)
    diag = kpos <= qpos

    def scores(j):
        qi, kt = pairs[j]
        return _dot(k_ref[0, 0, kt * t:(kt + 1) * t, :], qt_ref[0, 0, :, qi * t:(qi + 1) * t])

    def value_matmul(pending, l, acc):
        qi, kt, p, a = pending
        acc = a * acc + _dot(vt_ref[0, 0, :, kt * t:(kt + 1) * t], p)
        if kt == qi:
            o_ref[0, qi * t:(qi + 1) * t, :] = (acc / l).T.astype(o_ref.dtype)
        return acc

    s_cur, pending = scores(0), None
    m = l = acc = None
    for j, (qi, kt) in enumerate(pairs):
        s_next = scores(j + 1) if j + 1 < len(pairs) else None
        if pending is not None:
            acc = value_matmul(pending, l, acc)
        if kt == 0:
            m, l, acc = jnp.full((1, t), NEG, F32), jnp.zeros((1, t), F32), jnp.zeros((dv, t), F32)
        s = jnp.where(diag, s_cur, NEG) if kt == qi else s_cur
        m_new = jnp.maximum(m, jnp.max(s, axis=0, keepdims=True))
        p = jnp.exp2(s - m_new)
        a = jnp.exp2(m - m_new)
        l = a * l + jnp.sum(p, axis=0, keepdims=True)
        m = m_new
        pending = (qi, kt, p.astype(BF16), a)
        s_cur = s_next
    value_matmul(pending, l, acc)


def _mla_layer(x, shift, scale_v, gate, cos, sin, w_in, g_q, w_uq, g_kv, w_ukv, w_o, ln_g, ln_b, route):
    B, S, D = x.shape
    H, dn, dr, dv = MLA_HEADS, MLA_DN, MLA_DR, MLA_DV
    half = dr // 2
    kr = w_in[:, MLA_QR + MLA_KVR:]
    win_p = jnp.concatenate([w_in, _rot_half_cols(kr, half)], axis=1).astype(BF16)
    wq = w_uq.reshape(MLA_QR, H, dn + dr)
    wq = jnp.concatenate([wq, _rot_half_cols(wq[..., dn:], half)], axis=-1)
    wqt = jnp.transpose(wq, (1, 2, 0)).reshape(H * (dn + 2 * dr), MLA_QR).astype(BF16)
    wkv = w_ukv.reshape(MLA_KVR, H, dn + dv)
    wk = wkv[..., :dn].reshape(MLA_KVR, H * dn).astype(BF16)
    wvt = jnp.transpose(wkv[..., dn:], (1, 2, 0)).reshape(H * dv, MLA_KVR).astype(BF16)
    tab = jnp.concatenate([cos, cos, sin, sin], axis=-1)
    tab_t = jnp.swapaxes(tab, 1, 2)
    ts = min(512, S)
    nw = win_p.shape[1]
    hd = dn + 2 * dr
    qt, k, vt = pl.pallas_call(
        functools.partial(_mla_front_kernel, scale=LOG2E * (dn + dr) ** -0.5),
        grid=(B, S // ts),
        in_specs=[pl.BlockSpec((1, ts, D), lambda b, i: (b, i, 0)),
                  pl.BlockSpec((1, 1, D), lambda b, i: (b, 0, 0)),
                  pl.BlockSpec((1, 1, D), lambda b, i: (b, 0, 0)),
                  pl.BlockSpec((D, nw), lambda b, i: (0, 0)),
                  pl.BlockSpec((1, MLA_QR), lambda b, i: (0, 0)),
                  pl.BlockSpec((1, MLA_KVR), lambda b, i: (0, 0)),
                  pl.BlockSpec((H * hd, MLA_QR), lambda b, i: (0, 0)),
                  pl.BlockSpec((MLA_KVR, H * dn), lambda b, i: (0, 0)),
                  pl.BlockSpec((H * dv, MLA_KVR), lambda b, i: (0, 0)),
                  pl.BlockSpec((1, ts, LANES), lambda b, i: (b, i, 0)),
                  pl.BlockSpec((1, LANES, ts), lambda b, i: (b, 0, i))],
        out_specs=[pl.BlockSpec((1, H, hd, ts), lambda b, i: (b, 0, 0, i)),
                   pl.BlockSpec((1, H, ts, hd), lambda b, i: (b, 0, i, 0)),
                   pl.BlockSpec((1, H, dv, ts), lambda b, i: (b, 0, 0, i))],
        out_shape=[jax.ShapeDtypeStruct((B, H, hd, S), BF16),
                   jax.ShapeDtypeStruct((B, H, S, hd), BF16),
                   jax.ShapeDtypeStruct((B, H, dv, S), BF16)],
        compiler_params=_cp(("arbitrary", "arbitrary"), 48),
        name="mla_front",
    )(x, scale_v, shift, win_p, g_q.reshape(1, -1), g_kv.reshape(1, -1), wqt, wk, wvt, tab, tab_t)

    t = min(512, S)
    o = pl.pallas_call(
        functools.partial(_flash_causal_kernel, t=t),
        grid=(B, H),
        in_specs=[pl.BlockSpec((1, 1, hd, S), lambda b, h: (b, h, 0, 0)),
                  pl.BlockSpec((1, 1, S, hd), lambda b, h: (b, h, 0, 0)),
                  pl.BlockSpec((1, 1, dv, S), lambda b, h: (b, h, 0, 0))],
        out_specs=pl.BlockSpec((1, S, dv), lambda b, h: (b, 0, h)),
        out_shape=jax.ShapeDtypeStruct((B, S, H * dv), BF16),
        compiler_params=_cp(("arbitrary", "arbitrary"), 48),
        name="mla_attn",
    )(qt, k, vt)
    return _out_ln(o, w_o, x, gate, ln_g, ln_b, route)


def _proj_kernel(*refs, segs, chunk, has_rope):
    x_ref, sc_ref, sh_ref, w_ref, wt_ref = refs[:5]
    n_in = 9 if has_rope else 5
    outs = refs[n_in:]
    hb = (x_ref[0] * (1.0 + sc_ref[0]) + sh_ref[0]).astype(BF16)
    if has_rope:
        cos, sin = refs[5][0], refs[6][0]
        cos_t, sin_t = refs[7][0], refs[8][0]
    half = LANES // 2
    for (c0, width, kind, scale, transposed), o_ref in zip(segs, outs):
        for j in range(0, width, chunk):
            cw = min(chunk, width - j)
            if transposed:
                r = _nt_dot(wt_ref[c0 + j:c0 + j + cw, :], hb)
            else:
                r = _dot(hb, w_ref[:, c0 + j:c0 + j + cw])
            if kind == "rope":
                parts = []
                for t in range(0, cw, LANES):
                    if transposed:
                        u = r[t:t + LANES]
                        parts.append(u * cos_t + pltpu.roll(u, half, axis=0) * sin_t)
                    else:
                        u = r[:, t:t + LANES]
                        parts.append(u * cos + pltpu.roll(u, half, axis=1) * sin)
                r = jnp.concatenate(parts, axis=0 if transposed else 1) if len(parts) > 1 else parts[0]
            elif kind == "sigmoid":
                r = jax.nn.sigmoid(r)
            if scale != 1.0:
                r = r * scale
            if transposed:
                o_ref[0, j:j + cw, :] = r.astype(o_ref.dtype)
            else:
                o_ref[0, :, j:j + cw] = r.astype(o_ref.dtype)


def _mod_proj(x, shift, scale_v, w, wt, segs, out_dtypes, rope=None, ts=512, chunk=256):
    B, S, D = x.shape
    ts = min(ts, S)
    in_specs = [pl.BlockSpec((1, ts, D), lambda b, i: (b, i, 0)),
                pl.BlockSpec((1, 1, D), lambda b, i: (b, 0, 0)),
                pl.BlockSpec((1, 1, D), lambda b, i: (b, 0, 0)),
                pl.BlockSpec(w.shape, lambda b, i: (0, 0)),
                pl.BlockSpec(wt.shape, lambda b, i: (0, 0))]
    args = [x, scale_v, shift, w, wt]
    if rope is not None:
        cos, sin = rope
        in_specs += [pl.BlockSpec((1, ts, LANES), lambda b, i: (b, i, 0))] * 2
        in_specs += [pl.BlockSpec((1, LANES, ts), lambda b, i: (b, 0, i))] * 2
        args += [cos, sin, jnp.swapaxes(cos, 1, 2), jnp.swapaxes(sin, 1, 2)]
    out_specs, out_shape = [], []
    for sg, dt in zip(segs, out_dtypes):
        if sg[4]:
            out_specs.append(pl.BlockSpec((1, sg[1], ts), lambda b, i: (b, 0, i)))
            out_shape.append(jax.ShapeDtypeStruct((B, sg[1], S), dt))
        else:
            out_specs.append(pl.BlockSpec((1, ts, sg[1]), lambda b, i: (b, i, 0)))
            out_shape.append(jax.ShapeDtypeStruct((B, S, sg[1]), dt))
    return pl.pallas_call(
        functools.partial(_proj_kernel, segs=tuple(segs), chunk=chunk, has_rope=rope is not None),
        grid=(B, S // ts),
        in_specs=in_specs,
        out_specs=out_specs,
        out_shape=out_shape,
        compiler_params=_cp(("arbitrary", "arbitrary"), 48),
        name="mod_proj",
    )(*args)


def _nsa_cmp_kernel(kc_ref, vc_ref, pek_ref, pev_ref, wk1_ref, wk2_ref, wv1_ref, wv2t_ref, ko_ref, vto_ref):
    half = NSA_CMP_BLOCK // 2
    nchunk = kc_ref.shape[1]

    def hidden(src, pe_ref, w1_ref, kh):
        a = jnp.zeros((nchunk, w1_ref.shape[2]), F32)
        bm = jnp.zeros((nchunk, w1_ref.shape[2]), F32)
        for l in range(half):
            c0 = (l * NSA_KVH + kh) * NSA_DH
            t = src[0, :, c0:c0 + NSA_DH].astype(F32)
            a = a + _dot((t + pe_ref[l:l + 1, :]).astype(BF16), w1_ref[l])
            bm = bm + _dot((t + pe_ref[half + l:half + l + 1, :]).astype(BF16), w1_ref[half + l])
        hid = a + pltpu.roll(bm, nchunk - 1, axis=0)
        return (hid * jax.nn.sigmoid(hid)).astype(BF16)

    rowi = lax.broadcasted_iota(jnp.int32, (nchunk, 1), 0)
    coli = lax.broadcasted_iota(jnp.int32, (1, nchunk), 1)
    for kh in range(NSA_KVH):
        out = _dot(hidden(kc_ref, pek_ref, wk1_ref, kh), wk2_ref[...])
        ko_ref[0, kh] = jnp.where(rowi < nchunk - 1, out, 0.0).astype(ko_ref.dtype)
        out_t = _nt_dot(wv2t_ref[...], hidden(vc_ref, pev_ref, wv1_ref, kh))
        vto_ref[0, kh] = jnp.where(coli < nchunk - 1, out_t, 0.0).astype(vto_ref.dtype)


def _nsa_attn_kernel(qt_ref, kc_ref, vct_ref, ks_ref, vst_ref, kw_ref, vwt_ref, gt_ref, ovt_ref, e_ref, o_ref,
                     q4_sc, m_sc, l_sc, acc_sc, *, tq, tk, span, n_slc):
    qi = pl.program_id(2)
    G = NSA_G
    for g in range(G):
        q4_sc[:, g * tq:(g + 1) * tq] = qt_ref[0, g * NSA_DH:(g + 1) * NSA_DH, :]
    q4 = q4_sc[...]
    qpos = qi * tq + lax.broadcasted_iota(jnp.int32, (1, tq), 1)

    def tile_g(a):
        return jnp.concatenate([a] * G, axis=1)

    qpos4 = tile_g(qpos)

    kstart = pl.multiple_of(jnp.maximum(qi * tq + tq - span, 0), tq)
    s_win = _dot(kw_ref[0, pl.ds(kstart, span), :], q4)

    n_cmp = kc_ref.shape[2]
    nid = lax.broadcasted_iota(jnp.int32, (n_cmp, 1), 0)
    cmask = (nid * NSA_CMP_STRIDE + (NSA_CMP_BLOCK - 1) <= qpos4) & (nid < n_cmp - 1)
    s = jnp.where(cmask, _dot(kc_ref[0, 0], q4), NEG)
    p = jnp.where(cmask, jnp.exp2(s - jnp.max(s, axis=0, keepdims=True)), 0.0)
    l = jnp.sum(p, axis=0, keepdims=True)
    p_cmp = p / jnp.where(l > 0.0, l, 1.0)
    o_cmp = _dot(vct_ref[0, 0], p_cmp.astype(BF16))

    diff = qpos - (kstart + lax.broadcasted_iota(jnp.int32, (span, 1), 0))
    s_win = s_win + tile_g(jnp.where((diff >= 0) & (diff < NSA_WINDOW), 0.0, NEG))
    p = jnp.exp2(s_win - jnp.max(s_win, axis=0, keepdims=True))
    o_win = _dot(vwt_ref[0, :, pl.ds(kstart, span)], p.astype(BF16)) / jnp.sum(p, axis=0, keepdims=True)

    psum = p_cmp[:, 0:tq]
    for g in range(1, G):
        psum = psum + p_cmp[:, g * tq:(g + 1) * tq]
    p_hi, p_lo = _split_bf16(psum)
    imp = _dot(ovt_ref[...], p_hi) + _dot(ovt_ref[...], p_lo)
    nb = ovt_ref.shape[0]
    blk = lax.broadcasted_iota(jnp.int32, (nb, 1), 0)
    cur = jnp.right_shift(qpos, SLC_SHIFT)
    forced = (blk == 0) | (blk == cur) | (blk == cur - 1)
    valid = (blk * NSA_SLC_BLOCK <= qpos) & (blk < n_slc)
    work = jnp.where(valid, imp + jnp.where(forced, NSA_BONUS, 0.0), -jnp.inf)
    beaten = jnp.zeros((nb, tq), F32)
    for j in range(n_slc):
        wj = work[j:j + 1, :]
        beaten = beaten + jnp.where((wj > work) | ((wj == work) & (blk > j)), 1.0, 0.0)
    sel = jnp.where(beaten < float(min(NSA_TOPK, n_slc)), 1.0, 0.0).astype(BF16)

    m_sc[...] = jnp.full(m_sc.shape, NEG, F32)
    l_sc[...] = jnp.zeros(l_sc.shape, F32)
    acc_sc[...] = jnp.zeros(acc_sc.shape, F32)
    n_kt = ((qi + 1) * tq + tk - 1) // tk

    def slc_tiles(n):
        def scores(kt):
            return _dot(ks_ref[0, kt * tk:(kt + 1) * tk, :], q4)

        def value_matmul(kt, p, a):
            acc_sc[...] = a * acc_sc[...] + _dot(vst_ref[0, :, kt * tk:(kt + 1) * tk], p)

        s, pending = scores(0), None
        chosen = _dot(e_ref[0:n * tk, :], sel)
        kpos = lax.broadcasted_iota(jnp.int32, (n * tk, 1), 0)
        bias_all = jnp.where((chosen > 0.5) & (kpos <= qpos), 0.0, NEG)
        for kt in range(n):
            s_next = scores(kt + 1) if kt + 1 < n else None
            if pending is not None:
                value_matmul(*pending)
            s = s + tile_g(bias_all[kt * tk:(kt + 1) * tk])
            m_prev = m_sc[...]
            m_new = jnp.maximum(m_prev, jnp.max(s, axis=0, keepdims=True))
            p = jnp.exp2(s - m_new)
            a = jnp.exp2(m_prev - m_new)
            l_sc[...] = a * l_sc[...] + jnp.sum(p, axis=0, keepdims=True)
            m_sc[...] = m_new
            pending = (kt, p.astype(BF16), a)
            s = s_next
        value_matmul(*pending)

    for n in range(1, (ks_ref.shape[1] + tk - 1) // tk + 1):
        pl.when(n_kt == n)(functools.partial(slc_tiles, n))
    o_slc = acc_sc[...] / l_sc[...]

    gt = gt_ref[0]
    for g in range(G):
        cols = slice(g * tq, (g + 1) * tq)
        o = (gt[3 * g:3 * g + 1] * o_cmp[:, cols] + gt[3 * g + 1:3 * g + 2] * o_slc[:, cols]
             + gt[3 * g + 2:3 * g + 3] * o_win[:, cols])
        o_ref[0, :, g * NSA_DH:(g + 1) * NSA_DH] = o.T.astype(o_ref.dtype)


def _nsa_layer(x, shift, scale_v, gate, cos, sin, w_in, pe_k, pe_v, w_ck1, w_ck2, w_cv1, w_cv2, w_o,
               ln_g, ln_b, route):
    B, S, D = x.shape
    H, KH, Dh, G = NSA_HEADS, NSA_KVH, NSA_DH, NSA_G
    kvw = KH * Dh
    cuts = [int(v) for v in np.cumsum([H * Dh] + [kvw] * 6)]
    wq, wkc, wvc, wks, wvs, wkw, wvw, wgl = jnp.split(w_in, cuts, axis=1)
    wgl = jnp.pad(wgl.reshape(D, KH, G * 3), ((0, 0), (0, 0), (0, GATE_ROWS - G * 3))).reshape(D, KH * GATE_ROWS)
    w = jnp.concatenate([wkc, wks, wkw, wvc], axis=1).astype(BF16)
    wt = jnp.concatenate([wq, wvs, wvw, wgl], axis=1).T.astype(BF16)
    segs = [(0, H * Dh, "rope", LOG2E * Dh ** -0.5, True),
            (0, kvw, "rope", 1.0, False), (kvw, kvw, "rope", 1.0, False), (2 * kvw, kvw, "rope", 1.0, False),
            (3 * kvw, kvw, "plain", 1.0, False),
            (H * Dh, kvw, "plain", 1.0, True), (H * Dh + kvw, kvw, "plain", 1.0, True),
            (H * Dh + 2 * kvw, KH * GATE_ROWS, "sigmoid", 1.0, True)]
    rope = (jnp.concatenate([cos, cos], axis=-1), jnp.concatenate([-sin, sin], axis=-1))
    qt, kc, ks, kw, vc, vst, vwt, gates_t = _mod_proj(
        x, shift, scale_v, w, wt, segs, [BF16] * 7 + [F32], rope=rope)

    st = NSA_CMP_STRIDE
    nchunk = S // st
    kc2 = kc.reshape(B, nchunk, st * kvw)
    vc2 = vc.reshape(B, nchunk, st * kvw)
    hid = w_ck1.shape[-1]
    full = lambda shape: pl.BlockSpec(shape, lambda b: (0,) * len(shape))
    k_cmp, v_cmp_t = pl.pallas_call(
        _nsa_cmp_kernel,
        grid=(B,),
        in_specs=[pl.BlockSpec((1, nchunk, st * kvw), lambda b: (b, 0, 0)),
                  pl.BlockSpec((1, nchunk, st * kvw), lambda b: (b, 0, 0)),
                  full((NSA_CMP_BLOCK, Dh)), full((NSA_CMP_BLOCK, Dh)),
                  full((NSA_CMP_BLOCK, Dh, hid)), full((hid, Dh)),
                  full((NSA_CMP_BLOCK, Dh, hid)), full((Dh, hid))],
        out_specs=[pl.BlockSpec((1, KH, nchunk, Dh), lambda b: (b, 0, 0, 0)),
                   pl.BlockSpec((1, KH, Dh, nchunk), lambda b: (b, 0, 0, 0))],
        out_shape=[jax.ShapeDtypeStruct((B, KH, nchunk, Dh), BF16),
                   jax.ShapeDtypeStruct((B, KH, Dh, nchunk), BF16)],
        compiler_params=_cp(("arbitrary",), 32),
        name="nsa_cmp",
    )(kc2, vc2, pe_k, pe_v, w_ck1.astype(BF16), w_ck2.astype(BF16), w_cv1.astype(BF16), w_cv2.T.astype(BF16))

    n_slc = S // NSA_SLC_BLOCK
    nb = 32
    assert n_slc <= nb and nchunk % 8 == 0
    cmp_start = np.arange(nchunk) * st
    slc_start = np.arange(n_slc) * NSA_SLC_BLOCK
    ov = np.clip(np.minimum(cmp_start[:, None] + NSA_CMP_BLOCK, slc_start[None, :] + NSA_SLC_BLOCK)
                 - np.maximum(cmp_start[:, None], slc_start[None, :]), 0, None).astype(np.float32) / NSA_CMP_BLOCK
    ov[nchunk - 1:] = 0.0
    ov_t = np.zeros((nb, nchunk), np.float32)
    ov_t[:n_slc] = ov.T
    key_blk = np.zeros((S, nb), np.float32)
    key_blk[np.arange(S), np.arange(S) // NSA_SLC_BLOCK] = 1.0

    tq = min(256, S)
    tk = min(256, S)
    span = min(NSA_WINDOW + tq, S)
    k_spec = pl.BlockSpec((1, S, Dh), lambda b, kh, i: (b, 0, kh))
    vt_spec = pl.BlockSpec((1, Dh, S), lambda b, kh, i: (b, kh, 0))
    o = pl.pallas_call(
        functools.partial(_nsa_attn_kernel, tq=tq, tk=tk, span=span, n_slc=n_slc),
        grid=(B, KH, S // tq),
        in_specs=[pl.BlockSpec((1, G * Dh, tq), lambda b, kh, i: (b, kh, i)),
                  pl.BlockSpec((1, 1, nchunk, Dh), lambda b, kh, i: (b, kh, 0, 0)),
                  pl.BlockSpec((1, 1, Dh, nchunk), lambda b, kh, i: (b, kh, 0, 0)),
                  k_spec, vt_spec, k_spec, vt_spec,
                  pl.BlockSpec((1, GATE_ROWS, tq), lambda b, kh, i: (b, kh, i)),
                  pl.BlockSpec((nb, nchunk), lambda b, kh, i: (0, 0)),
                  pl.BlockSpec((S, nb), lambda b, kh, i: (0, 0))],
        out_specs=pl.BlockSpec((1, tq, G * Dh), lambda b, kh, i: (b, i, kh)),
        out_shape=jax.ShapeDtypeStruct((B, S, H * Dh), BF16),
        scratch_shapes=[pltpu.VMEM((Dh, G * tq), BF16), pltpu.VMEM((1, G * tq), F32),
                        pltpu.VMEM((1, G * tq), F32), pltpu.VMEM((Dh, G * tq), F32)],
        compiler_params=_cp(("arbitrary", "arbitrary", "arbitrary"), 32),
        name="nsa_attn",
    )(qt, k_cmp, v_cmp_t, ks, vst, kw, vwt, gates_t, jnp.asarray(ov_t, BF16), jnp.asarray(key_blk, BF16))
    return _out_ln(o, w_o, x, gate, ln_g, ln_b, route)


def _sb_attn_kernel(qt_ref, k_ref, vt_ref, u_ref, o_ref, r_sc, acc_sc, *, tq, tk):
    d, s_len = vt_ref.shape[1], vt_ref.shape[2]
    u = u_ref[...]
    nd = tq // tk

    def weights(z, strict):
        lk = -(jnp.maximum(z, 0.0) + jnp.log2(1.0 + jnp.exp2(-jnp.abs(z))))
        if strict is not None:
            lk = jnp.where(strict, lk, 0.0)
        between = r_sc[...] + _dot(u, lk.astype(BF16))
        a = jnp.exp2(lk + z + between)
        if strict is not None:
            a = jnp.where(strict, a, 0.0)
        r_sc[...] += jnp.sum(lk, axis=0, keepdims=True)
        return a.astype(BF16)

    for qi in range(s_len // tq):
        qt = qt_ref[0, :, qi * tq:(qi + 1) * tq]
        r_sc[...] = jnp.zeros(r_sc.shape, F32)
        acc_sc[...] = jnp.zeros(acc_sc.shape, F32)
        last = (qi + 1) * nd - 1
        tiles = [last - i for i in range(nd)] + ([qi * nd - 1] if qi > 0 else [])
        qpos = qi * tq + lax.broadcasted_iota(jnp.int32, (1, tq), 1)

        def scores(kt):
            return _dot(k_ref[0, kt * tk:(kt + 1) * tk, :], qt)

        z, pending = scores(tiles[0]), None
        for i, kt in enumerate(tiles):
            nxt = tiles[i + 1] if i + 1 < len(tiles) else max(qi * nd - 2, 0)
            z_next = scores(nxt)
            if pending is not None:
                acc_sc[...] += _dot(vt_ref[0, :, pending[0] * tk:(pending[0] + 1) * tk], pending[1])
            strict = (kt * tk + lax.broadcasted_iota(jnp.int32, (tk, 1), 0) < qpos) if kt >= qi * nd else None
            pending = (kt, weights(z, strict))
            z = z_next
        acc_sc[...] += _dot(vt_ref[0, :, pending[0] * tk:(pending[0] + 1) * tk], pending[1])

        if qi * nd - 2 >= 0:
            def scores_dyn(kt):
                return _dot(k_ref[0, pl.ds(pl.multiple_of(kt * tk, tk), tk), :], qt)

            def cond(carry):
                kt, _ = carry
                return (kt >= 0) & (jnp.max(r_sc[...]) > SB_EXIT_LOG2)

            def body(carry):
                kt, z = carry
                z_next = scores_dyn(jnp.maximum(kt - 1, 0))
                a = weights(z, None)
                acc_sc[...] += _dot(vt_ref[0, :, pl.ds(pl.multiple_of(kt * tk, tk), tk)], a)
                return kt - 1, z_next

            lax.while_loop(cond, body, (qi * nd - 2, z))
        o_ref[0, qi * tq:(qi + 1) * tq, :] = acc_sc[...].T.astype(o_ref.dtype)


def _sb_layer(x, shift, scale_v, gate, w_qkv, w_o, ln_g, ln_b, route):
    B, S, D = x.shape
    H, d = SB_HEADS, SB_DH
    hw = H * d
    w = w_qkv[:, hw:2 * hw].astype(BF16)
    wt = jnp.concatenate([w_qkv[:, :hw], w_qkv[:, 2 * hw:]], axis=1).T.astype(BF16)
    segs = [(0, hw, "plain", LOG2E * d ** -0.5, True), (0, hw, "plain", 1.0, False), (hw, hw, "plain", 1.0, True)]
    qt, k, vt = _mod_proj(x, shift, scale_v, w, wt, segs, [BF16] * 3)
    tq = min(512, S)
    tk = min(256, S)
    tri = np.triu(np.ones((tk, tk), np.float32), 1)
    o = pl.pallas_call(
        functools.partial(_sb_attn_kernel, tq=tq, tk=tk),
        grid=(B, H),
        in_specs=[pl.BlockSpec((1, d, S), lambda b, h: (b, h, 0)),
                  pl.BlockSpec((1, S, d), lambda b, h: (b, 0, h)),
                  pl.BlockSpec((1, d, S), lambda b, h: (b, h, 0)),
                  pl.BlockSpec((tk, tk), lambda b, h: (0, 0))],
        out_specs=pl.BlockSpec((1, S, d), lambda b, h: (b, 0, h)),
        out_shape=jax.ShapeDtypeStruct((B, S, hw), BF16),
        scratch_shapes=[pltpu.VMEM((1, tq), F32), pltpu.VMEM((d, tq), F32)],
        compiler_params=_cp(("arbitrary", "arbitrary"), 32),
        name="sb_attn",
    )(qt, k, vt, jnp.asarray(tri, BF16))
    return _out_ln(o, w_o, x, gate, ln_g, ln_b, route)


def _ffn_kernel(be_ref, seg_ref, nv_ref, nu_ref, x_ref, wg_ref, wu_ref, wd_ref, y_ref, wg_sc, wu_sc, wd_sc, act_sc):
    i = pl.program_id(0)
    n_blk = be_ref.shape[0]

    def stage(blk, src_ref, dst_sc):
        r = jnp.clip(blk, 0, n_blk - 1)
        first = (blk >= 0) & (blk < nu_ref[0]) & ((r == 0) | (be_ref[r] != be_ref[jnp.maximum(r - 1, 0)]))

        @pl.when(first)
        def _():
            dst_sc[seg_ref[r] % FFN_SLOTS] = src_ref[0, 0].astype(BF16)

    stage(i, wg_ref, wg_sc)
    stage(i - 1, wu_ref, wu_sc)
    stage(i - 2, wd_ref, wd_sc)
    blk = i - FFN_LEAD
    done = blk - 1
    has_new = (blk >= 0) & (blk < nu_ref[0])
    has_done = (done >= 0) & (done < nu_ref[0])
    slot_new = seg_ref[jnp.clip(blk, 0, n_blk - 1)] % FFN_SLOTS
    slot_done = seg_ref[jnp.clip(done, 0, n_blk - 1)] % FFN_SLOTS

    def rows_in():
        live = lax.broadcasted_iota(jnp.int32, (x_ref.shape[0], 1), 0) < nv_ref[jnp.clip(blk, 0, n_blk - 1)]
        return jnp.where(live, _unpack_pairs(x_ref[...]), 0.0).astype(BF16)

    act_in, act_out = (i + 1) % 2, i % 2

    @pl.when(has_new & has_done)
    def _():
        xb = rows_in()
        g = _dot(xb, wg_sc[slot_new])
        u = _dot(xb, wu_sc[slot_new])
        y_ref[...] = _pack_pairs(_dot(act_sc[act_in], wd_sc[slot_done]))
        act_sc[act_out] = (g * jax.nn.sigmoid(g) * u).astype(BF16)

    @pl.when(has_new & jnp.logical_not(has_done))
    def _():
        xb = rows_in()
        g = _dot(xb, wg_sc[slot_new])
        u = _dot(xb, wu_sc[slot_new])
        act_sc[act_out] = (g * jax.nn.sigmoid(g) * u).astype(BF16)

    @pl.when(has_done & jnp.logical_not(has_new))
    def _():
        y_ref[...] = _pack_pairs(_dot(act_sc[act_in], wd_sc[slot_done]))

    @pl.when((done >= nu_ref[0]) & (done < n_blk))
    def _():
        y_ref[...] = jnp.zeros(y_ref.shape, y_ref.dtype)


def _sc_gather(table, idx):
    n, d = idx.shape[0], table.shape[1]
    workers = SC_CORES * SC_SUBCORES
    per_w = n // workers
    assert per_w * workers == n and per_w % (2 * SC_CHUNK) == 0
    mesh = plsc.VectorSubcoreMesh(core_axis_name="c", subcore_axis_name="s")

    @functools.partial(
        pl.kernel, mesh=mesh,
        out_type=jax.ShapeDtypeStruct((n, d), table.dtype),
        scratch_types=[pltpu.VMEM((SC_CHUNK,), jnp.int32), pltpu.VMEM((SC_CHUNK,), jnp.int32),
                       pltpu.VMEM((SC_CHUNK, d), table.dtype), pltpu.VMEM((SC_CHUNK, d), table.dtype)]
                      + [pltpu.SemaphoreType.DMA] * 4,
        name="sc_gather")
    def gather(table_hbm, idx_hbm, out_hbm, idx_a, idx_b, rows_a, rows_b, sem_a, sem_b, sem_c, sem_d):
        wid = lax.axis_index("s") * SC_CORES + lax.axis_index("c")

        @pl.loop(0, per_w // (2 * SC_CHUNK))
        def _(j):
            base_a = pl.multiple_of(wid * per_w + j * (2 * SC_CHUNK), SC_CHUNK)
            base_b = pl.multiple_of(base_a + SC_CHUNK, SC_CHUNK)
            pltpu.sync_copy(idx_hbm.at[pl.ds(base_a, SC_CHUNK)], idx_a)
            pltpu.sync_copy(idx_hbm.at[pl.ds(base_b, SC_CHUNK)], idx_b)
            get_a = pltpu.async_copy(table_hbm.at[idx_a], rows_a, sem_a)
            get_b = pltpu.async_copy(table_hbm.at[idx_b], rows_b, sem_b)
            get_a.wait()
            put_a = pltpu.async_copy(rows_a, out_hbm.at[pl.ds(base_a, SC_CHUNK)], sem_c)
            get_b.wait()
            put_b = pltpu.async_copy(rows_b, out_hbm.at[pl.ds(base_b, SC_CHUNK)], sem_d)
            put_a.wait()
            put_b.wait()

    return gather(table, idx)


def _sc_dispatch(src, dest0, dest1, n_out):
    n, d = src.shape
    workers = SC_CORES * SC_SUBCORES
    per_w = n // workers
    assert per_w * workers == n and per_w % SC_CHUNK == 0
    mesh = plsc.VectorSubcoreMesh(core_axis_name="c", subcore_axis_name="s")

    @functools.partial(
        pl.kernel, mesh=mesh,
        out_type=jax.ShapeDtypeStruct((n_out, d), src.dtype),
        scratch_types=[pltpu.VMEM((SC_CHUNK,), jnp.int32),
                       pltpu.VMEM((SC_CHUNK,), jnp.int32),
                       pltpu.VMEM((SC_CHUNK, d), src.dtype)]
                      + [pltpu.SemaphoreType.DMA] * 3,
        name="sc_dispatch")
    def dispatch(src_hbm, d0_hbm, d1_hbm, out_hbm, i0_v, i1_v, rows_v, sem_a, sem_b, sem_c):
        wid = lax.axis_index("s") * SC_CORES + lax.axis_index("c")

        @pl.loop(0, per_w // SC_CHUNK)
        def _(j):
            base = pl.multiple_of(wid * per_w + j * SC_CHUNK, SC_CHUNK)
            get_0 = pltpu.async_copy(d0_hbm.at[pl.ds(base, SC_CHUNK)], i0_v, sem_a)
            get_1 = pltpu.async_copy(d1_hbm.at[pl.ds(base, SC_CHUNK)], i1_v, sem_b)
            get_r = pltpu.async_copy(src_hbm.at[pl.ds(base, SC_CHUNK)], rows_v, sem_c)
            get_0.wait()
            get_1.wait()
            get_r.wait()
            put_0 = pltpu.async_copy(rows_v, out_hbm.at[i0_v], sem_a)
            put_1 = pltpu.async_copy(rows_v, out_hbm.at[i1_v], sem_b)
            put_0.wait()
            put_1.wait()

    return dispatch(src, dest0, dest1)


def _moe_out_kernel(x_ref, y1_ref, y2_ref, info_ref, gate_ref, g_ref, b_ref, o_ref):
    info = info_ref[0]
    ts = info.shape[1]
    info_tok = jnp.concatenate([info, jnp.zeros((LANES - INFO_ROWS, ts), F32)], axis=0).T
    y = info_tok[:, 4:5] * _unpack_pairs(y1_ref[0, 0]) + info_tok[:, 5:6] * _unpack_pairs(y2_ref[0, 0])
    z = ALPHA * x_ref[0] + gate_ref[0] * y
    o_ref[0] = _layer_norm(z, g_ref[...], b_ref[...])


def _moe_out(x, y12, info, gate, g, b, ts=512):
    B, S, D = x.shape
    ts = min(ts, S)
    tok = pl.BlockSpec((1, ts, D), lambda bi, i: (bi, i, 0))
    return pl.pallas_call(
        _moe_out_kernel,
        grid=(B, S // ts),
        in_specs=[tok,
                  pl.BlockSpec((1, 1, ts, D // 2), lambda bi, i: (0, bi, i, 0)),
                  pl.BlockSpec((1, 1, ts, D // 2), lambda bi, i: (1, bi, i, 0)),
                  pl.BlockSpec((1, INFO_ROWS, ts), lambda bi, i: (bi, 0, i)),
                  pl.BlockSpec((1, 1, D), lambda bi, i: (bi, 0, 0)),
                  pl.BlockSpec((1, D), lambda bi, i: (0, 0)),
                  pl.BlockSpec((1, D), lambda bi, i: (0, 0))],
        out_specs=tok,
        out_shape=jax.ShapeDtypeStruct((B, S, D), F32),
        compiler_params=_cp(("arbitrary", "arbitrary"), 40),
        name="moe_out",
    )(x, y12, y12, info, gate, g.reshape(1, D), b.reshape(1, D))


def _moe_layer(x, routed, gate, layer, w_gate, w_up, w_down, ln_g, ln_b):
    B, S, D = x.shape
    T = B * S
    E, R, F = MOE_E, MOE_ROWS, w_gate.shape[-1]
    hb, info, cnt = routed

    counts = cnt[:, 0].astype(jnp.int32)
    padded = (counts + R - 1) // R * R
    ends = jnp.cumsum(padded)
    starts = ends - padded
    expert_ids = jnp.arange(E, dtype=jnp.int32)

    def per_expert(table, expert):
        return jnp.sum(jnp.where(expert[..., None] == expert_ids, table, 0), axis=-1)

    def row_of(expert, rank):
        return (per_expert(starts, expert.astype(jnp.int32)) + rank.astype(jnp.int32)).reshape(T)

    dest0 = row_of(info[:, 0, :], info[:, 2, :])
    dest1 = row_of(info[:, 1, :], info[:, 3, :])
    n_blk = (T * 2) // R + E
    P = n_blk * R
    blk_exp = jnp.minimum(jnp.sum(jnp.arange(n_blk)[:, None] * R >= ends[None, :], axis=1), E - 1).astype(jnp.int32)
    n_used = (ends[-1] // R).astype(jnp.int32).reshape(1)
    x_rows = _sc_dispatch(hb.reshape(T, D // 2), dest0, dest1, P)

    seg = jnp.concatenate([jnp.zeros((1,), jnp.int32),
                           jnp.cumsum((blk_exp[1:] != blk_exp[:-1]).astype(jnp.int32))])

    live = jnp.clip(per_expert(counts, blk_exp) + per_expert(starts, blk_exp) - jnp.arange(n_blk, dtype=jnp.int32) * R,
                    0, R).astype(jnp.int32)

    def blk_at(offset):
        return lambda i: jnp.clip(i - offset, 0, n_blk - 1)

    y_rows = pl.pallas_call(
        _ffn_kernel,
        grid_spec=pltpu.PrefetchScalarGridSpec(
            num_scalar_prefetch=4,
            grid=(n_blk + FFN_LEAD + 1,),
            in_specs=[pl.BlockSpec((R, D // 2), lambda i, be, sg, nv, nu: (blk_at(FFN_LEAD)(i), 0)),
                      pl.BlockSpec((1, 1, D, F), lambda i, be, sg, nv, nu: (layer, be[blk_at(0)(i)], 0, 0)),
                      pl.BlockSpec((1, 1, D, F), lambda i, be, sg, nv, nu: (layer, be[blk_at(1)(i)], 0, 0)),
                      pl.BlockSpec((1, 1, F, D), lambda i, be, sg, nv, nu: (layer, be[blk_at(2)(i)], 0, 0))],
            out_specs=pl.BlockSpec((R, D // 2), lambda i, be, sg, nv, nu: (blk_at(FFN_LEAD + 1)(i), 0)),
            scratch_shapes=[pltpu.VMEM((FFN_SLOTS, D, F), BF16), pltpu.VMEM((FFN_SLOTS, D, F), BF16),
                            pltpu.VMEM((FFN_SLOTS, F, D), BF16), pltpu.VMEM((2, R, F), BF16)]),
        out_shape=jax.ShapeDtypeStruct((P, D // 2), jnp.uint32),
        compiler_params=_cp(("arbitrary",), 48),
        name="moe_ffn",
    )(blk_exp, seg, live, n_used, x_rows, w_gate, w_up, w_down)

    y12 = _sc_gather(y_rows, jnp.concatenate([dest0, dest1])).reshape(2, B, S, D // 2)
    return _moe_out(x, y12, info, gate, ln_g, ln_b)


def kernel(x, c, positions, ada_w, ada_b, ln_g, ln_b, mla_w_in, mla_g_q, mla_w_uq, mla_g_kv, mla_w_ukv, mla_w_o, nsa_w_in, nsa_pe_k, nsa_pe_v, nsa_w_ck1, nsa_w_ck2, nsa_w_cv1, nsa_w_cv2, nsa_w_o, sb_w_qkv, sb_w_o, moe_w_grp, moe_b_grp, moe_w_rt, moe_b_rt, moe_w_gate, moe_w_up, moe_w_down):
    D = x.shape[-1]
    depth = ada_w.shape[0]
    cos_mla, sin_mla = _rope_tables(positions, MLA_DR)
    cos_nsa, sin_nsa = _rope_tables(positions, NSA_DH)
    mods = _ada_all(c, ada_w, ada_b)

    def mod(i, sub):
        m = mods[2 * i + sub][:, None, :]
        return m[..., :D], m[..., D:2 * D], m[..., 2 * D:]

    for i in range(depth):
        kind, j = i % N_MIXERS, i // N_MIXERS
        shift, scale_v, gate = mod(i, 0)
        shift2, scale2, gate2 = mod(i, 1)
        route = (scale2, shift2, moe_w_grp[i], moe_b_grp[i], moe_w_rt[i], moe_b_rt[i])
        if kind == 0:
            x, *routed = _mla_layer(x, shift, scale_v, gate, cos_mla, sin_mla, mla_w_in[j], mla_g_q[j], mla_w_uq[j],
                                    mla_g_kv[j], mla_w_ukv[j], mla_w_o[j], ln_g[i, 0], ln_b[i, 0], route)
        elif kind == 1:
            x, *routed = _nsa_layer(x, shift, scale_v, gate, cos_nsa, sin_nsa, nsa_w_in[j], nsa_pe_k[j], nsa_pe_v[j],
                                    nsa_w_ck1[j], nsa_w_ck2[j], nsa_w_cv1[j], nsa_w_cv2[j], nsa_w_o[j],
                                    ln_g[i, 0], ln_b[i, 0], route)
        else:
            x, *routed = _sb_layer(x, shift, scale_v, gate, sb_w_qkv[j], sb_w_o[j], ln_g[i, 0], ln_b[i, 0], route)
        x = _moe_layer(x, routed, gate2, i, moe_w_gate, moe_w_up, moe_w_down, ln_g[i, 1], ln_b[i, 1])
    return x
```

```python
import functools

import numpy as np
import jax
import jax.numpy as jnp
from jax import lax
from jax.experimental import pallas as pl
from jax.experimental.pallas import tpu as pltpu
from jax.experimental.pallas import tpu_sc as plsc

BF16 = jnp.bfloat16
F32 = jnp.float32

DEPTH = 4
N_MIXERS = 3
ROPE_THETA = 10000.0
LN_EPS = 1e-5
RMS_EPS = 1e-6
NEG = -1e30
LOG2E = 1.4426950408889634
ALPHA = (2 * DEPTH) ** 0.25

MLA_HEADS, MLA_DN, MLA_DR, MLA_DV = 8, 128, 64, 128
MLA_QR, MLA_KVR = 256, 256

NSA_HEADS, NSA_KVH, NSA_DH = 8, 2, 128
NSA_G = NSA_HEADS // NSA_KVH
NSA_CMP_BLOCK, NSA_CMP_STRIDE = 32, 16
NSA_SLC_BLOCK, NSA_TOPK = 64, 8
SLC_SHIFT = 6
NSA_WINDOW = 512
NSA_BONUS = 1e3
GATE_ROWS = 16
SUM_ROWS = 16

SB_HEADS, SB_DH = 8, 128
SB_EXIT_LOG2 = -160.0

MOE_GROUPS, MOE_EPG = 4, 8
EPG_SHIFT = 3
MOE_E = MOE_GROUPS * MOE_EPG
MOE_ROWS = 256
FFN_LEAD = 3
FFN_SLOTS = FFN_LEAD + 2
OUT_LN_ROWS = 256
ROUTER_ROWS = 64
INFO_ROWS = 8

SC_CORES, SC_SUBCORES = 2, 16
SC_CHUNK = 64

LANES = 128
VMEM_BYTES_V7X = 64 * 1024 * 1024


def _cp(sem, vmem_mb):
    assert vmem_mb * 2**20 < VMEM_BYTES_V7X
    return pltpu.CompilerParams(dimension_semantics=sem, vmem_limit_bytes=vmem_mb * 2**20)


def _nt_dot(a, b):
    return lax.dot_general(a, b, (((1,), (1,)), ((), ())), preferred_element_type=F32)


def _dot(a, b):
    return jnp.dot(a, b, preferred_element_type=F32)


def _split_bf16(a):
    hi = a.astype(BF16)
    lo = (a - hi.astype(F32)).astype(BF16)
    return hi, lo


def _pack_pairs(a):
    n = a.shape[1] // 2
    hi = pltpu.bitcast(a[:, :n].astype(BF16).astype(F32), jnp.uint32)
    lo = pltpu.bitcast(a[:, n:].astype(BF16).astype(F32), jnp.uint32)
    return hi | (lo >> 16)


def _unpack_pairs(u):
    hi = pltpu.bitcast(u & jnp.uint32(0xFFFF0000), F32)
    lo = pltpu.bitcast(u << 16, F32)
    return jnp.concatenate([hi, lo], axis=1)


def _layer_norm(z, g, b):
    mu = jnp.mean(z, axis=-1, keepdims=True)
    d = z - mu
    var = jnp.mean(d * d, axis=-1, keepdims=True)
    return d * lax.rsqrt(var + LN_EPS) * g + b


def _ada_kernel(c_ref, w_ref, b_ref, o_ref):
    c = c_ref[...]
    s_hi, s_lo = _split_bf16(c * jax.nn.sigmoid(c))
    w_hi, w_lo = _split_bf16(w_ref[0])
    o_ref[0] = _dot(s_hi, w_hi) + _dot(s_lo, w_hi) + _dot(s_hi, w_lo) + b_ref[0]


def _ada_all(c, ada_w, ada_b):
    L, two, D, D3 = ada_w.shape
    B = c.shape[0]
    n = L * two
    tn = 1024
    return pl.pallas_call(
        _ada_kernel,
        grid=(n, D3 // tn),
        in_specs=[pl.BlockSpec((B, D), lambda i, j: (0, 0)),
                  pl.BlockSpec((1, D, tn), lambda i, j: (i, 0, j)),
                  pl.BlockSpec((1, 1, tn), lambda i, j: (i, 0, j))],
        out_specs=pl.BlockSpec((1, B, tn), lambda i, j: (i, 0, j)),
        out_shape=jax.ShapeDtypeStruct((n, B, D3), F32),
        compiler_params=_cp(("arbitrary", "arbitrary"), 32),
        name="ada",
    )(c, ada_w.reshape(n, D, D3), ada_b.reshape(n, 1, D3))


def _route_tokens(x1, sc, sh, whi, wlo, bcol, tri, base_sc):
    h = x1 * (1.0 + sc) + sh
    h_hi, h_lo = _split_bf16(h)
    logit = _nt_dot(whi, h_hi) + _nt_dot(whi, h_lo) + _nt_dot(wlo, h_hi) + bcol
    tm = logit.shape[1]
    epg = MOE_EPG
    row = lax.broadcasted_iota(jnp.int32, (epg, 1), 0)
    row_f = row.astype(F32)

    def first_max(vals):
        mx = jnp.max(vals, axis=0, keepdims=True)
        idx = jnp.min(jnp.where(vals == mx, row_f, float(epg)), axis=0, keepdims=True)
        return mx, idx

    gl = jnp.where(row < MOE_GROUPS, logit[0:epg], -jnp.inf)
    gmax, g_idx = first_max(gl)
    g_w = 1.0 / jnp.sum(jnp.exp(gl - gmax), axis=0, keepdims=True)
    el = logit[epg:2 * epg]
    for g in range(1, MOE_GROUPS):
        el = jnp.where(g_idx == float(g), logit[epg * (g + 1):epg * (g + 2)], el)
    ep = jnp.exp(el - jnp.max(el, axis=0, keepdims=True))
    prob = ep / jnp.sum(ep, axis=0, keepdims=True)
    p1, i1 = first_max(prob)
    p2, i2 = first_max(jnp.where(row_f == i1, -1.0, prob))
    e1 = g_idx * float(epg) + i1
    e2 = g_idx * float(epg) + i2
    w1 = g_w * p1 / (p1 + p2)
    w2 = g_w * p2 / (p1 + p2)

    erow = lax.broadcasted_iota(jnp.int32, (MOE_E, 1), 0).astype(F32)
    hot1 = erow == e1
    hot2 = erow == e2
    cnt = jnp.where(hot1 | hot2, 1.0, 0.0)
    before = base_sc[:, 0:1] + _dot(cnt.astype(BF16), tri)
    r1 = jnp.sum(jnp.where(hot1, before, 0.0), axis=0, keepdims=True)
    r2 = jnp.sum(jnp.where(hot2, before, 0.0), axis=0, keepdims=True)
    base_sc[...] += jnp.sum(cnt, axis=1, keepdims=True)

    irow = lax.broadcasted_iota(jnp.int32, (INFO_ROWS, 1), 0)
    info = jnp.zeros((INFO_ROWS, tm), F32)
    for k, val in enumerate((e1, e2, r1, r2, w1, w2)):
        info = jnp.where(irow == k, val, info)
    return _pack_pairs(h_hi), info


def _out_ln_kernel(o_ref, w_ref, x_ref, gate_ref, g_ref, b_ref, sc_ref, sh_ref, whi_ref, wlo_ref, br_ref, tri_ref,
                   out_ref, hb_ref, info_ref, cnt_ref, base_sc):
    first = (pl.program_id(0) == 0) & (pl.program_id(1) == 0)

    @pl.when(first)
    def _():
        base_sc[...] = jnp.zeros(base_sc.shape, F32)

    ts = o_ref.shape[1]
    rows = min(OUT_LN_ROWS, ts)
    ys = [_dot(o_ref[0, 0:rows, :], w_ref[...])]
    for c in range(ts // rows):
        if c + 1 < ts // rows:
            ys.append(_dot(o_ref[0, (c + 1) * rows:(c + 2) * rows, :], w_ref[...]))
        sl = slice(c * rows, (c + 1) * rows)
        z = ALPHA * x_ref[0, sl, :] + gate_ref[0] * ys[c]
        x1 = _layer_norm(z, g_ref[...], b_ref[...])
        out_ref[0, sl, :] = x1
        packed, info = _route_tokens(x1, sc_ref[0], sh_ref[0], whi_ref[...], wlo_ref[...], br_ref[...],
                                     tri_ref[...], base_sc)
        hb_ref[0, sl, :] = packed
        info_ref[0, :, sl] = info
    cnt_ref[...] = base_sc[...]


def _out_ln(o, w_o, x, gate, g, b, route, ts=1024):
    B, S, D = x.shape
    K = o.shape[-1]
    ts = min(ts, S)
    scale2, shift2, w_grp, b_grp, w_rt, b_rt = route
    E, g0 = MOE_E, MOE_EPG
    w_r = jnp.zeros((ROUTER_ROWS, D), F32).at[:MOE_GROUPS].set(w_grp.T).at[g0:g0 + E].set(w_rt.T)
    w_hi = w_r.astype(BF16)
    w_lo = (w_r - w_hi.astype(F32)).astype(BF16)
    b_r = jnp.zeros((ROUTER_ROWS, 1), F32).at[:MOE_GROUPS, 0].set(b_grp).at[g0:g0 + E, 0].set(b_rt)
    tm = min(OUT_LN_ROWS, ts)
    tri = jnp.asarray(np.triu(np.ones((tm, tm), np.float32), 1), BF16)
    const = lambda shape: pl.BlockSpec(shape, lambda bi, i: (0,) * len(shape))
    per_batch = pl.BlockSpec((1, 1, D), lambda bi, i: (bi, 0, 0))
    return pl.pallas_call(
        _out_ln_kernel,
        grid=(B, S // ts),
        in_specs=[pl.BlockSpec((1, ts, K), lambda bi, i: (bi, i, 0)),
                  const((K, D)),
                  pl.BlockSpec((1, ts, D), lambda bi, i: (bi, i, 0)),
                  per_batch, const((1, D)), const((1, D)),
                  per_batch, per_batch,
                  const((ROUTER_ROWS, D)), const((ROUTER_ROWS, D)), const((ROUTER_ROWS, 1)), const((tm, tm))],
        out_specs=[pl.BlockSpec((1, ts, D), lambda bi, i: (bi, i, 0)),
                   pl.BlockSpec((1, ts, D // 2), lambda bi, i: (bi, i, 0)),
                   pl.BlockSpec((1, INFO_ROWS, ts), lambda bi, i: (bi, 0, i)),
                   const((E, LANES))],
        out_shape=[jax.ShapeDtypeStruct((B, S, D), F32),
                   jax.ShapeDtypeStruct((B, S, D // 2), jnp.uint32),
                   jax.ShapeDtypeStruct((B, INFO_ROWS, S), F32),
                   jax.ShapeDtypeStruct((E, LANES), F32)],
        scratch_shapes=[pltpu.VMEM((E, LANES), F32)],
        compiler_params=_cp(("arbitrary", "arbitrary"), 48),
        name="out_ln_route",
    )(o, w_o.astype(BF16), x, gate, g.reshape(1, D), b.reshape(1, D), scale2, shift2, w_hi, w_lo, b_r, tri)


def _rope_tables(positions, dim):
    inv_freq = 1.0 / (ROPE_THETA ** (jnp.arange(0, dim, 2, dtype=F32) / dim))
    ang = positions.astype(F32)[..., None] * inv_freq
    return jnp.cos(ang), jnp.sin(ang)


def _rot_half_cols(w, half):
    return jnp.concatenate([-w[..., half:], w[..., :half]], axis=-1)


def _mla_front_kernel(x_ref, sc_ref, sh_ref, win_ref, gq_ref, gkv_ref, wqt_ref, wk_ref, wvt_ref, t_ref, tt_ref,
                      qt_ref, k_ref, vt_ref, *, scale):
    hb = (x_ref[0] * (1.0 + sc_ref[0]) + sh_ref[0]).astype(BF16)
    r = _dot(hb, win_ref[...])
    tab = t_ref[0]
    tab_t = tt_ref[0]
    low = lax.broadcasted_iota(jnp.int32, tab.shape, 1) < MLA_DR

    def rms(cx, g):
        return (cx * lax.rsqrt(jnp.mean(cx * cx, axis=-1, keepdims=True) + RMS_EPS) * g).astype(BF16)

    cq = rms(r[:, :MLA_QR], gq_ref[...])
    ckv = rms(r[:, MLA_QR:MLA_QR + MLA_KVR], gkv_ref[...])
    t = r[:, MLA_QR + MLA_KVR:] * tab
    k_rope = jnp.where(low, t + pltpu.roll(t, MLA_DR, axis=1), 0.0).astype(BF16)
    zeros = jnp.zeros((MLA_DR, hb.shape[0]), BF16)
    hd = MLA_DN + 2 * MLA_DR
    q_all = _nt_dot(wqt_ref[...], cq)
    k_all = _dot(ckv, wk_ref[...])
    vt_all = _nt_dot(wvt_ref[...], ckv)
    for h in range(MLA_HEADS):
        qh = q_all[h * hd:(h + 1) * hd]
        tq = qh[MLA_DN:] * tab_t
        qt_ref[0, h, :MLA_DN, :] = (qh[:MLA_DN] * scale).astype(BF16)
        qt_ref[0, h, MLA_DN:MLA_DN + MLA_DR, :] = ((tq[:MLA_DR] + tq[MLA_DR:]) * scale).astype(BF16)
        qt_ref[0, h, MLA_DN + MLA_DR:, :] = zeros
        k_ref[0, h, :, :MLA_DN] = k_all[:, h * MLA_DN:(h + 1) * MLA_DN].astype(BF16)
        k_ref[0, h, :, MLA_DN:] = k_rope
        vt_ref[0, h] = vt_all[h * MLA_DV:(h + 1) * MLA_DV].astype(BF16)


def _flash_causal_kernel(qt_ref, k_ref, vt_ref, o_ref, *, t):
    dv, s_len = vt_ref.shape[2], vt_ref.shape[3]
    pairs = [(qi, kt) for qi in range(s_len // t) for kt in range(qi + 1)]
    kpos = lax.broadcasted_iota(jnp.int32, (t, 1), 0)
    qpos = lax.broadcasted_iota(jnp.int32, (1, t), 1)
    diag = kpos <= qpos

    def scores(j):
        qi, kt = pairs[j]
        return _dot(k_ref[0, 0, kt * t:(kt + 1) * t, :], qt_ref[0, 0, :, qi * t:(qi + 1) * t])

    def value_matmul(pending, l, acc):
        qi, kt, p, a = pending
        acc = a * acc + _dot(vt_ref[0, 0, :, kt * t:(kt + 1) * t], p)
        if kt == qi:
            o_ref[0, qi * t:(qi + 1) * t, :] = (acc / l).T.astype(o_ref.dtype)
        return acc

    s_cur, pending = scores(0), None
    m = l = acc = None
    for j, (qi, kt) in enumerate(pairs):
        s_next = scores(j + 1) if j + 1 < len(pairs) else None
        if pending is not None:
            acc = value_matmul(pending, l, acc)
        if kt == 0:
            m, l, acc = jnp.full((1, t), NEG, F32), jnp.zeros((1, t), F32), jnp.zeros((dv, t), F32)
        s = jnp.where(diag, s_cur, NEG) if kt == qi else s_cur
        m_new = jnp.maximum(m, jnp.max(s, axis=0, keepdims=True))
        p = jnp.exp2(s - m_new)
        a = jnp.exp2(m - m_new)
        l = a * l + jnp.sum(p, axis=0, keepdims=True)
        m = m_new
        pending = (qi, kt, p.astype(BF16), a)
        s_cur = s_next
    value_matmul(pending, l, acc)


def _mla_layer(x, shift, scale_v, gate, cos, sin, w_in, g_q, w_uq, g_kv, w_ukv, w_o, ln_g, ln_b, route):
    B, S, D = x.shape
    H, dn, dr, dv = MLA_HEADS, MLA_DN, MLA_DR, MLA_DV
    half = dr // 2
    kr = w_in[:, MLA_QR + MLA_KVR:]
    win_p = jnp.concatenate([w_in, _rot_half_cols(kr, half)], axis=1).astype(BF16)
    wq = w_uq.reshape(MLA_QR, H, dn + dr)
    wq = jnp.concatenate([wq, _rot_half_cols(wq[..., dn:], half)], axis=-1)
    wqt = jnp.transpose(wq, (1, 2, 0)).reshape(H * (dn + 2 * dr), MLA_QR).astype(BF16)
    wkv = w_ukv.reshape(MLA_KVR, H, dn + dv)
    wk = wkv[..., :dn].reshape(MLA_KVR, H * dn).astype(BF16)
    wvt = jnp.transpose(wkv[..., dn:], (1, 2, 0)).reshape(H * dv, MLA_KVR).astype(BF16)
    tab = jnp.concatenate([cos, cos, sin, sin], axis=-1)
    tab_t = jnp.swapaxes(tab, 1, 2)
    ts = min(512, S)
    nw = win_p.shape[1]
    hd = dn + 2 * dr
    qt, k, vt = pl.pallas_call(
        functools.partial(_mla_front_kernel, scale=LOG2E * (dn + dr) ** -0.5),
        grid=(B, S // ts),
        in_specs=[pl.BlockSpec((1, ts, D), lambda b, i: (b, i, 0)),
                  pl.BlockSpec((1, 1, D), lambda b, i: (b, 0, 0)),
                  pl.BlockSpec((1, 1, D), lambda b, i: (b, 0, 0)),
                  pl.BlockSpec((D, nw), lambda b, i: (0, 0)),
                  pl.BlockSpec((1, MLA_QR), lambda b, i: (0, 0)),
                  pl.BlockSpec((1, MLA_KVR), lambda b, i: (0, 0)),
                  pl.BlockSpec((H * hd, MLA_QR), lambda b, i: (0, 0)),
                  pl.BlockSpec((MLA_KVR, H * dn), lambda b, i: (0, 0)),
                  pl.BlockSpec((H * dv, MLA_KVR), lambda b, i: (0, 0)),
                  pl.BlockSpec((1, ts, LANES), lambda b, i: (b, i, 0)),
                  pl.BlockSpec((1, LANES, ts), lambda b, i: (b, 0, i))],
        out_specs=[pl.BlockSpec((1, H, hd, ts), lambda b, i: (b, 0, 0, i)),
                   pl.BlockSpec((1, H, ts, hd), lambda b, i: (b, 0, i, 0)),
                   pl.BlockSpec((1, H, dv, ts), lambda b, i: (b, 0, 0, i))],
        out_shape=[jax.ShapeDtypeStruct((B, H, hd, S), BF16),
                   jax.ShapeDtypeStruct((B, H, S, hd), BF16),
                   jax.ShapeDtypeStruct((B, H, dv, S), BF16)],
        compiler_params=_cp(("arbitrary", "arbitrary"), 48),
        name="mla_front",
    )(x, scale_v, shift, win_p, g_q.reshape(1, -1), g_kv.reshape(1, -1), wqt, wk, wvt, tab, tab_t)

    t = min(512, S)
    o = pl.pallas_call(
        functools.partial(_flash_causal_kernel, t=t),
        grid=(B, H),
        in_specs=[pl.BlockSpec((1, 1, hd, S), lambda b, h: (b, h, 0, 0)),
                  pl.BlockSpec((1, 1, S, hd), lambda b, h: (b, h, 0, 0)),
                  pl.BlockSpec((1, 1, dv, S), lambda b, h: (b, h, 0, 0))],
        out_specs=pl.BlockSpec((1, S, dv), lambda b, h: (b, 0, h)),
        out_shape=jax.ShapeDtypeStruct((B, S, H * dv), BF16),
        compiler_params=_cp(("arbitrary", "arbitrary"), 48),
        name="mla_attn",
    )(qt, k, vt)
    return _out_ln(o, w_o, x, gate, ln_g, ln_b, route)


def _proj_kernel(*refs, segs, chunk, has_rope):
    x_ref, sc_ref, sh_ref, w_ref, wt_ref = refs[:5]
    n_in = 9 if has_rope else 5
    outs = refs[n_in:]
    hb = (x_ref[0] * (1.0 + sc_ref[0]) + sh_ref[0]).astype(BF16)
    if has_rope:
        cos, sin = refs[5][0], refs[6][0]
        cos_t, sin_t = refs[7][0], refs[8][0]
    half = LANES // 2
    for (c0, width, kind, scale, transposed), o_ref in zip(segs, outs):
        for j in range(0, width, chunk):
            cw = min(chunk, width - j)
            if transposed:
                r = _nt_dot(wt_ref[c0 + j:c0 + j + cw, :], hb)
            else:
                r = _dot(hb, w_ref[:, c0 + j:c0 + j + cw])
            if kind == "rope":
                parts = []
                for t in range(0, cw, LANES):
                    if transposed:
                        u = r[t:t + LANES]
                        parts.append(u * cos_t + pltpu.roll(u, half, axis=0) * sin_t)
                    else:
                        u = r[:, t:t + LANES]
                        parts.append(u * cos + pltpu.roll(u, half, axis=1) * sin)
                r = jnp.concatenate(parts, axis=0 if transposed else 1) if len(parts) > 1 else parts[0]
            elif kind == "sigmoid":
                r = jax.nn.sigmoid(r)
            if scale != 1.0:
                r = r * scale
            if transposed:
                o_ref[0, j:j + cw, :] = r.astype(o_ref.dtype)
            else:
                o_ref[0, :, j:j + cw] = r.astype(o_ref.dtype)


def _mod_proj(x, shift, scale_v, w, wt, segs, out_dtypes, rope=None, ts=512, chunk=256):
    B, S, D = x.shape
    ts = min(ts, S)
    in_specs = [pl.BlockSpec((1, ts, D), lambda b, i: (b, i, 0)),
                pl.BlockSpec((1, 1, D), lambda b, i: (b, 0, 0)),
                pl.BlockSpec((1, 1, D), lambda b, i: (b, 0, 0)),
                pl.BlockSpec(w.shape, lambda b, i: (0, 0)),
                pl.BlockSpec(wt.shape, lambda b, i: (0, 0))]
    args = [x, scale_v, shift, w, wt]
    if rope is not None:
        cos, sin = rope
        in_specs += [pl.BlockSpec((1, ts, LANES), lambda b, i: (b, i, 0))] * 2
        in_specs += [pl.BlockSpec((1, LANES, ts), lambda b, i: (b, 0, i))] * 2
        args += [cos, sin, jnp.swapaxes(cos, 1, 2), jnp.swapaxes(sin, 1, 2)]
    out_specs, out_shape = [], []
    for sg, dt in zip(segs, out_dtypes):
        if sg[4]:
            out_specs.append(pl.BlockSpec((1, sg[1], ts), lambda b, i: (b, 0, i)))
            out_shape.append(jax.ShapeDtypeStruct((B, sg[1], S), dt))
        else:
            out_specs.append(pl.BlockSpec((1, ts, sg[1]), lambda b, i: (b, i, 0)))
            out_shape.append(jax.ShapeDtypeStruct((B, S, sg[1]), dt))
    return pl.pallas_call(
        functools.partial(_proj_kernel, segs=tuple(segs), chunk=chunk, has_rope=rope is not None),
        grid=(B, S // ts),
        in_specs=in_specs,
        out_specs=out_specs,
        out_shape=out_shape,
        compiler_params=_cp(("arbitrary", "arbitrary"), 48),
        name="mod_proj",
    )(*args)


def _nsa_cmp_kernel(kc_ref, vc_ref, pek_ref, pev_ref, wk1_ref, wk2_ref, wv1_ref, wv2t_ref, ko_ref, vto_ref):
    half = NSA_CMP_BLOCK // 2
    nchunk = kc_ref.shape[1]

    def hidden(src, pe_ref, w1_ref, kh):
        a = jnp.zeros((nchunk, w1_ref.shape[2]), F32)
        bm = jnp.zeros((nchunk, w1_ref.shape[2]), F32)
        for l in range(half):
            c0 = (l * NSA_KVH + kh) * NSA_DH
            t = src[0, :, c0:c0 + NSA_DH].astype(F32)
            a = a + _dot((t + pe_ref[l:l + 1, :]).astype(BF16), w1_ref[l])
            bm = bm + _dot((t + pe_ref[half + l:half + l + 1, :]).astype(BF16), w1_ref[half + l])
        hid = a + pltpu.roll(bm, nchunk - 1, axis=0)
        return (hid * jax.nn.sigmoid(hid)).astype(BF16)

    rowi = lax.broadcasted_iota(jnp.int32, (nchunk, 1), 0)
    coli = lax.broadcasted_iota(jnp.int32, (1, nchunk), 1)
    for kh in range(NSA_KVH):
        out = _dot(hidden(kc_ref, pek_ref, wk1_ref, kh), wk2_ref[...])
        ko_ref[0, kh] = jnp.where(rowi < nchunk - 1, out, 0.0).astype(ko_ref.dtype)
        out_t = _nt_dot(wv2t_ref[...], hidden(vc_ref, pev_ref, wv1_ref, kh))
        vto_ref[0, kh] = jnp.where(coli < nchunk - 1, out_t, 0.0).astype(vto_ref.dtype)


def _nsa_attn_kernel(qt_ref, kc_ref, vct_ref, ks_ref, vst_ref, kw_ref, vwt_ref, gt_ref, ovt_ref, e_ref, o_ref,
                     q4_sc, m_sc, acc_sc, *, tq, tk, span, n_slc):
    qi = pl.program_id(2)
    G = NSA_G
    for g in range(G):
        q4_sc[:, g * tq:(g + 1) * tq] = qt_ref[0, g * NSA_DH:(g + 1) * NSA_DH, :]
    q4 = q4_sc[...]
    qpos = qi * tq + lax.broadcasted_iota(jnp.int32, (1, tq), 1)

    def tile_g(a):
        return jnp.concatenate([a] * G, axis=1)

    qpos4 = tile_g(qpos)

    kstart = pl.multiple_of(jnp.maximum(qi * tq + tq - span, 0), tq)
    s_win = _dot(kw_ref[0, pl.ds(kstart, span), :], q4)

    n_cmp = kc_ref.shape[2]
    nid = lax.broadcasted_iota(jnp.int32, (n_cmp, 1), 0)
    cmask = (nid * NSA_CMP_STRIDE + (NSA_CMP_BLOCK - 1) <= qpos4) & (nid < n_cmp - 1)
    s = jnp.where(cmask, _dot(kc_ref[0, 0], q4), NEG)
    p = jnp.where(cmask, jnp.exp2(s - jnp.max(s, axis=0, keepdims=True)), 0.0)
    l = jnp.sum(p, axis=0, keepdims=True)
    p_cmp = p / jnp.where(l > 0.0, l, 1.0)
    o_cmp = _dot(vct_ref[0, 0], p_cmp.astype(BF16))

    diff = qpos - (kstart + lax.broadcasted_iota(jnp.int32, (span, 1), 0))
    s_win = s_win + tile_g(jnp.where((diff >= 0) & (diff < NSA_WINDOW), 0.0, NEG))
    m_win = jnp.max(s_win, axis=0, keepdims=True)
    ow = jnp.zeros((NSA_DH + SUM_ROWS, G * tq), F32)
    for c0 in range(0, span, tk):
        c1 = min(c0 + tk, span)
        p = jnp.exp2(s_win[c0:c1] - m_win).astype(BF16)
        v_aug = jnp.concatenate([vwt_ref[0, :, pl.ds(kstart + c0, c1 - c0)], jnp.ones((SUM_ROWS, c1 - c0), BF16)], axis=0)
        ow = ow + _dot(v_aug, p)
    o_win = ow[:NSA_DH] / ow[NSA_DH:NSA_DH + 1]

    psum = p_cmp[:, 0:tq]
    for g in range(1, G):
        psum = psum + p_cmp[:, g * tq:(g + 1) * tq]
    p_hi, p_lo = _split_bf16(psum)
    imp = _dot(ovt_ref[...], p_hi) + _dot(ovt_ref[...], p_lo)
    nb = ovt_ref.shape[0]
    blk = lax.broadcasted_iota(jnp.int32, (nb, 1), 0)
    cur = jnp.right_shift(qpos, SLC_SHIFT)
    forced = (blk == 0) | (blk == cur) | (blk == cur - 1)
    valid = (blk * NSA_SLC_BLOCK <= qpos) & (blk < n_slc)
    work = jnp.where(valid, imp + jnp.where(forced, NSA_BONUS, 0.0), -jnp.inf)
    beaten = jnp.zeros((nb, tq), F32)
    for j in range(n_slc):
        wj = work[j:j + 1, :]
        beaten = beaten + jnp.where((wj > work) | ((wj == work) & (blk > j)), 1.0, 0.0)
    sel = jnp.where(beaten < float(min(NSA_TOPK, n_slc)), 1.0, 0.0).astype(BF16)

    m_sc[...] = jnp.full(m_sc.shape, NEG, F32)
    acc_sc[...] = jnp.zeros(acc_sc.shape, F32)
    n_kt = ((qi + 1) * tq + tk - 1) // tk

    def slc_tiles(n):
        def scores(kt):
            return _dot(ks_ref[0, kt * tk:(kt + 1) * tk, :], q4)

        ones = jnp.ones((SUM_ROWS, tk), BF16)

        def value_matmul(kt, p, a):
            v_aug = jnp.concatenate([vst_ref[0, :, kt * tk:(kt + 1) * tk], ones], axis=0)
            acc_sc[...] = a * acc_sc[...] + _dot(v_aug, p)

        s, pending = scores(0), None
        chosen = _dot(e_ref[0:n * tk, :], sel)
        kpos = lax.broadcasted_iota(jnp.int32, (n * tk, 1), 0)
        bias_all = jnp.where((chosen > 0.5) & (kpos <= qpos), 0.0, NEG)
        for kt in range(n):
            s_next = scores(kt + 1) if kt + 1 < n else None
            if pending is not None:
                value_matmul(*pending)
            s = s + tile_g(bias_all[kt * tk:(kt + 1) * tk])
            m_prev = m_sc[...]
            m_new = jnp.maximum(m_prev, jnp.max(s, axis=0, keepdims=True))
            p = jnp.exp2(s - m_new)
            a = jnp.exp2(m_prev - m_new)
            m_sc[...] = m_new
            pending = (kt, p.astype(BF16), a)
            s = s_next
        value_matmul(*pending)

    for n in range(1, (ks_ref.shape[1] + tk - 1) // tk + 1):
        pl.when(n_kt == n)(functools.partial(slc_tiles, n))
    o_slc = acc_sc[0:NSA_DH, :] / acc_sc[NSA_DH:NSA_DH + 1, :]

    gt = gt_ref[0]
    for g in range(G):
        cols = slice(g * tq, (g + 1) * tq)
        o = (gt[3 * g:3 * g + 1] * o_cmp[:, cols] + gt[3 * g + 1:3 * g + 2] * o_slc[:, cols]
             + gt[3 * g + 2:3 * g + 3] * o_win[:, cols])
        o_ref[0, :, g * NSA_DH:(g + 1) * NSA_DH] = o.T.astype(o_ref.dtype)


def _nsa_layer(x, shift, scale_v, gate, cos, sin, w_in, pe_k, pe_v, w_ck1, w_ck2, w_cv1, w_cv2, w_o,
               ln_g, ln_b, route):
    B, S, D = x.shape
    H, KH, Dh, G = NSA_HEADS, NSA_KVH, NSA_DH, NSA_G
    kvw = KH * Dh
    cuts = [int(v) for v in np.cumsum([H * Dh] + [kvw] * 6)]
    wq, wkc, wvc, wks, wvs, wkw, wvw, wgl = jnp.split(w_in, cuts, axis=1)
    wgl = jnp.pad(wgl.reshape(D, KH, G * 3), ((0, 0), (0, 0), (0, GATE_ROWS - G * 3))).reshape(D, KH * GATE_ROWS)
    w = jnp.concatenate([wkc, wks, wkw, wvc], axis=1).astype(BF16)
    wt = jnp.concatenate([wq, wvs, wvw, wgl], axis=1).T.astype(BF16)
    segs = [(0, H * Dh, "rope", LOG2E * Dh ** -0.5, True),
            (0, kvw, "rope", 1.0, False), (kvw, kvw, "rope", 1.0, False), (2 * kvw, kvw, "rope", 1.0, False),
            (3 * kvw, kvw, "plain", 1.0, False),
            (H * Dh, kvw, "plain", 1.0, True), (H * Dh + kvw, kvw, "plain", 1.0, True),
            (H * Dh + 2 * kvw, KH * GATE_ROWS, "sigmoid", 1.0, True)]
    rope = (jnp.concatenate([cos, cos], axis=-1), jnp.concatenate([-sin, sin], axis=-1))
    qt, kc, ks, kw, vc, vst, vwt, gates_t = _mod_proj(
        x, shift, scale_v, w, wt, segs, [BF16] * 7 + [F32], rope=rope)

    st = NSA_CMP_STRIDE
    nchunk = S // st
    kc2 = kc.reshape(B, nchunk, st * kvw)
    vc2 = vc.reshape(B, nchunk, st * kvw)
    hid = w_ck1.shape[-1]
    full = lambda shape: pl.BlockSpec(shape, lambda b: (0,) * len(shape))
    k_cmp, v_cmp_t = pl.pallas_call(
        _nsa_cmp_kernel,
        grid=(B,),
        in_specs=[pl.BlockSpec((1, nchunk, st * kvw), lambda b: (b, 0, 0)),
                  pl.BlockSpec((1, nchunk, st * kvw), lambda b: (b, 0, 0)),
                  full((NSA_CMP_BLOCK, Dh)), full((NSA_CMP_BLOCK, Dh)),
                  full((NSA_CMP_BLOCK, Dh, hid)), full((hid, Dh)),
                  full((NSA_CMP_BLOCK, Dh, hid)), full((Dh, hid))],
        out_specs=[pl.BlockSpec((1, KH, nchunk, Dh), lambda b: (b, 0, 0, 0)),
                   pl.BlockSpec((1, KH, Dh, nchunk), lambda b: (b, 0, 0, 0))],
        out_shape=[jax.ShapeDtypeStruct((B, KH, nchunk, Dh), BF16),
                   jax.ShapeDtypeStruct((B, KH, Dh, nchunk), BF16)],
        compiler_params=_cp(("arbitrary",), 32),
        name="nsa_cmp",
    )(kc2, vc2, pe_k, pe_v, w_ck1.astype(BF16), w_ck2.astype(BF16), w_cv1.astype(BF16), w_cv2.T.astype(BF16))

    n_slc = S // NSA_SLC_BLOCK
    nb = 32
    assert n_slc <= nb and nchunk % 8 == 0
    cmp_start = np.arange(nchunk) * st
    slc_start = np.arange(n_slc) * NSA_SLC_BLOCK
    ov = np.clip(np.minimum(cmp_start[:, None] + NSA_CMP_BLOCK, slc_start[None, :] + NSA_SLC_BLOCK)
                 - np.maximum(cmp_start[:, None], slc_start[None, :]), 0, None).astype(np.float32) / NSA_CMP_BLOCK
    ov[nchunk - 1:] = 0.0
    ov_t = np.zeros((nb, nchunk), np.float32)
    ov_t[:n_slc] = ov.T
    key_blk = np.zeros((S, nb), np.float32)
    key_blk[np.arange(S), np.arange(S) // NSA_SLC_BLOCK] = 1.0

    tq = min(256, S)
    tk = min(256, S)
    span = min(NSA_WINDOW + tq, S)
    k_spec = pl.BlockSpec((1, S, Dh), lambda b, kh, i: (b, 0, kh))
    vt_spec = pl.BlockSpec((1, Dh, S), lambda b, kh, i: (b, kh, 0))
    o = pl.pallas_call(
        functools.partial(_nsa_attn_kernel, tq=tq, tk=tk, span=span, n_slc=n_slc),
        grid=(B, KH, S // tq),
        in_specs=[pl.BlockSpec((1, G * Dh, tq), lambda b, kh, i: (b, kh, i)),
                  pl.BlockSpec((1, 1, nchunk, Dh), lambda b, kh, i: (b, kh, 0, 0)),
                  pl.BlockSpec((1, 1, Dh, nchunk), lambda b, kh, i: (b, kh, 0, 0)),
                  k_spec, vt_spec, k_spec, vt_spec,
                  pl.BlockSpec((1, GATE_ROWS, tq), lambda b, kh, i: (b, kh, i)),
                  pl.BlockSpec((nb, nchunk), lambda b, kh, i: (0, 0)),
                  pl.BlockSpec((S, nb), lambda b, kh, i: (0, 0))],
        out_specs=pl.BlockSpec((1, tq, G * Dh), lambda b, kh, i: (b, i, kh)),
        out_shape=jax.ShapeDtypeStruct((B, S, H * Dh), BF16),
        scratch_shapes=[pltpu.VMEM((Dh, G * tq), BF16), pltpu.VMEM((1, G * tq), F32),
                        pltpu.VMEM((Dh + SUM_ROWS, G * tq), F32)],
        compiler_params=_cp(("arbitrary", "arbitrary", "arbitrary"), 32),
        name="nsa_attn",
    )(qt, k_cmp, v_cmp_t, ks, vst, kw, vwt, gates_t, jnp.asarray(ov_t, BF16), jnp.asarray(key_blk, BF16))
    return _out_ln(o, w_o, x, gate, ln_g, ln_b, route)


def _sb_attn_kernel(qt_ref, k_ref, vt_ref, u_ref, o_ref, r_sc, acc_sc, *, tq, tk):
    d, s_len = vt_ref.shape[1], vt_ref.shape[2]
    u = u_ref[...]
    nd = tq // tk

    def weights(z, strict):
        lk = -(jnp.maximum(z, 0.0) + jnp.log2(1.0 + jnp.exp2(-jnp.abs(z))))
        if strict is not None:
            lk = jnp.where(strict, lk, 0.0)
        between = r_sc[...] + _dot(u, lk.astype(BF16))
        a = jnp.exp2(lk + z + between)
        if strict is not None:
            a = jnp.where(strict, a, 0.0)
        r_sc[...] += jnp.sum(lk, axis=0, keepdims=True)
        return a.astype(BF16)

    for qi in range(s_len // tq):
        qt = qt_ref[0, :, qi * tq:(qi + 1) * tq]
        r_sc[...] = jnp.zeros(r_sc.shape, F32)
        acc_sc[...] = jnp.zeros(acc_sc.shape, F32)
        last = (qi + 1) * nd - 1
        tiles = [last - i for i in range(nd)] + ([qi * nd - 1] if qi > 0 else [])
        qpos = qi * tq + lax.broadcasted_iota(jnp.int32, (1, tq), 1)

        def scores(kt):
            return _dot(k_ref[0, kt * tk:(kt + 1) * tk, :], qt)

        z, pending = scores(tiles[0]), None
        for i, kt in enumerate(tiles):
            nxt = tiles[i + 1] if i + 1 < len(tiles) else max(qi * nd - 2, 0)
            z_next = scores(nxt)
            if pending is not None:
                acc_sc[...] += _dot(vt_ref[0, :, pending[0] * tk:(pending[0] + 1) * tk], pending[1])
            strict = (kt * tk + lax.broadcasted_iota(jnp.int32, (tk, 1), 0) < qpos) if kt >= qi * nd else None
            pending = (kt, weights(z, strict))
            z = z_next
        acc_sc[...] += _dot(vt_ref[0, :, pending[0] * tk:(pending[0] + 1) * tk], pending[1])

        if qi * nd - 2 >= 0:
            def scores_dyn(kt):
                return _dot(k_ref[0, pl.ds(pl.multiple_of(kt * tk, tk), tk), :], qt)

            def cond(carry):
                kt, _ = carry
                return (kt >= 0) & (jnp.max(r_sc[...]) > SB_EXIT_LOG2)

            def body(carry):
                kt, z = carry
                z_next = scores_dyn(jnp.maximum(kt - 1, 0))
                a = weights(z, None)
                acc_sc[...] += _dot(vt_ref[0, :, pl.ds(pl.multiple_of(kt * tk, tk), tk)], a)
                return kt - 1, z_next

            lax.while_loop(cond, body, (qi * nd - 2, z))
        o_ref[0, qi * tq:(qi + 1) * tq, :] = acc_sc[...].T.astype(o_ref.dtype)


def _sb_layer(x, shift, scale_v, gate, w_qkv, w_o, ln_g, ln_b, route):
    B, S, D = x.shape
    H, d = SB_HEADS, SB_DH
    hw = H * d
    w = w_qkv[:, hw:2 * hw].astype(BF16)
    wt = jnp.concatenate([w_qkv[:, :hw], w_qkv[:, 2 * hw:]], axis=1).T.astype(BF16)
    segs = [(0, hw, "plain", LOG2E * d ** -0.5, True), (0, hw, "plain", 1.0, False), (hw, hw, "plain", 1.0, True)]
    qt, k, vt = _mod_proj(x, shift, scale_v, w, wt, segs, [BF16] * 3)
    tq = min(512, S)
    tk = min(256, S)
    tri = np.triu(np.ones((tk, tk), np.float32), 1)
    o = pl.pallas_call(
        functools.partial(_sb_attn_kernel, tq=tq, tk=tk),
        grid=(B, H),
        in_specs=[pl.BlockSpec((1, d, S), lambda b, h: (b, h, 0)),
                  pl.BlockSpec((1, S, d), lambda b, h: (b, 0, h)),
                  pl.BlockSpec((1, d, S), lambda b, h: (b, h, 0)),
                  pl.BlockSpec((tk, tk), lambda b, h: (0, 0))],
        out_specs=pl.BlockSpec((1, S, d), lambda b, h: (b, 0, h)),
        out_shape=jax.ShapeDtypeStruct((B, S, hw), BF16),
        scratch_shapes=[pltpu.VMEM((1, tq), F32), pltpu.VMEM((d, tq), F32)],
        compiler_params=_cp(("arbitrary", "arbitrary"), 32),
        name="sb_attn",
    )(qt, k, vt, jnp.asarray(tri, BF16))
    return _out_ln(o, w_o, x, gate, ln_g, ln_b, route)


def _ffn_kernel(be_ref, seg_ref, nv_ref, nu_ref, x_ref, wg_ref, wu_ref, wd_ref, y_ref, wg_sc, wu_sc, wd_sc, act_sc):
    i = pl.program_id(0)
    n_blk = be_ref.shape[0]

    def stage(blk, src_ref, dst_sc):
        r = jnp.clip(blk, 0, n_blk - 1)
        first = (blk >= 0) & (blk < nu_ref[0]) & ((r == 0) | (be_ref[r] != be_ref[jnp.maximum(r - 1, 0)]))

        @pl.when(first)
        def _():
            dst_sc[seg_ref[r] % FFN_SLOTS] = src_ref[0, 0].astype(BF16)

    stage(i, wg_ref, wg_sc)
    stage(i - 1, wu_ref, wu_sc)
    stage(i - 2, wd_ref, wd_sc)
    blk = i - FFN_LEAD
    done = blk - 1
    has_new = (blk >= 0) & (blk < nu_ref[0])
    has_done = (done >= 0) & (done < nu_ref[0])
    slot_new = seg_ref[jnp.clip(blk, 0, n_blk - 1)] % FFN_SLOTS
    slot_done = seg_ref[jnp.clip(done, 0, n_blk - 1)] % FFN_SLOTS

    def rows_in():
        live = lax.broadcasted_iota(jnp.int32, (x_ref.shape[0], 1), 0) < nv_ref[jnp.clip(blk, 0, n_blk - 1)]
        return jnp.where(live, _unpack_pairs(x_ref[...]), 0.0).astype(BF16)

    act_in, act_out = (i + 1) % 2, i % 2

    @pl.when(has_new & has_done)
    def _():
        xb = rows_in()
        g = _dot(xb, wg_sc[slot_new])
        u = _dot(xb, wu_sc[slot_new])
        y_ref[...] = _pack_pairs(_dot(act_sc[act_in], wd_sc[slot_done]))
        act_sc[act_out] = (g * jax.nn.sigmoid(g) * u).astype(BF16)

    @pl.when(has_new & jnp.logical_not(has_done))
    def _():
        xb = rows_in()
        g = _dot(xb, wg_sc[slot_new])
        u = _dot(xb, wu_sc[slot_new])
        act_sc[act_out] = (g * jax.nn.sigmoid(g) * u).astype(BF16)

    @pl.when(has_done & jnp.logical_not(has_new))
    def _():
        y_ref[...] = _pack_pairs(_dot(act_sc[act_in], wd_sc[slot_done]))

    @pl.when((done >= nu_ref[0]) & (done < n_blk))
    def _():
        y_ref[...] = jnp.zeros(y_ref.shape, y_ref.dtype)


def _sc_gather(table, idx):
    n, d = idx.shape[0], table.shape[1]
    workers = SC_CORES * SC_SUBCORES
    per_w = n // workers
    assert per_w * workers == n and per_w % (2 * SC_CHUNK) == 0
    mesh = plsc.VectorSubcoreMesh(core_axis_name="c", subcore_axis_name="s")

    @functools.partial(
        pl.kernel, mesh=mesh,
        out_type=jax.ShapeDtypeStruct((n, d), table.dtype),
        scratch_types=[pltpu.VMEM((SC_CHUNK,), jnp.int32), pltpu.VMEM((SC_CHUNK,), jnp.int32),
                       pltpu.VMEM((SC_CHUNK, d), table.dtype), pltpu.VMEM((SC_CHUNK, d), table.dtype)]
                      + [pltpu.SemaphoreType.DMA] * 4,
        name="sc_gather")
    def gather(table_hbm, idx_hbm, out_hbm, idx_a, idx_b, rows_a, rows_b, sem_a, sem_b, sem_c, sem_d):
        wid = lax.axis_index("s") * SC_CORES + lax.axis_index("c")

        @pl.loop(0, per_w // (2 * SC_CHUNK))
        def _(j):
            base_a = pl.multiple_of(wid * per_w + j * (2 * SC_CHUNK), SC_CHUNK)
            base_b = pl.multiple_of(base_a + SC_CHUNK, SC_CHUNK)
            pltpu.sync_copy(idx_hbm.at[pl.ds(base_a, SC_CHUNK)], idx_a)
            pltpu.sync_copy(idx_hbm.at[pl.ds(base_b, SC_CHUNK)], idx_b)
            get_a = pltpu.async_copy(table_hbm.at[idx_a], rows_a, sem_a)
            get_b = pltpu.async_copy(table_hbm.at[idx_b], rows_b, sem_b)
            get_a.wait()
            put_a = pltpu.async_copy(rows_a, out_hbm.at[pl.ds(base_a, SC_CHUNK)], sem_c)
            get_b.wait()
            put_b = pltpu.async_copy(rows_b, out_hbm.at[pl.ds(base_b, SC_CHUNK)], sem_d)
            put_a.wait()
            put_b.wait()

    return gather(table, idx)


def _sc_dispatch(src, dest0, dest1, n_out):
    n, d = src.shape
    workers = SC_CORES * SC_SUBCORES
    per_w = n // workers
    assert per_w * workers == n and per_w % SC_CHUNK == 0
    mesh = plsc.VectorSubcoreMesh(core_axis_name="c", subcore_axis_name="s")

    @functools.partial(
        pl.kernel, mesh=mesh,
        out_type=jax.ShapeDtypeStruct((n_out, d), src.dtype),
        scratch_types=[pltpu.VMEM((SC_CHUNK,), jnp.int32),
                       pltpu.VMEM((SC_CHUNK,), jnp.int32),
                       pltpu.VMEM((SC_CHUNK, d), src.dtype)]
                      + [pltpu.SemaphoreType.DMA] * 3,
        name="sc_dispatch")
    def dispatch(src_hbm, d0_hbm, d1_hbm, out_hbm, i0_v, i1_v, rows_v, sem_a, sem_b, sem_c):
        wid = lax.axis_index("s") * SC_CORES + lax.axis_index("c")

        @pl.loop(0, per_w // SC_CHUNK)
        def _(j):
            base = pl.multiple_of(wid * per_w + j * SC_CHUNK, SC_CHUNK)
            get_0 = pltpu.async_copy(d0_hbm.at[pl.ds(base, SC_CHUNK)], i0_v, sem_a)
            get_1 = pltpu.async_copy(d1_hbm.at[pl.ds(base, SC_CHUNK)], i1_v, sem_b)
            get_r = pltpu.async_copy(src_hbm.at[pl.ds(base, SC_CHUNK)], rows_v, sem_c)
            get_0.wait()
            get_1.wait()
            get_r.wait()
            put_0 = pltpu.async_copy(rows_v, out_hbm.at[i0_v], sem_a)
            put_1 = pltpu.async_copy(rows_v, out_hbm.at[i1_v], sem_b)
            put_0.wait()
            put_1.wait()

    return dispatch(src, dest0, dest1)


def _moe_out_kernel(x_ref, y1_ref, y2_ref, info_ref, gate_ref, g_ref, b_ref, o_ref):
    info = info_ref[0]
    ts = info.shape[1]
    info_tok = jnp.concatenate([info, jnp.zeros((LANES - INFO_ROWS, ts), F32)], axis=0).T
    y = info_tok[:, 4:5] * _unpack_pairs(y1_ref[0, 0]) + info_tok[:, 5:6] * _unpack_pairs(y2_ref[0, 0])
    z = ALPHA * x_ref[0] + gate_ref[0] * y
    o_ref[0] = _layer_norm(z, g_ref[...], b_ref[...])


def _moe_out(x, y12, info, gate, g, b, ts=512):
    B, S, D = x.shape
    ts = min(ts, S)
    tok = pl.BlockSpec((1, ts, D), lambda bi, i: (bi, i, 0))
    return pl.pallas_call(
        _moe_out_kernel,
        grid=(B, S // ts),
        in_specs=[tok,
                  pl.BlockSpec((1, 1, ts, D // 2), lambda bi, i: (0, bi, i, 0)),
                  pl.BlockSpec((1, 1, ts, D // 2), lambda bi, i: (1, bi, i, 0)),
                  pl.BlockSpec((1, INFO_ROWS, ts), lambda bi, i: (bi, 0, i)),
                  pl.BlockSpec((1, 1, D), lambda bi, i: (bi, 0, 0)),
                  pl.BlockSpec((1, D), lambda bi, i: (0, 0)),
                  pl.BlockSpec((1, D), lambda bi, i: (0, 0))],
        out_specs=tok,
        out_shape=jax.ShapeDtypeStruct((B, S, D), F32),
        compiler_params=_cp(("arbitrary", "arbitrary"), 40),
        name="moe_out",
    )(x, y12, y12, info, gate, g.reshape(1, D), b.reshape(1, D))


def _moe_layer(x, routed, gate, layer, w_gate, w_up, w_down, ln_g, ln_b):
    B, S, D = x.shape
    T = B * S
    E, R, F = MOE_E, MOE_ROWS, w_gate.shape[-1]
    hb, info, cnt = routed

    counts = cnt[:, 0].astype(jnp.int32)
    padded = (counts + R - 1) // R * R
    ends = jnp.cumsum(padded)
    starts = ends - padded
    expert_ids = jnp.arange(E, dtype=jnp.int32)

    def per_expert(table, expert):
        return jnp.sum(jnp.where(expert[..., None] == expert_ids, table, 0), axis=-1)

    def row_of(expert, rank):
        return (per_expert(starts, expert.astype(jnp.int32)) + rank.astype(jnp.int32)).reshape(T)

    dest0 = row_of(info[:, 0, :], info[:, 2, :])
    dest1 = row_of(info[:, 1, :], info[:, 3, :])
    n_blk = (T * 2) // R + E
    P = n_blk * R
    blk_exp = jnp.minimum(jnp.sum(jnp.arange(n_blk)[:, None] * R >= ends[None, :], axis=1), E - 1).astype(jnp.int32)
    n_used = (ends[-1] // R).astype(jnp.int32).reshape(1)
    x_rows = _sc_dispatch(hb.reshape(T, D // 2), dest0, dest1, P)

    seg = jnp.concatenate([jnp.zeros((1,), jnp.int32),
                           jnp.cumsum((blk_exp[1:] != blk_exp[:-1]).astype(jnp.int32))])

    live = jnp.clip(per_expert(counts, blk_exp) + per_expert(starts, blk_exp) - jnp.arange(n_blk, dtype=jnp.int32) * R,
                    0, R).astype(jnp.int32)

    def blk_at(offset):
        return lambda i: jnp.clip(i - offset, 0, n_blk - 1)

    y_rows = pl.pallas_call(
        _ffn_kernel,
        grid_spec=pltpu.PrefetchScalarGridSpec(
            num_scalar_prefetch=4,
            grid=(n_blk + FFN_LEAD + 1,),
            in_specs=[pl.BlockSpec((R, D // 2), lambda i, be, sg, nv, nu: (blk_at(FFN_LEAD)(i), 0)),
                      pl.BlockSpec((1, 1, D, F), lambda i, be, sg, nv, nu: (layer, be[blk_at(0)(i)], 0, 0)),
                      pl.BlockSpec((1, 1, D, F), lambda i, be, sg, nv, nu: (layer, be[blk_at(1)(i)], 0, 0)),
                      pl.BlockSpec((1, 1, F, D), lambda i, be, sg, nv, nu: (layer, be[blk_at(2)(i)], 0, 0))],
            out_specs=pl.BlockSpec((R, D // 2), lambda i, be, sg, nv, nu: (blk_at(FFN_LEAD + 1)(i), 0)),
            scratch_shapes=[pltpu.VMEM((FFN_SLOTS, D, F), BF16), pltpu.VMEM((FFN_SLOTS, D, F), BF16),
                            pltpu.VMEM((FFN_SLOTS, F, D), BF16), pltpu.VMEM((2, R, F), BF16)]),
        out_shape=jax.ShapeDtypeStruct((P, D // 2), jnp.uint32),
        compiler_params=_cp(("arbitrary",), 48),
        name="moe_ffn",
    )(blk_exp, seg, live, n_used, x_rows, w_gate, w_up, w_down)

    y12 = _sc_gather(y_rows, jnp.concatenate([dest0, dest1])).reshape(2, B, S, D // 2)
    return _moe_out(x, y12, info, gate, ln_g, ln_b)


def kernel(x, c, positions, ada_w, ada_b, ln_g, ln_b, mla_w_in, mla_g_q, mla_w_uq, mla_g_kv, mla_w_ukv, mla_w_o, nsa_w_in, nsa_pe_k, nsa_pe_v, nsa_w_ck1, nsa_w_ck2, nsa_w_cv1, nsa_w_cv2, nsa_w_o, sb_w_qkv, sb_w_o, moe_w_grp, moe_b_grp, moe_w_rt, moe_b_rt, moe_w_gate, moe_w_up, moe_w_down):
    D = x.shape[-1]
    depth = ada_w.shape[0]
    cos_mla, sin_mla = _rope_tables(positions, MLA_DR)
    cos_nsa, sin_nsa = _rope_tables(positions, NSA_DH)
    mods = _ada_all(c, ada_w, ada_b)

    def mod(i, sub):
        m = mods[2 * i + sub][:, None, :]
        return m[..., :D], m[..., D:2 * D], m[..., 2 * D:]

    for i in range(depth):
        kind, j = i % N_MIXERS, i // N_MIXERS
        shift, scale_v, gate = mod(i, 0)
        shift2, scale2, gate2 = mod(i, 1)
        route = (scale2, shift2, moe_w_grp[i], moe_b_grp[i], moe_w_rt[i], moe_b_rt[i])
        if kind == 0:
            x, *routed = _mla_layer(x, shift, scale_v, gate, cos_mla, sin_mla, mla_w_in[j], mla_g_q[j], mla_w_uq[j],
                                    mla_g_kv[j], mla_w_ukv[j], mla_w_o[j], ln_g[i, 0], ln_b[i, 0], route)
        elif kind == 1:
            x, *routed = _nsa_layer(x, shift, scale_v, gate, cos_nsa, sin_nsa, nsa_w_in[j], nsa_pe_k[j], nsa_pe_v[j],
                                    nsa_w_ck1[j], nsa_w_ck2[j], nsa_w_cv1[j], nsa_w_cv2[j], nsa_w_o[j],
                                    ln_g[i, 0], ln_b[i, 0], route)
        else:
            x, *routed = _sb_layer(x, shift, scale_v, gate, sb_w_qkv[j], sb_w_o[j], ln_g[i, 0], ln_b[i, 0], route)
        x = _moe_layer(x, routed, gate2, i, moe_w_gate, moe_w_up, moe_w_down, ln_g[i, 1], ln_b[i, 1])
    return x
```

```python
import functools

import numpy as np
import jax
import jax.numpy as jnp
from jax import lax
from jax.experimental import pallas as pl
from jax.experimental.pallas import tpu as pltpu
from jax.experimental.pallas import tpu_sc as plsc

BF16 = jnp.bfloat16
F32 = jnp.float32

DEPTH = 4
N_MIXERS = 3
ROPE_THETA = 10000.0
LN_EPS = 1e-5
RMS_EPS = 1e-6
NEG = -1e30
LOG2E = 1.4426950408889634
ALPHA = (2 * DEPTH) ** 0.25

MLA_HEADS, MLA_DN, MLA_DR, MLA_DV = 8, 128, 64, 128
MLA_QR, MLA_KVR = 256, 256

NSA_HEADS, NSA_KVH, NSA_DH = 8, 2, 128
NSA_G = NSA_HEADS // NSA_KVH
NSA_CMP_BLOCK, NSA_CMP_STRIDE = 32, 16
NSA_SLC_BLOCK, NSA_TOPK = 64, 8
SLC_SHIFT = 6
NSA_WINDOW = 512
NSA_BONUS = 1e3
GATE_ROWS = 16
SUM_ROWS = 16

SB_HEADS, SB_DH = 8, 128
SB_EXIT_LOG2 = -160.0

MOE_GROUPS, MOE_EPG = 4, 8
MOE_E = MOE_GROUPS * MOE_EPG
MOE_ROWS = 256
FFN_LEAD = 3
FFN_SLOTS = FFN_LEAD + 2
OUT_LN_ROWS = 256
ROUTER_ROWS = 64
INFO_ROWS = 8

SC_CORES, SC_SUBCORES = 2, 16
SC_CHUNK = 64

LANES = 128
VMEM_BYTES_V7X = 64 * 1024 * 1024

ADA_COLS = 1024
PROJ_TOKENS = 512
PROJ_CHUNK = 256
OUT_LN_TOKENS = 1024
MOE_OUT_TOKENS = 512
MLA_TILE = 512
NSA_Q_TILE, NSA_K_TILE = 256, 256
NSA_SEL_ROWS = 32
SB_Q_TILE, SB_K_TILE = 512, 256
VMEM_MB_SMALL, VMEM_MB_MID, VMEM_MB_LARGE = 32, 40, 48


def _cp(sem, vmem_mb):
    assert vmem_mb * 2**20 < VMEM_BYTES_V7X
    return pltpu.CompilerParams(dimension_semantics=sem, vmem_limit_bytes=vmem_mb * 2**20)


def _nt_dot(a, b):
    return lax.dot_general(a, b, (((1,), (1,)), ((), ())), preferred_element_type=F32)


def _dot(a, b):
    return jnp.dot(a, b, preferred_element_type=F32)


def _split_bf16(a):
    hi = a.astype(BF16)
    lo = (a - hi.astype(F32)).astype(BF16)
    return hi, lo


def _pack_pairs(a):
    n = a.shape[1] // 2
    hi = pltpu.bitcast(a[:, :n].astype(BF16).astype(F32), jnp.uint32)
    lo = pltpu.bitcast(a[:, n:].astype(BF16).astype(F32), jnp.uint32)
    return hi | (lo >> 16)


def _unpack_pairs(u):
    hi = pltpu.bitcast(u & jnp.uint32(0xFFFF0000), F32)
    lo = pltpu.bitcast(u << 16, F32)
    return jnp.concatenate([hi, lo], axis=1)


def _layer_norm(z, g, b):
    mu = jnp.mean(z, axis=-1, keepdims=True)
    d = z - mu
    var = jnp.mean(d * d, axis=-1, keepdims=True)
    return d * lax.rsqrt(var + LN_EPS) * g + b


def _ada_kernel(c_ref, w_ref, b_ref, o_ref):
    c = c_ref[...]
    s_hi, s_lo = _split_bf16(c * jax.nn.sigmoid(c))
    w_hi, w_lo = _split_bf16(w_ref[0])
    o_ref[0] = _dot(s_hi, w_hi) + _dot(s_lo, w_hi) + _dot(s_hi, w_lo) + b_ref[0]


def _ada_all(c, ada_w, ada_b):
    L, two, D, D3 = ada_w.shape
    B = c.shape[0]
    n = L * two
    tn = ADA_COLS
    return pl.pallas_call(
        _ada_kernel,
        grid=(n, D3 // tn),
        in_specs=[pl.BlockSpec((B, D), lambda i, j: (0, 0)),
                  pl.BlockSpec((1, D, tn), lambda i, j: (i, 0, j)),
                  pl.BlockSpec((1, 1, tn), lambda i, j: (i, 0, j))],
        out_specs=pl.BlockSpec((1, B, tn), lambda i, j: (i, 0, j)),
        out_shape=jax.ShapeDtypeStruct((n, B, D3), F32),
        compiler_params=_cp(("arbitrary", "arbitrary"), VMEM_MB_SMALL),
        name="ada",
    )(c, ada_w.reshape(n, D, D3), ada_b.reshape(n, 1, D3))


def _route_tokens(x1, sc, sh, whi, wlo, bcol, tri, base_sc):
    h = x1 * (1.0 + sc) + sh
    h_hi, h_lo = _split_bf16(h)
    logit = _nt_dot(whi, h_hi) + _nt_dot(whi, h_lo) + _nt_dot(wlo, h_hi) + bcol
    tm = logit.shape[1]
    epg = MOE_EPG
    row = lax.broadcasted_iota(jnp.int32, (epg, 1), 0)
    row_f = row.astype(F32)

    def first_max(vals):
        mx = jnp.max(vals, axis=0, keepdims=True)
        idx = jnp.min(jnp.where(vals == mx, row_f, float(epg)), axis=0, keepdims=True)
        return mx, idx

    gl = jnp.where(row < MOE_GROUPS, logit[0:epg], -jnp.inf)
    gmax, g_idx = first_max(gl)
    g_w = 1.0 / jnp.sum(jnp.exp(gl - gmax), axis=0, keepdims=True)
    el = logit[epg:2 * epg]
    for g in range(1, MOE_GROUPS):
        el = jnp.where(g_idx == float(g), logit[epg * (g + 1):epg * (g + 2)], el)
    ep = jnp.exp(el - jnp.max(el, axis=0, keepdims=True))
    prob = ep / jnp.sum(ep, axis=0, keepdims=True)
    p1, i1 = first_max(prob)
    p2, i2 = first_max(jnp.where(row_f == i1, -1.0, prob))
    e1 = g_idx * float(epg) + i1
    e2 = g_idx * float(epg) + i2
    w1 = g_w * p1 / (p1 + p2)
    w2 = g_w * p2 / (p1 + p2)

    erow = lax.broadcasted_iota(jnp.int32, (MOE_E, 1), 0).astype(F32)
    hot1 = erow == e1
    hot2 = erow == e2
    cnt = jnp.where(hot1 | hot2, 1.0, 0.0)
    before = base_sc[:, 0:1] + _dot(cnt.astype(BF16), tri)
    r1 = jnp.sum(jnp.where(hot1, before, 0.0), axis=0, keepdims=True)
    r2 = jnp.sum(jnp.where(hot2, before, 0.0), axis=0, keepdims=True)
    base_sc[...] += jnp.sum(cnt, axis=1, keepdims=True)

    irow = lax.broadcasted_iota(jnp.int32, (INFO_ROWS, 1), 0)
    info = jnp.zeros((INFO_ROWS, tm), F32)
    for k, val in enumerate((e1, e2, r1, r2, w1, w2)):
        info = jnp.where(irow == k, val, info)
    return _pack_pairs(h_hi), info


def _out_ln_kernel(o_ref, w_ref, x_ref, gate_ref, g_ref, b_ref, sc_ref, sh_ref, whi_ref, wlo_ref, br_ref, tri_ref,
                   out_ref, hb_ref, info_ref, cnt_ref, base_sc):
    first = (pl.program_id(0) == 0) & (pl.program_id(1) == 0)

    @pl.when(first)
    def _():
        base_sc[...] = jnp.zeros(base_sc.shape, F32)

    ts = o_ref.shape[1]
    rows = min(OUT_LN_ROWS, ts)
    ys = [_dot(o_ref[0, 0:rows, :], w_ref[...])]
    for c in range(ts // rows):
        if c + 1 < ts // rows:
            ys.append(_dot(o_ref[0, (c + 1) * rows:(c + 2) * rows, :], w_ref[...]))
        sl = slice(c * rows, (c + 1) * rows)
        z = ALPHA * x_ref[0, sl, :] + gate_ref[0] * ys[c]
        x1 = _layer_norm(z, g_ref[...], b_ref[...])
        out_ref[0, sl, :] = x1
        packed, info = _route_tokens(x1, sc_ref[0], sh_ref[0], whi_ref[...], wlo_ref[...], br_ref[...],
                                     tri_ref[...], base_sc)
        hb_ref[0, sl, :] = packed
        info_ref[0, :, sl] = info
    cnt_ref[...] = base_sc[...]


def _out_ln(o, w_o, x, gate, g, b, route, ts=OUT_LN_TOKENS):
    B, S, D = x.shape
    K = o.shape[-1]
    ts = min(ts, S)
    scale2, shift2, w_grp, b_grp, w_rt, b_rt = route
    E, g0 = MOE_E, MOE_EPG
    w_r = jnp.zeros((ROUTER_ROWS, D), F32).at[:MOE_GROUPS].set(w_grp.T).at[g0:g0 + E].set(w_rt.T)
    w_hi = w_r.astype(BF16)
    w_lo = (w_r - w_hi.astype(F32)).astype(BF16)
    b_r = jnp.zeros((ROUTER_ROWS, 1), F32).at[:MOE_GROUPS, 0].set(b_grp).at[g0:g0 + E, 0].set(b_rt)
    tm = min(OUT_LN_ROWS, ts)
    tri = jnp.asarray(np.triu(np.ones((tm, tm), np.float32), 1), BF16)
    const = lambda shape: pl.BlockSpec(shape, lambda bi, i: (0,) * len(shape))
    per_batch = pl.BlockSpec((1, 1, D), lambda bi, i: (bi, 0, 0))
    return pl.pallas_call(
        _out_ln_kernel,
        grid=(B, S // ts),
        in_specs=[pl.BlockSpec((1, ts, K), lambda bi, i: (bi, i, 0)),
                  const((K, D)),
                  pl.BlockSpec((1, ts, D), lambda bi, i: (bi, i, 0)),
                  per_batch, const((1, D)), const((1, D)),
                  per_batch, per_batch,
                  const((ROUTER_ROWS, D)), const((ROUTER_ROWS, D)), const((ROUTER_ROWS, 1)), const((tm, tm))],
        out_specs=[pl.BlockSpec((1, ts, D), lambda bi, i: (bi, i, 0)),
                   pl.BlockSpec((1, ts, D // 2), lambda bi, i: (bi, i, 0)),
                   pl.BlockSpec((1, INFO_ROWS, ts), lambda bi, i: (bi, 0, i)),
                   const((E, LANES))],
        out_shape=[jax.ShapeDtypeStruct((B, S, D), F32),
                   jax.ShapeDtypeStruct((B, S, D // 2), jnp.uint32),
                   jax.ShapeDtypeStruct((B, INFO_ROWS, S), F32),
                   jax.ShapeDtypeStruct((E, LANES), F32)],
        scratch_shapes=[pltpu.VMEM((E, LANES), F32)],
        compiler_params=_cp(("arbitrary", "arbitrary"), VMEM_MB_LARGE),
        name="out_ln_route",
    )(o, w_o.astype(BF16), x, gate, g.reshape(1, D), b.reshape(1, D), scale2, shift2, w_hi, w_lo, b_r, tri)


def _rope_tables(positions, dim):
    inv_freq = 1.0 / (ROPE_THETA ** (jnp.arange(0, dim, 2, dtype=F32) / dim))
    ang = positions.astype(F32)[..., None] * inv_freq
    return jnp.cos(ang), jnp.sin(ang)


def _rot_half_cols(w, half):
    return jnp.concatenate([-w[..., half:], w[..., :half]], axis=-1)


def _mla_front_kernel(x_ref, sc_ref, sh_ref, win_ref, gq_ref, gkv_ref, wqt_ref, wk_ref, wvt_ref, t_ref, tt_ref,
                      qt_ref, k_ref, vt_ref, *, scale):
    hb = (x_ref[0] * (1.0 + sc_ref[0]) + sh_ref[0]).astype(BF16)
    r = _dot(hb, win_ref[...])
    tab = t_ref[0]
    tab_t = tt_ref[0]
    low = lax.broadcasted_iota(jnp.int32, tab.shape, 1) < MLA_DR

    def rms(cx, g):
        return (cx * lax.rsqrt(jnp.mean(cx * cx, axis=-1, keepdims=True) + RMS_EPS) * g).astype(BF16)

    cq = rms(r[:, :MLA_QR], gq_ref[...])
    ckv = rms(r[:, MLA_QR:MLA_QR + MLA_KVR], gkv_ref[...])
    t = r[:, MLA_QR + MLA_KVR:] * tab
    k_rope = jnp.where(low, t + pltpu.roll(t, MLA_DR, axis=1), 0.0).astype(BF16)
    zeros = jnp.zeros((MLA_DR, hb.shape[0]), BF16)
    hd = MLA_DN + 2 * MLA_DR
    q_all = _nt_dot(wqt_ref[...], cq)
    k_all = _dot(ckv, wk_ref[...])
    vt_all = _nt_dot(wvt_ref[...], ckv)
    for h in range(MLA_HEADS):
        qh = q_all[h * hd:(h + 1) * hd]
        tq = qh[MLA_DN:] * tab_t
        qt_ref[0, h, :MLA_DN, :] = (qh[:MLA_DN] * scale).astype(BF16)
        qt_ref[0, h, MLA_DN:MLA_DN + MLA_DR, :] = ((tq[:MLA_DR] + tq[MLA_DR:]) * scale).astype(BF16)
        qt_ref[0, h, MLA_DN + MLA_DR:, :] = zeros
        k_ref[0, h, :, :MLA_DN] = k_all[:, h * MLA_DN:(h + 1) * MLA_DN].astype(BF16)
        k_ref[0, h, :, MLA_DN:] = k_rope
        vt_ref[0, h] = vt_all[h * MLA_DV:(h + 1) * MLA_DV].astype(BF16)


def _flash_causal_kernel(qt_ref, k_ref, vt_ref, o_ref, *, t):
    dv, s_len = vt_ref.shape[2], vt_ref.shape[3]
    pairs = [(qi, kt) for qi in range(s_len // t) for kt in range(qi + 1)]
    kpos = lax.broadcasted_iota(jnp.int32, (t, 1), 0)
    qpos = lax.broadcasted_iota(jnp.int32, (1, t), 1)
    diag = kpos <= qpos

    def scores(j):
        qi, kt = pairs[j]
        return _dot(k_ref[0, 0, kt * t:(kt + 1) * t, :], qt_ref[0, 0, :, qi * t:(qi + 1) * t])

    def value_matmul(pending, l, acc):
        qi, kt, p, a = pending
        acc = a * acc + _dot(vt_ref[0, 0, :, kt * t:(kt + 1) * t], p)
        if kt == qi:
            o_ref[0, qi * t:(qi + 1) * t, :] = (acc / l).T.astype(o_ref.dtype)
        return acc

    s_cur, pending = scores(0), None
    m = l = acc = None
    for j, (qi, kt) in enumerate(pairs):
        s_next = scores(j + 1) if j + 1 < len(pairs) else None
        if pending is not None:
            acc = value_matmul(pending, l, acc)
        if kt == 0:
            m, l, acc = jnp.full((1, t), NEG, F32), jnp.zeros((1, t), F32), jnp.zeros((dv, t), F32)
        s = jnp.where(diag, s_cur, NEG) if kt == qi else s_cur
        m_new = jnp.maximum(m, jnp.max(s, axis=0, keepdims=True))
        p = jnp.exp2(s - m_new)
        a = jnp.exp2(m - m_new)
        l = a * l + jnp.sum(p, axis=0, keepdims=True)
        m = m_new
        pending = (qi, kt, p.astype(BF16), a)
        s_cur = s_next
    value_matmul(pending, l, acc)


def _mla_layer(x, shift, scale_v, gate, cos, sin, w_in, g_q, w_uq, g_kv, w_ukv, w_o, ln_g, ln_b, route):
    B, S, D = x.shape
    H, dn, dr, dv = MLA_HEADS, MLA_DN, MLA_DR, MLA_DV
    half = dr // 2
    kr = w_in[:, MLA_QR + MLA_KVR:]
    win_p = jnp.concatenate([w_in, _rot_half_cols(kr, half)], axis=1).astype(BF16)
    wq = w_uq.reshape(MLA_QR, H, dn + dr)
    wq = jnp.concatenate([wq, _rot_half_cols(wq[..., dn:], half)], axis=-1)
    wqt = jnp.transpose(wq, (1, 2, 0)).reshape(H * (dn + 2 * dr), MLA_QR).astype(BF16)
    wkv = w_ukv.reshape(MLA_KVR, H, dn + dv)
    wk = wkv[..., :dn].reshape(MLA_KVR, H * dn).astype(BF16)
    wvt = jnp.transpose(wkv[..., dn:], (1, 2, 0)).reshape(H * dv, MLA_KVR).astype(BF16)
    tab = jnp.concatenate([cos, cos, sin, sin], axis=-1)
    tab_t = jnp.swapaxes(tab, 1, 2)
    ts = min(PROJ_TOKENS, S)
    nw = win_p.shape[1]
    hd = dn + 2 * dr
    qt, k, vt = pl.pallas_call(
        functools.partial(_mla_front_kernel, scale=LOG2E * (dn + dr) ** -0.5),
        grid=(B, S // ts),
        in_specs=[pl.BlockSpec((1, ts, D), lambda b, i: (b, i, 0)),
                  pl.BlockSpec((1, 1, D), lambda b, i: (b, 0, 0)),
                  pl.BlockSpec((1, 1, D), lambda b, i: (b, 0, 0)),
                  pl.BlockSpec((D, nw), lambda b, i: (0, 0)),
                  pl.BlockSpec((1, MLA_QR), lambda b, i: (0, 0)),
                  pl.BlockSpec((1, MLA_KVR), lambda b, i: (0, 0)),
                  pl.BlockSpec((H * hd, MLA_QR), lambda b, i: (0, 0)),
                  pl.BlockSpec((MLA_KVR, H * dn), lambda b, i: (0, 0)),
                  pl.BlockSpec((H * dv, MLA_KVR), lambda b, i: (0, 0)),
                  pl.BlockSpec((1, ts, LANES), lambda b, i: (b, i, 0)),
                  pl.BlockSpec((1, LANES, ts), lambda b, i: (b, 0, i))],
        out_specs=[pl.BlockSpec((1, H, hd, ts), lambda b, i: (b, 0, 0, i)),
                   pl.BlockSpec((1, H, ts, hd), lambda b, i: (b, 0, i, 0)),
                   pl.BlockSpec((1, H, dv, ts), lambda b, i: (b, 0, 0, i))],
        out_shape=[jax.ShapeDtypeStruct((B, H, hd, S), BF16),
                   jax.ShapeDtypeStruct((B, H, S, hd), BF16),
                   jax.ShapeDtypeStruct((B, H, dv, S), BF16)],
        compiler_params=_cp(("arbitrary", "arbitrary"), VMEM_MB_LARGE),
        name="mla_front",
    )(x, scale_v, shift, win_p, g_q.reshape(1, -1), g_kv.reshape(1, -1), wqt, wk, wvt, tab, tab_t)

    t = min(MLA_TILE, S)
    o = pl.pallas_call(
        functools.partial(_flash_causal_kernel, t=t),
        grid=(B, H),
        in_specs=[pl.BlockSpec((1, 1, hd, S), lambda b, h: (b, h, 0, 0)),
                  pl.BlockSpec((1, 1, S, hd), lambda b, h: (b, h, 0, 0)),
                  pl.BlockSpec((1, 1, dv, S), lambda b, h: (b, h, 0, 0))],
        out_specs=pl.BlockSpec((1, S, dv), lambda b, h: (b, 0, h)),
        out_shape=jax.ShapeDtypeStruct((B, S, H * dv), BF16),
        compiler_params=_cp(("arbitrary", "arbitrary"), VMEM_MB_LARGE),
        name="mla_attn",
    )(qt, k, vt)
    return _out_ln(o, w_o, x, gate, ln_g, ln_b, route)


def _proj_kernel(*refs, segs, chunk, has_rope):
    x_ref, sc_ref, sh_ref, w_ref, wt_ref = refs[:5]
    n_in = 9 if has_rope else 5
    outs = refs[n_in:]
    hb = (x_ref[0] * (1.0 + sc_ref[0]) + sh_ref[0]).astype(BF16)
    if has_rope:
        cos, sin = refs[5][0], refs[6][0]
        cos_t, sin_t = refs[7][0], refs[8][0]
    half = LANES // 2
    for (c0, width, kind, scale, transposed), o_ref in zip(segs, outs):
        for j in range(0, width, chunk):
            cw = min(chunk, width - j)
            if transposed:
                r = _nt_dot(wt_ref[c0 + j:c0 + j + cw, :], hb)
            else:
                r = _dot(hb, w_ref[:, c0 + j:c0 + j + cw])
            if kind == "rope":
                parts = []
                for t in range(0, cw, LANES):
                    if transposed:
                        u = r[t:t + LANES]
                        parts.append(u * cos_t + pltpu.roll(u, half, axis=0) * sin_t)
                    else:
                        u = r[:, t:t + LANES]
                        parts.append(u * cos + pltpu.roll(u, half, axis=1) * sin)
                r = jnp.concatenate(parts, axis=0 if transposed else 1) if len(parts) > 1 else parts[0]
            elif kind == "sigmoid":
                r = jax.nn.sigmoid(r)
            if scale != 1.0:
                r = r * scale
            if transposed:
                o_ref[0, j:j + cw, :] = r.astype(o_ref.dtype)
            else:
                o_ref[0, :, j:j + cw] = r.astype(o_ref.dtype)


def _mod_proj(x, shift, scale_v, w, wt, segs, out_dtypes, rope=None, ts=PROJ_TOKENS, chunk=PROJ_CHUNK):
    B, S, D = x.shape
    ts = min(ts, S)
    in_specs = [pl.BlockSpec((1, ts, D), lambda b, i: (b, i, 0)),
                pl.BlockSpec((1, 1, D), lambda b, i: (b, 0, 0)),
                pl.BlockSpec((1, 1, D), lambda b, i: (b, 0, 0)),
                pl.BlockSpec(w.shape, lambda b, i: (0, 0)),
                pl.BlockSpec(wt.shape, lambda b, i: (0, 0))]
    args = [x, scale_v, shift, w, wt]
    if rope is not None:
        cos, sin = rope
        in_specs += [pl.BlockSpec((1, ts, LANES), lambda b, i: (b, i, 0))] * 2
        in_specs += [pl.BlockSpec((1, LANES, ts), lambda b, i: (b, 0, i))] * 2
        args += [cos, sin, jnp.swapaxes(cos, 1, 2), jnp.swapaxes(sin, 1, 2)]
    out_specs, out_shape = [], []
    for sg, dt in zip(segs, out_dtypes):
        if sg[4]:
            out_specs.append(pl.BlockSpec((1, sg[1], ts), lambda b, i: (b, 0, i)))
            out_shape.append(jax.ShapeDtypeStruct((B, sg[1], S), dt))
        else:
            out_specs.append(pl.BlockSpec((1, ts, sg[1]), lambda b, i: (b, i, 0)))
            out_shape.append(jax.ShapeDtypeStruct((B, S, sg[1]), dt))
    return pl.pallas_call(
        functools.partial(_proj_kernel, segs=tuple(segs), chunk=chunk, has_rope=rope is not None),
        grid=(B, S // ts),
        in_specs=in_specs,
        out_specs=out_specs,
        out_shape=out_shape,
        compiler_params=_cp(("arbitrary", "arbitrary"), VMEM_MB_LARGE),
        name="mod_proj",
    )(*args)


def _nsa_cmp_kernel(kc_ref, vc_ref, pek_ref, pev_ref, wk1_ref, wk2_ref, wv1_ref, wv2t_ref, ko_ref, vto_ref):
    half = NSA_CMP_BLOCK // 2
    nchunk = kc_ref.shape[1]

    def hidden(src, pe_ref, w1_ref, kh):
        a = jnp.zeros((nchunk, w1_ref.shape[2]), F32)
        bm = jnp.zeros((nchunk, w1_ref.shape[2]), F32)
        for l in range(half):
            c0 = (l * NSA_KVH + kh) * NSA_DH
            t = src[0, :, c0:c0 + NSA_DH].astype(F32)
            a = a + _dot((t + pe_ref[l:l + 1, :]).astype(BF16), w1_ref[l])
            bm = bm + _dot((t + pe_ref[half + l:half + l + 1, :]).astype(BF16), w1_ref[half + l])
        hid = a + pltpu.roll(bm, nchunk - 1, axis=0)
        return (hid * jax.nn.sigmoid(hid)).astype(BF16)

    rowi = lax.broadcasted_iota(jnp.int32, (nchunk, 1), 0)
    coli = lax.broadcasted_iota(jnp.int32, (1, nchunk), 1)
    for kh in range(NSA_KVH):
        out = _dot(hidden(kc_ref, pek_ref, wk1_ref, kh), wk2_ref[...])
        ko_ref[0, kh] = jnp.where(rowi < nchunk - 1, out, 0.0).astype(ko_ref.dtype)
        out_t = _nt_dot(wv2t_ref[...], hidden(vc_ref, pev_ref, wv1_ref, kh))
        vto_ref[0, kh] = jnp.where(coli < nchunk - 1, out_t, 0.0).astype(vto_ref.dtype)


def _nsa_attn_kernel(qt_ref, kc_ref, vct_ref, ks_ref, vst_ref, kw_ref, vwt_ref, gt_ref, ovt_ref, e_ref, o_ref,
                     q4_sc, m_sc, acc_sc, *, tq, tk, span, n_slc):
    qi = pl.program_id(2)
    G = NSA_G
    for g in range(G):
        q4_sc[:, g * tq:(g + 1) * tq] = qt_ref[0, g * NSA_DH:(g + 1) * NSA_DH, :]
    q4 = q4_sc[...]
    qpos = qi * tq + lax.broadcasted_iota(jnp.int32, (1, tq), 1)

    def tile_g(a):
        return jnp.concatenate([a] * G, axis=1)

    qpos4 = tile_g(qpos)

    kstart = pl.multiple_of(jnp.maximum(qi * tq + tq - span, 0), tq)
    s_win = _dot(kw_ref[0, pl.ds(kstart, span), :], q4)

    n_cmp = kc_ref.shape[2]
    nid = lax.broadcasted_iota(jnp.int32, (n_cmp, 1), 0)
    cmask = (nid * NSA_CMP_STRIDE + (NSA_CMP_BLOCK - 1) <= qpos4) & (nid < n_cmp - 1)
    s = jnp.where(cmask, _dot(kc_ref[0, 0], q4), NEG)
    p = jnp.where(cmask, jnp.exp2(s - jnp.max(s, axis=0, keepdims=True)), 0.0)
    l = jnp.sum(p, axis=0, keepdims=True)
    p_cmp = p / jnp.where(l > 0.0, l, 1.0)
    o_cmp = _dot(vct_ref[0, 0], p_cmp.astype(BF16))

    diff = qpos - (kstart + lax.broadcasted_iota(jnp.int32, (span, 1), 0))
    s_win = s_win + tile_g(jnp.where((diff >= 0) & (diff < NSA_WINDOW), 0.0, NEG))
    m_win = jnp.max(s_win, axis=0, keepdims=True)
    ow = jnp.zeros((NSA_DH + SUM_ROWS, G * tq), F32)
    for c0 in range(0, span, tk):
        c1 = min(c0 + tk, span)
        p = jnp.exp2(s_win[c0:c1] - m_win).astype(BF16)
        v_aug = jnp.concatenate([vwt_ref[0, :, pl.ds(kstart + c0, c1 - c0)], jnp.ones((SUM_ROWS, c1 - c0), BF16)], axis=0)
        ow = ow + _dot(v_aug, p)
    o_win = ow[:NSA_DH] / ow[NSA_DH:NSA_DH + 1]

    psum = p_cmp[:, 0:tq]
    for g in range(1, G):
        psum = psum + p_cmp[:, g * tq:(g + 1) * tq]
    p_hi, p_lo = _split_bf16(psum)
    imp = _dot(ovt_ref[...], p_hi) + _dot(ovt_ref[...], p_lo)
    nb = ovt_ref.shape[0]
    blk = lax.broadcasted_iota(jnp.int32, (nb, 1), 0)
    cur = jnp.right_shift(qpos, SLC_SHIFT)
    forced = (blk == 0) | (blk == cur) | (blk == cur - 1)
    valid = (blk * NSA_SLC_BLOCK <= qpos) & (blk < n_slc)
    work = jnp.where(valid, imp + jnp.where(forced, NSA_BONUS, 0.0), -jnp.inf)
    beaten = jnp.zeros((nb, tq), F32)
    for j in range(n_slc):
        wj = work[j:j + 1, :]
        beaten = beaten + jnp.where((wj > work) | ((wj == work) & (blk > j)), 1.0, 0.0)
    sel = jnp.where(beaten < float(min(NSA_TOPK, n_slc)), 1.0, 0.0).astype(BF16)

    m_sc[...] = jnp.full(m_sc.shape, NEG, F32)
    acc_sc[...] = jnp.zeros(acc_sc.shape, F32)
    n_kt = ((qi + 1) * tq + tk - 1) // tk

    def slc_tiles(n):
        def scores(kt):
            return _dot(ks_ref[0, kt * tk:(kt + 1) * tk, :], q4)

        ones = jnp.ones((SUM_ROWS, tk), BF16)

        def value_matmul(kt, p, a):
            v_aug = jnp.concatenate([vst_ref[0, :, kt * tk:(kt + 1) * tk], ones], axis=0)
            acc_sc[...] = a * acc_sc[...] + _dot(v_aug, p)

        s, pending = scores(0), None
        chosen = _dot(e_ref[0:n * tk, :], sel)
        kpos = lax.broadcasted_iota(jnp.int32, (n * tk, 1), 0)
        bias_all = jnp.where((chosen > 0.5) & (kpos <= qpos), 0.0, NEG)
        for kt in range(n):
            s_next = scores(kt + 1) if kt + 1 < n else None
            if pending is not None:
                value_matmul(*pending)
            s = s + tile_g(bias_all[kt * tk:(kt + 1) * tk])
            m_prev = m_sc[...]
            m_new = jnp.maximum(m_prev, jnp.max(s, axis=0, keepdims=True))
            p = jnp.exp2(s - m_new)
            a = jnp.exp2(m_prev - m_new)
            m_sc[...] = m_new
            pending = (kt, p.astype(BF16), a)
            s = s_next
        value_matmul(*pending)

    for n in range(1, (ks_ref.shape[1] + tk - 1) // tk + 1):
        pl.when(n_kt == n)(functools.partial(slc_tiles, n))
    o_slc = acc_sc[0:NSA_DH, :] / acc_sc[NSA_DH:NSA_DH + 1, :]

    gt = gt_ref[0]
    for g in range(G):
        cols = slice(g * tq, (g + 1) * tq)
        o = (gt[3 * g:3 * g + 1] * o_cmp[:, cols] + gt[3 * g + 1:3 * g + 2] * o_slc[:, cols]
             + gt[3 * g + 2:3 * g + 3] * o_win[:, cols])
        o_ref[0, :, g * NSA_DH:(g + 1) * NSA_DH] = o.T.astype(o_ref.dtype)


def _nsa_layer(x, shift, scale_v, gate, cos, sin, w_in, pe_k, pe_v, w_ck1, w_ck2, w_cv1, w_cv2, w_o,
               ln_g, ln_b, route):
    B, S, D = x.shape
    H, KH, Dh, G = NSA_HEADS, NSA_KVH, NSA_DH, NSA_G
    kvw = KH * Dh
    cuts = [int(v) for v in np.cumsum([H * Dh] + [kvw] * 6)]
    wq, wkc, wvc, wks, wvs, wkw, wvw, wgl = jnp.split(w_in, cuts, axis=1)
    wgl = jnp.pad(wgl.reshape(D, KH, G * 3), ((0, 0), (0, 0), (0, GATE_ROWS - G * 3))).reshape(D, KH * GATE_ROWS)
    w = jnp.concatenate([wkc, wks, wkw, wvc], axis=1).astype(BF16)
    wt = jnp.concatenate([wq, wvs, wvw, wgl], axis=1).T.astype(BF16)
    segs = [(0, H * Dh, "rope", LOG2E * Dh ** -0.5, True),
            (0, kvw, "rope", 1.0, False), (kvw, kvw, "rope", 1.0, False), (2 * kvw, kvw, "rope", 1.0, False),
            (3 * kvw, kvw, "plain", 1.0, False),
            (H * Dh, kvw, "plain", 1.0, True), (H * Dh + kvw, kvw, "plain", 1.0, True),
            (H * Dh + 2 * kvw, KH * GATE_ROWS, "sigmoid", 1.0, True)]
    rope = (jnp.concatenate([cos, cos], axis=-1), jnp.concatenate([-sin, sin], axis=-1))
    qt, kc, ks, kw, vc, vst, vwt, gates_t = _mod_proj(
        x, shift, scale_v, w, wt, segs, [BF16] * 7 + [F32], rope=rope)

    st = NSA_CMP_STRIDE
    nchunk = S // st
    kc2 = kc.reshape(B, nchunk, st * kvw)
    vc2 = vc.reshape(B, nchunk, st * kvw)
    hid = w_ck1.shape[-1]
    full = lambda shape: pl.BlockSpec(shape, lambda b: (0,) * len(shape))
    k_cmp, v_cmp_t = pl.pallas_call(
        _nsa_cmp_kernel,
        grid=(B,),
        in_specs=[pl.BlockSpec((1, nchunk, st * kvw), lambda b: (b, 0, 0)),
                  pl.BlockSpec((1, nchunk, st * kvw), lambda b: (b, 0, 0)),
                  full((NSA_CMP_BLOCK, Dh)), full((NSA_CMP_BLOCK, Dh)),
                  full((NSA_CMP_BLOCK, Dh, hid)), full((hid, Dh)),
                  full((NSA_CMP_BLOCK, Dh, hid)), full((Dh, hid))],
        out_specs=[pl.BlockSpec((1, KH, nchunk, Dh), lambda b: (b, 0, 0, 0)),
                   pl.BlockSpec((1, KH, Dh, nchunk), lambda b: (b, 0, 0, 0))],
        out_shape=[jax.ShapeDtypeStruct((B, KH, nchunk, Dh), BF16),
                   jax.ShapeDtypeStruct((B, KH, Dh, nchunk), BF16)],
        compiler_params=_cp(("arbitrary",), VMEM_MB_SMALL),
        name="nsa_cmp",
    )(kc2, vc2, pe_k, pe_v, w_ck1.astype(BF16), w_ck2.astype(BF16), w_cv1.astype(BF16), w_cv2.T.astype(BF16))

    n_slc = S // NSA_SLC_BLOCK
    nb = NSA_SEL_ROWS
    assert n_slc <= nb and nchunk % 8 == 0
    cmp_start = np.arange(nchunk) * st
    slc_start = np.arange(n_slc) * NSA_SLC_BLOCK
    ov = np.clip(np.minimum(cmp_start[:, None] + NSA_CMP_BLOCK, slc_start[None, :] + NSA_SLC_BLOCK)
                 - np.maximum(cmp_start[:, None], slc_start[None, :]), 0, None).astype(np.float32) / NSA_CMP_BLOCK
    ov[nchunk - 1:] = 0.0
    ov_t = np.zeros((nb, nchunk), np.float32)
    ov_t[:n_slc] = ov.T
    key_blk = np.zeros((S, nb), np.float32)
    key_blk[np.arange(S), np.arange(S) // NSA_SLC_BLOCK] = 1.0

    tq = min(NSA_Q_TILE, S)
    tk = min(NSA_K_TILE, S)
    span = min(NSA_WINDOW + tq, S)
    k_spec = pl.BlockSpec((1, S, Dh), lambda b, kh, i: (b, 0, kh))
    vt_spec = pl.BlockSpec((1, Dh, S), lambda b, kh, i: (b, kh, 0))
    o = pl.pallas_call(
        functools.partial(_nsa_attn_kernel, tq=tq, tk=tk, span=span, n_slc=n_slc),
        grid=(B, KH, S // tq),
        in_specs=[pl.BlockSpec((1, G * Dh, tq), lambda b, kh, i: (b, kh, i)),
                  pl.BlockSpec((1, 1, nchunk, Dh), lambda b, kh, i: (b, kh, 0, 0)),
                  pl.BlockSpec((1, 1, Dh, nchunk), lambda b, kh, i: (b, kh, 0, 0)),
                  k_spec, vt_spec, k_spec, vt_spec,
                  pl.BlockSpec((1, GATE_ROWS, tq), lambda b, kh, i: (b, kh, i)),
                  pl.BlockSpec((nb, nchunk), lambda b, kh, i: (0, 0)),
                  pl.BlockSpec((S, nb), lambda b, kh, i: (0, 0))],
        out_specs=pl.BlockSpec((1, tq, G * Dh), lambda b, kh, i: (b, i, kh)),
        out_shape=jax.ShapeDtypeStruct((B, S, H * Dh), BF16),
        scratch_shapes=[pltpu.VMEM((Dh, G * tq), BF16), pltpu.VMEM((1, G * tq), F32),
                        pltpu.VMEM((Dh + SUM_ROWS, G * tq), F32)],
        compiler_params=_cp(("arbitrary", "arbitrary", "arbitrary"), VMEM_MB_SMALL),
        name="nsa_attn",
    )(qt, k_cmp, v_cmp_t, ks, vst, kw, vwt, gates_t, jnp.asarray(ov_t, BF16), jnp.asarray(key_blk, BF16))
    return _out_ln(o, w_o, x, gate, ln_g, ln_b, route)


def _sb_attn_kernel(qt_ref, k_ref, vt_ref, u_ref, o_ref, r_sc, acc_sc, *, tq, tk):
    d, s_len = vt_ref.shape[1], vt_ref.shape[2]
    u = u_ref[...]
    nd = tq // tk

    def weights(z, strict):
        lk = -(jnp.maximum(z, 0.0) + jnp.log2(1.0 + jnp.exp2(-jnp.abs(z))))
        if strict is not None:
            lk = jnp.where(strict, lk, 0.0)
        between = r_sc[...] + _dot(u, lk.astype(BF16))
        a = jnp.exp2(lk + z + between)
        if strict is not None:
            a = jnp.where(strict, a, 0.0)
        r_sc[...] += jnp.sum(lk, axis=0, keepdims=True)
        return a.astype(BF16)

    for qi in range(s_len // tq):
        qt = qt_ref[0, :, qi * tq:(qi + 1) * tq]
        r_sc[...] = jnp.zeros(r_sc.shape, F32)
        acc_sc[...] = jnp.zeros(acc_sc.shape, F32)
        last = (qi + 1) * nd - 1
        tiles = [last - i for i in range(nd)] + ([qi * nd - 1] if qi > 0 else [])
        qpos = qi * tq + lax.broadcasted_iota(jnp.int32, (1, tq), 1)

        def scores(kt):
            return _dot(k_ref[0, kt * tk:(kt + 1) * tk, :], qt)

        z, pending = scores(tiles[0]), None
        for i, kt in enumerate(tiles):
            nxt = tiles[i + 1] if i + 1 < len(tiles) else max(qi * nd - 2, 0)
            z_next = scores(nxt)
            if pending is not None:
                acc_sc[...] += _dot(vt_ref[0, :, pending[0] * tk:(pending[0] + 1) * tk], pending[1])
            strict = (kt * tk + lax.broadcasted_iota(jnp.int32, (tk, 1), 0) < qpos) if kt >= qi * nd else None
            pending = (kt, weights(z, strict))
            z = z_next
        acc_sc[...] += _dot(vt_ref[0, :, pending[0] * tk:(pending[0] + 1) * tk], pending[1])

        if qi * nd - 2 >= 0:
            def scores_dyn(kt):
                return _dot(k_ref[0, pl.ds(pl.multiple_of(kt * tk, tk), tk), :], qt)

            def cond(carry):
                kt, _ = carry
                return (kt >= 0) & (jnp.max(r_sc[...]) > SB_EXIT_LOG2)

            def body(carry):
                kt, z = carry
                z_next = scores_dyn(jnp.maximum(kt - 1, 0))
                a = weights(z, None)
                acc_sc[...] += _dot(vt_ref[0, :, pl.ds(pl.multiple_of(kt * tk, tk), tk)], a)
                return kt - 1, z_next

            lax.while_loop(cond, body, (qi * nd - 2, z))
        o_ref[0, qi * tq:(qi + 1) * tq, :] = acc_sc[...].T.astype(o_ref.dtype)


def _sb_layer(x, shift, scale_v, gate, w_qkv, w_o, ln_g, ln_b, route):
    B, S, D = x.shape
    H, d = SB_HEADS, SB_DH
    hw = H * d
    w = w_qkv[:, hw:2 * hw].astype(BF16)
    wt = jnp.concatenate([w_qkv[:, :hw], w_qkv[:, 2 * hw:]], axis=1).T.astype(BF16)
    segs = [(0, hw, "plain", LOG2E * d ** -0.5, True), (0, hw, "plain", 1.0, False), (hw, hw, "plain", 1.0, True)]
    qt, k, vt = _mod_proj(x, shift, scale_v, w, wt, segs, [BF16] * 3)
    tq = min(SB_Q_TILE, S)
    tk = min(SB_K_TILE, S)
    tri = np.triu(np.ones((tk, tk), np.float32), 1)
    o = pl.pallas_call(
        functools.partial(_sb_attn_kernel, tq=tq, tk=tk),
        grid=(B, H),
        in_specs=[pl.BlockSpec((1, d, S), lambda b, h: (b, h, 0)),
                  pl.BlockSpec((1, S, d), lambda b, h: (b, 0, h)),
                  pl.BlockSpec((1, d, S), lambda b, h: (b, h, 0)),
                  pl.BlockSpec((tk, tk), lambda b, h: (0, 0))],
        out_specs=pl.BlockSpec((1, S, d), lambda b, h: (b, 0, h)),
        out_shape=jax.ShapeDtypeStruct((B, S, hw), BF16),
        scratch_shapes=[pltpu.VMEM((1, tq), F32), pltpu.VMEM((d, tq), F32)],
        compiler_params=_cp(("arbitrary", "arbitrary"), VMEM_MB_SMALL),
        name="sb_attn",
    )(qt, k, vt, jnp.asarray(tri, BF16))
    return _out_ln(o, w_o, x, gate, ln_g, ln_b, route)


def _ffn_kernel(be_ref, seg_ref, nv_ref, nu_ref, x_ref, wg_ref, wu_ref, wd_ref, y_ref, wg_sc, wu_sc, wd_sc, act_sc):
    i = pl.program_id(0)
    n_blk = be_ref.shape[0]

    def stage(blk, src_ref, dst_sc):
        r = jnp.clip(blk, 0, n_blk - 1)
        first = (blk >= 0) & (blk < nu_ref[0]) & ((r == 0) | (be_ref[r] != be_ref[jnp.maximum(r - 1, 0)]))

        @pl.when(first)
        def _():
            dst_sc[seg_ref[r] % FFN_SLOTS] = src_ref[0, 0].astype(BF16)

    stage(i, wg_ref, wg_sc)
    stage(i - 1, wu_ref, wu_sc)
    stage(i - 2, wd_ref, wd_sc)
    blk = i - FFN_LEAD
    done = blk - 1
    has_new = (blk >= 0) & (blk < nu_ref[0])
    has_done = (done >= 0) & (done < nu_ref[0])
    slot_new = seg_ref[jnp.clip(blk, 0, n_blk - 1)] % FFN_SLOTS
    slot_done = seg_ref[jnp.clip(done, 0, n_blk - 1)] % FFN_SLOTS

    def rows_in():
        live = lax.broadcasted_iota(jnp.int32, (x_ref.shape[0], 1), 0) < nv_ref[jnp.clip(blk, 0, n_blk - 1)]
        return jnp.where(live, _unpack_pairs(x_ref[...]), 0.0).astype(BF16)

    act_in, act_out = (i + 1) % 2, i % 2

    @pl.when(has_new & has_done)
    def _():
        xb = rows_in()
        g = _dot(xb, wg_sc[slot_new])
        u = _dot(xb, wu_sc[slot_new])
        y_ref[...] = _pack_pairs(_dot(act_sc[act_in], wd_sc[slot_done]))
        act_sc[act_out] = (g * jax.nn.sigmoid(g) * u).astype(BF16)

    @pl.when(has_new & jnp.logical_not(has_done))
    def _():
        xb = rows_in()
        g = _dot(xb, wg_sc[slot_new])
        u = _dot(xb, wu_sc[slot_new])
        act_sc[act_out] = (g * jax.nn.sigmoid(g) * u).astype(BF16)

    @pl.when(has_done & jnp.logical_not(has_new))
    def _():
        y_ref[...] = _pack_pairs(_dot(act_sc[act_in], wd_sc[slot_done]))

    @pl.when((done >= nu_ref[0]) & (done < n_blk))
    def _():
        y_ref[...] = jnp.zeros(y_ref.shape, y_ref.dtype)


def _sc_gather(table, idx):
    n, d = idx.shape[0], table.shape[1]
    workers = SC_CORES * SC_SUBCORES
    per_w = n // workers
    assert per_w * workers == n and per_w % (2 * SC_CHUNK) == 0
    mesh = plsc.VectorSubcoreMesh(core_axis_name="c", subcore_axis_name="s")

    @functools.partial(
        pl.kernel, mesh=mesh,
        out_type=jax.ShapeDtypeStruct((n, d), table.dtype),
        scratch_types=[pltpu.VMEM((SC_CHUNK,), jnp.int32), pltpu.VMEM((SC_CHUNK,), jnp.int32),
                       pltpu.VMEM((SC_CHUNK, d), table.dtype), pltpu.VMEM((SC_CHUNK, d), table.dtype)]
                      + [pltpu.SemaphoreType.DMA] * 4,
        name="sc_gather")
    def gather(table_hbm, idx_hbm, out_hbm, idx_a, idx_b, rows_a, rows_b, sem_a, sem_b, sem_c, sem_d):
        wid = lax.axis_index("s") * SC_CORES + lax.axis_index("c")

        @pl.loop(0, per_w // (2 * SC_CHUNK))
        def _(j):
            base_a = pl.multiple_of(wid * per_w + j * (2 * SC_CHUNK), SC_CHUNK)
            base_b = pl.multiple_of(base_a + SC_CHUNK, SC_CHUNK)
            pltpu.sync_copy(idx_hbm.at[pl.ds(base_a, SC_CHUNK)], idx_a)
            pltpu.sync_copy(idx_hbm.at[pl.ds(base_b, SC_CHUNK)], idx_b)
            get_a = pltpu.async_copy(table_hbm.at[idx_a], rows_a, sem_a)
            get_b = pltpu.async_copy(table_hbm.at[idx_b], rows_b, sem_b)
            get_a.wait()
            put_a = pltpu.async_copy(rows_a, out_hbm.at[pl.ds(base_a, SC_CHUNK)], sem_c)
            get_b.wait()
            put_b = pltpu.async_copy(rows_b, out_hbm.at[pl.ds(base_b, SC_CHUNK)], sem_d)
            put_a.wait()
            put_b.wait()

    return gather(table, idx)


def _sc_dispatch(src, dest0, dest1, n_out):
    n, d = src.shape
    workers = SC_CORES * SC_SUBCORES
    per_w = n // workers
    assert per_w * workers == n and per_w % SC_CHUNK == 0
    mesh = plsc.VectorSubcoreMesh(core_axis_name="c", subcore_axis_name="s")

    @functools.partial(
        pl.kernel, mesh=mesh,
        out_type=jax.ShapeDtypeStruct((n_out, d), src.dtype),
        scratch_types=[pltpu.VMEM((SC_CHUNK,), jnp.int32),
                       pltpu.VMEM((SC_CHUNK,), jnp.int32),
                       pltpu.VMEM((SC_CHUNK, d), src.dtype)]
                      + [pltpu.SemaphoreType.DMA] * 3,
        name="sc_dispatch")
    def dispatch(src_hbm, d0_hbm, d1_hbm, out_hbm, i0_v, i1_v, rows_v, sem_a, sem_b, sem_c):
        wid = lax.axis_index("s") * SC_CORES + lax.axis_index("c")

        @pl.loop(0, per_w // SC_CHUNK)
        def _(j):
            base = pl.multiple_of(wid * per_w + j * SC_CHUNK, SC_CHUNK)
            get_0 = pltpu.async_copy(d0_hbm.at[pl.ds(base, SC_CHUNK)], i0_v, sem_a)
            get_1 = pltpu.async_copy(d1_hbm.at[pl.ds(base, SC_CHUNK)], i1_v, sem_b)
            get_r = pltpu.async_copy(src_hbm.at[pl.ds(base, SC_CHUNK)], rows_v, sem_c)
            get_0.wait()
            get_1.wait()
            get_r.wait()
            put_0 = pltpu.async_copy(rows_v, out_hbm.at[i0_v], sem_a)
            put_1 = pltpu.async_copy(rows_v, out_hbm.at[i1_v], sem_b)
            put_0.wait()
            put_1.wait()

    return dispatch(src, dest0, dest1)


def _moe_out_kernel(x_ref, y1_ref, y2_ref, info_ref, gate_ref, g_ref, b_ref, o_ref):
    info = info_ref[0]
    ts = info.shape[1]
    info_tok = jnp.concatenate([info, jnp.zeros((LANES - INFO_ROWS, ts), F32)], axis=0).T
    y = info_tok[:, 4:5] * _unpack_pairs(y1_ref[0, 0]) + info_tok[:, 5:6] * _unpack_pairs(y2_ref[0, 0])
    z = ALPHA * x_ref[0] + gate_ref[0] * y
    o_ref[0] = _layer_norm(z, g_ref[...], b_ref[...])


def _moe_out(x, y12, info, gate, g, b, ts=MOE_OUT_TOKENS):
    B, S, D = x.shape
    ts = min(ts, S)
    tok = pl.BlockSpec((1, ts, D), lambda bi, i: (bi, i, 0))
    return pl.pallas_call(
        _moe_out_kernel,
        grid=(B, S // ts),
        in_specs=[tok,
                  pl.BlockSpec((1, 1, ts, D // 2), lambda bi, i: (0, bi, i, 0)),
                  pl.BlockSpec((1, 1, ts, D // 2), lambda bi, i: (1, bi, i, 0)),
                  pl.BlockSpec((1, INFO_ROWS, ts), lambda bi, i: (bi, 0, i)),
                  pl.BlockSpec((1, 1, D), lambda bi, i: (bi, 0, 0)),
                  pl.BlockSpec((1, D), lambda bi, i: (0, 0)),
                  pl.BlockSpec((1, D), lambda bi, i: (0, 0))],
        out_specs=tok,
        out_shape=jax.ShapeDtypeStruct((B, S, D), F32),
        compiler_params=_cp(("arbitrary", "arbitrary"), VMEM_MB_MID),
        name="moe_out",
    )(x, y12, y12, info, gate, g.reshape(1, D), b.reshape(1, D))


def _moe_layer(x, routed, gate, layer, w_gate, w_up, w_down, ln_g, ln_b):
    B, S, D = x.shape
    T = B * S
    E, R, F = MOE_E, MOE_ROWS, w_gate.shape[-1]
    hb, info, cnt = routed

    counts = cnt[:, 0].astype(jnp.int32)
    padded = (counts + R - 1) // R * R
    ends = jnp.cumsum(padded)
    starts = ends - padded
    expert_ids = jnp.arange(E, dtype=jnp.int32)

    def per_expert(table, expert):
        return jnp.sum(jnp.where(expert[..., None] == expert_ids, table, 0), axis=-1)

    def row_of(expert, rank):
        return (per_expert(starts, expert.astype(jnp.int32)) + rank.astype(jnp.int32)).reshape(T)

    dest0 = row_of(info[:, 0, :], info[:, 2, :])
    dest1 = row_of(info[:, 1, :], info[:, 3, :])
    n_blk = (T * 2) // R + E
    P = n_blk * R
    blk_exp = jnp.minimum(jnp.sum(jnp.arange(n_blk)[:, None] * R >= ends[None, :], axis=1), E - 1).astype(jnp.int32)
    n_used = (ends[-1] // R).astype(jnp.int32).reshape(1)
    x_rows = _sc_dispatch(hb.reshape(T, D // 2), dest0, dest1, P)

    seg = jnp.concatenate([jnp.zeros((1,), jnp.int32),
                           jnp.cumsum((blk_exp[1:] != blk_exp[:-1]).astype(jnp.int32))])

    live = jnp.clip(per_expert(counts, blk_exp) + per_expert(starts, blk_exp) - jnp.arange(n_blk, dtype=jnp.int32) * R,
                    0, R).astype(jnp.int32)

    def blk_at(offset):
        return lambda i: jnp.clip(i - offset, 0, n_blk - 1)

    y_rows = pl.pallas_call(
        _ffn_kernel,
        grid_spec=pltpu.PrefetchScalarGridSpec(
            num_scalar_prefetch=4,
            grid=(n_blk + FFN_LEAD + 1,),
            in_specs=[pl.BlockSpec((R, D // 2), lambda i, be, sg, nv, nu: (blk_at(FFN_LEAD)(i), 0)),
                      pl.BlockSpec((1, 1, D, F), lambda i, be, sg, nv, nu: (layer, be[blk_at(0)(i)], 0, 0)),
                      pl.BlockSpec((1, 1, D, F), lambda i, be, sg, nv, nu: (layer, be[blk_at(1)(i)], 0, 0)),
                      pl.BlockSpec((1, 1, F, D), lambda i, be, sg, nv, nu: (layer, be[blk_at(2)(i)], 0, 0))],
            out_specs=pl.BlockSpec((R, D // 2), lambda i, be, sg, nv, nu: (blk_at(FFN_LEAD + 1)(i), 0)),
            scratch_shapes=[pltpu.VMEM((FFN_SLOTS, D, F), BF16), pltpu.VMEM((FFN_SLOTS, D, F), BF16),
                            pltpu.VMEM((FFN_SLOTS, F, D), BF16), pltpu.VMEM((2, R, F), BF16)]),
        out_shape=jax.ShapeDtypeStruct((P, D // 2), jnp.uint32),
        compiler_params=_cp(("arbitrary",), VMEM_MB_LARGE),
        name="moe_ffn",
    )(blk_exp, seg, live, n_used, x_rows, w_gate, w_up, w_down)

    y12 = _sc_gather(y_rows, jnp.concatenate([dest0, dest1])).reshape(2, B, S, D // 2)
    return _moe_out(x, y12, info, gate, ln_g, ln_b)


def kernel(x, c, positions, ada_w, ada_b, ln_g, ln_b, mla_w_in, mla_g_q, mla_w_uq, mla_g_kv, mla_w_ukv, mla_w_o, nsa_w_in, nsa_pe_k, nsa_pe_v, nsa_w_ck1, nsa_w_ck2, nsa_w_cv1, nsa_w_cv2, nsa_w_o, sb_w_qkv, sb_w_o, moe_w_grp, moe_b_grp, moe_w_rt, moe_b_rt, moe_w_gate, moe_w_up, moe_w_down):
    D = x.shape[-1]
    depth = ada_w.shape[0]
    cos_mla, sin_mla = _rope_tables(positions, MLA_DR)
    cos_nsa, sin_nsa = _rope_tables(positions, NSA_DH)
    mods = _ada_all(c, ada_w, ada_b)

    def mod(i, sub):
        m = mods[2 * i + sub][:, None, :]
        return m[..., :D], m[..., D:2 * D], m[..., 2 * D:]

    for i in range(depth):
        kind, j = i % N_MIXERS, i // N_MIXERS
        shift, scale_v, gate = mod(i, 0)
        shift2, scale2, gate2 = mod(i, 1)
        route = (scale2, shift2, moe_w_grp[i], moe_b_grp[i], moe_w_rt[i], moe_b_rt[i])
        if kind == 0:
            x, *routed = _mla_layer(x, shift, scale_v, gate, cos_mla, sin_mla, mla_w_in[j], mla_g_q[j], mla_w_uq[j],
                                    mla_g_kv[j], mla_w_ukv[j], mla_w_o[j], ln_g[i, 0], ln_b[i, 0], route)
        elif kind == 1:
            x, *routed = _nsa_layer(x, shift, scale_v, gate, cos_nsa, sin_nsa, nsa_w_in[j], nsa_pe_k[j], nsa_pe_v[j],
                                    nsa_w_ck1[j], nsa_w_ck2[j], nsa_w_cv1[j], nsa_w_cv2[j], nsa_w_o[j],
                                    ln_g[i, 0], ln_b[i, 0], route)
        else:
            x, *routed = _sb_layer(x, shift, scale_v, gate, sb_w_qkv[j], sb_w_o[j], ln_g[i, 0], ln_b[i, 0], route)
        x = _moe_layer(x, routed, gate2, i, moe_w_gate, moe_w_up, moe_w_down, ln_g[i, 1], ln_b[i, 1])
    return x
```

```python
import functools

import numpy as np
import jax
import jax.numpy as jnp
from jax import lax
from jax.experimental import pallas as pl
from jax.experimental.pallas import tpu as pltpu
from jax.experimental.pallas import tpu_sc as plsc

BF16 = jnp.bfloat16
F32 = jnp.float32

DEPTH = 4
N_MIXERS = 3
ROPE_THETA = 10000.0
LN_EPS = 1e-5
RMS_EPS = 1e-6
NEG = -1e30
LOG2E = 1.4426950408889634
ALPHA = (2 * DEPTH) ** 0.25

MLA_HEADS, MLA_DN, MLA_DR, MLA_DV = 8, 128, 64, 128
MLA_QR, MLA_KVR = 256, 256

NSA_HEADS, NSA_KVH, NSA_DH = 8, 2, 128
NSA_G = NSA_HEADS // NSA_KVH
NSA_CMP_BLOCK, NSA_CMP_STRIDE = 32, 16
NSA_SLC_BLOCK, NSA_TOPK = 64, 8
SLC_SHIFT = 6
NSA_WINDOW = 512
NSA_BONUS = 1e3
GATE_ROWS = 16
SUM_ROWS = 16

SB_HEADS, SB_DH = 8, 128
SB_EXIT_LOG2 = -160.0

MOE_GROUPS, MOE_EPG = 4, 8
MOE_E = MOE_GROUPS * MOE_EPG
MOE_ROWS = 256
FFN_LEAD = 3
FFN_SLOTS = FFN_LEAD + 2
OUT_LN_ROWS = 512
ROUTER_ROWS = 64
INFO_ROWS = 8

SC_CORES, SC_SUBCORES = 2, 16
SC_CHUNK = 64

LANES = 128
VMEM_BYTES_V7X = 64 * 1024 * 1024

ADA_COLS = 1024
PROJ_TOKENS = 512
PROJ_CHUNK = 256
OUT_LN_TOKENS = 1024
MOE_OUT_TOKENS = 512
MLA_TILE = 512
NSA_Q_TILE, NSA_K_TILE = 256, 256
NSA_SEL_ROWS = 32
SB_Q_TILE, SB_K_TILE = 512, 256
VMEM_MB_SMALL, VMEM_MB_MID, VMEM_MB_LARGE = 32, 40, 48


def _cp(sem, vmem_mb):
    assert vmem_mb * 2**20 < VMEM_BYTES_V7X
    return pltpu.CompilerParams(dimension_semantics=sem, vmem_limit_bytes=vmem_mb * 2**20)


def _nt_dot(a, b):
    return lax.dot_general(a, b, (((1,), (1,)), ((), ())), preferred_element_type=F32)


def _dot(a, b):
    return jnp.dot(a, b, preferred_element_type=F32)


def _split_bf16(a):
    hi = a.astype(BF16)
    lo = (a - hi.astype(F32)).astype(BF16)
    return hi, lo


def _pack_pairs(a):
    n = a.shape[1] // 2
    hi = pltpu.bitcast(a[:, :n].astype(BF16).astype(F32), jnp.uint32)
    lo = pltpu.bitcast(a[:, n:].astype(BF16).astype(F32), jnp.uint32)
    return hi | (lo >> 16)


def _unpack_pairs(u):
    hi = pltpu.bitcast(u & jnp.uint32(0xFFFF0000), F32)
    lo = pltpu.bitcast(u << 16, F32)
    return jnp.concatenate([hi, lo], axis=1)


def _layer_norm(z, g, b):
    mu = jnp.mean(z, axis=-1, keepdims=True)
    d = z - mu
    var = jnp.mean(d * d, axis=-1, keepdims=True)
    return d * lax.rsqrt(var + LN_EPS) * g + b


def _ada_kernel(c_ref, w_ref, b_ref, o_ref):
    c = c_ref[...]
    s_hi, s_lo = _split_bf16(c * jax.nn.sigmoid(c))
    w_hi, w_lo = _split_bf16(w_ref[0])
    o_ref[0] = _dot(s_hi, w_hi) + _dot(s_lo, w_hi) + _dot(s_hi, w_lo) + b_ref[0]


def _ada_all(c, ada_w, ada_b):
    L, two, D, D3 = ada_w.shape
    B = c.shape[0]
    n = L * two
    tn = ADA_COLS
    return pl.pallas_call(
        _ada_kernel,
        grid=(n, D3 // tn),
        in_specs=[pl.BlockSpec((B, D), lambda i, j: (0, 0)),
                  pl.BlockSpec((1, D, tn), lambda i, j: (i, 0, j)),
                  pl.BlockSpec((1, 1, tn), lambda i, j: (i, 0, j))],
        out_specs=pl.BlockSpec((1, B, tn), lambda i, j: (i, 0, j)),
        out_shape=jax.ShapeDtypeStruct((n, B, D3), F32),
        compiler_params=_cp(("arbitrary", "arbitrary"), VMEM_MB_SMALL),
        name="ada",
    )(c, ada_w.reshape(n, D, D3), ada_b.reshape(n, 1, D3))


def _route_tokens(x1, sc, sh, whi, wlo, bcol, tri, base_sc):
    h = x1 * (1.0 + sc) + sh
    h_hi, h_lo = _split_bf16(h)
    logit = _nt_dot(whi, h_hi) + _nt_dot(whi, h_lo) + _nt_dot(wlo, h_hi) + bcol
    tm = logit.shape[1]
    epg = MOE_EPG
    row = lax.broadcasted_iota(jnp.int32, (epg, 1), 0)
    row_f = row.astype(F32)

    def first_max(vals):
        mx = jnp.max(vals, axis=0, keepdims=True)
        idx = jnp.min(jnp.where(vals == mx, row_f, float(epg)), axis=0, keepdims=True)
        return mx, idx

    gl = jnp.where(row < MOE_GROUPS, logit[0:epg], -jnp.inf)
    gmax, g_idx = first_max(gl)
    g_w = 1.0 / jnp.sum(jnp.exp(gl - gmax), axis=0, keepdims=True)
    el = logit[epg:2 * epg]
    for g in range(1, MOE_GROUPS):
        el = jnp.where(g_idx == float(g), logit[epg * (g + 1):epg * (g + 2)], el)
    ep = jnp.exp(el - jnp.max(el, axis=0, keepdims=True))
    prob = ep / jnp.sum(ep, axis=0, keepdims=True)
    p1, i1 = first_max(prob)
    p2, i2 = first_max(jnp.where(row_f == i1, -1.0, prob))
    e1 = g_idx * float(epg) + i1
    e2 = g_idx * float(epg) + i2
    w1 = g_w * p1 / (p1 + p2)
    w2 = g_w * p2 / (p1 + p2)

    erow = lax.broadcasted_iota(jnp.int32, (MOE_E, 1), 0).astype(F32)
    hot1 = erow == e1
    hot2 = erow == e2
    cnt = jnp.where(hot1 | hot2, 1.0, 0.0)
    before = base_sc[:, 0:1] + _dot(cnt.astype(BF16), tri)
    r1 = jnp.sum(jnp.where(hot1, before, 0.0), axis=0, keepdims=True)
    r2 = jnp.sum(jnp.where(hot2, before, 0.0), axis=0, keepdims=True)
    base_sc[...] += jnp.sum(cnt, axis=1, keepdims=True)

    irow = lax.broadcasted_iota(jnp.int32, (INFO_ROWS, 1), 0)
    info = jnp.zeros((INFO_ROWS, tm), F32)
    for k, val in enumerate((e1, e2, r1, r2, w1, w2)):
        info = jnp.where(irow == k, val, info)
    return _pack_pairs(h_hi), info


def _out_ln_kernel(o_ref, w_ref, x_ref, gate_ref, g_ref, b_ref, sc_ref, sh_ref, whi_ref, wlo_ref, br_ref, tri_ref,
                   out_ref, hb_ref, info_ref, cnt_ref, base_sc):
    first = (pl.program_id(0) == 0) & (pl.program_id(1) == 0)

    @pl.when(first)
    def _():
        base_sc[...] = jnp.zeros(base_sc.shape, F32)

    ts = o_ref.shape[1]
    rows = min(OUT_LN_ROWS, ts)
    ys = [_dot(o_ref[0, 0:rows, :], w_ref[...])]
    for c in range(ts // rows):
        if c + 1 < ts // rows:
            ys.append(_dot(o_ref[0, (c + 1) * rows:(c + 2) * rows, :], w_ref[...]))
        sl = slice(c * rows, (c + 1) * rows)
        z = ALPHA * x_ref[0, sl, :] + gate_ref[0] * ys[c]
        x1 = _layer_norm(z, g_ref[...], b_ref[...])
        out_ref[0, sl, :] = x1
        packed, info = _route_tokens(x1, sc_ref[0], sh_ref[0], whi_ref[...], wlo_ref[...], br_ref[...],
                                     tri_ref[...], base_sc)
        hb_ref[0, sl, :] = packed
        info_ref[0, :, sl] = info
    cnt_ref[...] = base_sc[...]


def _out_ln(o, w_o, x, gate, g, b, route, ts=OUT_LN_TOKENS):
    B, S, D = x.shape
    K = o.shape[-1]
    ts = min(ts, S)
    scale2, shift2, w_grp, b_grp, w_rt, b_rt = route
    E, g0 = MOE_E, MOE_EPG
    w_r = jnp.zeros((ROUTER_ROWS, D), F32).at[:MOE_GROUPS].set(w_grp.T).at[g0:g0 + E].set(w_rt.T)
    w_hi = w_r.astype(BF16)
    w_lo = (w_r - w_hi.astype(F32)).astype(BF16)
    b_r = jnp.zeros((ROUTER_ROWS, 1), F32).at[:MOE_GROUPS, 0].set(b_grp).at[g0:g0 + E, 0].set(b_rt)
    tm = min(OUT_LN_ROWS, ts)
    tri = jnp.asarray(np.triu(np.ones((tm, tm), np.float32), 1), BF16)
    const = lambda shape: pl.BlockSpec(shape, lambda bi, i: (0,) * len(shape))
    per_batch = pl.BlockSpec((1, 1, D), lambda bi, i: (bi, 0, 0))
    return pl.pallas_call(
        _out_ln_kernel,
        grid=(B, S // ts),
        in_specs=[pl.BlockSpec((1, ts, K), lambda bi, i: (bi, i, 0)),
                  const((K, D)),
                  pl.BlockSpec((1, ts, D), lambda bi, i: (bi, i, 0)),
                  per_batch, const((1, D)), const((1, D)),
                  per_batch, per_batch,
                  const((ROUTER_ROWS, D)), const((ROUTER_ROWS, D)), const((ROUTER_ROWS, 1)), const((tm, tm))],
        out_specs=[pl.BlockSpec((1, ts, D), lambda bi, i: (bi, i, 0)),
                   pl.BlockSpec((1, ts, D // 2), lambda bi, i: (bi, i, 0)),
                   pl.BlockSpec((1, INFO_ROWS, ts), lambda bi, i: (bi, 0, i)),
                   const((E, LANES))],
        out_shape=[jax.ShapeDtypeStruct((B, S, D), F32),
                   jax.ShapeDtypeStruct((B, S, D // 2), jnp.uint32),
                   jax.ShapeDtypeStruct((B, INFO_ROWS, S), F32),
                   jax.ShapeDtypeStruct((E, LANES), F32)],
        scratch_shapes=[pltpu.VMEM((E, LANES), F32)],
        compiler_params=_cp(("arbitrary", "arbitrary"), VMEM_MB_LARGE),
        name="out_ln_route",
    )(o, w_o.astype(BF16), x, gate, g.reshape(1, D), b.reshape(1, D), scale2, shift2, w_hi, w_lo, b_r, tri)


def _rope_tables(positions, dim):
    inv_freq = 1.0 / (ROPE_THETA ** (jnp.arange(0, dim, 2, dtype=F32) / dim))
    ang = positions.astype(F32)[..., None] * inv_freq
    return jnp.cos(ang), jnp.sin(ang)


def _rot_half_cols(w, half):
    return jnp.concatenate([-w[..., half:], w[..., :half]], axis=-1)


def _mla_front_kernel(x_ref, sc_ref, sh_ref, win_ref, gq_ref, gkv_ref, wqt_ref, wk_ref, wvt_ref, t_ref, tt_ref,
                      qt_ref, k_ref, vt_ref, *, scale):
    hb = (x_ref[0] * (1.0 + sc_ref[0]) + sh_ref[0]).astype(BF16)
    r = _dot(hb, win_ref[...])
    tab = t_ref[0]
    tab_t = tt_ref[0]
    low = lax.broadcasted_iota(jnp.int32, tab.shape, 1) < MLA_DR

    def rms(cx, g):
        return (cx * lax.rsqrt(jnp.mean(cx * cx, axis=-1, keepdims=True) + RMS_EPS) * g).astype(BF16)

    cq = rms(r[:, :MLA_QR], gq_ref[...])
    ckv = rms(r[:, MLA_QR:MLA_QR + MLA_KVR], gkv_ref[...])
    t = r[:, MLA_QR + MLA_KVR:] * tab
    k_rope = jnp.where(low, t + pltpu.roll(t, MLA_DR, axis=1), 0.0).astype(BF16)
    zeros = jnp.zeros((MLA_DR, hb.shape[0]), BF16)
    hd = MLA_DN + 2 * MLA_DR
    q_all = _nt_dot(wqt_ref[...], cq)
    k_all = _dot(ckv, wk_ref[...])
    vt_all = _nt_dot(wvt_ref[...], ckv)
    for h in range(MLA_HEADS):
        qh = q_all[h * hd:(h + 1) * hd]
        tq = qh[MLA_DN:] * tab_t
        qt_ref[0, h, :MLA_DN, :] = (qh[:MLA_DN] * scale).astype(BF16)
        qt_ref[0, h, MLA_DN:MLA_DN + MLA_DR, :] = ((tq[:MLA_DR] + tq[MLA_DR:]) * scale).astype(BF16)
        qt_ref[0, h, MLA_DN + MLA_DR:, :] = zeros
        k_ref[0, h, :, :MLA_DN] = k_all[:, h * MLA_DN:(h + 1) * MLA_DN].astype(BF16)
        k_ref[0, h, :, MLA_DN:] = k_rope
        vt_ref[0, h] = vt_all[h * MLA_DV:(h + 1) * MLA_DV].astype(BF16)


def _flash_causal_kernel(qt_ref, k_ref, vt_ref, o_ref, *, t):
    dv, s_len = vt_ref.shape[2], vt_ref.shape[3]
    pairs = [(qi, kt) for qi in range(s_len // t) for kt in range(qi + 1)]
    kpos = lax.broadcasted_iota(jnp.int32, (t, 1), 0)
    qpos = lax.broadcasted_iota(jnp.int32, (1, t), 1)
    diag = kpos <= qpos

    def scores(j):
        qi, kt = pairs[j]
        return _dot(k_ref[0, 0, kt * t:(kt + 1) * t, :], qt_ref[0, 0, :, qi * t:(qi + 1) * t])

    def value_matmul(pending, l, acc):
        qi, kt, p, a = pending
        acc = a * acc + _dot(vt_ref[0, 0, :, kt * t:(kt + 1) * t], p)
        if kt == qi:
            o_ref[0, qi * t:(qi + 1) * t, :] = (acc / l).T.astype(o_ref.dtype)
        return acc

    s_cur, pending = scores(0), None
    m = l = acc = None
    for j, (qi, kt) in enumerate(pairs):
        s_next = scores(j + 1) if j + 1 < len(pairs) else None
        if pending is not None:
            acc = value_matmul(pending, l, acc)
        if kt == 0:
            m, l, acc = jnp.full((1, t), NEG, F32), jnp.zeros((1, t), F32), jnp.zeros((dv, t), F32)
        s = jnp.where(diag, s_cur, NEG) if kt == qi else s_cur
        m_new = jnp.maximum(m, jnp.max(s, axis=0, keepdims=True))
        p = jnp.exp2(s - m_new)
        a = jnp.exp2(m - m_new)
        l = a * l + jnp.sum(p, axis=0, keepdims=True)
        m = m_new
        pending = (qi, kt, p.astype(BF16), a)
        s_cur = s_next
    value_matmul(pending, l, acc)


def _mla_layer(x, shift, scale_v, gate, cos, sin, w_in, g_q, w_uq, g_kv, w_ukv, w_o, ln_g, ln_b, route):
    B, S, D = x.shape
    H, dn, dr, dv = MLA_HEADS, MLA_DN, MLA_DR, MLA_DV
    half = dr // 2
    kr = w_in[:, MLA_QR + MLA_KVR:]
    win_p = jnp.concatenate([w_in, _rot_half_cols(kr, half)], axis=1).astype(BF16)
    wq = w_uq.reshape(MLA_QR, H, dn + dr)
    wq = jnp.concatenate([wq, _rot_half_cols(wq[..., dn:], half)], axis=-1)
    wqt = jnp.transpose(wq, (1, 2, 0)).reshape(H * (dn + 2 * dr), MLA_QR).astype(BF16)
    wkv = w_ukv.reshape(MLA_KVR, H, dn + dv)
    wk = wkv[..., :dn].reshape(MLA_KVR, H * dn).astype(BF16)
    wvt = jnp.transpose(wkv[..., dn:], (1, 2, 0)).reshape(H * dv, MLA_KVR).astype(BF16)
    tab = jnp.concatenate([cos, cos, sin, sin], axis=-1)
    tab_t = jnp.swapaxes(tab, 1, 2)
    ts = min(PROJ_TOKENS, S)
    nw = win_p.shape[1]
    hd = dn + 2 * dr
    qt, k, vt = pl.pallas_call(
        functools.partial(_mla_front_kernel, scale=LOG2E * (dn + dr) ** -0.5),
        grid=(B, S // ts),
        in_specs=[pl.BlockSpec((1, ts, D), lambda b, i: (b, i, 0)),
                  pl.BlockSpec((1, 1, D), lambda b, i: (b, 0, 0)),
                  pl.BlockSpec((1, 1, D), lambda b, i: (b, 0, 0)),
                  pl.BlockSpec((D, nw), lambda b, i: (0, 0)),
                  pl.BlockSpec((1, MLA_QR), lambda b, i: (0, 0)),
                  pl.BlockSpec((1, MLA_KVR), lambda b, i: (0, 0)),
                  pl.BlockSpec((H * hd, MLA_QR), lambda b, i: (0, 0)),
                  pl.BlockSpec((MLA_KVR, H * dn), lambda b, i: (0, 0)),
                  pl.BlockSpec((H * dv, MLA_KVR), lambda b, i: (0, 0)),
                  pl.BlockSpec((1, ts, LANES), lambda b, i: (b, i, 0)),
                  pl.BlockSpec((1, LANES, ts), lambda b, i: (b, 0, i))],
        out_specs=[pl.BlockSpec((1, H, hd, ts), lambda b, i: (b, 0, 0, i)),
                   pl.BlockSpec((1, H, ts, hd), lambda b, i: (b, 0, i, 0)),
                   pl.BlockSpec((1, H, dv, ts), lambda b, i: (b, 0, 0, i))],
        out_shape=[jax.ShapeDtypeStruct((B, H, hd, S), BF16),
                   jax.ShapeDtypeStruct((B, H, S, hd), BF16),
                   jax.ShapeDtypeStruct((B, H, dv, S), BF16)],
        compiler_params=_cp(("arbitrary", "arbitrary"), VMEM_MB_LARGE),
        name="mla_front",
    )(x, scale_v, shift, win_p, g_q.reshape(1, -1), g_kv.reshape(1, -1), wqt, wk, wvt, tab, tab_t)

    t = min(MLA_TILE, S)
    o = pl.pallas_call(
        functools.partial(_flash_causal_kernel, t=t),
        grid=(B, H),
        in_specs=[pl.BlockSpec((1, 1, hd, S), lambda b, h: (b, h, 0, 0)),
                  pl.BlockSpec((1, 1, S, hd), lambda b, h: (b, h, 0, 0)),
                  pl.BlockSpec((1, 1, dv, S), lambda b, h: (b, h, 0, 0))],
        out_specs=pl.BlockSpec((1, S, dv), lambda b, h: (b, 0, h)),
        out_shape=jax.ShapeDtypeStruct((B, S, H * dv), BF16),
        compiler_params=_cp(("arbitrary", "arbitrary"), VMEM_MB_LARGE),
        name="mla_attn",
    )(qt, k, vt)
    return _out_ln(o, w_o, x, gate, ln_g, ln_b, route)


def _proj_kernel(*refs, segs, chunk, has_rope):
    x_ref, sc_ref, sh_ref, w_ref, wt_ref = refs[:5]
    n_in = 9 if has_rope else 5
    outs = refs[n_in:]
    hb = (x_ref[0] * (1.0 + sc_ref[0]) + sh_ref[0]).astype(BF16)
    if has_rope:
        cos, sin = refs[5][0], refs[6][0]
        cos_t, sin_t = refs[7][0], refs[8][0]
    half = LANES // 2
    for (c0, width, kind, scale, transposed), o_ref in zip(segs, outs):
        for j in range(0, width, chunk):
            cw = min(chunk, width - j)
            if transposed:
                r = _nt_dot(wt_ref[c0 + j:c0 + j + cw, :], hb)
            else:
                r = _dot(hb, w_ref[:, c0 + j:c0 + j + cw])
            if kind == "rope":
                parts = []
                for t in range(0, cw, LANES):
                    if transposed:
                        u = r[t:t + LANES]
                        parts.append(u * cos_t + pltpu.roll(u, half, axis=0) * sin_t)
                    else:
                        u = r[:, t:t + LANES]
                        parts.append(u * cos + pltpu.roll(u, half, axis=1) * sin)
                r = jnp.concatenate(parts, axis=0 if transposed else 1) if len(parts) > 1 else parts[0]
            elif kind == "sigmoid":
                r = jax.nn.sigmoid(r)
            if scale != 1.0:
                r = r * scale
            if transposed:
                o_ref[0, j:j + cw, :] = r.astype(o_ref.dtype)
            else:
                o_ref[0, :, j:j + cw] = r.astype(o_ref.dtype)


def _mod_proj(x, shift, scale_v, w, wt, segs, out_dtypes, rope=None, ts=PROJ_TOKENS, chunk=PROJ_CHUNK):
    B, S, D = x.shape
    ts = min(ts, S)
    in_specs = [pl.BlockSpec((1, ts, D), lambda b, i: (b, i, 0)),
                pl.BlockSpec((1, 1, D), lambda b, i: (b, 0, 0)),
                pl.BlockSpec((1, 1, D), lambda b, i: (b, 0, 0)),
                pl.BlockSpec(w.shape, lambda b, i: (0, 0)),
                pl.BlockSpec(wt.shape, lambda b, i: (0, 0))]
    args = [x, scale_v, shift, w, wt]
    if rope is not None:
        cos, sin = rope
        in_specs += [pl.BlockSpec((1, ts, LANES), lambda b, i: (b, i, 0))] * 2
        in_specs += [pl.BlockSpec((1, LANES, ts), lambda b, i: (b, 0, i))] * 2
        args += [cos, sin, jnp.swapaxes(cos, 1, 2), jnp.swapaxes(sin, 1, 2)]
    out_specs, out_shape = [], []
    for sg, dt in zip(segs, out_dtypes):
        if sg[4]:
            out_specs.append(pl.BlockSpec((1, sg[1], ts), lambda b, i: (b, 0, i)))
            out_shape.append(jax.ShapeDtypeStruct((B, sg[1], S), dt))
        else:
            out_specs.append(pl.BlockSpec((1, ts, sg[1]), lambda b, i: (b, i, 0)))
            out_shape.append(jax.ShapeDtypeStruct((B, S, sg[1]), dt))
    return pl.pallas_call(
        functools.partial(_proj_kernel, segs=tuple(segs), chunk=chunk, has_rope=rope is not None),
        grid=(B, S // ts),
        in_specs=in_specs,
        out_specs=out_specs,
        out_shape=out_shape,
        compiler_params=_cp(("arbitrary", "arbitrary"), VMEM_MB_LARGE),
        name="mod_proj",
    )(*args)


def _nsa_cmp_kernel(kc_ref, vc_ref, pek_ref, pev_ref, wk1_ref, wk2_ref, wv1_ref, wv2t_ref, ko_ref, vto_ref):
    half = NSA_CMP_BLOCK // 2
    nchunk = kc_ref.shape[1]

    def hidden(src, pe_ref, w1_ref, kh):
        a = jnp.zeros((nchunk, w1_ref.shape[2]), F32)
        bm = jnp.zeros((nchunk, w1_ref.shape[2]), F32)
        for l in range(half):
            c0 = (l * NSA_KVH + kh) * NSA_DH
            t = src[0, :, c0:c0 + NSA_DH].astype(F32)
            a = a + _dot((t + pe_ref[l:l + 1, :]).astype(BF16), w1_ref[l])
            bm = bm + _dot((t + pe_ref[half + l:half + l + 1, :]).astype(BF16), w1_ref[half + l])
        hid = a + pltpu.roll(bm, nchunk - 1, axis=0)
        return (hid * jax.nn.sigmoid(hid)).astype(BF16)

    rowi = lax.broadcasted_iota(jnp.int32, (nchunk, 1), 0)
    coli = lax.broadcasted_iota(jnp.int32, (1, nchunk), 1)
    for kh in range(NSA_KVH):
        out = _dot(hidden(kc_ref, pek_ref, wk1_ref, kh), wk2_ref[...])
        ko_ref[0, kh] = jnp.where(rowi < nchunk - 1, out, 0.0).astype(ko_ref.dtype)
        out_t = _nt_dot(wv2t_ref[...], hidden(vc_ref, pev_ref, wv1_ref, kh))
        vto_ref[0, kh] = jnp.where(coli < nchunk - 1, out_t, 0.0).astype(vto_ref.dtype)


def _nsa_attn_kernel(qt_ref, kc_ref, vct_ref, ks_ref, vst_ref, kw_ref, vwt_ref, gt_ref, ovt_ref, e_ref, o_ref,
                     q4_sc, m_sc, acc_sc, *, tq, tk, span, n_slc):
    qi = pl.program_id(2)
    G = NSA_G
    for g in range(G):
        q4_sc[:, g * tq:(g + 1) * tq] = qt_ref[0, g * NSA_DH:(g + 1) * NSA_DH, :]
    q4 = q4_sc[...]
    qpos = qi * tq + lax.broadcasted_iota(jnp.int32, (1, tq), 1)

    def tile_g(a):
        return jnp.concatenate([a] * G, axis=1)

    qpos4 = tile_g(qpos)

    kstart = pl.multiple_of(jnp.maximum(qi * tq + tq - span, 0), tq)
    s_win = _dot(kw_ref[0, pl.ds(kstart, span), :], q4)

    n_cmp = kc_ref.shape[2]
    nid = lax.broadcasted_iota(jnp.int32, (n_cmp, 1), 0)
    cmask = (nid * NSA_CMP_STRIDE + (NSA_CMP_BLOCK - 1) <= qpos4) & (nid < n_cmp - 1)
    s = jnp.where(cmask, _dot(kc_ref[0, 0], q4), NEG)
    p = jnp.where(cmask, jnp.exp2(s - jnp.max(s, axis=0, keepdims=True)), 0.0)
    l = jnp.sum(p, axis=0, keepdims=True)
    p_cmp = p / jnp.where(l > 0.0, l, 1.0)
    o_cmp = _dot(vct_ref[0, 0], p_cmp.astype(BF16))

    diff = qpos - (kstart + lax.broadcasted_iota(jnp.int32, (span, 1), 0))
    s_win = s_win + tile_g(jnp.where((diff >= 0) & (diff < NSA_WINDOW), 0.0, NEG))
    m_win = jnp.max(s_win, axis=0, keepdims=True)
    ow = jnp.zeros((NSA_DH + SUM_ROWS, G * tq), F32)
    for c0 in range(0, span, tk):
        c1 = min(c0 + tk, span)
        p = jnp.exp2(s_win[c0:c1] - m_win).astype(BF16)
        v_aug = jnp.concatenate([vwt_ref[0, :, pl.ds(kstart + c0, c1 - c0)], jnp.ones((SUM_ROWS, c1 - c0), BF16)], axis=0)
        ow = ow + _dot(v_aug, p)
    o_win = ow[:NSA_DH] / ow[NSA_DH:NSA_DH + 1]

    psum = p_cmp[:, 0:tq]
    for g in range(1, G):
        psum = psum + p_cmp[:, g * tq:(g + 1) * tq]
    p_hi, p_lo = _split_bf16(psum)
    imp = _dot(ovt_ref[...], p_hi) + _dot(ovt_ref[...], p_lo)
    nb = ovt_ref.shape[0]
    blk = lax.broadcasted_iota(jnp.int32, (nb, 1), 0)
    cur = jnp.right_shift(qpos, SLC_SHIFT)
    forced = (blk == 0) | (blk == cur) | (blk == cur - 1)
    valid = (blk * NSA_SLC_BLOCK <= qpos) & (blk < n_slc)
    work = jnp.where(valid, imp + jnp.where(forced, NSA_BONUS, 0.0), -jnp.inf)
    beaten = jnp.zeros((nb, tq), F32)
    for j in range(n_slc):
        wj = work[j:j + 1, :]
        beaten = beaten + jnp.where((wj > work) | ((wj == work) & (blk > j)), 1.0, 0.0)
    sel = jnp.where(beaten < float(min(NSA_TOPK, n_slc)), 1.0, 0.0).astype(BF16)

    m_sc[...] = jnp.full(m_sc.shape, NEG, F32)
    acc_sc[...] = jnp.zeros(acc_sc.shape, F32)
    n_kt = ((qi + 1) * tq + tk - 1) // tk

    def slc_tiles(n):
        def scores(kt):
            return _dot(ks_ref[0, kt * tk:(kt + 1) * tk, :], q4)

        ones = jnp.ones((SUM_ROWS, tk), BF16)

        def value_matmul(kt, p, a):
            v_aug = jnp.concatenate([vst_ref[0, :, kt * tk:(kt + 1) * tk], ones], axis=0)
            acc_sc[...] = a * acc_sc[...] + _dot(v_aug, p)

        s, pending = scores(0), None
        chosen = _dot(e_ref[0:n * tk, :], sel)
        kpos = lax.broadcasted_iota(jnp.int32, (n * tk, 1), 0)
        bias_all = jnp.where((chosen > 0.5) & (kpos <= qpos), 0.0, NEG)
        for kt in range(n):
            s_next = scores(kt + 1) if kt + 1 < n else None
            if pending is not None:
                value_matmul(*pending)
            s = s + tile_g(bias_all[kt * tk:(kt + 1) * tk])
            m_prev = m_sc[...]
            m_new = jnp.maximum(m_prev, jnp.max(s, axis=0, keepdims=True))
            p = jnp.exp2(s - m_new)
            a = jnp.exp2(m_prev - m_new)
            m_sc[...] = m_new
            pending = (kt, p.astype(BF16), a)
            s = s_next
        value_matmul(*pending)

    for n in range(1, (ks_ref.shape[1] + tk - 1) // tk + 1):
        pl.when(n_kt == n)(functools.partial(slc_tiles, n))
    o_slc = acc_sc[0:NSA_DH, :] / acc_sc[NSA_DH:NSA_DH + 1, :]

    gt = gt_ref[0]
    for g in range(G):
        cols = slice(g * tq, (g + 1) * tq)
        o = (gt[3 * g:3 * g + 1] * o_cmp[:, cols] + gt[3 * g + 1:3 * g + 2] * o_slc[:, cols]
             + gt[3 * g + 2:3 * g + 3] * o_win[:, cols])
        o_ref[0, :, g * NSA_DH:(g + 1) * NSA_DH] = o.T.astype(o_ref.dtype)


def _nsa_layer(x, shift, scale_v, gate, cos, sin, w_in, pe_k, pe_v, w_ck1, w_ck2, w_cv1, w_cv2, w_o,
               ln_g, ln_b, route):
    B, S, D = x.shape
    H, KH, Dh, G = NSA_HEADS, NSA_KVH, NSA_DH, NSA_G
    kvw = KH * Dh
    cuts = [int(v) for v in np.cumsum([H * Dh] + [kvw] * 6)]
    wq, wkc, wvc, wks, wvs, wkw, wvw, wgl = jnp.split(w_in, cuts, axis=1)
    wgl = jnp.pad(wgl.reshape(D, KH, G * 3), ((0, 0), (0, 0), (0, GATE_ROWS - G * 3))).reshape(D, KH * GATE_ROWS)
    w = jnp.concatenate([wkc, wks, wkw, wvc], axis=1).astype(BF16)
    wt = jnp.concatenate([wq, wvs, wvw, wgl], axis=1).T.astype(BF16)
    segs = [(0, H * Dh, "rope", LOG2E * Dh ** -0.5, True),
            (0, kvw, "rope", 1.0, False), (kvw, kvw, "rope", 1.0, False), (2 * kvw, kvw, "rope", 1.0, False),
            (3 * kvw, kvw, "plain", 1.0, False),
            (H * Dh, kvw, "plain", 1.0, True), (H * Dh + kvw, kvw, "plain", 1.0, True),
            (H * Dh + 2 * kvw, KH * GATE_ROWS, "sigmoid", 1.0, True)]
    rope = (jnp.concatenate([cos, cos], axis=-1), jnp.concatenate([-sin, sin], axis=-1))
    qt, kc, ks, kw, vc, vst, vwt, gates_t = _mod_proj(
        x, shift, scale_v, w, wt, segs, [BF16] * 7 + [F32], rope=rope)

    st = NSA_CMP_STRIDE
    nchunk = S // st
    kc2 = kc.reshape(B, nchunk, st * kvw)
    vc2 = vc.reshape(B, nchunk, st * kvw)
    hid = w_ck1.shape[-1]
    full = lambda shape: pl.BlockSpec(shape, lambda b: (0,) * len(shape))
    k_cmp, v_cmp_t = pl.pallas_call(
        _nsa_cmp_kernel,
        grid=(B,),
        in_specs=[pl.BlockSpec((1, nchunk, st * kvw), lambda b: (b, 0, 0)),
                  pl.BlockSpec((1, nchunk, st * kvw), lambda b: (b, 0, 0)),
                  full((NSA_CMP_BLOCK, Dh)), full((NSA_CMP_BLOCK, Dh)),
                  full((NSA_CMP_BLOCK, Dh, hid)), full((hid, Dh)),
                  full((NSA_CMP_BLOCK, Dh, hid)), full((Dh, hid))],
        out_specs=[pl.BlockSpec((1, KH, nchunk, Dh), lambda b: (b, 0, 0, 0)),
                   pl.BlockSpec((1, KH, Dh, nchunk), lambda b: (b, 0, 0, 0))],
        out_shape=[jax.ShapeDtypeStruct((B, KH, nchunk, Dh), BF16),
                   jax.ShapeDtypeStruct((B, KH, Dh, nchunk), BF16)],
        compiler_params=_cp(("arbitrary",), VMEM_MB_SMALL),
        name="nsa_cmp",
    )(kc2, vc2, pe_k, pe_v, w_ck1.astype(BF16), w_ck2.astype(BF16), w_cv1.astype(BF16), w_cv2.T.astype(BF16))

    n_slc = S // NSA_SLC_BLOCK
    nb = NSA_SEL_ROWS
    assert n_slc <= nb and nchunk % 8 == 0
    cmp_start = np.arange(nchunk) * st
    slc_start = np.arange(n_slc) * NSA_SLC_BLOCK
    ov = np.clip(np.minimum(cmp_start[:, None] + NSA_CMP_BLOCK, slc_start[None, :] + NSA_SLC_BLOCK)
                 - np.maximum(cmp_start[:, None], slc_start[None, :]), 0, None).astype(np.float32) / NSA_CMP_BLOCK
    ov[nchunk - 1:] = 0.0
    ov_t = np.zeros((nb, nchunk), np.float32)
    ov_t[:n_slc] = ov.T
    key_blk = np.zeros((S, nb), np.float32)
    key_blk[np.arange(S), np.arange(S) // NSA_SLC_BLOCK] = 1.0

    tq = min(NSA_Q_TILE, S)
    tk = min(NSA_K_TILE, S)
    span = min(NSA_WINDOW + tq, S)
    k_spec = pl.BlockSpec((1, S, Dh), lambda b, kh, i: (b, 0, kh))
    vt_spec = pl.BlockSpec((1, Dh, S), lambda b, kh, i: (b, kh, 0))
    o = pl.pallas_call(
        functools.partial(_nsa_attn_kernel, tq=tq, tk=tk, span=span, n_slc=n_slc),
        grid=(B, KH, S // tq),
        in_specs=[pl.BlockSpec((1, G * Dh, tq), lambda b, kh, i: (b, kh, i)),
                  pl.BlockSpec((1, 1, nchunk, Dh), lambda b, kh, i: (b, kh, 0, 0)),
                  pl.BlockSpec((1, 1, Dh, nchunk), lambda b, kh, i: (b, kh, 0, 0)),
                  k_spec, vt_spec, k_spec, vt_spec,
                  pl.BlockSpec((1, GATE_ROWS, tq), lambda b, kh, i: (b, kh, i)),
                  pl.BlockSpec((nb, nchunk), lambda b, kh, i: (0, 0)),
                  pl.BlockSpec((S, nb), lambda b, kh, i: (0, 0))],
        out_specs=pl.BlockSpec((1, tq, G * Dh), lambda b, kh, i: (b, i, kh)),
        out_shape=jax.ShapeDtypeStruct((B, S, H * Dh), BF16),
        scratch_shapes=[pltpu.VMEM((Dh, G * tq), BF16), pltpu.VMEM((1, G * tq), F32),
                        pltpu.VMEM((Dh + SUM_ROWS, G * tq), F32)],
        compiler_params=_cp(("arbitrary", "arbitrary", "arbitrary"), VMEM_MB_SMALL),
        name="nsa_attn",
    )(qt, k_cmp, v_cmp_t, ks, vst, kw, vwt, gates_t, jnp.asarray(ov_t, BF16), jnp.asarray(key_blk, BF16))
    return _out_ln(o, w_o, x, gate, ln_g, ln_b, route)


def _sb_attn_kernel(qt_ref, k_ref, vt_ref, u_ref, o_ref, r_sc, acc_sc, *, tq, tk):
    d, s_len = vt_ref.shape[1], vt_ref.shape[2]
    u = u_ref[...]
    nd = tq // tk

    def weights(z, strict):
        lk = -(jnp.maximum(z, 0.0) + jnp.log2(1.0 + jnp.exp2(-jnp.abs(z))))
        if strict is not None:
            lk = jnp.where(strict, lk, 0.0)
        between = r_sc[...] + _dot(u, lk.astype(BF16))
        a = jnp.exp2(lk + z + between)
        if strict is not None:
            a = jnp.where(strict, a, 0.0)
        r_sc[...] += jnp.sum(lk, axis=0, keepdims=True)
        return a.astype(BF16)

    for qi in range(s_len // tq):
        qt = qt_ref[0, :, qi * tq:(qi + 1) * tq]
        r_sc[...] = jnp.zeros(r_sc.shape, F32)
        acc_sc[...] = jnp.zeros(acc_sc.shape, F32)
        last = (qi + 1) * nd - 1
        tiles = [last - i for i in range(nd)] + ([qi * nd - 1] if qi > 0 else [])
        qpos = qi * tq + lax.broadcasted_iota(jnp.int32, (1, tq), 1)

        def scores(kt):
            return _dot(k_ref[0, kt * tk:(kt + 1) * tk, :], qt)

        z, pending = scores(tiles[0]), None
        for i, kt in enumerate(tiles):
            nxt = tiles[i + 1] if i + 1 < len(tiles) else max(qi * nd - 2, 0)
            z_next = scores(nxt)
            if pending is not None:
                acc_sc[...] += _dot(vt_ref[0, :, pending[0] * tk:(pending[0] + 1) * tk], pending[1])
            strict = (kt * tk + lax.broadcasted_iota(jnp.int32, (tk, 1), 0) < qpos) if kt >= qi * nd else None
            pending = (kt, weights(z, strict))
            z = z_next
        acc_sc[...] += _dot(vt_ref[0, :, pending[0] * tk:(pending[0] + 1) * tk], pending[1])

        if qi * nd - 2 >= 0:
            def scores_dyn(kt):
                return _dot(k_ref[0, pl.ds(pl.multiple_of(kt * tk, tk), tk), :], qt)

            def cond(carry):
                kt, _ = carry
                return (kt >= 0) & (jnp.max(r_sc[...]) > SB_EXIT_LOG2)

            def body(carry):
                kt, z = carry
                z_next = scores_dyn(jnp.maximum(kt - 1, 0))
                a = weights(z, None)
                acc_sc[...] += _dot(vt_ref[0, :, pl.ds(pl.multiple_of(kt * tk, tk), tk)], a)
                return kt - 1, z_next

            lax.while_loop(cond, body, (qi * nd - 2, z))
        o_ref[0, qi * tq:(qi + 1) * tq, :] = acc_sc[...].T.astype(o_ref.dtype)


def _sb_layer(x, shift, scale_v, gate, w_qkv, w_o, ln_g, ln_b, route):
    B, S, D = x.shape
    H, d = SB_HEADS, SB_DH
    hw = H * d
    w = w_qkv[:, hw:2 * hw].astype(BF16)
    wt = jnp.concatenate([w_qkv[:, :hw], w_qkv[:, 2 * hw:]], axis=1).T.astype(BF16)
    segs = [(0, hw, "plain", LOG2E * d ** -0.5, True), (0, hw, "plain", 1.0, False), (hw, hw, "plain", 1.0, True)]
    qt, k, vt = _mod_proj(x, shift, scale_v, w, wt, segs, [BF16] * 3)
    tq = min(SB_Q_TILE, S)
    tk = min(SB_K_TILE, S)
    tri = np.triu(np.ones((tk, tk), np.float32), 1)
    o = pl.pallas_call(
        functools.partial(_sb_attn_kernel, tq=tq, tk=tk),
        grid=(B, H),
        in_specs=[pl.BlockSpec((1, d, S), lambda b, h: (b, h, 0)),
                  pl.BlockSpec((1, S, d), lambda b, h: (b, 0, h)),
                  pl.BlockSpec((1, d, S), lambda b, h: (b, h, 0)),
                  pl.BlockSpec((tk, tk), lambda b, h: (0, 0))],
        out_specs=pl.BlockSpec((1, S, d), lambda b, h: (b, 0, h)),
        out_shape=jax.ShapeDtypeStruct((B, S, hw), BF16),
        scratch_shapes=[pltpu.VMEM((1, tq), F32), pltpu.VMEM((d, tq), F32)],
        compiler_params=_cp(("arbitrary", "arbitrary"), VMEM_MB_SMALL),
        name="sb_attn",
    )(qt, k, vt, jnp.asarray(tri, BF16))
    return _out_ln(o, w_o, x, gate, ln_g, ln_b, route)


def _ffn_kernel(be_ref, seg_ref, nv_ref, nu_ref, x_ref, wg_ref, wu_ref, wd_ref, y_ref, wg_sc, wu_sc, wd_sc, act_sc):
    i = pl.program_id(0)
    n_blk = be_ref.shape[0]

    def stage(blk, src_ref, dst_sc):
        r = jnp.clip(blk, 0, n_blk - 1)
        first = (blk >= 0) & (blk < nu_ref[0]) & ((r == 0) | (be_ref[r] != be_ref[jnp.maximum(r - 1, 0)]))

        @pl.when(first)
        def _():
            dst_sc[seg_ref[r] % FFN_SLOTS] = src_ref[0, 0].astype(BF16)

    stage(i, wg_ref, wg_sc)
    stage(i - 1, wu_ref, wu_sc)
    stage(i - 2, wd_ref, wd_sc)
    blk = i - FFN_LEAD
    done = blk - 1
    has_new = (blk >= 0) & (blk < nu_ref[0])
    has_done = (done >= 0) & (done < nu_ref[0])
    slot_new = seg_ref[jnp.clip(blk, 0, n_blk - 1)] % FFN_SLOTS
    slot_done = seg_ref[jnp.clip(done, 0, n_blk - 1)] % FFN_SLOTS

    def rows_in():
        live = lax.broadcasted_iota(jnp.int32, (x_ref.shape[0], 1), 0) < nv_ref[jnp.clip(blk, 0, n_blk - 1)]
        return jnp.where(live, _unpack_pairs(x_ref[...]), 0.0).astype(BF16)

    act_in, act_out = (i + 1) % 2, i % 2

    @pl.when(has_new & has_done)
    def _():
        xb = rows_in()
        g = _dot(xb, wg_sc[slot_new])
        u = _dot(xb, wu_sc[slot_new])
        y_ref[...] = _pack_pairs(_dot(act_sc[act_in], wd_sc[slot_done]))
        act_sc[act_out] = (g * jax.nn.sigmoid(g) * u).astype(BF16)

    @pl.when(has_new & jnp.logical_not(has_done))
    def _():
        xb = rows_in()
        g = _dot(xb, wg_sc[slot_new])
        u = _dot(xb, wu_sc[slot_new])
        act_sc[act_out] = (g * jax.nn.sigmoid(g) * u).astype(BF16)

    @pl.when(has_done & jnp.logical_not(has_new))
    def _():
        y_ref[...] = _pack_pairs(_dot(act_sc[act_in], wd_sc[slot_done]))

    @pl.when((done >= nu_ref[0]) & (done < n_blk))
    def _():
        y_ref[...] = jnp.zeros(y_ref.shape, y_ref.dtype)


def _sc_gather(table, idx):
    n, d = idx.shape[0], table.shape[1]
    workers = SC_CORES * SC_SUBCORES
    per_w = n // workers
    assert per_w * workers == n and per_w % (2 * SC_CHUNK) == 0
    mesh = plsc.VectorSubcoreMesh(core_axis_name="c", subcore_axis_name="s")

    @functools.partial(
        pl.kernel, mesh=mesh,
        out_type=jax.ShapeDtypeStruct((n, d), table.dtype),
        scratch_types=[pltpu.VMEM((SC_CHUNK,), jnp.int32), pltpu.VMEM((SC_CHUNK,), jnp.int32),
                       pltpu.VMEM((SC_CHUNK, d), table.dtype), pltpu.VMEM((SC_CHUNK, d), table.dtype)]
                      + [pltpu.SemaphoreType.DMA] * 4,
        name="sc_gather")
    def gather(table_hbm, idx_hbm, out_hbm, idx_a, idx_b, rows_a, rows_b, sem_a, sem_b, sem_c, sem_d):
        wid = lax.axis_index("s") * SC_CORES + lax.axis_index("c")

        @pl.loop(0, per_w // (2 * SC_CHUNK))
        def _(j):
            base_a = pl.multiple_of(wid * per_w + j * (2 * SC_CHUNK), SC_CHUNK)
            base_b = pl.multiple_of(base_a + SC_CHUNK, SC_CHUNK)
            pltpu.sync_copy(idx_hbm.at[pl.ds(base_a, SC_CHUNK)], idx_a)
            pltpu.sync_copy(idx_hbm.at[pl.ds(base_b, SC_CHUNK)], idx_b)
            get_a = pltpu.async_copy(table_hbm.at[idx_a], rows_a, sem_a)
            get_b = pltpu.async_copy(table_hbm.at[idx_b], rows_b, sem_b)
            get_a.wait()
            put_a = pltpu.async_copy(rows_a, out_hbm.at[pl.ds(base_a, SC_CHUNK)], sem_c)
            get_b.wait()
            put_b = pltpu.async_copy(rows_b, out_hbm.at[pl.ds(base_b, SC_CHUNK)], sem_d)
            put_a.wait()
            put_b.wait()

    return gather(table, idx)


def _sc_dispatch(src, dest0, dest1, n_out):
    n, d = src.shape
    workers = SC_CORES * SC_SUBCORES
    per_w = n // workers
    assert per_w * workers == n and per_w % SC_CHUNK == 0
    mesh = plsc.VectorSubcoreMesh(core_axis_name="c", subcore_axis_name="s")

    @functools.partial(
        pl.kernel, mesh=mesh,
        out_type=jax.ShapeDtypeStruct((n_out, d), src.dtype),
        scratch_types=[pltpu.VMEM((SC_CHUNK,), jnp.int32),
                       pltpu.VMEM((SC_CHUNK,), jnp.int32),
                       pltpu.VMEM((SC_CHUNK, d), src.dtype)]
                      + [pltpu.SemaphoreType.DMA] * 3,
        name="sc_dispatch")
    def dispatch(src_hbm, d0_hbm, d1_hbm, out_hbm, i0_v, i1_v, rows_v, sem_a, sem_b, sem_c):
        wid = lax.axis_index("s") * SC_CORES + lax.axis_index("c")

        @pl.loop(0, per_w // SC_CHUNK)
        def _(j):
            base = pl.multiple_of(wid * per_w + j * SC_CHUNK, SC_CHUNK)
            get_0 = pltpu.async_copy(d0_hbm.at[pl.ds(base, SC_CHUNK)], i0_v, sem_a)
            get_1 = pltpu.async_copy(d1_hbm.at[pl.ds(base, SC_CHUNK)], i1_v, sem_b)
            get_r = pltpu.async_copy(src_hbm.at[pl.ds(base, SC_CHUNK)], rows_v, sem_c)
            get_0.wait()
            get_1.wait()
            get_r.wait()
            put_0 = pltpu.async_copy(rows_v, out_hbm.at[i0_v], sem_a)
            put_1 = pltpu.async_copy(rows_v, out_hbm.at[i1_v], sem_b)
            put_0.wait()
            put_1.wait()

    return dispatch(src, dest0, dest1)


def _moe_out_kernel(x_ref, y1_ref, y2_ref, info_ref, gate_ref, g_ref, b_ref, o_ref):
    info = info_ref[0]
    ts = info.shape[1]
    info_tok = jnp.concatenate([info, jnp.zeros((LANES - INFO_ROWS, ts), F32)], axis=0).T
    y = info_tok[:, 4:5] * _unpack_pairs(y1_ref[0, 0]) + info_tok[:, 5:6] * _unpack_pairs(y2_ref[0, 0])
    z = ALPHA * x_ref[0] + gate_ref[0] * y
    o_ref[0] = _layer_norm(z, g_ref[...], b_ref[...])


def _moe_out(x, y12, info, gate, g, b, ts=MOE_OUT_TOKENS):
    B, S, D = x.shape
    ts = min(ts, S)
    tok = pl.BlockSpec((1, ts, D), lambda bi, i: (bi, i, 0))
    return pl.pallas_call(
        _moe_out_kernel,
        grid=(B, S // ts),
        in_specs=[tok,
                  pl.BlockSpec((1, 1, ts, D // 2), lambda bi, i: (0, bi, i, 0)),
                  pl.BlockSpec((1, 1, ts, D // 2), lambda bi, i: (1, bi, i, 0)),
                  pl.BlockSpec((1, INFO_ROWS, ts), lambda bi, i: (bi, 0, i)),
                  pl.BlockSpec((1, 1, D), lambda bi, i: (bi, 0, 0)),
                  pl.BlockSpec((1, D), lambda bi, i: (0, 0)),
                  pl.BlockSpec((1, D), lambda bi, i: (0, 0))],
        out_specs=tok,
        out_shape=jax.ShapeDtypeStruct((B, S, D), F32),
        compiler_params=_cp(("arbitrary", "arbitrary"), VMEM_MB_MID),
        name="moe_out",
    )(x, y12, y12, info, gate, g.reshape(1, D), b.reshape(1, D))


def _moe_layer(x, routed, gate, layer, w_gate, w_up, w_down, ln_g, ln_b):
    B, S, D = x.shape
    T = B * S
    E, R, F = MOE_E, MOE_ROWS, w_gate.shape[-1]
    hb, info, cnt = routed

    counts = cnt[:, 0].astype(jnp.int32)
    padded = (counts + R - 1) // R * R
    ends = jnp.cumsum(padded)
    starts = ends - padded
    expert_ids = jnp.arange(E, dtype=jnp.int32)

    def per_expert(table, expert):
        return jnp.sum(jnp.where(expert[..., None] == expert_ids, table, 0), axis=-1)

    def row_of(expert, rank):
        return (per_expert(starts, expert.astype(jnp.int32)) + rank.astype(jnp.int32)).reshape(T)

    dest0 = row_of(info[:, 0, :], info[:, 2, :])
    dest1 = row_of(info[:, 1, :], info[:, 3, :])
    n_blk = (T * 2) // R + E
    P = n_blk * R
    blk_exp = jnp.minimum(jnp.sum(jnp.arange(n_blk)[:, None] * R >= ends[None, :], axis=1), E - 1).astype(jnp.int32)
    n_used = (ends[-1] // R).astype(jnp.int32).reshape(1)
    x_rows = _sc_dispatch(hb.reshape(T, D // 2), dest0, dest1, P)

    seg = jnp.concatenate([jnp.zeros((1,), jnp.int32),
                           jnp.cumsum((blk_exp[1:] != blk_exp[:-1]).astype(jnp.int32))])

    live = jnp.clip(per_expert(counts, blk_exp) + per_expert(starts, blk_exp) - jnp.arange(n_blk, dtype=jnp.int32) * R,
                    0, R).astype(jnp.int32)

    def blk_at(offset):
        return lambda i: jnp.clip(i - offset, 0, n_blk - 1)

    y_rows = pl.pallas_call(
        _ffn_kernel,
        grid_spec=pltpu.PrefetchScalarGridSpec(
            num_scalar_prefetch=4,
            grid=(n_blk + FFN_LEAD + 1,),
            in_specs=[pl.BlockSpec((R, D // 2), lambda i, be, sg, nv, nu: (blk_at(FFN_LEAD)(i), 0)),
                      pl.BlockSpec((1, 1, D, F), lambda i, be, sg, nv, nu: (layer, be[blk_at(0)(i)], 0, 0)),
                      pl.BlockSpec((1, 1, D, F), lambda i, be, sg, nv, nu: (layer, be[blk_at(1)(i)], 0, 0)),
                      pl.BlockSpec((1, 1, F, D), lambda i, be, sg, nv, nu: (layer, be[blk_at(2)(i)], 0, 0))],
            out_specs=pl.BlockSpec((R, D // 2), lambda i, be, sg, nv, nu: (blk_at(FFN_LEAD + 1)(i), 0)),
            scratch_shapes=[pltpu.VMEM((FFN_SLOTS, D, F), BF16), pltpu.VMEM((FFN_SLOTS, D, F), BF16),
                            pltpu.VMEM((FFN_SLOTS, F, D), BF16), pltpu.VMEM((2, R, F), BF16)]),
        out_shape=jax.ShapeDtypeStruct((P, D // 2), jnp.uint32),
        compiler_params=_cp(("arbitrary",), VMEM_MB_LARGE),
        name="moe_ffn",
    )(blk_exp, seg, live, n_used, x_rows, w_gate, w_up, w_down)

    y12 = _sc_gather(y_rows, jnp.concatenate([dest0, dest1])).reshape(2, B, S, D // 2)
    return _moe_out(x, y12, info, gate, ln_g, ln_b)


def kernel(x, c, positions, ada_w, ada_b, ln_g, ln_b, mla_w_in, mla_g_q, mla_w_uq, mla_g_kv, mla_w_ukv, mla_w_o, nsa_w_in, nsa_pe_k, nsa_pe_v, nsa_w_ck1, nsa_w_ck2, nsa_w_cv1, nsa_w_cv2, nsa_w_o, sb_w_qkv, sb_w_o, moe_w_grp, moe_b_grp, moe_w_rt, moe_b_rt, moe_w_gate, moe_w_up, moe_w_down):
    D = x.shape[-1]
    depth = ada_w.shape[0]
    cos_mla, sin_mla = _rope_tables(positions, MLA_DR)
    cos_nsa, sin_nsa = _rope_tables(positions, NSA_DH)
    mods = _ada_all(c, ada_w, ada_b)

    def mod(i, sub):
        m = mods[2 * i + sub][:, None, :]
        return m[..., :D], m[..., D:2 * D], m[..., 2 * D:]

    for i in range(depth):
        kind, j = i % N_MIXERS, i // N_MIXERS
        shift, scale_v, gate = mod(i, 0)
        shift2, scale2, gate2 = mod(i, 1)
        route = (scale2, shift2, moe_w_grp[i], moe_b_grp[i], moe_w_rt[i], moe_b_rt[i])
        if kind == 0:
            x, *routed = _mla_layer(x, shift, scale_v, gate, cos_mla, sin_mla, mla_w_in[j], mla_g_q[j], mla_w_uq[j],
                                    mla_g_kv[j], mla_w_ukv[j], mla_w_o[j], ln_g[i, 0], ln_b[i, 0], route)
        elif kind == 1:
            x, *routed = _nsa_layer(x, shift, scale_v, gate, cos_nsa, sin_nsa, nsa_w_in[j], nsa_pe_k[j], nsa_pe_v[j],
                                    nsa_w_ck1[j], nsa_w_ck2[j], nsa_w_cv1[j], nsa_w_cv2[j], nsa_w_o[j],
                                    ln_g[i, 0], ln_b[i, 0], route)
        else:
            x, *routed = _sb_layer(x, shift, scale_v, gate, sb_w_qkv[j], sb_w_o[j], ln_g[i, 0], ln_b[i, 0], route)
        x = _moe_layer(x, routed, gate2, i, moe_w_gate, moe_w_up, moe_w_down, ln_g[i, 1], ln_b[i, 1])
    return x
```
